```python
import jax, jax.numpy as jnp
from jax import lax
import numpy as np

D_MODEL = 4096
BATCH = 4
SEQ = 2048
DEPTH = 1
DEC_BATCH = 8
DEC_SEQ = 32
PAST_LEN = 1024

CHUNK = 64
N_META = 16
EPS = 1e-6
V_HEAD = 128
QK_NOPE = 128
QK_ROPE = 64
Q_LORA = 1536
KV_LORA = 512
MLA_HEADS = (D_MODEL // 2) // V_HEAD
MLA_WIDTH = MLA_HEADS * V_HEAD
ATTN_SCALE = (QK_NOPE + QK_ROPE) ** -0.5
ROPE_THETA = 10000.0
Q_BLOCK = 128
ML_DV = 512
ML_DQK = 256
ML_HEADS = (D_MODEL // 2) // ML_DV
ML_WIDTH = ML_HEADS * ML_DV
ML_BLOCK = 16
D_MIX = MLA_WIDTH + ML_WIDTH
OFF_CQ = 0
OFF_CKV = OFF_CQ + Q_LORA
OFF_KPE = OFF_CKV + KV_LORA
OFF_MQ = OFF_KPE + QK_ROPE
OFF_MK = OFF_MQ + ML_HEADS * ML_DQK
OFF_MV = OFF_MK + ML_HEADS * ML_DQK
OFF_MO = OFF_MV + ML_WIDTH
OFF_MI = OFF_MO + ML_WIDTH
OFF_MF = OFF_MI + ML_HEADS
OFF_ZA = OFF_MF + ML_HEADS
OFF_ZB = OFF_ZA + MLA_WIDTH
IN_COLS = OFF_ZB + ML_WIDTH

kernel_name = 'hymba_mla_mlstm_streaming_step'


def rmsnorm(x, g):
    xf = x.astype(jnp.float32)
    y = xf * lax.rsqrt(jnp.mean(xf * xf, axis=-1, keepdims=True) + EPS)
    return (y * g.astype(jnp.float32)).astype(x.dtype)


def rope(x, pos):
    half = QK_ROPE // 2
    inv_freq = ROPE_THETA ** (-jnp.arange(half, dtype=jnp.float32) / half)
    ang = pos.astype(jnp.float32)[:, None] * inv_freq[None, :]
    cos = jnp.cos(ang)[None, :, None, :]
    sin = jnp.sin(ang)[None, :, None, :]
    xf = x.astype(jnp.float32)
    x1, x2 = xf[..., :half], xf[..., half:]
    return jnp.concatenate([x1 * cos - x2 * sin, x2 * cos + x1 * sin], axis=-1).astype(x.dtype)


def chunk_id(p):
    return jnp.where(p < N_META, 0, (p - N_META) // CHUNK + 1)


def project(h, w_in):
    p = h @ w_in
    return (p[..., OFF_CQ:OFF_CKV], p[..., OFF_CKV:OFF_KPE], p[..., OFF_KPE:OFF_MQ],
            p[..., OFF_MQ:OFF_MK], p[..., OFF_MK:OFF_MV], p[..., OFF_MV:OFF_MO],
            p[..., OFF_MO:OFF_MI], p[..., OFF_MI:OFF_MF], p[..., OFF_MF:OFF_ZA],
            p[..., OFF_ZA:OFF_ZB], p[..., OFF_ZB:IN_COLS])


def mla_latent(ckv_raw, kpe_raw, pos, kv_a_norm, k_rope_norm):
    ckv = rmsnorm(ckv_raw, kv_a_norm)
    kpe = rope(rmsnorm(kpe_raw, k_rope_norm)[:, :, None, :], pos)[:, :, 0, :]
    return ckv, kpe


def mla_queries(cq_raw, pos, q_a_norm, w_q_up, q_nope_norm, q_rope_norm):
    B, T, _ = cq_raw.shape
    q = (rmsnorm(cq_raw, q_a_norm) @ w_q_up).reshape(B, T, MLA_HEADS, QK_NOPE + QK_ROPE)
    q_nope = rmsnorm(q[..., :QK_NOPE], q_nope_norm)
    q_pe = rope(rmsnorm(q[..., QK_NOPE:], q_rope_norm), pos)
    return q_nope, q_pe


def mla_expand(ckv, w_kv_up, k_nope_norm):
    B, S, _ = ckv.shape
    kv = (ckv @ w_kv_up).reshape(B, S, MLA_HEADS, QK_NOPE + V_HEAD)
    return rmsnorm(kv[..., :QK_NOPE], k_nope_norm), kv[..., QK_NOPE:]


def mla_attend(q_nope, q_pe, k_nope, k_pe, v, mask):
    s = (jnp.einsum('bqhd,bshd->bhqs', q_nope, k_nope)
         + jnp.einsum('bqhd,bsd->bhqs', q_pe, k_pe)).astype(jnp.float32) * ATTN_SCALE
    if mask is not None:
        s = jnp.where(mask[None, None], s, -jnp.inf)
    p = jax.nn.softmax(s, axis=-1).astype(v.dtype)
    return jnp.einsum('bhqs,bshd->bqhd', p, v)


def mla_prompt_attention(q_nope, q_pe, k_nope, k_pe, v):
    B, T = q_nope.shape[:2]
    n_qb = -(-T // Q_BLOCK)
    pad = n_qb * Q_BLOCK - T
    qn = jnp.pad(q_nope, ((0, 0), (0, pad), (0, 0), (0, 0)))
    qp = jnp.pad(q_pe, ((0, 0), (0, pad), (0, 0), (0, 0)))
    cid_k = chunk_id(jnp.arange(T))

    def one_block(i):
        start = i * Q_BLOCK
        cid_q = chunk_id(start + jnp.arange(Q_BLOCK))
        mask = cid_q[:, None] >= cid_k[None, :]
        return mla_attend(lax.dynamic_slice_in_dim(qn, start, Q_BLOCK, 1),
                          lax.dynamic_slice_in_dim(qp, start, Q_BLOCK, 1), k_nope, k_pe, v, mask)

    out = lax.map(one_block, jnp.arange(n_qb))
    return jnp.moveaxis(out, 0, 1).reshape(B, n_qb * Q_BLOCK, MLA_HEADS, V_HEAD)[:, :T]


def mlstm_block(carry, blk):
    C0, n0, m0 = carry
    q, k, v, ig, lf = blk
    L = q.shape[2]
    b = jnp.cumsum(lf, axis=-1)
    a = b + m0[..., None]
    D = b[..., :, None] - b[..., None, :] + ig[..., None, :]
    D = jnp.where(jnp.tril(jnp.ones((L, L), dtype=bool)), D, -jnp.inf)
    m = jnp.maximum(a, D.max(axis=-1))
    w_inter = jnp.exp(a - m)
    qk = jnp.einsum('bhtd,bhsd->bhts', q, k) * jnp.exp(D - m[..., None])
    num = w_inter[..., None] * jnp.einsum('bhtd,bhde->bhte', q, C0) + jnp.einsum('bhts,bhse->bhte', qk, v)
    den = w_inter * jnp.einsum('bhtd,bhd->bht', q, n0) + qk.sum(axis=-1)
    h = num / jnp.maximum(jnp.abs(den), jnp.exp(-m))[..., None]
    bL = b[..., -1]
    g = bL[..., None] - b + ig
    mL = jnp.maximum(bL + m0, g.max(axis=-1))
    decay = jnp.exp(bL + m0 - mL)
    ws = jnp.exp(g - mL[..., None])
    C1 = decay[..., None, None] * C0 + jnp.einsum('bhs,bhsd,bhse->bhde', ws, k, v)
    n1 = decay[..., None] * n0 + jnp.einsum('bhs,bhsd->bhd', ws, k)
    return (C1, n1, mL), h


def mlstm_branch(mq, mk, mv, mo, mi, mf, b_igate, b_fgate, ml_out_norm, state, block):
    B, T, _ = mq.shape
    f32 = jnp.float32
    heads = lambda a, d: a.reshape(B, T, ML_HEADS, d).transpose(0, 2, 1, 3).astype(f32)
    q = heads(mq, ML_DQK)
    k = heads(mk, ML_DQK) * (ML_DQK ** -0.5)
    v = heads(mv, ML_DV)
    ig = (mi + b_igate).astype(f32).transpose(0, 2, 1)
    lf = jax.nn.log_sigmoid((mf + b_fgate).astype(f32)).transpose(0, 2, 1)
    nb = T // block
    to_blocks = lambda a: jnp.moveaxis(a.reshape(B, ML_HEADS, nb, block, *a.shape[3:]), 2, 0)
    state = tuple(s.astype(f32) for s in state)
    state, h = lax.scan(mlstm_block, state,
                        (to_blocks(q), to_blocks(k), to_blocks(v), to_blocks(ig), to_blocks(lf)))
    h = jnp.moveaxis(h, 0, 2).reshape(B, ML_HEADS, T, ML_DV).transpose(0, 2, 1, 3)
    h = jax.nn.sigmoid(mo.astype(f32)).reshape(B, T, ML_HEADS, ML_DV) * rmsnorm(h, ml_out_norm)
    return h.reshape(B, T, ML_WIDTH).astype(mq.dtype), state


def merge(x_res, attn, za, ml_h, zb, w_out):
    B, T, _ = x_res.shape
    a = attn.reshape(B, T, MLA_WIDTH) * jax.nn.silu(za)
    m = ml_h * jax.nn.silu(zb)
    return x_res + jnp.concatenate([a, m], axis=-1) @ w_out


def prompt_path(x, params):
    (meta_tokens, norm_gain, w_in, b_igate, b_fgate, q_a_norm, w_q_up, q_nope_norm, q_rope_norm,
     kv_a_norm, k_rope_norm, w_kv_up, k_nope_norm, ml_out_norm, w_out) = params
    B, S, _ = x.shape
    T = N_META + S
    seq = jnp.concatenate([jnp.broadcast_to(meta_tokens.astype(x.dtype)[None], (B, N_META, D_MODEL)), x], axis=1)
    pos = jnp.arange(T)
    cq, ckv_raw, kpe_raw, mq, mk, mv, mo, mi, mf, za, zb = project(rmsnorm(seq, norm_gain), w_in)
    ckv, kpe = mla_latent(ckv_raw, kpe_raw, pos, kv_a_norm, k_rope_norm)
    q_nope, q_pe = mla_queries(cq, pos, q_a_norm, w_q_up, q_nope_norm, q_rope_norm)
    k_nope, v = mla_expand(ckv, w_kv_up, k_nope_norm)
    attn = mla_prompt_attention(q_nope, q_pe, k_nope, kpe, v)
    state0 = (jnp.zeros((B, ML_HEADS, ML_DQK, ML_DV), jnp.float32),
              jnp.zeros((B, ML_HEADS, ML_DQK), jnp.float32),
              jnp.zeros((B, ML_HEADS), jnp.float32))
    ml_h, (C, n, m) = mlstm_branch(mq, mk, mv, mo, mi, mf, b_igate, b_fgate, ml_out_norm, state0, ML_BLOCK)
    y = merge(seq, attn, za, ml_h, zb, w_out)[:, N_META:]
    return y, ckv, kpe, C, n, m


def sample_path(x, cache_ckv, cache_kpe, state, params):
    (meta_tokens, norm_gain, w_in, b_igate, b_fgate, q_a_norm, w_q_up, q_nope_norm, q_rope_norm,
     kv_a_norm, k_rope_norm, w_kv_up, k_nope_norm, ml_out_norm, w_out) = params
    B, L, _ = x.shape
    P = cache_ckv.shape[1]
    mp = rmsnorm(meta_tokens.astype(x.dtype), norm_gain)[None] @ w_in[:, OFF_CKV:OFF_MQ]
    meta_ckv, meta_kpe = mla_latent(mp[..., :KV_LORA], mp[..., KV_LORA:], jnp.arange(N_META), kv_a_norm, k_rope_norm)
    pos = N_META + P + jnp.arange(L)
    cq, ckv_raw, kpe_raw, mq, mk, mv, mo, mi, mf, za, zb = project(rmsnorm(x, norm_gain), w_in)
    ckv_new, kpe_new = mla_latent(ckv_raw, kpe_raw, pos, kv_a_norm, k_rope_norm)
    q_nope, q_pe = mla_queries(cq, pos, q_a_norm, w_q_up, q_nope_norm, q_rope_norm)
    ckv_all = jnp.concatenate([jnp.broadcast_to(meta_ckv, (B, N_META, KV_LORA)),
                               cache_ckv.astype(ckv_new.dtype), ckv_new], axis=1)
    kpe_all = jnp.concatenate([jnp.broadcast_to(meta_kpe, (B, N_META, QK_ROPE)),
                               cache_kpe.astype(kpe_new.dtype), kpe_new], axis=1)
    k_nope, v = mla_expand(ckv_all, w_kv_up, k_nope_norm)
    attn = mla_attend(q_nope, q_pe, k_nope, kpe_all, v, None)
    ml_h, (C, n, m) = mlstm_branch(mq, mk, mv, mo, mi, mf, b_igate, b_fgate, ml_out_norm, state, L)
    y = merge(x, attn, za, ml_h, zb, w_out)
    return y, ckv_new, kpe_new, C, n, m


def setup_inputs(seed: int = 0) -> dict:
    key = jax.random.key(seed)
    ks = jax.random.split(key, 24)
    nrm = lambda k, shape, s=1.0: s * jax.random.normal(k, shape, jnp.float32)
    gain = lambda k, shape: 1.0 + 0.02 * jax.random.normal(k, shape, jnp.float32)
    return {
        'x_prompt': nrm(ks[0], (BATCH, SEQ, D_MODEL)),
        'x_sample': nrm(ks[1], (DEC_BATCH, DEC_SEQ, D_MODEL)),
        'cache_ckv': nrm(ks[2], (DEC_BATCH, PAST_LEN, KV_LORA)),
        'cache_kpe': nrm(ks[3], (DEC_BATCH, PAST_LEN, QK_ROPE)),
        'state_C': nrm(ks[4], (DEC_BATCH, ML_HEADS, ML_DQK, ML_DV), 0.1),
        'state_n': nrm(ks[5], (DEC_BATCH, ML_HEADS, ML_DQK), 0.5),
        'state_m': nrm(ks[6], (DEC_BATCH, ML_HEADS)),
        'meta_tokens': nrm(ks[7], (N_META, D_MODEL)),
        'norm_gain': gain(ks[8], (D_MODEL,)),
        'w_in': nrm(ks[9], (D_MODEL, IN_COLS), D_MODEL ** -0.5),
        'b_igate': nrm(ks[10], (ML_HEADS,), 0.1),
        'b_fgate': 3.0 + nrm(ks[11], (ML_HEADS,), 0.1),
        'q_a_norm': gain(ks[12], (Q_LORA,)),
        'w_q_up': nrm(ks[13], (Q_LORA, MLA_HEADS * (QK_NOPE + QK_ROPE)), Q_LORA ** -0.5),
        'q_nope_norm': gain(ks[14], (QK_NOPE,)),
        'q_rope_norm': gain(ks[15], (QK_ROPE,)),
        'kv_a_norm': gain(ks[16], (KV_LORA,)),
        'k_rope_norm': gain(ks[17], (QK_ROPE,)),
        'w_kv_up': nrm(ks[18], (KV_LORA, MLA_HEADS * (QK_NOPE + V_HEAD)), KV_LORA ** -0.5),
        'k_nope_norm': gain(ks[19], (QK_NOPE,)),
        'ml_out_norm': gain(ks[20], (ML_HEADS, ML_DV)),
        'w_out': nrm(ks[21], (D_MIX, D_MODEL), D_MIX ** -0.5),
    }


def reference(x_prompt, x_sample, cache_ckv, cache_kpe, state_C, state_n, state_m, meta_tokens, norm_gain,
              w_in, b_igate, b_fgate, q_a_norm, w_q_up, q_nope_norm, q_rope_norm, kv_a_norm, k_rope_norm,
              w_kv_up, k_nope_norm, ml_out_norm, w_out):
    params = (meta_tokens, norm_gain, w_in, b_igate, b_fgate, q_a_norm, w_q_up, q_nope_norm, q_rope_norm,
              kv_a_norm, k_rope_norm, w_kv_up, k_nope_norm, ml_out_norm, w_out)
    y_prompt, ckv_p, kpe_p, C_p, n_p, m_p = prompt_path(x_prompt, params)
    y_sample, ckv_s, kpe_s, C_s, n_s, m_s = sample_path(x_sample, cache_ckv, cache_kpe,
                                                        (state_C, state_n, state_m), params)
    return (y_prompt, y_sample, ckv_p, kpe_p, C_p, n_p, m_p, ckv_s, kpe_s, C_s, n_s, m_s)
```

```python
import functools

import jax
import jax.numpy as jnp
from jax import lax
from jax.experimental import pallas as pl
from jax.experimental.pallas import tpu as pltpu

CHUNK = 64
EPS = 1e-6
ROPE_THETA = 10000.0
V_HEAD = 128
QK_NOPE = 128
QK_ROPE = 64
HALF_ROPE = QK_ROPE // 2
LANE = 128
QK_PAD = QK_NOPE + LANE
COL_TILE = 512
NEG_BIG = -1e30
VMEM_LIMIT = 56 * 1024 * 1024

F32 = jnp.float32
BF16 = jnp.bfloat16


def _cparams(sem):
    return pltpu.CompilerParams(dimension_semantics=sem, vmem_limit_bytes=VMEM_LIMIT)


def _dot(a, b):
    return jnp.dot(a, b, preferred_element_type=F32)


def _dot_nt(a, b):
    return lax.dot_general(a, b, (((1,), (1,)), ((), ())), preferred_element_type=F32)


def _rms(x, n):
    return x * lax.rsqrt(jnp.sum(x * x, axis=-1, keepdims=True) * (1.0 / n) + EPS)


def _rope_pad(x, cos, sin_signed):
    lane = lax.broadcasted_iota(jnp.int32, x.shape, 1)
    rot = jnp.where(lane < HALF_ROPE, pltpu.roll(x, LANE - HALF_ROPE, 1), pltpu.roll(x, HALF_ROPE, 1))
    return x * cos + rot * sin_signed


def _proj_kernel(x_ref, g_ref, w_ref, ws_ref, o_ref, os_ref, hn_ref):
    @pl.when(pl.program_id(1) == 0)
    def _():
        x = x_ref[...]
        hn = (_rms(x, x.shape[-1]) * g_ref[...]).astype(BF16)
        hn_ref[...] = hn
        os_ref[...] = _dot(hn, ws_ref[...])

    o_ref[...] = _dot(hn_ref[...], w_ref[...]).astype(o_ref.dtype)


def _proj(x, gain, w_main, w_small, tm):
    m, d = x.shape
    n = w_main.shape[1]
    ns = w_small.shape[1]
    return pl.pallas_call(
        _proj_kernel,
        grid=(m // tm, n // COL_TILE),
        in_specs=[
            pl.BlockSpec((tm, d), lambda i, j: (i, 0)),
            pl.BlockSpec((1, d), lambda i, j: (0, 0)),
            pl.BlockSpec((d, COL_TILE), lambda i, j: (0, j)),
            pl.BlockSpec((d, ns), lambda i, j: (0, 0)),
        ],
        out_specs=[
            pl.BlockSpec((tm, COL_TILE), lambda i, j: (i, j)),
            pl.BlockSpec((tm, ns), lambda i, j: (i, 0)),
        ],
        out_shape=[jax.ShapeDtypeStruct((m, n), BF16), jax.ShapeDtypeStruct((m, ns), F32)],
        scratch_shapes=[pltpu.VMEM((tm, d), BF16)],
        compiler_params=_cparams(("parallel", "arbitrary")),
        name="proj",
    )(x, gain, w_main, w_small)


def _latent_q_kernel(*refs, n_cq, kv_lora, heads_per_step):
    cq_refs = refs[:n_cq]
    (pf_ref, cos_ref, sin_ref, qan_ref, wq_ref, qng_ref, qrg_ref, kvg_ref, krg_ref,
     q_ref, ckv_ref, kpe_ref, kpep_ref, cqn_ref) = refs[n_cq:]
    cos = cos_ref[...]
    sin = sin_ref[...]

    @pl.when(pl.program_id(1) == 0)
    def _():
        cq = jnp.concatenate([r[...].astype(F32) for r in cq_refs], axis=1)
        cqn_ref[...] = (_rms(cq, cq.shape[-1]) * qan_ref[...]).astype(BF16)
        pf = pf_ref[...]
        ckv_raw = pf[:, :kv_lora]
        ckv_ref[...] = _rms(ckv_raw, kv_lora) * kvg_ref[...]
        tail = pf[:, kv_lora:kv_lora + LANE]
        lane = lax.broadcasted_iota(jnp.int32, tail.shape, 1)
        kraw = jnp.where(lane < QK_ROPE, tail, 0.0)
        kpe = _rope_pad(_rms(kraw, QK_ROPE) * krg_ref[...], cos, sin)
        kpe_ref[...] = kpe[:, :QK_ROPE]
        kpep_ref[...] = kpe.astype(BF16)

    qf = _dot(cqn_ref[...], wq_ref[...])
    for h in range(heads_per_step):
        base = h * QK_PAD
        nope = qf[:, base:base + QK_NOPE]
        pe = qf[:, base + QK_NOPE:base + QK_PAD]
        q_ref[:, base:base + QK_NOPE] = (_rms(nope, QK_NOPE) * qng_ref[...]).astype(BF16)
        q_ref[:, base + QK_NOPE:base + QK_PAD] = _rope_pad(_rms(pe, QK_ROPE) * qrg_ref[...], cos, sin).astype(BF16)


def _latent_q(p_bf, p_f32, cq_off, q_lora, kv_lora, cos, sin, qan, wq, qng, qrg, kvg, krg, tm, heads_per_step):
    m = p_bf.shape[0]
    n_heads = wq.shape[1] // QK_PAD
    n_cq = q_lora // COL_TILE
    cq0 = cq_off // COL_TILE
    ns = p_f32.shape[1]
    wstep = heads_per_step * QK_PAD
    row = lambda i, j: (i, 0)
    const = lambda i, j: (0, 0)
    in_specs = [pl.BlockSpec((tm, COL_TILE), functools.partial(lambda i, j, c: (i, c), c=cq0 + c))
                for c in range(n_cq)]
    in_specs += [
        pl.BlockSpec((tm, ns), row),
        pl.BlockSpec((tm, LANE), row),
        pl.BlockSpec((tm, LANE), row),
        pl.BlockSpec((1, q_lora), const),
        pl.BlockSpec((q_lora, wstep), lambda i, j: (0, j)),
        pl.BlockSpec((1, LANE), const),
        pl.BlockSpec((1, LANE), const),
        pl.BlockSpec((1, kv_lora), const),
        pl.BlockSpec((1, LANE), const),
    ]
    return pl.pallas_call(
        functools.partial(_latent_q_kernel, n_cq=n_cq, kv_lora=kv_lora, heads_per_step=heads_per_step),
        grid=(m // tm, n_heads // heads_per_step),
        in_specs=in_specs,
        out_specs=[
            pl.BlockSpec((tm, wstep), lambda i, j: (i, j)),
            pl.BlockSpec((tm, kv_lora), row),
            pl.BlockSpec((tm, QK_ROPE), row),
            pl.BlockSpec((tm, LANE), row),
        ],
        out_shape=[
            jax.ShapeDtypeStruct((m, n_heads * QK_PAD), BF16),
            jax.ShapeDtypeStruct((m, kv_lora), F32),
            jax.ShapeDtypeStruct((m, QK_ROPE), F32),
            jax.ShapeDtypeStruct((m, LANE), BF16),
        ],
        scratch_shapes=[pltpu.VMEM((tm, q_lora), BF16)],
        compiler_params=_cparams(("parallel", "arbitrary")),
        name="latent_q",
    )(*([p_bf] * n_cq), p_f32, cos, sin, qan, wq, qng, qrg, kvg, krg)


def _kv_expand_kernel(ckv_ref, kpep_ref, wk_ref, wv_ref, kng_ref, k_ref, v_ref, *, n_heads):
    c = ckv_ref[...].astype(BF16)
    kf = _dot(c, wk_ref[...])
    v_ref[...] = _dot(c, wv_ref[...]).astype(BF16)
    kpep = kpep_ref[...]
    for h in range(n_heads):
        kn = _rms(kf[:, h * QK_NOPE:(h + 1) * QK_NOPE], QK_NOPE) * kng_ref[...]
        k_ref[:, h * QK_PAD:h * QK_PAD + QK_NOPE] = kn.astype(BF16)
        k_ref[:, h * QK_PAD + QK_NOPE:(h + 1) * QK_PAD] = kpep


def _kv_expand(ckv, kpep, wk, wv, kng, tm):
    r, kv_lora = ckv.shape
    n_heads = wk.shape[1] // QK_NOPE
    row = lambda i: (i, 0)
    const = lambda i: (0, 0)
    return pl.pallas_call(
        functools.partial(_kv_expand_kernel, n_heads=n_heads),
        grid=(r // tm,),
        in_specs=[
            pl.BlockSpec((tm, kv_lora), row),
            pl.BlockSpec((tm, LANE), row),
            pl.BlockSpec((kv_lora, n_heads * QK_NOPE), const),
            pl.BlockSpec((kv_lora, n_heads * V_HEAD), const),
            pl.BlockSpec((1, LANE), const),
        ],
        out_specs=[
            pl.BlockSpec((tm, n_heads * QK_PAD), row),
            pl.BlockSpec((tm, n_heads * V_HEAD), row),
        ],
        out_shape=[
            jax.ShapeDtypeStruct((r, n_heads * QK_PAD), BF16),
            jax.ShapeDtypeStruct((r, n_heads * V_HEAD), BF16),
        ],
        compiler_params=_cparams(("parallel",)),
        name="kv_expand",
    )(ckv, kpep, wk, wv, kng)


def _softmax_step(carry, s, v):
    m, l, acc = carry
    m_new = jnp.maximum(m, jnp.max(s, axis=-1, keepdims=True))
    alpha = jnp.exp(m - m_new)
    p = jnp.exp(s - m_new)
    l = alpha * l + jnp.sum(p, axis=-1, keepdims=True)
    acc = alpha * acc + _dot(p.astype(BF16), v)
    return m_new, l, acc


def _softmax_first(s, v):
    m = jnp.max(s, axis=-1, keepdims=True)
    p = jnp.exp(s - m)
    return m, jnp.sum(p, axis=-1, keepdims=True), _dot(p.astype(BF16), v)


def _gate_out(carry, za_ref, o_ref):
    _, l, acc = carry
    za = za_ref[...].astype(F32)
    o_ref[...] = (acc / l * (za * jax.nn.sigmoid(za))).astype(o_ref.dtype)


def _attn_prompt_kernel(q_ref, k_ref, v_ref, km_ref, vm_ref, za_ref, o_ref, *, tq, scale, n_meta):
    qi = pl.program_id(2)
    q = q_ref[...]
    s = _dot_nt(q, km_ref[...]) * scale
    col = lax.broadcasted_iota(jnp.int32, s.shape, 1)
    carry = _softmax_first(jnp.where(col < n_meta, s, -jnp.inf), vm_ref[...])

    def body(t, carry):
        start = pl.multiple_of(t * tq, tq)
        s = _dot_nt(q, k_ref[pl.ds(start, tq), :]) * scale
        return _softmax_step(carry, s, v_ref[pl.ds(start, tq), :])

    carry = lax.fori_loop(0, qi, body, carry)
    start = pl.multiple_of(qi * tq, tq)
    s = _dot_nt(q, k_ref[pl.ds(start, tq), :]) * scale
    rq = lax.broadcasted_iota(jnp.int32, s.shape, 0) // CHUNK
    ck = lax.broadcasted_iota(jnp.int32, s.shape, 1) // CHUNK
    carry = _softmax_step(carry, jnp.where(rq >= ck, s, -jnp.inf), v_ref[pl.ds(start, tq), :])
    _gate_out(carry, za_ref, o_ref)


def _attn_prompt(q, k, v, k_small, v_small, meta_blk, p_bf, za_off, batch, seq, tq, n_meta):
    n_heads = q.shape[1] // QK_PAD
    nqb = seq // tq
    scale = float((QK_NOPE + QK_ROPE) ** -0.5)
    za0 = za_off // V_HEAD
    return pl.pallas_call(
        functools.partial(_attn_prompt_kernel, tq=tq, scale=scale, n_meta=n_meta),
        grid=(batch, n_heads, nqb),
        in_specs=[
            pl.BlockSpec((tq, QK_PAD), lambda b, h, i: (b * nqb + i, h)),
            pl.BlockSpec((seq, QK_PAD), lambda b, h, i: (b, h)),
            pl.BlockSpec((seq, V_HEAD), lambda b, h, i: (b, h)),
            pl.BlockSpec((LANE, QK_PAD), lambda b, h, i: (meta_blk, h)),
            pl.BlockSpec((LANE, V_HEAD), lambda b, h, i: (meta_blk, h)),
            pl.BlockSpec((tq, V_HEAD), lambda b, h, i: (b * nqb + i, za0 + h)),
        ],
        out_specs=pl.BlockSpec((tq, V_HEAD), lambda b, h, i: (b * nqb + i, h)),
        out_shape=jax.ShapeDtypeStruct((batch * seq, n_heads * V_HEAD), BF16),
        compiler_params=_cparams(("parallel", "parallel", "arbitrary")),
        name="attn_prompt",
    )(q, k, v, k_small, v_small, p_bf)


def _attn_sample_kernel(q_ref, kc_ref, vc_ref, ks_ref, vs_ref, za_ref, o_ref, *, lq, scale, n_meta, meta_row):
    b = pl.program_id(0)
    q = q_ref[...]
    carry = _softmax_first(_dot_nt(q, kc_ref[...]) * scale, vc_ref[...])
    s = _dot_nt(q, ks_ref[...]) * scale
    col = lax.broadcasted_iota(jnp.int32, s.shape, 1)
    own = (col >= b * lq) & (col < (b + 1) * lq)
    meta = (col >= meta_row) & (col < meta_row + n_meta)
    carry = _softmax_step(carry, jnp.where(own | meta, s, -jnp.inf), vs_ref[...])
    _gate_out(carry, za_ref, o_ref)


def _attn_sample(q, kc, vc, ks, vs, p_bf, za_off, batch, lq, past, n_meta):
    n_heads = q.shape[1] // QK_PAD
    rows = ks.shape[0]
    scale = float((QK_NOPE + QK_ROPE) ** -0.5)
    za0 = za_off // V_HEAD
    return pl.pallas_call(
        functools.partial(_attn_sample_kernel, lq=lq, scale=scale, n_meta=n_meta, meta_row=batch * lq),
        grid=(batch, n_heads),
        in_specs=[
            pl.BlockSpec((lq, QK_PAD), lambda b, h: (b, h)),
            pl.BlockSpec((past, QK_PAD), lambda b, h: (b, h)),
            pl.BlockSpec((past, V_HEAD), lambda b, h: (b, h)),
            pl.BlockSpec((rows, QK_PAD), lambda b, h: (0, h)),
            pl.BlockSpec((rows, V_HEAD), lambda b, h: (0, h)),
            pl.BlockSpec((lq, V_HEAD), lambda b, h: (b, za0 + h)),
        ],
        out_specs=pl.BlockSpec((lq, V_HEAD), lambda b, h: (b, h)),
        out_shape=jax.ShapeDtypeStruct((batch * lq, n_heads * V_HEAD), BF16),
        compiler_params=_cparams(("parallel", "arbitrary")),
        name="attn_sample",
    )(q, kc, vc, ks, vs, p_bf)


def _log_sigmoid(x):
    return jnp.minimum(x, 0.0) - jnp.log1p(jnp.exp(-jnp.abs(x)))


def _pad_rows(a, rows):
    if a.shape[0] == rows:
        return a
    return jnp.concatenate([a, jnp.zeros((rows - a.shape[0], a.shape[1]), a.dtype)], axis=0)


def _mlstm_kernel(q_ref, k_ref, v_ref, mo_ref, zb_ref, g_ref, gb_ref, gain_ref, c0_ref, n0_ref, m0_ref,
                  h_ref, c_ref, n_ref, m_ref, *, n_heads, lc, lp, dk, dv):
    @pl.when(pl.program_id(1) == 0)
    def _():
        c_ref[...] = c0_ref[...]
        n_ref[...] = n0_ref[...]
        m_ref[...] = m0_ref[...]

    hi = lax.Precision.HIGHEST
    gates = _pad_rows(g_ref[...] + gb_ref[...], lp)
    gates_t = gates.T
    t_col = lax.broadcasted_iota(jnp.int32, (lp, 1), 0)
    t_row = lax.broadcasted_iota(jnp.int32, (1, lp), 1)
    r_idx = lax.broadcasted_iota(jnp.int32, (lp, lp), 0)
    c_idx = lax.broadcasted_iota(jnp.int32, (lp, lp), 1)
    causal = r_idx >= c_idx
    lower = causal.astype(F32)
    upper = (c_idx >= r_idx).astype(F32)

    for h in range(n_heads):
        li = QK_ROPE + h
        lf_lane = QK_ROPE + n_heads + h
        ig_col = jnp.where(t_col < lc, gates[:, li:li + 1], NEG_BIG)
        ig_row = jnp.where(t_row < lc, gates_t[li:li + 1, :], NEG_BIG)
        lf_col = jnp.where(t_col < lc, _log_sigmoid(gates[:, lf_lane:lf_lane + 1]), 0.0)
        lf_row = jnp.where(t_row < lc, _log_sigmoid(gates_t[lf_lane:lf_lane + 1, :]), 0.0)
        b_col = jnp.dot(lower, jnp.broadcast_to(lf_col, (lp, LANE)), precision=hi,
                        preferred_element_type=F32)[:, 0:1]
        b_row = jnp.dot(jnp.broadcast_to(lf_row, (8, lp)), upper, precision=hi,
                        preferred_element_type=F32)[0:1, :]
        m0 = m_ref[0, 0:1, h:h + 1]
        d = jnp.where(causal, b_col - b_row + ig_row, -jnp.inf)
        a_col = b_col + m0
        m = jnp.maximum(a_col, jnp.max(d, axis=-1, keepdims=True))
        w_inter = jnp.exp(a_col - m)

        q = _pad_rows(q_ref[:, h * dk:(h + 1) * dk], lp)
        k = _pad_rows(k_ref[:, h * dk:(h + 1) * dk], lp) * (dk ** -0.5)
        v = _pad_rows(v_ref[:, h * dv:(h + 1) * dv], lp)
        qk = _dot_nt(q, k) * jnp.exp(d - m)
        c_old = c_ref[0, h]
        n_old = n_ref[0, h:h + 1, :]
        num = w_inter * _dot(q, c_old.astype(BF16)) + _dot(qk.astype(BF16), v)
        den = (w_inter * jnp.sum(q.astype(F32) * n_old, axis=-1, keepdims=True)
               + jnp.sum(qk, axis=-1, keepdims=True))
        hid = num / jnp.maximum(jnp.abs(den), jnp.exp(-m))

        b_last = b_col[lp - 1:lp, :]
        g_row = b_last - b_row + ig_row
        g_col = b_last - b_col + ig_col
        m_new = jnp.maximum(b_last + m0, jnp.max(g_row, axis=-1, keepdims=True))
        decay = jnp.exp(b_last + m0 - m_new)
        kw = k.astype(F32) * jnp.exp(g_col - m_new)
        c_ref[0, h] = decay * c_old + _dot(kw.T.astype(BF16), v)
        n_ref[0, h:h + 1, :] = decay * n_old + jnp.sum(kw, axis=0, keepdims=True)
        m_ref[0, 0:1, h:h + 1] = m_new

        hid = hid[:lc]
        mo = mo_ref[:, h * dv:(h + 1) * dv].astype(F32)
        zb = zb_ref[:, h * dv:(h + 1) * dv].astype(F32)
        out = jax.nn.sigmoid(mo) * (_rms(hid, dv) * gain_ref[:, h * dv:(h + 1) * dv])
        h_ref[:, h * dv:(h + 1) * dv] = (out * (zb * jax.nn.sigmoid(zb))).astype(h_ref.dtype)


def _mlstm(p_bf, p_f32, offs, gate_bias, gain, c0, n0, m0, batch, seq, lc, row0, share_state, n_heads, dk, dv):
    lp = -(-lc // LANE) * LANE
    nch = seq // lc
    blk0 = row0 // lc
    wqk = n_heads * dk
    wv = n_heads * dv
    gate_blk = p_f32.shape[1] // LANE - 1
    rows = lambda col: (lambda b, c: (blk0 + b * nch + c, col))
    state = (lambda b, c: (0, 0, 0, 0)) if share_state else (lambda b, c: (b, 0, 0, 0))
    state3 = (lambda b, c: (0, 0, 0)) if share_state else (lambda b, c: (b, 0, 0))
    out_rows = batch * seq
    return pl.pallas_call(
        functools.partial(_mlstm_kernel, n_heads=n_heads, lc=lc, lp=lp, dk=dk, dv=dv),
        grid=(batch, nch),
        in_specs=[
            pl.BlockSpec((lc, wqk), rows(offs["mq"] // wqk)),
            pl.BlockSpec((lc, wqk), rows(offs["mk"] // wqk)),
            pl.BlockSpec((lc, wv), rows(offs["mv"] // wv)),
            pl.BlockSpec((lc, wv), rows(offs["mo"] // wv)),
            pl.BlockSpec((lc, wv), rows(offs["zb"] // wv)),
            pl.BlockSpec((lc, LANE), rows(gate_blk)),
            pl.BlockSpec((1, LANE), lambda b, c: (0, 0)),
            pl.BlockSpec((1, wv), lambda b, c: (0, 0)),
            pl.BlockSpec((1, n_heads, dk, dv), state),
            pl.BlockSpec((1, n_heads, dk), state3),
            pl.BlockSpec((1, 1, n_heads), state3),
        ],
        out_specs=[
            pl.BlockSpec((lc, wv), lambda b, c: (b * nch + c, 0)),
            pl.BlockSpec((1, n_heads, dk, dv), lambda b, c: (b, 0, 0, 0)),
            pl.BlockSpec((1, n_heads, dk), lambda b, c: (b, 0, 0)),
            pl.BlockSpec((1, 1, n_heads), lambda b, c: (b, 0, 0)),
        ],
        out_shape=[
            jax.ShapeDtypeStruct((out_rows, wv), BF16),
            jax.ShapeDtypeStruct((batch, n_heads, dk, dv), F32),
            jax.ShapeDtypeStruct((batch, n_heads, dk), F32),
            jax.ShapeDtypeStruct((batch, 1, n_heads), F32),
        ],
        compiler_params=_cparams(("parallel", "arbitrary")),
        name="mlstm",
    )(p_bf, p_bf, p_bf, p_bf, p_bf, p_f32, gate_bias, gain, c0, n0, m0)


def _out_proj_kernel(a_ref, m_ref, wa_ref, wm_ref, x_ref, o_ref):
    o_ref[...] = x_ref[...] + _dot(a_ref[...], wa_ref[...]) + _dot(m_ref[...], wm_ref[...])


def _out_proj(a, ml, w_out, x, tm, tn):
    rows, wa = a.shape
    wm = ml.shape[1]
    d = w_out.shape[1]
    assert wa == wm
    return pl.pallas_call(
        _out_proj_kernel,
        grid=(rows // tm, d // tn),
        in_specs=[
            pl.BlockSpec((tm, wa), lambda i, j: (i, 0)),
            pl.BlockSpec((tm, wm), lambda i, j: (i, 0)),
            pl.BlockSpec((wa, tn), lambda i, j: (0, j)),
            pl.BlockSpec((wm, tn), lambda i, j: (1, j)),
            pl.BlockSpec((tm, tn), lambda i, j: (i, j)),
        ],
        out_specs=pl.BlockSpec((tm, tn), lambda i, j: (i, j)),
        out_shape=jax.ShapeDtypeStruct((rows, d), F32),
        compiler_params=_cparams(("parallel", "arbitrary")),
        name="out_proj",
    )(a, ml, w_out, w_out, x)


def _rope_tables(pos):
    inv_freq = ROPE_THETA ** (-jnp.arange(HALF_ROPE, dtype=F32) / HALF_ROPE)
    ang = pos.astype(F32)[:, None] * inv_freq[None, :]
    cos, sin = jnp.cos(ang), jnp.sin(ang)
    zero = jnp.zeros((pos.shape[0], LANE - QK_ROPE), F32)
    return jnp.concatenate([cos, cos, zero], axis=1), jnp.concatenate([-sin, sin, zero], axis=1)


def _pad_lanes(vec):
    return jnp.pad(vec.astype(F32), (0, LANE - vec.shape[0]))[None, :]


def _row_tile(rows, target):
    t = min(rows, target)
    while rows % t:
        t //= 2
    return t


def kernel(x_prompt, x_sample, cache_ckv, cache_kpe, state_C, state_n, state_m, meta_tokens, norm_gain, w_in,
           b_igate, b_fgate, q_a_norm, w_q_up, q_nope_norm, q_rope_norm, kv_a_norm, k_rope_norm, w_kv_up,
           k_nope_norm, ml_out_norm, w_out):
    batch, seq, d = x_prompt.shape
    dbatch, dseq, _ = x_sample.shape
    past = cache_ckv.shape[1]
    n_meta = meta_tokens.shape[0]
    q_lora = q_a_norm.shape[0]
    kv_lora = kv_a_norm.shape[0]
    ml_heads, dv = ml_out_norm.shape
    dk = state_n.shape[-1]
    mla_heads = w_kv_up.shape[1] // (QK_NOPE + V_HEAD)
    mla_w = mla_heads * V_HEAD
    ml_w = ml_heads * dv
    mqk_w = ml_heads * dk
    assert q_lora % COL_TILE == 0 and kv_lora % LANE == 0 and 2 * ml_heads <= LANE - QK_ROPE
    assert mla_w == ml_w and w_out.shape[0] == mla_w + ml_w

    o_cq = 0
    o_ckv = o_cq + q_lora
    o_kpe = o_ckv + kv_lora
    o_mq = o_kpe + QK_ROPE
    o_mk = o_mq + mqk_w
    o_mv = o_mk + mqk_w
    o_mo = o_mv + ml_w
    o_mi = o_mo + ml_w
    o_mf = o_mi + ml_heads
    o_za = o_mf + ml_heads
    o_zb = o_za + mla_w
    sl = lambda a, o, n: a[:, o:o + n]

    w_main = jnp.concatenate([sl(w_in, o_mq, mqk_w), sl(w_in, o_mk, mqk_w), sl(w_in, o_mv, ml_w),
                              sl(w_in, o_mo, ml_w), sl(w_in, o_za, mla_w), sl(w_in, o_zb, ml_w),
                              sl(w_in, o_cq, q_lora)], axis=1).astype(BF16)
    offs = {"mq": 0, "mk": mqk_w, "mv": 2 * mqk_w, "mo": 2 * mqk_w + ml_w, "za": 2 * mqk_w + 2 * ml_w,
            "zb": 2 * mqk_w + 2 * ml_w + mla_w, "cq": 2 * mqk_w + 3 * ml_w + mla_w}
    gate_pad = LANE - QK_ROPE - 2 * ml_heads
    w_small = jnp.concatenate([sl(w_in, o_ckv, kv_lora), sl(w_in, o_kpe, QK_ROPE), sl(w_in, o_mi, ml_heads),
                               sl(w_in, o_mf, ml_heads), jnp.zeros((d, gate_pad), F32)], axis=1).astype(BF16)
    gate_bias = jnp.concatenate([jnp.zeros((QK_ROPE,), F32), b_igate.astype(F32), b_fgate.astype(F32),
                                 jnp.zeros((gate_pad,), F32)])[None, :]
    wq3 = w_q_up.reshape(q_lora, mla_heads, QK_NOPE + QK_ROPE)
    wq = jnp.pad(wq3, ((0, 0), (0, 0), (0, QK_PAD - QK_NOPE - QK_ROPE))).reshape(q_lora, mla_heads * QK_PAD)
    wq = wq.astype(BF16)
    wkv3 = w_kv_up.reshape(kv_lora, mla_heads, QK_NOPE + V_HEAD)
    wk = wkv3[:, :, :QK_NOPE].reshape(kv_lora, mla_heads * QK_NOPE).astype(BF16)
    wv = wkv3[:, :, QK_NOPE:].reshape(kv_lora, mla_heads * V_HEAD).astype(BF16)
    w_out_bf = w_out.astype(BF16)
    gain_row = norm_gain.astype(F32)[None, :]
    qan = q_a_norm.astype(F32)[None, :]
    kvg = kv_a_norm.astype(F32)[None, :]
    qng = q_nope_norm.astype(F32)[None, :]
    kng = k_nope_norm.astype(F32)[None, :]
    qrg = _pad_lanes(q_rope_norm)
    krg = _pad_lanes(k_rope_norm)
    ml_gain = ml_out_norm.astype(F32).reshape(1, ml_w)

    n_p = batch * seq
    n_s = dbatch * dseq
    assert n_s % LANE == 0 and n_meta <= LANE
    small_rows = n_s + LANE
    xp = x_prompt.reshape(n_p, d)
    xs = jnp.concatenate([x_sample.reshape(n_s, d), meta_tokens.astype(x_sample.dtype),
                          jnp.zeros((small_rows - n_s - n_meta, d), x_sample.dtype)], axis=0)
    meta_blk = n_s // LANE

    pos_p = jnp.tile(n_meta + jnp.arange(seq), batch)
    pos_s = jnp.concatenate([jnp.tile(n_meta + past + jnp.arange(dseq), dbatch), jnp.arange(n_meta),
                             jnp.zeros((small_rows - n_s - n_meta,), jnp.int32)])
    cos_p, sin_p = _rope_tables(pos_p)
    cos_s, sin_s = _rope_tables(pos_s)

    tm_p = _row_tile(n_p, 512)
    pbf_p, pf_p = _proj(xp, gain_row, w_main, w_small, tm_p)
    pbf_s, pf_s = _proj(xs, gain_row, w_main, w_small, small_rows)

    hps = max(1, mla_heads // 2)
    q_p, ckv_p, kpe_p, kpep_p = _latent_q(pbf_p, pf_p, offs["cq"], q_lora, kv_lora, cos_p, sin_p, qan, wq,
                                          qng, qrg, kvg, krg, tm_p, hps)
    q_s, ckv_s, kpe_s, kpep_s = _latent_q(pbf_s, pf_s, offs["cq"], q_lora, kv_lora, cos_s, sin_s, qan, wq,
                                          qng, qrg, kvg, krg, small_rows, hps)

    k_p, v_p = _kv_expand(ckv_p, kpep_p, wk, wv, kng, tm_p)
    k_s, v_s = _kv_expand(ckv_s, kpep_s, wk, wv, kng, small_rows)
    cache_rows = dbatch * past
    cache_kpep = jnp.pad(cache_kpe.reshape(cache_rows, QK_ROPE), ((0, 0), (0, LANE - QK_ROPE))).astype(BF16)
    k_c, v_c = _kv_expand(cache_ckv.reshape(cache_rows, kv_lora).astype(F32), cache_kpep, wk, wv, kng,
                          _row_tile(cache_rows, 512))

    tq = _row_tile(seq, 256)
    attn_p = _attn_prompt(q_p, k_p, v_p, k_s, v_s, meta_blk, pbf_p, offs["za"], batch, seq, tq, n_meta)
    attn_s = _attn_sample(q_s, k_c, v_c, k_s, v_s, pbf_s, offs["za"], dbatch, dseq, past, n_meta)

    zc = jnp.zeros((1, ml_heads, dk, dv), F32)
    zn = jnp.zeros((1, ml_heads, dk), F32)
    zm = jnp.zeros((1, 1, ml_heads), F32)
    _, c_m, n_m, m_m = _mlstm(pbf_s, pf_s, offs, gate_bias, ml_gain, zc, zn, zm, 1, n_meta, n_meta, n_s, True,
                              ml_heads, dk, dv)
    lc_p = _row_tile(seq, 128)
    ml_p, c_p, n_pr, m_p = _mlstm(pbf_p, pf_p, offs, gate_bias, ml_gain, c_m, n_m, m_m, batch, seq, lc_p, 0,
                                  True, ml_heads, dk, dv)
    ml_s, c_s, n_sm, m_s = _mlstm(pbf_s, pf_s, offs, gate_bias, ml_gain, state_C.astype(F32),
                                  state_n.astype(F32), state_m.astype(F32).reshape(dbatch, 1, ml_heads),
                                  dbatch, dseq, dseq, 0, False, ml_heads, dk, dv)

    y_p = _out_proj(attn_p, ml_p, w_out_bf, xp, _row_tile(n_p, 512), _row_tile(d, 1024))
    y_s = _out_proj(attn_s, ml_s, w_out_bf, xs, n_s, _row_tile(d, 1024))

    meta_ckv = ckv_s[n_s:n_s + n_meta]
    meta_kpe = kpe_s[n_s:n_s + n_meta]
    ckv_prompt = jnp.concatenate([jnp.broadcast_to(meta_ckv[None], (batch, n_meta, kv_lora)),
                                  ckv_p.reshape(batch, seq, kv_lora)], axis=1)
    kpe_prompt = jnp.concatenate([jnp.broadcast_to(meta_kpe[None], (batch, n_meta, QK_ROPE)),
                                  kpe_p.reshape(batch, seq, QK_ROPE)], axis=1)
    return (y_p.reshape(batch, seq, d), y_s.reshape(dbatch, dseq, d), ckv_prompt, kpe_prompt,
            c_p, n_pr, m_p.reshape(batch, ml_heads),
            ckv_s[:n_s].reshape(dbatch, dseq, kv_lora), kpe_s[:n_s].reshape(dbatch, dseq, QK_ROPE),
            c_s, n_sm, m_s.reshape(dbatch, ml_heads))
```

```python
import functools

import jax
import jax.numpy as jnp
from jax import lax
from jax.experimental import pallas as pl
from jax.experimental.pallas import tpu as pltpu

CHUNK = 64
EPS = 1e-6
ROPE_THETA = 10000.0
V_HEAD = 128
QK_NOPE = 128
QK_ROPE = 64
HALF_ROPE = QK_ROPE // 2
LANE = 128
QK_PAD = QK_NOPE + LANE
COL_TILE = 512
NEG_BIG = -1e30
ATTN_TQ = 512
VMEM_LIMIT = 56 * 1024 * 1024

F32 = jnp.float32
BF16 = jnp.bfloat16


def _cparams(sem):
    return pltpu.CompilerParams(dimension_semantics=sem, vmem_limit_bytes=VMEM_LIMIT)


def _dot(a, b):
    return jnp.dot(a, b, preferred_element_type=F32)


def _dot_nt(a, b):
    return lax.dot_general(a, b, (((1,), (1,)), ((), ())), preferred_element_type=F32)


def _rms(x, n):
    return x * lax.rsqrt(jnp.sum(x * x, axis=-1, keepdims=True) * (1.0 / n) + EPS)


def _rope_pad(x, cos, sin_signed):
    lane = lax.broadcasted_iota(jnp.int32, x.shape, 1)
    rot = jnp.where(lane < HALF_ROPE, pltpu.roll(x, LANE - HALF_ROPE, 1), pltpu.roll(x, HALF_ROPE, 1))
    return x * cos + rot * sin_signed


def _regroup_kernel(w_ref, o_ref, *, segments, zero_ranges):
    for src, n, dst in segments:
        o_ref[:, dst:dst + n] = w_ref[:, src:src + n].astype(o_ref.dtype)
    for dst, n in zero_ranges:
        o_ref[:, dst:dst + n] = jnp.zeros((o_ref.shape[0], n), o_ref.dtype)


def _regroup(w, segments, out_cols, tr):
    rows, cols = w.shape
    covered = sorted((dst, n) for _, n, dst in segments)
    zero_ranges, pos = [], 0
    for dst, n in covered + [(out_cols, 0)]:
        if dst > pos:
            zero_ranges.append((pos, dst - pos))
        pos = dst + n
    return pl.pallas_call(
        functools.partial(_regroup_kernel, segments=tuple(segments), zero_ranges=tuple(zero_ranges)),
        grid=(rows // tr,),
        in_specs=[pl.BlockSpec((tr, cols), lambda i: (i, 0))],
        out_specs=pl.BlockSpec((tr, out_cols), lambda i: (i, 0)),
        out_shape=jax.ShapeDtypeStruct((rows, out_cols), BF16),
        compiler_params=_cparams(("parallel",)),
        name="regroup",
    )(w)


def _proj_kernel(x_ref, g_ref, w_ref, ws_ref, o_ref, os_ref, hn_ref):
    @pl.when(pl.program_id(1) == 0)
    def _():
        x = x_ref[...]
        hn = (_rms(x, x.shape[-1]) * g_ref[...]).astype(BF16)
        hn_ref[...] = hn
        os_ref[...] = _dot(hn, ws_ref[...])

    o_ref[...] = _dot(hn_ref[...], w_ref[...]).astype(o_ref.dtype)


def _proj(x, gain, w_main, w_small, tm):
    m, d = x.shape
    n = w_main.shape[1]
    ns = w_small.shape[1]
    return pl.pallas_call(
        _proj_kernel,
        grid=(m // tm, n // COL_TILE),
        in_specs=[
            pl.BlockSpec((tm, d), lambda i, j: (i, 0)),
            pl.BlockSpec((1, d), lambda i, j: (0, 0)),
            pl.BlockSpec((d, COL_TILE), lambda i, j: (0, j)),
            pl.BlockSpec((d, ns), lambda i, j: (0, 0)),
        ],
        out_specs=[
            pl.BlockSpec((tm, COL_TILE), lambda i, j: (i, j)),
            pl.BlockSpec((tm, ns), lambda i, j: (i, 0)),
        ],
        out_shape=[jax.ShapeDtypeStruct((m, n), BF16), jax.ShapeDtypeStruct((m, ns), F32)],
        scratch_shapes=[pltpu.VMEM((tm, d), BF16)],
        compiler_params=_cparams(("parallel", "arbitrary")),
        name="proj",
    )(x, gain, w_main, w_small)


def _latent_q_kernel(*refs, n_cq, kv_lora, heads_per_step):
    cq_refs = refs[:n_cq]
    (pf_ref, cos_ref, sin_ref, qan_ref, wq_ref, qng_ref, qrg_ref, kvg_ref, krg_ref,
     q_ref, ckv_ref, kpe_ref, kpep_ref, cqn_ref) = refs[n_cq:]
    cos = cos_ref[...]
    sin = sin_ref[...]

    @pl.when(pl.program_id(1) == 0)
    def _():
        cq = jnp.concatenate([r[...].astype(F32) for r in cq_refs], axis=1)
        cqn_ref[...] = (_rms(cq, cq.shape[-1]) * qan_ref[...]).astype(BF16)
        pf = pf_ref[...]
        ckv_raw = pf[:, :kv_lora]
        ckv_ref[...] = _rms(ckv_raw, kv_lora) * kvg_ref[...]
        tail = pf[:, kv_lora:kv_lora + LANE]
        lane = lax.broadcasted_iota(jnp.int32, tail.shape, 1)
        kraw = jnp.where(lane < QK_ROPE, tail, 0.0)
        kpe = _rope_pad(_rms(kraw, QK_ROPE) * krg_ref[...], cos, sin)
        kpe_ref[...] = kpe[:, :QK_ROPE]
        kpep_ref[...] = kpe.astype(BF16)

    qf = _dot(cqn_ref[...], wq_ref[...])
    for h in range(heads_per_step):
        base = h * QK_PAD
        nope = qf[:, base:base + QK_NOPE]
        pe = qf[:, base + QK_NOPE:base + QK_PAD]
        q_ref[:, base:base + QK_NOPE] = (_rms(nope, QK_NOPE) * qng_ref[...]).astype(BF16)
        q_ref[:, base + QK_NOPE:base + QK_PAD] = _rope_pad(_rms(pe, QK_ROPE) * qrg_ref[...], cos, sin).astype(BF16)


def _latent_q(p_bf, p_f32, cq_off, q_lora, kv_lora, cos, sin, qan, wq, qng, qrg, kvg, krg, tm, heads_per_step):
    m = p_bf.shape[0]
    n_heads = wq.shape[1] // QK_PAD
    n_cq = q_lora // COL_TILE
    cq0 = cq_off // COL_TILE
    ns = p_f32.shape[1]
    wstep = heads_per_step * QK_PAD
    row = lambda i, j: (i, 0)
    const = lambda i, j: (0, 0)
    in_specs = [pl.BlockSpec((tm, COL_TILE), functools.partial(lambda i, j, c: (i, c), c=cq0 + c))
                for c in range(n_cq)]
    in_specs += [
        pl.BlockSpec((tm, ns), row),
        pl.BlockSpec((tm, LANE), row),
        pl.BlockSpec((tm, LANE), row),
        pl.BlockSpec((1, q_lora), const),
        pl.BlockSpec((q_lora, wstep), lambda i, j: (0, j)),
        pl.BlockSpec((1, LANE), const),
        pl.BlockSpec((1, LANE), const),
        pl.BlockSpec((1, kv_lora), const),
        pl.BlockSpec((1, LANE), const),
    ]
    return pl.pallas_call(
        functools.partial(_latent_q_kernel, n_cq=n_cq, kv_lora=kv_lora, heads_per_step=heads_per_step),
        grid=(m // tm, n_heads // heads_per_step),
        in_specs=in_specs,
        out_specs=[
            pl.BlockSpec((tm, wstep), lambda i, j: (i, j)),
            pl.BlockSpec((tm, kv_lora), row),
            pl.BlockSpec((tm, QK_ROPE), row),
            pl.BlockSpec((tm, LANE), row),
        ],
        out_shape=[
            jax.ShapeDtypeStruct((m, n_heads * QK_PAD), BF16),
            jax.ShapeDtypeStruct((m, kv_lora), F32),
            jax.ShapeDtypeStruct((m, QK_ROPE), F32),
            jax.ShapeDtypeStruct((m, LANE), BF16),
        ],
        scratch_shapes=[pltpu.VMEM((tm, q_lora), BF16)],
        compiler_params=_cparams(("parallel", "arbitrary")),
        name="latent_q",
    )(*([p_bf] * n_cq), p_f32, cos, sin, qan, wq, qng, qrg, kvg, krg)


def _kv_expand_kernel(ckv_ref, kpep_ref, wk_ref, wv_ref, kng_ref, k_ref, v_ref, *, n_heads):
    c = ckv_ref[...].astype(BF16)
    kf = _dot(c, wk_ref[...])
    v_ref[...] = _dot(c, wv_ref[...]).astype(BF16)
    kpep = kpep_ref[...]
    for h in range(n_heads):
        kn = _rms(kf[:, h * QK_NOPE:(h + 1) * QK_NOPE], QK_NOPE) * kng_ref[...]
        k_ref[:, h * QK_PAD:h * QK_PAD + QK_NOPE] = kn.astype(BF16)
        k_ref[:, h * QK_PAD + QK_NOPE:(h + 1) * QK_PAD] = kpep


def _kv_expand(ckv, kpep, wk, wv, kng, tm):
    r, kv_lora = ckv.shape
    n_heads = wk.shape[1] // QK_NOPE
    row = lambda i: (i, 0)
    const = lambda i: (0, 0)
    return pl.pallas_call(
        functools.partial(_kv_expand_kernel, n_heads=n_heads),
        grid=(r // tm,),
        in_specs=[
            pl.BlockSpec((tm, kv_lora), row),
            pl.BlockSpec((tm, LANE), row),
            pl.BlockSpec((kv_lora, n_heads * QK_NOPE), const),
            pl.BlockSpec((kv_lora, n_heads * V_HEAD), const),
            pl.BlockSpec((1, LANE), const),
        ],
        out_specs=[
            pl.BlockSpec((tm, n_heads * QK_PAD), row),
            pl.BlockSpec((tm, n_heads * V_HEAD), row),
        ],
        out_shape=[
            jax.ShapeDtypeStruct((r, n_heads * QK_PAD), BF16),
            jax.ShapeDtypeStruct((r, n_heads * V_HEAD), BF16),
        ],
        compiler_params=_cparams(("parallel",)),
        name="kv_expand",
    )(ckv, kpep, wk, wv, kng)


def _lane_chunk_sum(p):
    out = p[:, :LANE]
    for c in range(1, p.shape[1] // LANE):
        out = out + p[:, c * LANE:(c + 1) * LANE]
    return out


def _gated(lvec, acc, za):
    za = za.astype(F32)
    return acc / jnp.sum(lvec, axis=-1, keepdims=True) * (za * jax.nn.sigmoid(za))


def _softmax_scale():
    return float((QK_NOPE + QK_ROPE) ** -0.5 * 1.4426950408889634)


def _attn_prompt_block(n_full, q_ref, k_ref, v_ref, km_ref, vm_ref, za_ref, o_ref, s_ref, *, tq, c, n_meta):
    td = min(tq, 256)
    q = q_ref[...]
    pieces = [(i * tq, tq, 0, False) for i in range(n_full)]
    pieces += [(n_full * tq + j * td, td, j * td, True) for j in range(tq // td)]

    def upd(full, r0, fn):
        return fn(full) if r0 == 0 else jnp.concatenate([full[:r0], fn(full[r0:])], axis=0)

    col = lax.broadcasted_iota(jnp.int32, (tq, LANE), 1)
    t = jnp.where(col < n_meta, _dot_nt(q, km_ref[...]) * c, -jnp.inf)
    s_ref[:, :LANE] = t
    mx = t
    off = LANE
    for start, rows, r0, masked in pieces:
        t = _dot_nt(q[r0:], k_ref[start:start + rows, :]) * c
        if masked:
            rq = lax.broadcasted_iota(jnp.int32, t.shape, 0) // CHUNK
            ck = lax.broadcasted_iota(jnp.int32, t.shape, 1) // CHUNK
            t = jnp.where(rq >= ck, t, -jnp.inf)
        s_ref[r0:, off:off + rows] = t
        tmax = t[:, :LANE]
        for ch in range(1, rows // LANE):
            tmax = jnp.maximum(tmax, t[:, ch * LANE:(ch + 1) * LANE])
        mx = upd(mx, r0, lambda a: jnp.maximum(a, tmax))
        off += rows
    m = jnp.max(mx, axis=-1, keepdims=True)

    p = jnp.exp2(s_ref[:, :LANE] - m)
    lvec = p
    acc = _dot(p.astype(BF16), vm_ref[...])
    off = LANE
    for start, rows, r0, _ in pieces:
        p = jnp.exp2(s_ref[r0:, off:off + rows] - m[r0:])
        pv = _dot(p.astype(BF16), v_ref[start:start + rows, :])
        psum = _lane_chunk_sum(p)
        lvec = upd(lvec, r0, lambda a: a + psum)
        acc = upd(acc, r0, lambda a: a + pv)
        off += rows
    o_ref[...] = _gated(lvec, acc, za_ref[...]).astype(o_ref.dtype)


def _attn_prompt_kernel(q_ref, k_ref, v_ref, km_ref, vm_ref, za_ref, o_ref, s_ref, *, nqb, **kw):
    qi = pl.program_id(2)
    for n_full in range(nqb):
        pl.when(qi == n_full)(
            functools.partial(_attn_prompt_block, n_full, q_ref, k_ref, v_ref, km_ref, vm_ref, za_ref, o_ref, s_ref,
                              **kw))


def _attn_prompt(q, k, v, k_small, v_small, meta_blk, p_bf, za_off, batch, seq, tq, n_meta):
    n_heads = q.shape[1] // QK_PAD
    nqb = seq // tq
    za0 = za_off // V_HEAD
    assert za_off % V_HEAD == 0 and tq % CHUNK == 0
    return pl.pallas_call(
        functools.partial(_attn_prompt_kernel, nqb=nqb, tq=tq, c=_softmax_scale(), n_meta=n_meta),
        grid=(batch, n_heads, nqb),
        in_specs=[
            pl.BlockSpec((tq, QK_PAD), lambda b, h, i: (b * nqb + i, h)),
            pl.BlockSpec((seq, QK_PAD), lambda b, h, i: (b, h)),
            pl.BlockSpec((seq, V_HEAD), lambda b, h, i: (b, h)),
            pl.BlockSpec((LANE, QK_PAD), lambda b, h, i: (meta_blk, h)),
            pl.BlockSpec((LANE, V_HEAD), lambda b, h, i: (meta_blk, h)),
            pl.BlockSpec((tq, V_HEAD), lambda b, h, i: (b * nqb + i, za0 + h)),
        ],
        out_specs=pl.BlockSpec((tq, V_HEAD), lambda b, h, i: (b * nqb + i, h)),
        out_shape=jax.ShapeDtypeStruct((batch * seq, n_heads * V_HEAD), BF16),
        scratch_shapes=[pltpu.VMEM((tq, LANE + seq), F32)],
        compiler_params=_cparams(("parallel", "parallel", "arbitrary")),
        name="attn_prompt",
    )(q, k, v, k_small, v_small, p_bf)


def _attn_sample_kernel(q_ref, kc_ref, vc_ref, ks_ref, vs_ref, za_ref, o_ref, *, heads, lq, c, n_meta, meta_row):
    b = pl.program_id(0)
    col = lax.broadcasted_iota(jnp.int32, (lq, ks_ref.shape[0]), 1)
    own = (col >= b * lq) & (col < (b + 1) * lq)
    meta = (col >= meta_row) & (col < meta_row + n_meta)
    visible = own | meta
    for h in range(heads):
        qs = slice(h * QK_PAD, (h + 1) * QK_PAD)
        vs = slice(h * V_HEAD, (h + 1) * V_HEAD)
        q = q_ref[:, qs]
        t_cache = _dot_nt(q, kc_ref[:, qs]) * c
        t_new = jnp.where(visible, _dot_nt(q, ks_ref[:, qs]) * c, -jnp.inf)
        m = jnp.maximum(jnp.max(t_cache, axis=-1, keepdims=True), jnp.max(t_new, axis=-1, keepdims=True))
        p_cache = jnp.exp2(t_cache - m)
        p_new = jnp.exp2(t_new - m)
        lvec = _lane_chunk_sum(p_cache) + _lane_chunk_sum(p_new)
        acc = _dot(p_cache.astype(BF16), vc_ref[:, vs]) + _dot(p_new.astype(BF16), vs_ref[:, vs])
        o_ref[:, vs] = _gated(lvec, acc, za_ref[:, vs]).astype(o_ref.dtype)


def _attn_sample(q, kc, vc, ks, vs, p_bf, za_off, batch, lq, past, n_meta, heads):
    n_heads = q.shape[1] // QK_PAD
    rows = ks.shape[0]
    za0 = za_off // (V_HEAD * heads)
    assert za_off % (V_HEAD * heads) == 0 and n_heads % heads == 0
    return pl.pallas_call(
        functools.partial(_attn_sample_kernel, heads=heads, lq=lq, c=_softmax_scale(), n_meta=n_meta,
                          meta_row=batch * lq),
        grid=(batch, n_heads // heads),
        in_specs=[
            pl.BlockSpec((lq, heads * QK_PAD), lambda b, h: (b, h)),
            pl.BlockSpec((past, heads * QK_PAD), lambda b, h: (b, h)),
            pl.BlockSpec((past, heads * V_HEAD), lambda b, h: (b, h)),
            pl.BlockSpec((rows, heads * QK_PAD), lambda b, h: (0, h)),
            pl.BlockSpec((rows, heads * V_HEAD), lambda b, h: (0, h)),
            pl.BlockSpec((lq, heads * V_HEAD), lambda b, h: (b, za0 + h)),
        ],
        out_specs=pl.BlockSpec((lq, heads * V_HEAD), lambda b, h: (b, h)),
        out_shape=jax.ShapeDtypeStruct((batch * lq, n_heads * V_HEAD), BF16),
        compiler_params=_cparams(("parallel", "arbitrary")),
        name="attn_sample",
    )(q, kc, vc, ks, vs, p_bf)


def _log_sigmoid(x):
    return jnp.minimum(x, 0.0) - jnp.log1p(jnp.exp(-jnp.abs(x)))


def _pad_rows(a, rows):
    if a.shape[0] == rows:
        return a
    return jnp.concatenate([a, jnp.zeros((rows - a.shape[0], a.shape[1]), a.dtype)], axis=0)


def _mlstm_kernel(q_ref, k_ref, v_ref, mo_ref, zb_ref, g_ref, gb_ref, gain_ref, c0_ref, n0_ref, m0_ref,
                  h_ref, c_ref, n_ref, m_ref, *, n_heads, lc, lp, dk, dv):
    @pl.when(pl.program_id(1) == 0)
    def _():
        c_ref[...] = c0_ref[...]
        n_ref[...] = n0_ref[...]
        m_ref[...] = m0_ref[...]

    hi = lax.Precision.HIGHEST
    gates = _pad_rows(g_ref[...] + gb_ref[...], lp)
    gates_t = gates.T
    t_col = lax.broadcasted_iota(jnp.int32, (lp, 1), 0)
    t_row = lax.broadcasted_iota(jnp.int32, (1, lp), 1)
    r_idx = lax.broadcasted_iota(jnp.int32, (lp, lp), 0)
    c_idx = lax.broadcasted_iota(jnp.int32, (lp, lp), 1)
    causal = r_idx >= c_idx
    lower = causal.astype(F32)
    upper = (c_idx >= r_idx).astype(F32)

    for h in range(n_heads):
        li = QK_ROPE + h
        lf_lane = QK_ROPE + n_heads + h
        ig_col = jnp.where(t_col < lc, gates[:, li:li + 1], NEG_BIG)
        ig_row = jnp.where(t_row < lc, gates_t[li:li + 1, :], NEG_BIG)
        lf_col = jnp.where(t_col < lc, _log_sigmoid(gates[:, lf_lane:lf_lane + 1]), 0.0)
        lf_row = jnp.where(t_row < lc, _log_sigmoid(gates_t[lf_lane:lf_lane + 1, :]), 0.0)
        b_col = jnp.dot(lower, jnp.broadcast_to(lf_col, (lp, LANE)), precision=hi,
                        preferred_element_type=F32)[:, 0:1]
        b_row = jnp.dot(jnp.broadcast_to(lf_row, (8, lp)), upper, precision=hi,
                        preferred_element_type=F32)[0:1, :]
        m0 = m_ref[0, 0:1, h:h + 1]
        d = jnp.where(causal, b_col - b_row + ig_row, -jnp.inf)
        a_col = b_col + m0
        m = jnp.maximum(a_col, jnp.max(d, axis=-1, keepdims=True))
        w_inter = jnp.exp(a_col - m)

        q = _pad_rows(q_ref[:, h * dk:(h + 1) * dk], lp)
        k = _pad_rows(k_ref[:, h * dk:(h + 1) * dk], lp) * (dk ** -0.5)
        v = _pad_rows(v_ref[:, h * dv:(h + 1) * dv], lp)
        qk = _dot_nt(q, k) * jnp.exp(d - m)
        c_old = c_ref[0, h]
        n_old = n_ref[0, h:h + 1, :]
        num = w_inter * _dot(q, c_old.astype(BF16)) + _dot(qk.astype(BF16), v)
        den = (w_inter * jnp.sum(q.astype(F32) * n_old, axis=-1, keepdims=True)
               + jnp.sum(qk, axis=-1, keepdims=True))
        hid = num / jnp.maximum(jnp.abs(den), jnp.exp(-m))

        b_last = b_col[lp - 1:lp, :]
        g_row = b_last - b_row + ig_row
        g_col = b_last - b_col + ig_col
        m_new = jnp.maximum(b_last + m0, jnp.max(g_row, axis=-1, keepdims=True))
        decay = jnp.exp(b_last + m0 - m_new)
        kw = k.astype(F32) * jnp.exp(g_col - m_new)
        c_ref[0, h] = decay * c_old + _dot(kw.T.astype(BF16), v)
        n_ref[0, h:h + 1, :] = decay * n_old + jnp.sum(kw, axis=0, keepdims=True)
        m_ref[0, 0:1, h:h + 1] = m_new

        hid = hid[:lc]
        mo = mo_ref[:, h * dv:(h + 1) * dv].astype(F32)
        zb = zb_ref[:, h * dv:(h + 1) * dv].astype(F32)
        out = jax.nn.sigmoid(mo) * (_rms(hid, dv) * gain_ref[:, h * dv:(h + 1) * dv])
        h_ref[:, h * dv:(h + 1) * dv] = (out * (zb * jax.nn.sigmoid(zb))).astype(h_ref.dtype)


def _mlstm(p_bf, p_f32, offs, gate_bias, gain, c0, n0, m0, batch, seq, lc, row0, share_state, n_heads, dk, dv):
    lp = -(-lc // LANE) * LANE
    nch = seq // lc
    blk0 = row0 // lc
    wqk = n_heads * dk
    wv = n_heads * dv
    gate_blk = p_f32.shape[1] // LANE - 1
    rows = lambda col: (lambda b, c: (blk0 + b * nch + c, col))
    state = (lambda b, c: (0, 0, 0, 0)) if share_state else (lambda b, c: (b, 0, 0, 0))
    state3 = (lambda b, c: (0, 0, 0)) if share_state else (lambda b, c: (b, 0, 0))
    out_rows = batch * seq
    return pl.pallas_call(
        functools.partial(_mlstm_kernel, n_heads=n_heads, lc=lc, lp=lp, dk=dk, dv=dv),
        grid=(batch, nch),
        in_specs=[
            pl.BlockSpec((lc, wqk), rows(offs["mq"] // wqk)),
            pl.BlockSpec((lc, wqk), rows(offs["mk"] // wqk)),
            pl.BlockSpec((lc, wv), rows(offs["mv"] // wv)),
            pl.BlockSpec((lc, wv), rows(offs["mo"] // wv)),
            pl.BlockSpec((lc, wv), rows(offs["zb"] // wv)),
            pl.BlockSpec((lc, LANE), rows(gate_blk)),
            pl.BlockSpec((1, LANE), lambda b, c: (0, 0)),
            pl.BlockSpec((1, wv), lambda b, c: (0, 0)),
            pl.BlockSpec((1, n_heads, dk, dv), state),
            pl.BlockSpec((1, n_heads, dk), state3),
            pl.BlockSpec((1, 1, n_heads), state3),
        ],
        out_specs=[
            pl.BlockSpec((lc, wv), lambda b, c: (b * nch + c, 0)),
            pl.BlockSpec((1, n_heads, dk, dv), lambda b, c: (b, 0, 0, 0)),
            pl.BlockSpec((1, n_heads, dk), lambda b, c: (b, 0, 0)),
            pl.BlockSpec((1, 1, n_heads), lambda b, c: (b, 0, 0)),
        ],
        out_shape=[
            jax.ShapeDtypeStruct((out_rows, wv), BF16),
            jax.ShapeDtypeStruct((batch, n_heads, dk, dv), F32),
            jax.ShapeDtypeStruct((batch, n_heads, dk), F32),
            jax.ShapeDtypeStruct((batch, 1, n_heads), F32),
        ],
        compiler_params=_cparams(("parallel", "arbitrary")),
        name="mlstm",
    )(p_bf, p_bf, p_bf, p_bf, p_bf, p_f32, gate_bias, gain, c0, n0, m0)


def _out_proj_kernel(a_ref, m_ref, wa_ref, wm_ref, x_ref, o_ref):
    o_ref[...] = x_ref[...] + _dot(a_ref[...], wa_ref[...]) + _dot(m_ref[...], wm_ref[...])


def _out_proj(a, ml, w_out, x, tm, tn):
    rows, wa = a.shape
    wm = ml.shape[1]
    d = w_out.shape[1]
    assert wa == wm
    return pl.pallas_call(
        _out_proj_kernel,
        grid=(rows // tm, d // tn),
        in_specs=[
            pl.BlockSpec((tm, wa), lambda i, j: (i, 0)),
            pl.BlockSpec((tm, wm), lambda i, j: (i, 0)),
            pl.BlockSpec((wa, tn), lambda i, j: (0, j)),
            pl.BlockSpec((wm, tn), lambda i, j: (1, j)),
            pl.BlockSpec((tm, tn), lambda i, j: (i, j)),
        ],
        out_specs=pl.BlockSpec((tm, tn), lambda i, j: (i, j)),
        out_shape=jax.ShapeDtypeStruct((rows, d), F32),
        compiler_params=_cparams(("parallel", "arbitrary")),
        name="out_proj",
    )(a, ml, w_out, w_out, x)


def _rope_tables(pos):
    inv_freq = ROPE_THETA ** (-jnp.arange(HALF_ROPE, dtype=F32) / HALF_ROPE)
    ang = pos.astype(F32)[:, None] * inv_freq[None, :]
    cos, sin = jnp.cos(ang), jnp.sin(ang)
    zero = jnp.zeros((pos.shape[0], LANE - QK_ROPE), F32)
    return jnp.concatenate([cos, cos, zero], axis=1), jnp.concatenate([-sin, sin, zero], axis=1)


def _pad_lanes(vec):
    return jnp.pad(vec.astype(F32), (0, LANE - vec.shape[0]))[None, :]


def _row_tile(rows, target):
    t = min(rows, target)
    while rows % t:
        t //= 2
    return t


def kernel(x_prompt, x_sample, cache_ckv, cache_kpe, state_C, state_n, state_m, meta_tokens, norm_gain, w_in,
           b_igate, b_fgate, q_a_norm, w_q_up, q_nope_norm, q_rope_norm, kv_a_norm, k_rope_norm, w_kv_up,
           k_nope_norm, ml_out_norm, w_out):
    batch, seq, d = x_prompt.shape
    dbatch, dseq, _ = x_sample.shape
    past = cache_ckv.shape[1]
    n_meta = meta_tokens.shape[0]
    q_lora = q_a_norm.shape[0]
    kv_lora = kv_a_norm.shape[0]
    ml_heads, dv = ml_out_norm.shape
    dk = state_n.shape[-1]
    mla_heads = w_kv_up.shape[1] // (QK_NOPE + V_HEAD)
    mla_w = mla_heads * V_HEAD
    ml_w = ml_heads * dv
    mqk_w = ml_heads * dk
    assert q_lora % COL_TILE == 0 and kv_lora % LANE == 0 and 2 * ml_heads <= LANE - QK_ROPE
    assert mla_w == ml_w and w_out.shape[0] == mla_w + ml_w

    o_cq = 0
    o_ckv = o_cq + q_lora
    o_kpe = o_ckv + kv_lora
    o_mq = o_kpe + QK_ROPE
    o_mk = o_mq + mqk_w
    o_mv = o_mk + mqk_w
    o_mo = o_mv + ml_w
    o_mi = o_mo + ml_w
    o_mf = o_mi + ml_heads
    o_za = o_mf + ml_heads
    o_zb = o_za + mla_w
    sl = lambda a, o, n: a[:, o:o + n]

    offs = {"mq": 0, "mk": mqk_w, "mv": 2 * mqk_w, "mo": 2 * mqk_w + ml_w, "za": 2 * mqk_w + 2 * ml_w,
            "zb": 2 * mqk_w + 2 * ml_w + mla_w, "cq": 2 * mqk_w + 3 * ml_w + mla_w}
    w_main = _regroup(w_in, [(o_mq, mqk_w, offs["mq"]), (o_mk, mqk_w, offs["mk"]), (o_mv, ml_w, offs["mv"]),
                             (o_mo, ml_w, offs["mo"]), (o_za, mla_w, offs["za"]), (o_zb, ml_w, offs["zb"]),
                             (o_cq, q_lora, offs["cq"])], offs["cq"] + q_lora, _row_tile(d, 256))
    gate_pad = LANE - QK_ROPE - 2 * ml_heads
    w_small = jnp.concatenate([sl(w_in, o_ckv, kv_lora), sl(w_in, o_kpe, QK_ROPE), sl(w_in, o_mi, ml_heads),
                               sl(w_in, o_mf, ml_heads), jnp.zeros((d, gate_pad), F32)], axis=1).astype(BF16)
    gate_bias = jnp.concatenate([jnp.zeros((QK_ROPE,), F32), b_igate.astype(F32), b_fgate.astype(F32),
                                 jnp.zeros((gate_pad,), F32)])[None, :]
    qk_w = QK_NOPE + QK_ROPE
    kv_w = QK_NOPE + V_HEAD
    wq = _regroup(w_q_up, [(h * qk_w, qk_w, h * QK_PAD) for h in range(mla_heads)], mla_heads * QK_PAD,
                  _row_tile(q_lora, 512))
    wk = _regroup(w_kv_up, [(h * kv_w, QK_NOPE, h * QK_NOPE) for h in range(mla_heads)], mla_heads * QK_NOPE,
                  kv_lora)
    wv = _regroup(w_kv_up, [(h * kv_w + QK_NOPE, V_HEAD, h * V_HEAD) for h in range(mla_heads)],
                  mla_heads * V_HEAD, kv_lora)
    w_out_bf = w_out.astype(BF16)
    gain_row = norm_gain.astype(F32)[None, :]
    qan = q_a_norm.astype(F32)[None, :]
    kvg = kv_a_norm.astype(F32)[None, :]
    qng = q_nope_norm.astype(F32)[None, :]
    kng = k_nope_norm.astype(F32)[None, :]
    qrg = _pad_lanes(q_rope_norm)
    krg = _pad_lanes(k_rope_norm)
    ml_gain = ml_out_norm.astype(F32).reshape(1, ml_w)

    n_p = batch * seq
    n_s = dbatch * dseq
    assert n_s % LANE == 0 and n_meta <= LANE
    small_rows = n_s + LANE
    xp = x_prompt.reshape(n_p, d)
    xs = jnp.concatenate([x_sample.reshape(n_s, d), meta_tokens.astype(x_sample.dtype),
                          jnp.zeros((small_rows - n_s - n_meta, d), x_sample.dtype)], axis=0)
    meta_blk = n_s // LANE

    pos_p = jnp.tile(n_meta + jnp.arange(seq), batch)
    pos_s = jnp.concatenate([jnp.tile(n_meta + past + jnp.arange(dseq), dbatch), jnp.arange(n_meta),
                             jnp.zeros((small_rows - n_s - n_meta,), jnp.int32)])
    cos_p, sin_p = _rope_tables(pos_p)
    cos_s, sin_s = _rope_tables(pos_s)

    tm_p = _row_tile(n_p, 512)
    pbf_p, pf_p = _proj(xp, gain_row, w_main, w_small, tm_p)
    pbf_s, pf_s = _proj(xs, gain_row, w_main, w_small, small_rows)

    hps = max(1, mla_heads // 2)
    q_p, ckv_p, kpe_p, kpep_p = _latent_q(pbf_p, pf_p, offs["cq"], q_lora, kv_lora, cos_p, sin_p, qan, wq,
                                          qng, qrg, kvg, krg, tm_p, hps)
    q_s, ckv_s, kpe_s, kpep_s = _latent_q(pbf_s, pf_s, offs["cq"], q_lora, kv_lora, cos_s, sin_s, qan, wq,
                                          qng, qrg, kvg, krg, small_rows, hps)

    k_p, v_p = _kv_expand(ckv_p, kpep_p, wk, wv, kng, tm_p)
    k_s, v_s = _kv_expand(ckv_s, kpep_s, wk, wv, kng, small_rows)
    cache_rows = dbatch * past
    cache_kpep = jnp.pad(cache_kpe.reshape(cache_rows, QK_ROPE), ((0, 0), (0, LANE - QK_ROPE))).astype(BF16)
    k_c, v_c = _kv_expand(cache_ckv.reshape(cache_rows, kv_lora).astype(F32), cache_kpep, wk, wv, kng,
                          _row_tile(cache_rows, 512))

    tq = _row_tile(seq, ATTN_TQ)
    attn_p = _attn_prompt(q_p, k_p, v_p, k_s, v_s, meta_blk, pbf_p, offs["za"], batch, seq, tq, n_meta)
    attn_s = _attn_sample(q_s, k_c, v_c, k_s, v_s, pbf_s, offs["za"], dbatch, dseq, past, n_meta,
                          min(4, mla_heads))

    zc = jnp.zeros((1, ml_heads, dk, dv), F32)
    zn = jnp.zeros((1, ml_heads, dk), F32)
    zm = jnp.zeros((1, 1, ml_heads), F32)
    _, c_m, n_m, m_m = _mlstm(pbf_s, pf_s, offs, gate_bias, ml_gain, zc, zn, zm, 1, n_meta, n_meta, n_s, True,
                              ml_heads, dk, dv)
    lc_p = _row_tile(seq, 128)
    ml_p, c_p, n_pr, m_p = _mlstm(pbf_p, pf_p, offs, gate_bias, ml_gain, c_m, n_m, m_m, batch, seq, lc_p, 0,
                                  True, ml_heads, dk, dv)
    ml_s, c_s, n_sm, m_s = _mlstm(pbf_s, pf_s, offs, gate_bias, ml_gain, state_C.astype(F32),
                                  state_n.astype(F32), state_m.astype(F32).reshape(dbatch, 1, ml_heads),
                                  dbatch, dseq, dseq, 0, False, ml_heads, dk, dv)

    y_p = _out_proj(attn_p, ml_p, w_out_bf, xp, _row_tile(n_p, 512), _row_tile(d, 1024))
    y_s = _out_proj(attn_s, ml_s, w_out_bf, xs, n_s, _row_tile(d, 1024))

    meta_ckv = ckv_s[n_s:n_s + n_meta]
    meta_kpe = kpe_s[n_s:n_s + n_meta]
    ckv_prompt = jnp.concatenate([jnp.broadcast_to(meta_ckv[None], (batch, n_meta, kv_lora)),
                                  ckv_p.reshape(batch, seq, kv_lora)], axis=1)
    kpe_prompt = jnp.concatenate([jnp.broadcast_to(meta_kpe[None], (batch, n_meta, QK_ROPE)),
                                  kpe_p.reshape(batch, seq, QK_ROPE)], axis=1)
    return (y_p.reshape(batch, seq, d), y_s.reshape(dbatch, dseq, d), ckv_prompt, kpe_prompt,
            c_p, n_pr, m_p.reshape(batch, ml_heads),
            ckv_s[:n_s].reshape(dbatch, dseq, kv_lora), kpe_s[:n_s].reshape(dbatch, dseq, QK_ROPE),
            c_s, n_sm, m_s.reshape(dbatch, ml_heads))
```

```python
import functools

import jax
import jax.numpy as jnp
from jax import lax
from jax.experimental import pallas as pl
from jax.experimental.pallas import tpu as pltpu

CHUNK = 64
EPS = 1e-6
ROPE_THETA = 10000.0
V_HEAD = 128
QK_NOPE = 128
QK_ROPE = 64
HALF_ROPE = QK_ROPE // 2
LANE = 128
QK_PAD = QK_NOPE + LANE
COL_TILE = 512
NEG_BIG = -1e30
ATTN_TQ = 512
VMEM_LIMIT = 56 * 1024 * 1024

F32 = jnp.float32
BF16 = jnp.bfloat16


def _cparams(sem):
    return pltpu.CompilerParams(dimension_semantics=sem, vmem_limit_bytes=VMEM_LIMIT)


def _dot(a, b):
    return jnp.dot(a, b, preferred_element_type=F32)


def _dot_nt(a, b):
    return lax.dot_general(a, b, (((1,), (1,)), ((), ())), preferred_element_type=F32)


def _rms(x, n):
    return x * lax.rsqrt(jnp.sum(x * x, axis=-1, keepdims=True) * (1.0 / n) + EPS)


def _rope_pad(x, cos, sin_signed):
    lane = lax.broadcasted_iota(jnp.int32, x.shape, 1)
    rot = jnp.where(lane < HALF_ROPE, pltpu.roll(x, LANE - HALF_ROPE, 1), pltpu.roll(x, HALF_ROPE, 1))
    return x * cos + rot * sin_signed


def _regroup_kernel(w_ref, o_ref, *, segments, zero_ranges):
    for src, n, dst in segments:
        o_ref[:, dst:dst + n] = w_ref[:, src:src + n].astype(o_ref.dtype)
    for dst, n in zero_ranges:
        o_ref[:, dst:dst + n] = jnp.zeros((o_ref.shape[0], n), o_ref.dtype)


def _regroup(w, segments, out_cols, tr):
    rows, cols = w.shape
    covered = sorted((dst, n) for _, n, dst in segments)
    zero_ranges, pos = [], 0
    for dst, n in covered + [(out_cols, 0)]:
        if dst > pos:
            zero_ranges.append((pos, dst - pos))
        pos = dst + n
    return pl.pallas_call(
        functools.partial(_regroup_kernel, segments=tuple(segments), zero_ranges=tuple(zero_ranges)),
        grid=(rows // tr,),
        in_specs=[pl.BlockSpec((tr, cols), lambda i: (i, 0))],
        out_specs=pl.BlockSpec((tr, out_cols), lambda i: (i, 0)),
        out_shape=jax.ShapeDtypeStruct((rows, out_cols), BF16),
        compiler_params=_cparams(("parallel",)),
        name="regroup",
    )(w)


def _regroup_rows_kernel(w_ref, om_ref, os_ref, *, main_segments, small_segments):
    for src, n, dst in main_segments:
        om_ref[dst:dst + n, :] = w_ref[src:src + n, :].astype(om_ref.dtype)
    parts = [w_ref[src:src + n, :] for src, n in small_segments]
    used = sum(n for _, n in small_segments)
    parts.append(jnp.zeros((os_ref.shape[0] - used, w_ref.shape[1]), w_ref.dtype))
    os_ref[...] = jnp.concatenate(parts, axis=0).astype(os_ref.dtype)


def _regroup_rows(wt, main_segments, main_rows, small_segments, small_rows, tc):
    rows, cols = wt.shape
    return pl.pallas_call(
        functools.partial(_regroup_rows_kernel, main_segments=tuple(main_segments),
                          small_segments=tuple(small_segments)),
        grid=(cols // tc,),
        in_specs=[pl.BlockSpec((rows, tc), lambda i: (0, i))],
        out_specs=[pl.BlockSpec((main_rows, tc), lambda i: (0, i)), pl.BlockSpec((small_rows, tc), lambda i: (0, i))],
        out_shape=[jax.ShapeDtypeStruct((main_rows, cols), BF16), jax.ShapeDtypeStruct((small_rows, cols), BF16)],
        compiler_params=_cparams(("parallel",)),
        name="regroup_rows",
    )(wt)


def _proj_kernel(x_ref, g_ref, w_ref, ws_ref, o_ref, os_ref, hn_ref):
    @pl.when(pl.program_id(1) == 0)
    def _():
        x = x_ref[...]
        hn = (_rms(x, x.shape[-1]) * g_ref[...]).astype(BF16)
        hn_ref[...] = hn
        os_ref[...] = _dot_nt(hn, ws_ref[...])

    o_ref[...] = _dot_nt(hn_ref[...], w_ref[...]).astype(o_ref.dtype)


def _proj(x, gain, w_main, w_small, tm):
    m, d = x.shape
    n = w_main.shape[0]
    ns = w_small.shape[0]
    return pl.pallas_call(
        _proj_kernel,
        grid=(m // tm, n // COL_TILE),
        in_specs=[
            pl.BlockSpec((tm, d), lambda i, j: (i, 0)),
            pl.BlockSpec((1, d), lambda i, j: (0, 0)),
            pl.BlockSpec((COL_TILE, d), lambda i, j: (j, 0)),
            pl.BlockSpec((ns, d), lambda i, j: (0, 0)),
        ],
        out_specs=[
            pl.BlockSpec((tm, COL_TILE), lambda i, j: (i, j)),
            pl.BlockSpec((tm, ns), lambda i, j: (i, 0)),
        ],
        out_shape=[jax.ShapeDtypeStruct((m, n), BF16), jax.ShapeDtypeStruct((m, ns), F32)],
        scratch_shapes=[pltpu.VMEM((tm, d), BF16)],
        compiler_params=_cparams(("parallel", "arbitrary")),
        name="proj",
    )(x, gain, w_main, w_small)


def _latent_q_kernel(*refs, n_cq, kv_lora, heads_per_step):
    cq_refs = refs[:n_cq]
    (pf_ref, cos_ref, sin_ref, qan_ref, wq_ref, qng_ref, qrg_ref, kvg_ref, krg_ref,
     q_ref, ckv_ref, kpe_ref, kpep_ref, cqn_ref) = refs[n_cq:]
    cos = cos_ref[...]
    sin = sin_ref[...]

    @pl.when(pl.program_id(1) == 0)
    def _():
        cq = jnp.concatenate([r[...].astype(F32) for r in cq_refs], axis=1)
        cqn_ref[...] = (_rms(cq, cq.shape[-1]) * qan_ref[...]).astype(BF16)
        pf = pf_ref[...]
        ckv_raw = pf[:, :kv_lora]
        ckv_ref[...] = _rms(ckv_raw, kv_lora) * kvg_ref[...]
        tail = pf[:, kv_lora:kv_lora + LANE]
        lane = lax.broadcasted_iota(jnp.int32, tail.shape, 1)
        kraw = jnp.where(lane < QK_ROPE, tail, 0.0)
        kpe = _rope_pad(_rms(kraw, QK_ROPE) * krg_ref[...], cos, sin)
        kpe_ref[...] = kpe[:, :QK_ROPE]
        kpep_ref[...] = kpe.astype(BF16)

    qf = _dot(cqn_ref[...], wq_ref[...])
    for h in range(heads_per_step):
        base = h * QK_PAD
        nope = qf[:, base:base + QK_NOPE]
        pe = qf[:, base + QK_NOPE:base + QK_PAD]
        q_ref[:, base:base + QK_NOPE] = (_rms(nope, QK_NOPE) * qng_ref[...]).astype(BF16)
        q_ref[:, base + QK_NOPE:base + QK_PAD] = _rope_pad(_rms(pe, QK_ROPE) * qrg_ref[...], cos, sin).astype(BF16)


def _latent_q(p_bf, p_f32, cq_off, q_lora, kv_lora, cos, sin, qan, wq, qng, qrg, kvg, krg, tm, heads_per_step):
    m = p_bf.shape[0]
    n_heads = wq.shape[1] // QK_PAD
    n_cq = q_lora // COL_TILE
    cq0 = cq_off // COL_TILE
    ns = p_f32.shape[1]
    wstep = heads_per_step * QK_PAD
    row = lambda i, j: (i, 0)
    const = lambda i, j: (0, 0)
    in_specs = [pl.BlockSpec((tm, COL_TILE), functools.partial(lambda i, j, c: (i, c), c=cq0 + c))
                for c in range(n_cq)]
    in_specs += [
        pl.BlockSpec((tm, ns), row),
        pl.BlockSpec((tm, LANE), row),
        pl.BlockSpec((tm, LANE), row),
        pl.BlockSpec((1, q_lora), const),
        pl.BlockSpec((q_lora, wstep), lambda i, j: (0, j)),
        pl.BlockSpec((1, LANE), const),
        pl.BlockSpec((1, LANE), const),
        pl.BlockSpec((1, kv_lora), const),
        pl.BlockSpec((1, LANE), const),
    ]
    return pl.pallas_call(
        functools.partial(_latent_q_kernel, n_cq=n_cq, kv_lora=kv_lora, heads_per_step=heads_per_step),
        grid=(m // tm, n_heads // heads_per_step),
        in_specs=in_specs,
        out_specs=[
            pl.BlockSpec((tm, wstep), lambda i, j: (i, j)),
            pl.BlockSpec((tm, kv_lora), row),
            pl.BlockSpec((tm, QK_ROPE), row),
            pl.BlockSpec((tm, LANE), row),
        ],
        out_shape=[
            jax.ShapeDtypeStruct((m, n_heads * QK_PAD), BF16),
            jax.ShapeDtypeStruct((m, kv_lora), F32),
            jax.ShapeDtypeStruct((m, QK_ROPE), F32),
            jax.ShapeDtypeStruct((m, LANE), BF16),
        ],
        scratch_shapes=[pltpu.VMEM((tm, q_lora), BF16)],
        compiler_params=_cparams(("parallel", "arbitrary")),
        name="latent_q",
    )(*([p_bf] * n_cq), p_f32, cos, sin, qan, wq, qng, qrg, kvg, krg)


def _kv_expand_kernel(ckv_ref, kpep_ref, wk_ref, wv_ref, kng_ref, k_ref, v_ref, *, n_heads):
    c = ckv_ref[...].astype(BF16)
    kf = _dot(c, wk_ref[...])
    v_ref[...] = _dot(c, wv_ref[...]).astype(BF16)
    kpep = kpep_ref[...]
    for h in range(n_heads):
        kn = _rms(kf[:, h * QK_NOPE:(h + 1) * QK_NOPE], QK_NOPE) * kng_ref[...]
        k_ref[:, h * QK_PAD:h * QK_PAD + QK_NOPE] = kn.astype(BF16)
        k_ref[:, h * QK_PAD + QK_NOPE:(h + 1) * QK_PAD] = kpep


def _kv_expand(ckv, kpep, wk, wv, kng, tm):
    r, kv_lora = ckv.shape
    n_heads = wk.shape[1] // QK_NOPE
    row = lambda i: (i, 0)
    const = lambda i: (0, 0)
    return pl.pallas_call(
        functools.partial(_kv_expand_kernel, n_heads=n_heads),
        grid=(r // tm,),
        in_specs=[
            pl.BlockSpec((tm, kv_lora), row),
            pl.BlockSpec((tm, LANE), row),
            pl.BlockSpec((kv_lora, n_heads * QK_NOPE), const),
            pl.BlockSpec((kv_lora, n_heads * V_HEAD), const),
            pl.BlockSpec((1, LANE), const),
        ],
        out_specs=[
            pl.BlockSpec((tm, n_heads * QK_PAD), row),
            pl.BlockSpec((tm, n_heads * V_HEAD), row),
        ],
        out_shape=[
            jax.ShapeDtypeStruct((r, n_heads * QK_PAD), BF16),
            jax.ShapeDtypeStruct((r, n_heads * V_HEAD), BF16),
        ],
        compiler_params=_cparams(("parallel",)),
        name="kv_expand",
    )(ckv, kpep, wk, wv, kng)


def _lane_chunk_sum(p):
    out = p[:, :LANE]
    for c in range(1, p.shape[1] // LANE):
        out = out + p[:, c * LANE:(c + 1) * LANE]
    return out


def _gated(lvec, acc, za):
    za = za.astype(F32)
    return acc / jnp.sum(lvec, axis=-1, keepdims=True) * (za * jax.nn.sigmoid(za))


def _softmax_scale():
    return float((QK_NOPE + QK_ROPE) ** -0.5 * 1.4426950408889634)


def _attn_prompt_block(n_full, q_ref, k_ref, v_ref, km_ref, vm_ref, za_ref, o_ref, s_ref, *, tq, c, n_meta):
    td = min(tq, 256)
    q = q_ref[...]
    pieces = [(i * tq, tq, 0, False) for i in range(n_full)]
    pieces += [(n_full * tq + j * td, td, j * td, True) for j in range(tq // td)]

    def upd(full, r0, fn):
        return fn(full) if r0 == 0 else jnp.concatenate([full[:r0], fn(full[r0:])], axis=0)

    col = lax.broadcasted_iota(jnp.int32, (tq, LANE), 1)
    t = jnp.where(col < n_meta, _dot_nt(q, km_ref[...]) * c, -jnp.inf)
    s_ref[:, :LANE] = t
    mx = t
    off = LANE
    for start, rows, r0, masked in pieces:
        t = _dot_nt(q[r0:], k_ref[start:start + rows, :]) * c
        if masked:
            rq = lax.broadcasted_iota(jnp.int32, t.shape, 0) // CHUNK
            ck = lax.broadcasted_iota(jnp.int32, t.shape, 1) // CHUNK
            t = jnp.where(rq >= ck, t, -jnp.inf)
        s_ref[r0:, off:off + rows] = t
        tmax = t[:, :LANE]
        for ch in range(1, rows // LANE):
            tmax = jnp.maximum(tmax, t[:, ch * LANE:(ch + 1) * LANE])
        mx = upd(mx, r0, lambda a: jnp.maximum(a, tmax))
        off += rows
    m = jnp.max(mx, axis=-1, keepdims=True)

    p = jnp.exp2(s_ref[:, :LANE] - m)
    lvec = p
    acc = _dot(p.astype(BF16), vm_ref[...])
    off = LANE
    for start, rows, r0, _ in pieces:
        p = jnp.exp2(s_ref[r0:, off:off + rows] - m[r0:])
        pv = _dot(p.astype(BF16), v_ref[start:start + rows, :])
        psum = _lane_chunk_sum(p)
        lvec = upd(lvec, r0, lambda a: a + psum)
        acc = upd(acc, r0, lambda a: a + pv)
        off += rows
    o_ref[...] = _gated(lvec, acc, za_ref[...]).astype(o_ref.dtype)


def _attn_prompt_kernel(q_ref, k_ref, v_ref, km_ref, vm_ref, za_ref, o_ref, s_ref, *, nqb, **kw):
    qi = pl.program_id(2)
    for n_full in range(nqb):
        pl.when(qi == n_full)(
            functools.partial(_attn_prompt_block, n_full, q_ref, k_ref, v_ref, km_ref, vm_ref, za_ref, o_ref, s_ref,
                              **kw))


def _attn_prompt(q, k, v, k_small, v_small, meta_blk, p_bf, za_off, batch, seq, tq, n_meta):
    n_heads = q.shape[1] // QK_PAD
    nqb = seq // tq
    za0 = za_off // V_HEAD
    assert za_off % V_HEAD == 0 and tq % CHUNK == 0
    return pl.pallas_call(
        functools.partial(_attn_prompt_kernel, nqb=nqb, tq=tq, c=_softmax_scale(), n_meta=n_meta),
        grid=(batch, n_heads, nqb),
        in_specs=[
            pl.BlockSpec((tq, QK_PAD), lambda b, h, i: (b * nqb + i, h)),
            pl.BlockSpec((seq, QK_PAD), lambda b, h, i: (b, h)),
            pl.BlockSpec((seq, V_HEAD), lambda b, h, i: (b, h)),
            pl.BlockSpec((LANE, QK_PAD), lambda b, h, i: (meta_blk, h)),
            pl.BlockSpec((LANE, V_HEAD), lambda b, h, i: (meta_blk, h)),
            pl.BlockSpec((tq, V_HEAD), lambda b, h, i: (b * nqb + i, za0 + h)),
        ],
        out_specs=pl.BlockSpec((tq, V_HEAD), lambda b, h, i: (b * nqb + i, h)),
        out_shape=jax.ShapeDtypeStruct((batch * seq, n_heads * V_HEAD), BF16),
        scratch_shapes=[pltpu.VMEM((tq, LANE + seq), F32)],
        compiler_params=_cparams(("parallel", "parallel", "arbitrary")),
        name="attn_prompt",
    )(q, k, v, k_small, v_small, p_bf)


def _attn_sample_kernel(q_ref, kc_ref, vc_ref, ks_ref, vs_ref, za_ref, o_ref, *, heads, lq, c, n_meta, meta_row):
    b = pl.program_id(0)
    col = lax.broadcasted_iota(jnp.int32, (lq, ks_ref.shape[0]), 1)
    own = (col >= b * lq) & (col < (b + 1) * lq)
    meta = (col >= meta_row) & (col < meta_row + n_meta)
    visible = own | meta
    for h in range(heads):
        qs = slice(h * QK_PAD, (h + 1) * QK_PAD)
        vs = slice(h * V_HEAD, (h + 1) * V_HEAD)
        q = q_ref[:, qs]
        t_cache = _dot_nt(q, kc_ref[:, qs]) * c
        t_new = jnp.where(visible, _dot_nt(q, ks_ref[:, qs]) * c, -jnp.inf)
        m = jnp.maximum(jnp.max(t_cache, axis=-1, keepdims=True), jnp.max(t_new, axis=-1, keepdims=True))
        p_cache = jnp.exp2(t_cache - m)
        p_new = jnp.exp2(t_new - m)
        lvec = _lane_chunk_sum(p_cache) + _lane_chunk_sum(p_new)
        acc = _dot(p_cache.astype(BF16), vc_ref[:, vs]) + _dot(p_new.astype(BF16), vs_ref[:, vs])
        o_ref[:, vs] = _gated(lvec, acc, za_ref[:, vs]).astype(o_ref.dtype)


def _attn_sample(q, kc, vc, ks, vs, p_bf, za_off, batch, lq, past, n_meta, heads):
    n_heads = q.shape[1] // QK_PAD
    rows = ks.shape[0]
    za0 = za_off // (V_HEAD * heads)
    assert za_off % (V_HEAD * heads) == 0 and n_heads % heads == 0
    return pl.pallas_call(
        functools.partial(_attn_sample_kernel, heads=heads, lq=lq, c=_softmax_scale(), n_meta=n_meta,
                          meta_row=batch * lq),
        grid=(batch, n_heads // heads),
        in_specs=[
            pl.BlockSpec((lq, heads * QK_PAD), lambda b, h: (b, h)),
            pl.BlockSpec((past, heads * QK_PAD), lambda b, h: (b, h)),
            pl.BlockSpec((past, heads * V_HEAD), lambda b, h: (b, h)),
            pl.BlockSpec((rows, heads * QK_PAD), lambda b, h: (0, h)),
            pl.BlockSpec((rows, heads * V_HEAD), lambda b, h: (0, h)),
            pl.BlockSpec((lq, heads * V_HEAD), lambda b, h: (b, za0 + h)),
        ],
        out_specs=pl.BlockSpec((lq, heads * V_HEAD), lambda b, h: (b, h)),
        out_shape=jax.ShapeDtypeStruct((batch * lq, n_heads * V_HEAD), BF16),
        compiler_params=_cparams(("parallel", "arbitrary")),
        name="attn_sample",
    )(q, kc, vc, ks, vs, p_bf)


def _log_sigmoid(x):
    return jnp.minimum(x, 0.0) - jnp.log1p(jnp.exp(-jnp.abs(x)))


def _pad_rows(a, rows):
    if a.shape[0] == rows:
        return a
    return jnp.concatenate([a, jnp.zeros((rows - a.shape[0], a.shape[1]), a.dtype)], axis=0)


def _mlstm_kernel(q_ref, k_ref, v_ref, mo_ref, zb_ref, g_ref, gb_ref, gain_ref, c0_ref, n0_ref, m0_ref,
                  h_ref, c_ref, n_ref, m_ref, *, n_heads, lc, lp, dk, dv):
    @pl.when(pl.program_id(1) == 0)
    def _():
        c_ref[...] = c0_ref[...]
        n_ref[...] = n0_ref[...]
        m_ref[...] = m0_ref[...]

    hi = lax.Precision.HIGHEST
    gates = _pad_rows(g_ref[...] + gb_ref[...], lp)
    gates_t = gates.T
    t_col = lax.broadcasted_iota(jnp.int32, (lp, 1), 0)
    t_row = lax.broadcasted_iota(jnp.int32, (1, lp), 1)
    r_idx = lax.broadcasted_iota(jnp.int32, (lp, lp), 0)
    c_idx = lax.broadcasted_iota(jnp.int32, (lp, lp), 1)
    causal = r_idx >= c_idx
    lower = causal.astype(F32)
    upper = (c_idx >= r_idx).astype(F32)

    for h in range(n_heads):
        li = QK_ROPE + h
        lf_lane = QK_ROPE + n_heads + h
        ig_col = jnp.where(t_col < lc, gates[:, li:li + 1], NEG_BIG)
        ig_row = jnp.where(t_row < lc, gates_t[li:li + 1, :], NEG_BIG)
        lf_col = jnp.where(t_col < lc, _log_sigmoid(gates[:, lf_lane:lf_lane + 1]), 0.0)
        lf_row = jnp.where(t_row < lc, _log_sigmoid(gates_t[lf_lane:lf_lane + 1, :]), 0.0)
        b_col = jnp.dot(lower, jnp.broadcast_to(lf_col, (lp, LANE)), precision=hi,
                        preferred_element_type=F32)[:, 0:1]
        b_row = jnp.dot(jnp.broadcast_to(lf_row, (8, lp)), upper, precision=hi,
                        preferred_element_type=F32)[0:1, :]
        m0 = m_ref[0, 0:1, h:h + 1]
        d = jnp.where(causal, b_col - b_row + ig_row, -jnp.inf)
        a_col = b_col + m0
        m = jnp.maximum(a_col, jnp.max(d, axis=-1, keepdims=True))
        w_inter = jnp.exp(a_col - m)

        q = _pad_rows(q_ref[:, h * dk:(h + 1) * dk], lp)
        k = _pad_rows(k_ref[:, h * dk:(h + 1) * dk], lp) * (dk ** -0.5)
        v = _pad_rows(v_ref[:, h * dv:(h + 1) * dv], lp)
        qk = _dot_nt(q, k) * jnp.exp(d - m)
        c_old = c_ref[0, h]
        n_old = n_ref[0, h:h + 1, :]
        num = w_inter * _dot(q, c_old.astype(BF16)) + _dot(qk.astype(BF16), v)
        den = (w_inter * jnp.sum(q.astype(F32) * n_old, axis=-1, keepdims=True)
               + jnp.sum(qk, axis=-1, keepdims=True))
        hid = num / jnp.maximum(jnp.abs(den), jnp.exp(-m))

        b_last = b_col[lp - 1:lp, :]
        g_row = b_last - b_row + ig_row
        g_col = b_last - b_col + ig_col
        m_new = jnp.maximum(b_last + m0, jnp.max(g_row, axis=-1, keepdims=True))
        decay = jnp.exp(b_last + m0 - m_new)
        kw = k.astype(F32) * jnp.exp(g_col - m_new)
        c_ref[0, h] = decay * c_old + _dot(kw.T.astype(BF16), v)
        n_ref[0, h:h + 1, :] = decay * n_old + jnp.sum(kw, axis=0, keepdims=True)
        m_ref[0, 0:1, h:h + 1] = m_new

        hid = hid[:lc]
        mo = mo_ref[:, h * dv:(h + 1) * dv].astype(F32)
        zb = zb_ref[:, h * dv:(h + 1) * dv].astype(F32)
        out = jax.nn.sigmoid(mo) * (_rms(hid, dv) * gain_ref[:, h * dv:(h + 1) * dv])
        h_ref[:, h * dv:(h + 1) * dv] = (out * (zb * jax.nn.sigmoid(zb))).astype(h_ref.dtype)


def _mlstm(p_bf, p_f32, offs, gate_bias, gain, c0, n0, m0, batch, seq, lc, row0, share_state, n_heads, dk, dv):
    lp = -(-lc // LANE) * LANE
    nch = seq // lc
    blk0 = row0 // lc
    wqk = n_heads * dk
    wv = n_heads * dv
    gate_blk = p_f32.shape[1] // LANE - 1
    rows = lambda col: (lambda b, c: (blk0 + b * nch + c, col))
    state = (lambda b, c: (0, 0, 0, 0)) if share_state else (lambda b, c: (b, 0, 0, 0))
    state3 = (lambda b, c: (0, 0, 0)) if share_state else (lambda b, c: (b, 0, 0))
    out_rows = batch * seq
    return pl.pallas_call(
        functools.partial(_mlstm_kernel, n_heads=n_heads, lc=lc, lp=lp, dk=dk, dv=dv),
        grid=(batch, nch),
        in_specs=[
            pl.BlockSpec((lc, wqk), rows(offs["mq"] // wqk)),
            pl.BlockSpec((lc, wqk), rows(offs["mk"] // wqk)),
            pl.BlockSpec((lc, wv), rows(offs["mv"] // wv)),
            pl.BlockSpec((lc, wv), rows(offs["mo"] // wv)),
            pl.BlockSpec((lc, wv), rows(offs["zb"] // wv)),
            pl.BlockSpec((lc, LANE), rows(gate_blk)),
            pl.BlockSpec((1, LANE), lambda b, c: (0, 0)),
            pl.BlockSpec((1, wv), lambda b, c: (0, 0)),
            pl.BlockSpec((1, n_heads, dk, dv), state),
            pl.BlockSpec((1, n_heads, dk), state3),
            pl.BlockSpec((1, 1, n_heads), state3),
        ],
        out_specs=[
            pl.BlockSpec((lc, wv), lambda b, c: (b * nch + c, 0)),
            pl.BlockSpec((1, n_heads, dk, dv), lambda b, c: (b, 0, 0, 0)),
            pl.BlockSpec((1, n_heads, dk), lambda b, c: (b, 0, 0)),
            pl.BlockSpec((1, 1, n_heads), lambda b, c: (b, 0, 0)),
        ],
        out_shape=[
            jax.ShapeDtypeStruct((out_rows, wv), BF16),
            jax.ShapeDtypeStruct((batch, n_heads, dk, dv), F32),
            jax.ShapeDtypeStruct((batch, n_heads, dk), F32),
            jax.ShapeDtypeStruct((batch, 1, n_heads), F32),
        ],
        compiler_params=_cparams(("parallel", "arbitrary")),
        name="mlstm",
    )(p_bf, p_bf, p_bf, p_bf, p_bf, p_f32, gate_bias, gain, c0, n0, m0)


def _out_proj_kernel(a_ref, m_ref, wa_ref, wm_ref, x_ref, o_ref):
    o_ref[...] = x_ref[...] + _dot(a_ref[...], wa_ref[...]) + _dot(m_ref[...], wm_ref[...])


def _out_proj(a, ml, w_out, x, tm, tn):
    rows, wa = a.shape
    wm = ml.shape[1]
    d = w_out.shape[1]
    assert wa == wm
    return pl.pallas_call(
        _out_proj_kernel,
        grid=(rows // tm, d // tn),
        in_specs=[
            pl.BlockSpec((tm, wa), lambda i, j: (i, 0)),
            pl.BlockSpec((tm, wm), lambda i, j: (i, 0)),
            pl.BlockSpec((wa, tn), lambda i, j: (0, j)),
            pl.BlockSpec((wm, tn), lambda i, j: (1, j)),
            pl.BlockSpec((tm, tn), lambda i, j: (i, j)),
        ],
        out_specs=pl.BlockSpec((tm, tn), lambda i, j: (i, j)),
        out_shape=jax.ShapeDtypeStruct((rows, d), F32),
        compiler_params=_cparams(("parallel", "arbitrary")),
        name="out_proj",
    )(a, ml, w_out, w_out, x)


def _rope_tables(pos):
    inv_freq = ROPE_THETA ** (-jnp.arange(HALF_ROPE, dtype=F32) / HALF_ROPE)
    ang = pos.astype(F32)[:, None] * inv_freq[None, :]
    cos, sin = jnp.cos(ang), jnp.sin(ang)
    zero = jnp.zeros((pos.shape[0], LANE - QK_ROPE), F32)
    return jnp.concatenate([cos, cos, zero], axis=1), jnp.concatenate([-sin, sin, zero], axis=1)


def _pad_lanes(vec):
    return jnp.pad(vec.astype(F32), (0, LANE - vec.shape[0]))[None, :]


def _row_tile(rows, target):
    t = min(rows, target)
    while rows % t:
        t //= 2
    return t


def kernel(x_prompt, x_sample, cache_ckv, cache_kpe, state_C, state_n, state_m, meta_tokens, norm_gain, w_in,
           b_igate, b_fgate, q_a_norm, w_q_up, q_nope_norm, q_rope_norm, kv_a_norm, k_rope_norm, w_kv_up,
           k_nope_norm, ml_out_norm, w_out):
    batch, seq, d = x_prompt.shape
    dbatch, dseq, _ = x_sample.shape
    past = cache_ckv.shape[1]
    n_meta = meta_tokens.shape[0]
    q_lora = q_a_norm.shape[0]
    kv_lora = kv_a_norm.shape[0]
    ml_heads, dv = ml_out_norm.shape
    dk = state_n.shape[-1]
    mla_heads = w_kv_up.shape[1] // (QK_NOPE + V_HEAD)
    mla_w = mla_heads * V_HEAD
    ml_w = ml_heads * dv
    mqk_w = ml_heads * dk
    assert q_lora % COL_TILE == 0 and kv_lora % LANE == 0 and 2 * ml_heads <= LANE - QK_ROPE
    assert mla_w == ml_w and w_out.shape[0] == mla_w + ml_w

    o_cq = 0
    o_ckv = o_cq + q_lora
    o_kpe = o_ckv + kv_lora
    o_mq = o_kpe + QK_ROPE
    o_mk = o_mq + mqk_w
    o_mv = o_mk + mqk_w
    o_mo = o_mv + ml_w
    o_mi = o_mo + ml_w
    o_mf = o_mi + ml_heads
    o_za = o_mf + ml_heads
    o_zb = o_za + mla_w

    offs = {"mq": 0, "mk": mqk_w, "mv": 2 * mqk_w, "mo": 2 * mqk_w + ml_w, "za": 2 * mqk_w + 2 * ml_w,
            "zb": 2 * mqk_w + 2 * ml_w + mla_w, "cq": 2 * mqk_w + 3 * ml_w + mla_w}
    gate_pad = LANE - QK_ROPE - 2 * ml_heads
    w_main, w_small = _regroup_rows(
        w_in.T,
        [(o_mq, mqk_w, offs["mq"]), (o_mk, mqk_w, offs["mk"]), (o_mv, ml_w, offs["mv"]), (o_mo, ml_w, offs["mo"]),
         (o_za, mla_w, offs["za"]), (o_zb, ml_w, offs["zb"]), (o_cq, q_lora, offs["cq"])], offs["cq"] + q_lora,
        [(o_ckv, kv_lora), (o_kpe, QK_ROPE), (o_mi, 2 * ml_heads)], kv_lora + LANE, _row_tile(d, 256))
    gate_bias = jnp.concatenate([jnp.zeros((QK_ROPE,), F32), b_igate.astype(F32), b_fgate.astype(F32),
                                 jnp.zeros((gate_pad,), F32)])[None, :]
    qk_w = QK_NOPE + QK_ROPE
    kv_w = QK_NOPE + V_HEAD
    wq = _regroup(w_q_up, [(h * qk_w, qk_w, h * QK_PAD) for h in range(mla_heads)], mla_heads * QK_PAD,
                  _row_tile(q_lora, 512))
    wk = _regroup(w_kv_up, [(h * kv_w, QK_NOPE, h * QK_NOPE) for h in range(mla_heads)], mla_heads * QK_NOPE,
                  kv_lora)
    wv = _regroup(w_kv_up, [(h * kv_w + QK_NOPE, V_HEAD, h * V_HEAD) for h in range(mla_heads)],
                  mla_heads * V_HEAD, kv_lora)
    w_out_bf = w_out.astype(BF16)
    gain_row = norm_gain.astype(F32)[None, :]
    qan = q_a_norm.astype(F32)[None, :]
    kvg = kv_a_norm.astype(F32)[None, :]
    qng = q_nope_norm.astype(F32)[None, :]
    kng = k_nope_norm.astype(F32)[None, :]
    qrg = _pad_lanes(q_rope_norm)
    krg = _pad_lanes(k_rope_norm)
    ml_gain = ml_out_norm.astype(F32).reshape(1, ml_w)

    n_p = batch * seq
    n_s = dbatch * dseq
    assert n_s % LANE == 0 and n_meta <= LANE
    small_rows = n_s + LANE
    xp = x_prompt.reshape(n_p, d)
    xs = jnp.concatenate([x_sample.reshape(n_s, d), meta_tokens.astype(x_sample.dtype),
                          jnp.zeros((small_rows - n_s - n_meta, d), x_sample.dtype)], axis=0)
    meta_blk = n_s // LANE

    pos_p = jnp.tile(n_meta + jnp.arange(seq), batch)
    pos_s = jnp.concatenate([jnp.tile(n_meta + past + jnp.arange(dseq), dbatch), jnp.arange(n_meta),
                             jnp.zeros((small_rows - n_s - n_meta,), jnp.int32)])
    cos_p, sin_p = _rope_tables(pos_p)
    cos_s, sin_s = _rope_tables(pos_s)

    tm_p = _row_tile(n_p, 512)
    pbf_p, pf_p = _proj(xp, gain_row, w_main, w_small, tm_p)
    pbf_s, pf_s = _proj(xs, gain_row, w_main, w_small, small_rows)

    hps = max(1, mla_heads // 2)
    q_p, ckv_p, kpe_p, kpep_p = _latent_q(pbf_p, pf_p, offs["cq"], q_lora, kv_lora, cos_p, sin_p, qan, wq,
                                          qng, qrg, kvg, krg, tm_p, hps)
    q_s, ckv_s, kpe_s, kpep_s = _latent_q(pbf_s, pf_s, offs["cq"], q_lora, kv_lora, cos_s, sin_s, qan, wq,
                                          qng, qrg, kvg, krg, small_rows, hps)

    k_p, v_p = _kv_expand(ckv_p, kpep_p, wk, wv, kng, tm_p)
    k_s, v_s = _kv_expand(ckv_s, kpep_s, wk, wv, kng, small_rows)
    cache_rows = dbatch * past
    cache_kpep = jnp.pad(cache_kpe.reshape(cache_rows, QK_ROPE), ((0, 0), (0, LANE - QK_ROPE))).astype(BF16)
    k_c, v_c = _kv_expand(cache_ckv.reshape(cache_rows, kv_lora).astype(F32), cache_kpep, wk, wv, kng,
                          _row_tile(cache_rows, 512))

    tq = _row_tile(seq, ATTN_TQ)
    attn_p = _attn_prompt(q_p, k_p, v_p, k_s, v_s, meta_blk, pbf_p, offs["za"], batch, seq, tq, n_meta)
    attn_s = _attn_sample(q_s, k_c, v_c, k_s, v_s, pbf_s, offs["za"], dbatch, dseq, past, n_meta,
                          min(4, mla_heads))

    zc = jnp.zeros((1, ml_heads, dk, dv), F32)
    zn = jnp.zeros((1, ml_heads, dk), F32)
    zm = jnp.zeros((1, 1, ml_heads), F32)
    _, c_m, n_m, m_m = _mlstm(pbf_s, pf_s, offs, gate_bias, ml_gain, zc, zn, zm, 1, n_meta, n_meta, n_s, True,
                              ml_heads, dk, dv)
    lc_p = _row_tile(seq, 256)
    ml_p, c_p, n_pr, m_p = _mlstm(pbf_p, pf_p, offs, gate_bias, ml_gain, c_m, n_m, m_m, batch, seq, lc_p, 0,
                                  True, ml_heads, dk, dv)
    ml_s, c_s, n_sm, m_s = _mlstm(pbf_s, pf_s, offs, gate_bias, ml_gain, state_C.astype(F32),
                                  state_n.astype(F32), state_m.astype(F32).reshape(dbatch, 1, ml_heads),
                                  dbatch, dseq, dseq, 0, False, ml_heads, dk, dv)

    y_p = _out_proj(attn_p, ml_p, w_out_bf, xp, _row_tile(n_p, 512), _row_tile(d, 1024))
    y_s = _out_proj(attn_s, ml_s, w_out_bf, xs, n_s, _row_tile(d, 1024))

    meta_ckv = ckv_s[n_s:n_s + n_meta]
    meta_kpe = kpe_s[n_s:n_s + n_meta]
    ckv_prompt = jnp.concatenate([jnp.broadcast_to(meta_ckv[None], (batch, n_meta, kv_lora)),
                                  ckv_p.reshape(batch, seq, kv_lora)], axis=1)
    kpe_prompt = jnp.concatenate([jnp.broadcast_to(meta_kpe[None], (batch, n_meta, QK_ROPE)),
                                  kpe_p.reshape(batch, seq, QK_ROPE)], axis=1)
    return (y_p.reshape(batch, seq, d), y_s.reshape(dbatch, dseq, d), ckv_prompt, kpe_prompt,
            c_p, n_pr, m_p.reshape(batch, ml_heads),
            ckv_s[:n_s].reshape(dbatch, dseq, kv_lora), kpe_s[:n_s].reshape(dbatch, dseq, QK_ROPE),
            c_s, n_sm, m_s.reshape(dbatch, ml_heads))
```

```python
import functools

import jax
import jax.numpy as jnp
from jax import lax
from jax.experimental import pallas as pl
from jax.experimental.pallas import tpu as pltpu

CHUNK = 64
EPS = 1e-6
ROPE_THETA = 10000.0
V_HEAD = 128
QK_NOPE = 128
QK_ROPE = 64
HALF_ROPE = QK_ROPE // 2
LANE = 128
QK_PAD = QK_NOPE + LANE
COL_TILE = 512
NEG_BIG = -1e30
ATTN_TQ = 512
ATTN_HEADS = 4
VMEM_LIMIT = 56 * 1024 * 1024

F32 = jnp.float32
BF16 = jnp.bfloat16


def _cparams(sem):
    return pltpu.CompilerParams(dimension_semantics=sem, vmem_limit_bytes=VMEM_LIMIT)


def _dot(a, b):
    return jnp.dot(a, b, preferred_element_type=F32)


def _dot_nt(a, b):
    return lax.dot_general(a, b, (((1,), (1,)), ((), ())), preferred_element_type=F32)


def _rms(x, n):
    return x * lax.rsqrt(jnp.sum(x * x, axis=-1, keepdims=True) * (1.0 / n) + EPS)


def _rope_pad(x, cos, sin_signed):
    lane = lax.broadcasted_iota(jnp.int32, x.shape, 1)
    rot = jnp.where(lane < HALF_ROPE, pltpu.roll(x, LANE - HALF_ROPE, 1), pltpu.roll(x, HALF_ROPE, 1))
    return x * cos + rot * sin_signed


def _regroup_kernel(w_ref, o_ref, *, segments, zero_ranges):
    for src, n, dst in segments:
        o_ref[:, dst:dst + n] = w_ref[:, src:src + n].astype(o_ref.dtype)
    for dst, n in zero_ranges:
        o_ref[:, dst:dst + n] = jnp.zeros((o_ref.shape[0], n), o_ref.dtype)


def _regroup(w, segments, out_cols, tr):
    rows, cols = w.shape
    covered = sorted((dst, n) for _, n, dst in segments)
    zero_ranges, pos = [], 0
    for dst, n in covered + [(out_cols, 0)]:
        if dst > pos:
            zero_ranges.append((pos, dst - pos))
        pos = dst + n
    return pl.pallas_call(
        functools.partial(_regroup_kernel, segments=tuple(segments), zero_ranges=tuple(zero_ranges)),
        grid=(rows // tr,),
        in_specs=[pl.BlockSpec((tr, cols), lambda i: (i, 0))],
        out_specs=pl.BlockSpec((tr, out_cols), lambda i: (i, 0)),
        out_shape=jax.ShapeDtypeStruct((rows, out_cols), BF16),
        compiler_params=_cparams(("parallel",)),
        name="regroup",
    )(w)


def _regroup_rows_kernel(w_ref, om_ref, os_ref, *, main_segments, small_segments):
    for src, n, dst in main_segments:
        om_ref[dst:dst + n, :] = w_ref[src:src + n, :].astype(om_ref.dtype)
    parts = [w_ref[src:src + n, :] for src, n in small_segments]
    used = sum(n for _, n in small_segments)
    parts.append(jnp.zeros((os_ref.shape[0] - used, w_ref.shape[1]), w_ref.dtype))
    os_ref[...] = jnp.concatenate(parts, axis=0).astype(os_ref.dtype)


def _regroup_rows(wt, main_segments, main_rows, small_segments, small_rows, tc):
    rows, cols = wt.shape
    return pl.pallas_call(
        functools.partial(_regroup_rows_kernel, main_segments=tuple(main_segments),
                          small_segments=tuple(small_segments)),
        grid=(cols // tc,),
        in_specs=[pl.BlockSpec((rows, tc), lambda i: (0, i))],
        out_specs=[pl.BlockSpec((main_rows, tc), lambda i: (0, i)), pl.BlockSpec((small_rows, tc), lambda i: (0, i))],
        out_shape=[jax.ShapeDtypeStruct((main_rows, cols), BF16), jax.ShapeDtypeStruct((small_rows, cols), BF16)],
        compiler_params=_cparams(("parallel",)),
        name="regroup_rows",
    )(wt)


def _proj_kernel(x_ref, g_ref, w_ref, ws_ref, o_ref, os_ref, hn_ref):
    @pl.when(pl.program_id(1) == 0)
    def _():
        x = x_ref[...]
        hn = (_rms(x, x.shape[-1]) * g_ref[...]).astype(BF16)
        hn_ref[...] = hn
        os_ref[...] = _dot_nt(hn, ws_ref[...])

    o_ref[...] = _dot_nt(hn_ref[...], w_ref[...]).astype(o_ref.dtype)


def _proj(x, gain, w_main, w_small, tm):
    m, d = x.shape
    n = w_main.shape[0]
    ns = w_small.shape[0]
    return pl.pallas_call(
        _proj_kernel,
        grid=(m // tm, n // COL_TILE),
        in_specs=[
            pl.BlockSpec((tm, d), lambda i, j: (i, 0)),
            pl.BlockSpec((1, d), lambda i, j: (0, 0)),
            pl.BlockSpec((COL_TILE, d), lambda i, j: (j, 0)),
            pl.BlockSpec((ns, d), lambda i, j: (0, 0)),
        ],
        out_specs=[
            pl.BlockSpec((tm, COL_TILE), lambda i, j: (i, j)),
            pl.BlockSpec((tm, ns), lambda i, j: (i, 0)),
        ],
        out_shape=[jax.ShapeDtypeStruct((m, n), BF16), jax.ShapeDtypeStruct((m, ns), F32)],
        scratch_shapes=[pltpu.VMEM((tm, d), BF16)],
        compiler_params=_cparams(("parallel", "arbitrary")),
        name="proj",
    )(x, gain, w_main, w_small)


def _latent_q_kernel(*refs, n_cq, kv_lora, heads_per_step):
    cq_refs = refs[:n_cq]
    (pf_ref, cos_ref, sin_ref, qan_ref, wq_ref, qng_ref, qrg_ref, kvg_ref, krg_ref,
     q_ref, ckv_ref, kpe_ref, kpep_ref, cqn_ref) = refs[n_cq:]
    cos = cos_ref[...]
    sin = sin_ref[...]

    @pl.when(pl.program_id(1) == 0)
    def _():
        cq = jnp.concatenate([r[...].astype(F32) for r in cq_refs], axis=1)
        cqn_ref[...] = (_rms(cq, cq.shape[-1]) * qan_ref[...]).astype(BF16)
        pf = pf_ref[...]
        ckv_raw = pf[:, :kv_lora]
        ckv_ref[...] = _rms(ckv_raw, kv_lora) * kvg_ref[...]
        tail = pf[:, kv_lora:kv_lora + LANE]
        lane = lax.broadcasted_iota(jnp.int32, tail.shape, 1)
        kraw = jnp.where(lane < QK_ROPE, tail, 0.0)
        kpe = _rope_pad(_rms(kraw, QK_ROPE) * krg_ref[...], cos, sin)
        kpe_ref[...] = kpe[:, :QK_ROPE]
        kpep_ref[...] = kpe.astype(BF16)

    qf = _dot(cqn_ref[...], wq_ref[...])
    for h in range(heads_per_step):
        base = h * QK_PAD
        nope = qf[:, base:base + QK_NOPE]
        pe = qf[:, base + QK_NOPE:base + QK_PAD]
        q_ref[:, base:base + QK_NOPE] = (_rms(nope, QK_NOPE) * qng_ref[...]).astype(BF16)
        q_ref[:, base + QK_NOPE:base + QK_PAD] = _rope_pad(_rms(pe, QK_ROPE) * qrg_ref[...], cos, sin).astype(BF16)


def _latent_q(p_bf, p_f32, cq_off, q_lora, kv_lora, cos, sin, qan, wq, qng, qrg, kvg, krg, tm, heads_per_step):
    m = p_bf.shape[0]
    n_heads = wq.shape[1] // QK_PAD
    n_cq = q_lora // COL_TILE
    cq0 = cq_off // COL_TILE
    ns = p_f32.shape[1]
    wstep = heads_per_step * QK_PAD
    row = lambda i, j: (i, 0)
    const = lambda i, j: (0, 0)
    in_specs = [pl.BlockSpec((tm, COL_TILE), functools.partial(lambda i, j, c: (i, c), c=cq0 + c))
                for c in range(n_cq)]
    in_specs += [
        pl.BlockSpec((tm, ns), row),
        pl.BlockSpec((tm, LANE), row),
        pl.BlockSpec((tm, LANE), row),
        pl.BlockSpec((1, q_lora), const),
        pl.BlockSpec((q_lora, wstep), lambda i, j: (0, j)),
        pl.BlockSpec((1, LANE), const),
        pl.BlockSpec((1, LANE), const),
        pl.BlockSpec((1, kv_lora), const),
        pl.BlockSpec((1, LANE), const),
    ]
    return pl.pallas_call(
        functools.partial(_latent_q_kernel, n_cq=n_cq, kv_lora=kv_lora, heads_per_step=heads_per_step),
        grid=(m // tm, n_heads // heads_per_step),
        in_specs=in_specs,
        out_specs=[
            pl.BlockSpec((tm, wstep), lambda i, j: (i, j)),
            pl.BlockSpec((tm, kv_lora), row),
            pl.BlockSpec((tm, QK_ROPE), row),
            pl.BlockSpec((tm, LANE), row),
        ],
        out_shape=[
            jax.ShapeDtypeStruct((m, n_heads * QK_PAD), BF16),
            jax.ShapeDtypeStruct((m, kv_lora), F32),
            jax.ShapeDtypeStruct((m, QK_ROPE), F32),
            jax.ShapeDtypeStruct((m, LANE), BF16),
        ],
        scratch_shapes=[pltpu.VMEM((tm, q_lora), BF16)],
        compiler_params=_cparams(("parallel", "arbitrary")),
        name="latent_q",
    )(*([p_bf] * n_cq), p_f32, cos, sin, qan, wq, qng, qrg, kvg, krg)


def _kv_expand_kernel(ckv_ref, kpep_ref, wk_ref, wv_ref, kng_ref, k_ref, v_ref, *, n_heads):
    c = ckv_ref[...].astype(BF16)
    kf = _dot(c, wk_ref[...])
    v_ref[...] = _dot(c, wv_ref[...]).astype(BF16)
    kpep = kpep_ref[...]
    for h in range(n_heads):
        kn = _rms(kf[:, h * QK_NOPE:(h + 1) * QK_NOPE], QK_NOPE) * kng_ref[...]
        k_ref[:, h * QK_PAD:h * QK_PAD + QK_NOPE] = kn.astype(BF16)
        k_ref[:, h * QK_PAD + QK_NOPE:(h + 1) * QK_PAD] = kpep


def _kv_expand(ckv, kpep, wk, wv, kng, tm):
    r, kv_lora = ckv.shape
    n_heads = wk.shape[1] // QK_NOPE
    row = lambda i: (i, 0)
    const = lambda i: (0, 0)
    return pl.pallas_call(
        functools.partial(_kv_expand_kernel, n_heads=n_heads),
        grid=(r // tm,),
        in_specs=[
            pl.BlockSpec((tm, kv_lora), row),
            pl.BlockSpec((tm, LANE), row),
            pl.BlockSpec((kv_lora, n_heads * QK_NOPE), const),
            pl.BlockSpec((kv_lora, n_heads * V_HEAD), const),
            pl.BlockSpec((1, LANE), const),
        ],
        out_specs=[
            pl.BlockSpec((tm, n_heads * QK_PAD), row),
            pl.BlockSpec((tm, n_heads * V_HEAD), row),
        ],
        out_shape=[
            jax.ShapeDtypeStruct((r, n_heads * QK_PAD), BF16),
            jax.ShapeDtypeStruct((r, n_heads * V_HEAD), BF16),
        ],
        compiler_params=_cparams(("parallel",)),
        name="kv_expand",
    )(ckv, kpep, wk, wv, kng)


def _lane_chunk_sum(p):
    out = p[:, :LANE]
    for c in range(1, p.shape[1] // LANE):
        out = out + p[:, c * LANE:(c + 1) * LANE]
    return out


def _gated(lvec, acc, za):
    za = za.astype(F32)
    return acc / jnp.sum(lvec, axis=-1, keepdims=True) * (za * jax.nn.sigmoid(za))


def _softmax_scale():
    return float((QK_NOPE + QK_ROPE) ** -0.5 * 1.4426950408889634)


def _attn_prompt_block(n_full, q_ref, k_ref, km_ref, za_ref, o_ref, s_ref, p_ref, va_ref, vma_ref, *,
                       tq, heads, c, n_meta):
    td = min(tq, 256)
    pieces = [(i * tq, tq, 0, False) for i in range(n_full)]
    pieces += [(n_full * tq + j * td, td, j * td, True) for j in range(tq // td)]

    def upd(full, r0, fn):
        return fn(full) if r0 == 0 else jnp.concatenate([full[:r0], fn(full[r0:])], axis=0)

    def scores(h):
        qs = slice(h * QK_PAD, (h + 1) * QK_PAD)
        q = q_ref[:, qs]
        col = lax.broadcasted_iota(jnp.int32, (tq, LANE), 1)
        t = jnp.where(col < n_meta, _dot_nt(q, km_ref[:, qs]) * c, -jnp.inf)
        s_ref[h, :, :LANE] = t
        mx = t
        off = LANE
        for start, rows, r0, masked in pieces:
            t = _dot_nt(q[r0:], k_ref[start:start + rows, qs]) * c
            if masked:
                rq = lax.broadcasted_iota(jnp.int32, t.shape, 0) // CHUNK
                ck = lax.broadcasted_iota(jnp.int32, t.shape, 1) // CHUNK
                t = jnp.where(rq >= ck, t, -jnp.inf)
            s_ref[h, r0:, off:off + rows] = t
            tmax = t[:, :LANE]
            for ch in range(1, rows // LANE):
                tmax = jnp.maximum(tmax, t[:, ch * LANE:(ch + 1) * LANE])
            mx = upd(mx, r0, lambda a: jnp.maximum(a, tmax))
            off += rows
        return jnp.max(mx, axis=-1, keepdims=True)

    n_keys = (n_full + 1) * tq

    def values(h, m):
        vs = slice(h * V_HEAD, (h + 1) * V_HEAD)
        va = slice(2 * h * V_HEAD, 2 * (h + 1) * V_HEAD)
        p_ref[h, :, :LANE] = jnp.exp2(s_ref[h, :, :LANE] - m).astype(BF16)
        off = LANE
        for start, rows, r0, _ in pieces:
            if r0:
                p_ref[h, :r0, off:off + rows] = jnp.zeros((r0, rows), BF16)
            p_ref[h, r0:, off:off + rows] = jnp.exp2(s_ref[h, r0:, off:off + rows] - m[r0:]).astype(BF16)
            off += rows
        acc = (_dot(p_ref[h, :, :LANE], vma_ref[:, va])
               + _dot(p_ref[h, :, LANE:LANE + n_keys], va_ref[:n_keys, va]))
        za = za_ref[:, vs].astype(F32)
        out = acc[:, :V_HEAD] / acc[:, V_HEAD:V_HEAD + 1] * (za * jax.nn.sigmoid(za))
        o_ref[:, vs] = out.astype(o_ref.dtype)

    maxes = [scores(h) for h in range(heads)]
    for h in range(heads):
        values(h, maxes[h])


def _attn_prompt_kernel(q_ref, k_ref, v_ref, km_ref, vm_ref, za_ref, o_ref, s_ref, p_ref, va_ref, vma_ref, *,
                        nqb, heads, **kw):
    qi = pl.program_id(2)

    @pl.when(qi == 0)
    def _():
        for h in range(heads):
            vs = slice(h * V_HEAD, (h + 1) * V_HEAD)
            va_ref[:, 2 * h * V_HEAD:(2 * h + 1) * V_HEAD] = v_ref[:, vs]
            va_ref[:, (2 * h + 1) * V_HEAD:(2 * h + 2) * V_HEAD] = jnp.ones((va_ref.shape[0], V_HEAD), BF16)
            vma_ref[:, 2 * h * V_HEAD:(2 * h + 1) * V_HEAD] = vm_ref[:, vs]
            vma_ref[:, (2 * h + 1) * V_HEAD:(2 * h + 2) * V_HEAD] = jnp.ones((vma_ref.shape[0], V_HEAD), BF16)

    for n_full in range(nqb):
        pl.when(qi == n_full)(
            functools.partial(_attn_prompt_block, n_full, q_ref, k_ref, km_ref, za_ref, o_ref, s_ref, p_ref, va_ref,
                              vma_ref, heads=heads, **kw))


def _attn_prompt(q, k, v, k_small, v_small, meta_blk, p_bf, za_off, batch, seq, tq, heads, n_meta):
    n_heads = q.shape[1] // QK_PAD
    nqb = seq // tq
    za0 = za_off // (V_HEAD * heads)
    assert za_off % (V_HEAD * heads) == 0 and n_heads % heads == 0 and tq % CHUNK == 0
    return pl.pallas_call(
        functools.partial(_attn_prompt_kernel, nqb=nqb, tq=tq, heads=heads, c=_softmax_scale(), n_meta=n_meta),
        grid=(batch, n_heads // heads, nqb),
        in_specs=[
            pl.BlockSpec((tq, heads * QK_PAD), lambda b, h, i: (b * nqb + i, h)),
            pl.BlockSpec((seq, heads * QK_PAD), lambda b, h, i: (b, h)),
            pl.BlockSpec((seq, heads * V_HEAD), lambda b, h, i: (b, h)),
            pl.BlockSpec((LANE, heads * QK_PAD), lambda b, h, i: (meta_blk, h)),
            pl.BlockSpec((LANE, heads * V_HEAD), lambda b, h, i: (meta_blk, h)),
            pl.BlockSpec((tq, heads * V_HEAD), lambda b, h, i: (b * nqb + i, za0 + h)),
        ],
        out_specs=pl.BlockSpec((tq, heads * V_HEAD), lambda b, h, i: (b * nqb + i, h)),
        out_shape=jax.ShapeDtypeStruct((batch * seq, n_heads * V_HEAD), BF16),
        scratch_shapes=[pltpu.VMEM((heads, tq, LANE + seq), F32), pltpu.VMEM((heads, tq, LANE + seq), BF16),
                        pltpu.VMEM((seq, 2 * heads * V_HEAD), BF16), pltpu.VMEM((LANE, 2 * heads * V_HEAD), BF16)],
        compiler_params=_cparams(("parallel", "parallel", "arbitrary")),
        name="attn_prompt",
    )(q, k, v, k_small, v_small, p_bf)


def _attn_sample_kernel(q_ref, kc_ref, vc_ref, ks_ref, vs_ref, za_ref, o_ref, *, heads, lq, c, n_meta, meta_row):
    b = pl.program_id(0)
    col = lax.broadcasted_iota(jnp.int32, (lq, ks_ref.shape[0]), 1)
    own = (col >= b * lq) & (col < (b + 1) * lq)
    meta = (col >= meta_row) & (col < meta_row + n_meta)
    visible = own | meta
    for h in range(heads):
        qs = slice(h * QK_PAD, (h + 1) * QK_PAD)
        vs = slice(h * V_HEAD, (h + 1) * V_HEAD)
        q = q_ref[:, qs]
        t_cache = _dot_nt(q, kc_ref[:, qs]) * c
        t_new = jnp.where(visible, _dot_nt(q, ks_ref[:, qs]) * c, -jnp.inf)
        m = jnp.maximum(jnp.max(t_cache, axis=-1, keepdims=True), jnp.max(t_new, axis=-1, keepdims=True))
        p_cache = jnp.exp2(t_cache - m)
        p_new = jnp.exp2(t_new - m)
        lvec = _lane_chunk_sum(p_cache) + _lane_chunk_sum(p_new)
        acc = _dot(p_cache.astype(BF16), vc_ref[:, vs]) + _dot(p_new.astype(BF16), vs_ref[:, vs])
        o_ref[:, vs] = _gated(lvec, acc, za_ref[:, vs]).astype(o_ref.dtype)


def _attn_sample(q, kc, vc, ks, vs, p_bf, za_off, batch, lq, past, n_meta, heads):
    n_heads = q.shape[1] // QK_PAD
    rows = ks.shape[0]
    za0 = za_off // (V_HEAD * heads)
    assert za_off % (V_HEAD * heads) == 0 and n_heads % heads == 0
    return pl.pallas_call(
        functools.partial(_attn_sample_kernel, heads=heads, lq=lq, c=_softmax_scale(), n_meta=n_meta,
                          meta_row=batch * lq),
        grid=(batch, n_heads // heads),
        in_specs=[
            pl.BlockSpec((lq, heads * QK_PAD), lambda b, h: (b, h)),
            pl.BlockSpec((past, heads * QK_PAD), lambda b, h: (b, h)),
            pl.BlockSpec((past, heads * V_HEAD), lambda b, h: (b, h)),
            pl.BlockSpec((rows, heads * QK_PAD), lambda b, h: (0, h)),
            pl.BlockSpec((rows, heads * V_HEAD), lambda b, h: (0, h)),
            pl.BlockSpec((lq, heads * V_HEAD), lambda b, h: (b, za0 + h)),
        ],
        out_specs=pl.BlockSpec((lq, heads * V_HEAD), lambda b, h: (b, h)),
        out_shape=jax.ShapeDtypeStruct((batch * lq, n_heads * V_HEAD), BF16),
        compiler_params=_cparams(("parallel", "arbitrary")),
        name="attn_sample",
    )(q, kc, vc, ks, vs, p_bf)


def _log_sigmoid(x):
    return jnp.minimum(x, 0.0) - jnp.log1p(jnp.exp(-jnp.abs(x)))


def _pad_rows(a, rows):
    if a.shape[0] == rows:
        return a
    return jnp.concatenate([a, jnp.zeros((rows - a.shape[0], a.shape[1]), a.dtype)], axis=0)


def _mlstm_kernel(q_ref, k_ref, v_ref, mo_ref, zb_ref, g_ref, gb_ref, gain_ref, c0_ref, n0_ref, m0_ref,
                  h_ref, c_ref, n_ref, m_ref, *, n_heads, lc, lp, dk, dv):
    @pl.when(pl.program_id(1) == 0)
    def _():
        c_ref[...] = c0_ref[...]
        n_ref[...] = n0_ref[...]
        m_ref[...] = m0_ref[...]

    hi = lax.Precision.HIGHEST
    gates = _pad_rows(g_ref[...] + gb_ref[...], lp)
    gates_t = gates.T
    t_col = lax.broadcasted_iota(jnp.int32, (lp, 1), 0)
    t_row = lax.broadcasted_iota(jnp.int32, (1, lp), 1)
    r_idx = lax.broadcasted_iota(jnp.int32, (lp, lp), 0)
    c_idx = lax.broadcasted_iota(jnp.int32, (lp, lp), 1)
    causal = r_idx >= c_idx
    lower = causal.astype(F32)
    upper = (c_idx >= r_idx).astype(F32)
    b_cols = jnp.dot(lower, jnp.where(t_col < lc, _log_sigmoid(gates), 0.0), precision=hi,
                     preferred_element_type=F32)
    b_rows = jnp.dot(jnp.where(t_row < lc, _log_sigmoid(gates_t), 0.0), upper, precision=hi,
                     preferred_element_type=F32)

    for h in range(n_heads):
        li = QK_ROPE + h
        lf_lane = QK_ROPE + n_heads + h
        ig_col = jnp.where(t_col < lc, gates[:, li:li + 1], NEG_BIG)
        ig_row = jnp.where(t_row < lc, gates_t[li:li + 1, :], NEG_BIG)
        b_col = b_cols[:, lf_lane:lf_lane + 1]
        b_row = b_rows[lf_lane:lf_lane + 1, :]
        m0 = m_ref[0, 0:1, h:h + 1]
        d = jnp.where(causal, b_col - b_row + ig_row, -jnp.inf)
        a_col = b_col + m0
        m = jnp.maximum(a_col, jnp.max(d, axis=-1, keepdims=True))
        w_inter = jnp.exp(a_col - m)

        q = _pad_rows(q_ref[:, h * dk:(h + 1) * dk], lp)
        k = _pad_rows(k_ref[:, h * dk:(h + 1) * dk], lp) * (dk ** -0.5)
        v = _pad_rows(v_ref[:, h * dv:(h + 1) * dv], lp)
        qk = _dot_nt(q, k) * jnp.exp(d - m)
        c_old = c_ref[0, h]
        n_old = n_ref[0, h:h + 1, :]
        num = w_inter * _dot(q, c_old.astype(BF16)) + _dot(qk.astype(BF16), v)
        den = (w_inter * jnp.sum(q.astype(F32) * n_old, axis=-1, keepdims=True)
               + jnp.sum(qk, axis=-1, keepdims=True))
        hid = num / jnp.maximum(jnp.abs(den), jnp.exp(-m))

        b_last = b_col[lp - 1:lp, :]
        g_row = b_last - b_row + ig_row
        g_col = b_last - b_col + ig_col
        m_new = jnp.maximum(b_last + m0, jnp.max(g_row, axis=-1, keepdims=True))
        decay = jnp.exp(b_last + m0 - m_new)
        kw = k.astype(F32) * jnp.exp(g_col - m_new)
        c_ref[0, h] = decay * c_old + _dot(kw.T.astype(BF16), v)
        n_ref[0, h:h + 1, :] = decay * n_old + jnp.sum(kw, axis=0, keepdims=True)
        m_ref[0, 0:1, h:h + 1] = m_new

        hid = hid[:lc]
        mo = mo_ref[:, h * dv:(h + 1) * dv].astype(F32)
        zb = zb_ref[:, h * dv:(h + 1) * dv].astype(F32)
        out = jax.nn.sigmoid(mo) * (_rms(hid, dv) * gain_ref[:, h * dv:(h + 1) * dv])
        h_ref[:, h * dv:(h + 1) * dv] = (out * (zb * jax.nn.sigmoid(zb))).astype(h_ref.dtype)


def _mlstm(p_bf, p_f32, offs, gate_bias, gain, c0, n0, m0, batch, seq, lc, row0, share_state, n_heads, dk, dv):
    lp = -(-lc // LANE) * LANE
    nch = seq // lc
    blk0 = row0 // lc
    wqk = n_heads * dk
    wv = n_heads * dv
    gate_blk = p_f32.shape[1] // LANE - 1
    rows = lambda col: (lambda b, c: (blk0 + b * nch + c, col))
    state = (lambda b, c: (0, 0, 0, 0)) if share_state else (lambda b, c: (b, 0, 0, 0))
    state3 = (lambda b, c: (0, 0, 0)) if share_state else (lambda b, c: (b, 0, 0))
    out_rows = batch * seq
    return pl.pallas_call(
        functools.partial(_mlstm_kernel, n_heads=n_heads, lc=lc, lp=lp, dk=dk, dv=dv),
        grid=(batch, nch),
        in_specs=[
            pl.BlockSpec((lc, wqk), rows(offs["mq"] // wqk)),
            pl.BlockSpec((lc, wqk), rows(offs["mk"] // wqk)),
            pl.BlockSpec((lc, wv), rows(offs["mv"] // wv)),
            pl.BlockSpec((lc, wv), rows(offs["mo"] // wv)),
            pl.BlockSpec((lc, wv), rows(offs["zb"] // wv)),
            pl.BlockSpec((lc, LANE), rows(gate_blk)),
            pl.BlockSpec((1, LANE), lambda b, c: (0, 0)),
            pl.BlockSpec((1, wv), lambda b, c: (0, 0)),
            pl.BlockSpec((1, n_heads, dk, dv), state),
            pl.BlockSpec((1, n_heads, dk), state3),
            pl.BlockSpec((1, 1, n_heads), state3),
        ],
        out_specs=[
            pl.BlockSpec((lc, wv), lambda b, c: (b * nch + c, 0)),
            pl.BlockSpec((1, n_heads, dk, dv), lambda b, c: (b, 0, 0, 0)),
            pl.BlockSpec((1, n_heads, dk), lambda b, c: (b, 0, 0)),
            pl.BlockSpec((1, 1, n_heads), lambda b, c: (b, 0, 0)),
        ],
        out_shape=[
            jax.ShapeDtypeStruct((out_rows, wv), BF16),
            jax.ShapeDtypeStruct((batch, n_heads, dk, dv), F32),
            jax.ShapeDtypeStruct((batch, n_heads, dk), F32),
            jax.ShapeDtypeStruct((batch, 1, n_heads), F32),
        ],
        compiler_params=_cparams(("parallel", "arbitrary")),
        name="mlstm",
    )(p_bf, p_bf, p_bf, p_bf, p_bf, p_f32, gate_bias, gain, c0, n0, m0)


def _out_proj_kernel(a_ref, m_ref, wa_ref, wm_ref, x_ref, o_ref):
    o_ref[...] = x_ref[...] + _dot(a_ref[...], wa_ref[...]) + _dot(m_ref[...], wm_ref[...])


def _out_proj(a, ml, w_out, x, tm, tn):
    rows, wa = a.shape
    wm = ml.shape[1]
    d = w_out.shape[1]
    assert wa == wm
    return pl.pallas_call(
        _out_proj_kernel,
        grid=(rows // tm, d // tn),
        in_specs=[
            pl.BlockSpec((tm, wa), lambda i, j: (i, 0)),
            pl.BlockSpec((tm, wm), lambda i, j: (i, 0)),
            pl.BlockSpec((wa, tn), lambda i, j: (0, j)),
            pl.BlockSpec((wm, tn), lambda i, j: (1, j)),
            pl.BlockSpec((tm, tn), lambda i, j: (i, j)),
        ],
        out_specs=pl.BlockSpec((tm, tn), lambda i, j: (i, j)),
        out_shape=jax.ShapeDtypeStruct((rows, d), F32),
        compiler_params=_cparams(("parallel", "arbitrary")),
        name="out_proj",
    )(a, ml, w_out, w_out, x)


def _rope_tables(pos):
    inv_freq = ROPE_THETA ** (-jnp.arange(HALF_ROPE, dtype=F32) / HALF_ROPE)
    ang = pos.astype(F32)[:, None] * inv_freq[None, :]
    cos, sin = jnp.cos(ang), jnp.sin(ang)
    zero = jnp.zeros((pos.shape[0], LANE - QK_ROPE), F32)
    return jnp.concatenate([cos, cos, zero], axis=1), jnp.concatenate([-sin, sin, zero], axis=1)


def _pad_lanes(vec):
    return jnp.pad(vec.astype(F32), (0, LANE - vec.shape[0]))[None, :]


def _row_tile(rows, target):
    t = min(rows, target)
    while rows % t:
        t //= 2
    return t


def kernel(x_prompt, x_sample, cache_ckv, cache_kpe, state_C, state_n, state_m, meta_tokens, norm_gain, w_in,
           b_igate, b_fgate, q_a_norm, w_q_up, q_nope_norm, q_rope_norm, kv_a_norm, k_rope_norm, w_kv_up,
           k_nope_norm, ml_out_norm, w_out):
    batch, seq, d = x_prompt.shape
    dbatch, dseq, _ = x_sample.shape
    past = cache_ckv.shape[1]
    n_meta = meta_tokens.shape[0]
    q_lora = q_a_norm.shape[0]
    kv_lora = kv_a_norm.shape[0]
    ml_heads, dv = ml_out_norm.shape
    dk = state_n.shape[-1]
    mla_heads = w_kv_up.shape[1] // (QK_NOPE + V_HEAD)
    mla_w = mla_heads * V_HEAD
    ml_w = ml_heads * dv
    mqk_w = ml_heads * dk
    assert q_lora % COL_TILE == 0 and kv_lora % LANE == 0 and 2 * ml_heads <= LANE - QK_ROPE
    assert mla_w == ml_w and w_out.shape[0] == mla_w + ml_w

    o_cq = 0
    o_ckv = o_cq + q_lora
    o_kpe = o_ckv + kv_lora
    o_mq = o_kpe + QK_ROPE
    o_mk = o_mq + mqk_w
    o_mv = o_mk + mqk_w
    o_mo = o_mv + ml_w
    o_mi = o_mo + ml_w
    o_mf = o_mi + ml_heads
    o_za = o_mf + ml_heads
    o_zb = o_za + mla_w

    offs = {"mq": 0, "mk": mqk_w, "mv": 2 * mqk_w, "mo": 2 * mqk_w + ml_w, "za": 2 * mqk_w + 2 * ml_w,
            "zb": 2 * mqk_w + 2 * ml_w + mla_w, "cq": 2 * mqk_w + 3 * ml_w + mla_w}
    gate_pad = LANE - QK_ROPE - 2 * ml_heads
    w_main, w_small = _regroup_rows(
        w_in.T,
        [(o_mq, mqk_w, offs["mq"]), (o_mk, mqk_w, offs["mk"]), (o_mv, ml_w, offs["mv"]), (o_mo, ml_w, offs["mo"]),
         (o_za, mla_w, offs["za"]), (o_zb, ml_w, offs["zb"]), (o_cq, q_lora, offs["cq"])], offs["cq"] + q_lora,
        [(o_ckv, kv_lora), (o_kpe, QK_ROPE), (o_mi, 2 * ml_heads)], kv_lora + LANE, _row_tile(d, 256))
    gate_bias = jnp.concatenate([jnp.zeros((QK_ROPE,), F32), b_igate.astype(F32), b_fgate.astype(F32),
                                 jnp.zeros((gate_pad,), F32)])[None, :]
    qk_w = QK_NOPE + QK_ROPE
    kv_w = QK_NOPE + V_HEAD
    wq = _regroup(w_q_up, [(h * qk_w, qk_w, h * QK_PAD) for h in range(mla_heads)], mla_heads * QK_PAD,
                  _row_tile(q_lora, 512))
    wk = _regroup(w_kv_up, [(h * kv_w, QK_NOPE, h * QK_NOPE) for h in range(mla_heads)], mla_heads * QK_NOPE,
                  kv_lora)
    wv = _regroup(w_kv_up, [(h * kv_w + QK_NOPE, V_HEAD, h * V_HEAD) for h in range(mla_heads)],
                  mla_heads * V_HEAD, kv_lora)
    w_out_bf = w_out.astype(BF16)
    gain_row = norm_gain.astype(F32)[None, :]
    qan = q_a_norm.astype(F32)[None, :]
    kvg = kv_a_norm.astype(F32)[None, :]
    qng = q_nope_norm.astype(F32)[None, :]
    kng = k_nope_norm.astype(F32)[None, :]
    qrg = _pad_lanes(q_rope_norm)
    krg = _pad_lanes(k_rope_norm)
    ml_gain = ml_out_norm.astype(F32).reshape(1, ml_w)

    n_p = batch * seq
    n_s = dbatch * dseq
    assert n_s % LANE == 0 and n_meta <= LANE
    small_rows = n_s + LANE
    xp = x_prompt.reshape(n_p, d)
    xs = jnp.concatenate([x_sample.reshape(n_s, d), meta_tokens.astype(x_sample.dtype),
                          jnp.zeros((small_rows - n_s - n_meta, d), x_sample.dtype)], axis=0)
    meta_blk = n_s // LANE

    pos_p = jnp.tile(n_meta + jnp.arange(seq), batch)
    pos_s = jnp.concatenate([jnp.tile(n_meta + past + jnp.arange(dseq), dbatch), jnp.arange(n_meta),
                             jnp.zeros((small_rows - n_s - n_meta,), jnp.int32)])
    cos_p, sin_p = _rope_tables(pos_p)
    cos_s, sin_s = _rope_tables(pos_s)

    tm_p = _row_tile(n_p, 512)
    pbf_p, pf_p = _proj(xp, gain_row, w_main, w_small, tm_p)
    pbf_s, pf_s = _proj(xs, gain_row, w_main, w_small, small_rows)

    hps = max(1, mla_heads // 2)
    q_p, ckv_p, kpe_p, kpep_p = _latent_q(pbf_p, pf_p, offs["cq"], q_lora, kv_lora, cos_p, sin_p, qan, wq,
                                          qng, qrg, kvg, krg, tm_p, hps)
    q_s, ckv_s, kpe_s, kpep_s = _latent_q(pbf_s, pf_s, offs["cq"], q_lora, kv_lora, cos_s, sin_s, qan, wq,
                                          qng, qrg, kvg, krg, small_rows, hps)

    k_p, v_p = _kv_expand(ckv_p, kpep_p, wk, wv, kng, tm_p)
    k_s, v_s = _kv_expand(ckv_s, kpep_s, wk, wv, kng, small_rows)
    cache_rows = dbatch * past
    cache_kpep = jnp.pad(cache_kpe.reshape(cache_rows, QK_ROPE), ((0, 0), (0, LANE - QK_ROPE))).astype(BF16)
    k_c, v_c = _kv_expand(cache_ckv.reshape(cache_rows, kv_lora).astype(F32), cache_kpep, wk, wv, kng,
                          _row_tile(cache_rows, 512))

    tq = _row_tile(seq, ATTN_TQ)
    attn_p = _attn_prompt(q_p, k_p, v_p, k_s, v_s, meta_blk, pbf_p, offs["za"], batch, seq, tq,
                          min(ATTN_HEADS, mla_heads), n_meta)
    attn_s = _attn_sample(q_s, k_c, v_c, k_s, v_s, pbf_s, offs["za"], dbatch, dseq, past, n_meta,
                          min(4, mla_heads))

    zc = jnp.zeros((1, ml_heads, dk, dv), F32)
    zn = jnp.zeros((1, ml_heads, dk), F32)
    zm = jnp.zeros((1, 1, ml_heads), F32)
    _, c_m, n_m, m_m = _mlstm(pbf_s, pf_s, offs, gate_bias, ml_gain, zc, zn, zm, 1, n_meta, n_meta, n_s, True,
                              ml_heads, dk, dv)
    lc_p = _row_tile(seq, 256)
    ml_p, c_p, n_pr, m_p = _mlstm(pbf_p, pf_p, offs, gate_bias, ml_gain, c_m, n_m, m_m, batch, seq, lc_p, 0,
                                  True, ml_heads, dk, dv)
    ml_s, c_s, n_sm, m_s = _mlstm(pbf_s, pf_s, offs, gate_bias, ml_gain, state_C.astype(F32),
                                  state_n.astype(F32), state_m.astype(F32).reshape(dbatch, 1, ml_heads),
                                  dbatch, dseq, dseq, 0, False, ml_heads, dk, dv)

    y_p = _out_proj(attn_p, ml_p, w_out_bf, xp, _row_tile(n_p, 512), _row_tile(d, 1024))
    y_s = _out_proj(attn_s, ml_s, w_out_bf, xs, n_s, _row_tile(d, 1024))

    meta_ckv = ckv_s[n_s:n_s + n_meta]
    meta_kpe = kpe_s[n_s:n_s + n_meta]
    ckv_prompt = jnp.concatenate([jnp.broadcast_to(meta_ckv[None], (batch, n_meta, kv_lora)),
                                  ckv_p.reshape(batch, seq, kv_lora)], axis=1)
    kpe_prompt = jnp.concatenate([jnp.broadcast_to(meta_kpe[None], (batch, n_meta, QK_ROPE)),
                                  kpe_p.reshape(batch, seq, QK_ROPE)], axis=1)
    return (y_p.reshape(batch, seq, d), y_s.reshape(dbatch, dseq, d), ckv_prompt, kpe_prompt,
            c_p, n_pr, m_p.reshape(batch, ml_heads),
            ckv_s[:n_s].reshape(dbatch, dseq, kv_lora), kpe_s[:n_s].reshape(dbatch, dseq, QK_ROPE),
            c_s, n_sm, m_s.reshape(dbatch, ml_heads))
```

```python
import functools

import jax
import jax.numpy as jnp
from jax import lax
from jax.experimental import pallas as pl
from jax.experimental.pallas import tpu as pltpu

CHUNK = 64
EPS = 1e-6
ROPE_THETA = 10000.0
V_HEAD = 128
QK_NOPE = 128
QK_ROPE = 64
HALF_ROPE = QK_ROPE // 2
LANE = 128
QK_PAD = QK_NOPE + LANE
COL_TILE = 512
NEG_BIG = -1e30
ATTN_TQ = 512
ATTN_HEADS = 4
VMEM_LIMIT = 56 * 1024 * 1024

F32 = jnp.float32
BF16 = jnp.bfloat16


def _cparams(sem):
    return pltpu.CompilerParams(dimension_semantics=sem, vmem_limit_bytes=VMEM_LIMIT)


def _dot(a, b):
    return jnp.dot(a, b, preferred_element_type=F32)


def _dot_nt(a, b):
    return lax.dot_general(a, b, (((1,), (1,)), ((), ())), preferred_element_type=F32)


def _rms(x, n):
    return x * lax.rsqrt(jnp.sum(x * x, axis=-1, keepdims=True) * (1.0 / n) + EPS)


def _rope_pad(x, cos, sin_signed):
    lane = lax.broadcasted_iota(jnp.int32, x.shape, 1)
    rot = jnp.where(lane < HALF_ROPE, pltpu.roll(x, LANE - HALF_ROPE, 1), pltpu.roll(x, HALF_ROPE, 1))
    return x * cos + rot * sin_signed


def _regroup_kernel(w_ref, o_ref, *, segments, zero_ranges):
    for src, n, dst in segments:
        o_ref[:, dst:dst + n] = w_ref[:, src:src + n].astype(o_ref.dtype)
    for dst, n in zero_ranges:
        o_ref[:, dst:dst + n] = jnp.zeros((o_ref.shape[0], n), o_ref.dtype)


def _regroup(w, segments, out_cols, tr):
    rows, cols = w.shape
    covered = sorted((dst, n) for _, n, dst in segments)
    zero_ranges, pos = [], 0
    for dst, n in covered + [(out_cols, 0)]:
        if dst > pos:
            zero_ranges.append((pos, dst - pos))
        pos = dst + n
    return pl.pallas_call(
        functools.partial(_regroup_kernel, segments=tuple(segments), zero_ranges=tuple(zero_ranges)),
        grid=(rows // tr,),
        in_specs=[pl.BlockSpec((tr, cols), lambda i: (i, 0))],
        out_specs=pl.BlockSpec((tr, out_cols), lambda i: (i, 0)),
        out_shape=jax.ShapeDtypeStruct((rows, out_cols), BF16),
        compiler_params=_cparams(("parallel",)),
        name="regroup",
    )(w)


def _regroup_rows_kernel(w_ref, om_ref, os_ref, *, main_segments, small_segments):
    for src, n, dst in main_segments:
        om_ref[dst:dst + n, :] = w_ref[src:src + n, :].astype(om_ref.dtype)
    parts = [w_ref[src:src + n, :] for src, n in small_segments]
    used = sum(n for _, n in small_segments)
    parts.append(jnp.zeros((os_ref.shape[0] - used, w_ref.shape[1]), w_ref.dtype))
    os_ref[...] = jnp.concatenate(parts, axis=0).astype(os_ref.dtype)


def _regroup_rows(wt, main_segments, main_rows, small_segments, small_rows, tc):
    rows, cols = wt.shape
    return pl.pallas_call(
        functools.partial(_regroup_rows_kernel, main_segments=tuple(main_segments),
                          small_segments=tuple(small_segments)),
        grid=(cols // tc,),
        in_specs=[pl.BlockSpec((rows, tc), lambda i: (0, i))],
        out_specs=[pl.BlockSpec((main_rows, tc), lambda i: (0, i)), pl.BlockSpec((small_rows, tc), lambda i: (0, i))],
        out_shape=[jax.ShapeDtypeStruct((main_rows, cols), BF16), jax.ShapeDtypeStruct((small_rows, cols), BF16)],
        compiler_params=_cparams(("parallel",)),
        name="regroup_rows",
    )(wt)


def _proj_kernel(x_ref, g_ref, w_ref, ws_ref, o_ref, os_ref, hn_ref, *, n_sub):
    j = pl.program_id(1)
    ts = x_ref.shape[0]

    @pl.when(j < n_sub)
    def _():
        x = x_ref[...]
        hn = (_rms(x, x.shape[-1]) * g_ref[...]).astype(BF16)
        hn_ref[pl.ds(pl.multiple_of(j * ts, ts), ts), :] = hn
        os_ref[...] = _dot_nt(hn, ws_ref[...])

    @pl.when(j >= n_sub)
    def _():
        o_ref[...] = _dot_nt(hn_ref[...], w_ref[...]).astype(o_ref.dtype)


def _proj(x, gain, w_main, w_small, tm, n_sub):
    m, d = x.shape
    n = w_main.shape[0]
    ns = w_small.shape[0]
    ts = tm // n_sub
    sub = lambda i, j: (i * n_sub + jnp.minimum(j, n_sub - 1), 0)
    feat = lambda j: jnp.maximum(j - n_sub, 0)
    return pl.pallas_call(
        functools.partial(_proj_kernel, n_sub=n_sub),
        grid=(m // tm, n_sub + n // COL_TILE),
        in_specs=[
            pl.BlockSpec((ts, d), sub),
            pl.BlockSpec((1, d), lambda i, j: (0, 0)),
            pl.BlockSpec((COL_TILE, d), lambda i, j: (feat(j), 0)),
            pl.BlockSpec((ns, d), lambda i, j: (0, 0)),
        ],
        out_specs=[
            pl.BlockSpec((tm, COL_TILE), lambda i, j: (i, feat(j))),
            pl.BlockSpec((ts, ns), sub),
        ],
        out_shape=[jax.ShapeDtypeStruct((m, n), BF16), jax.ShapeDtypeStruct((m, ns), F32)],
        scratch_shapes=[pltpu.VMEM((tm, d), BF16)],
        compiler_params=_cparams(("parallel", "arbitrary")),
        name="proj",
    )(x, gain, w_main, w_small)


def _latent_q_kernel(*refs, n_cq, kv_lora, heads_per_step):
    cq_refs = refs[:n_cq]
    (pf_ref, cos_ref, sin_ref, qan_ref, wq_ref, qng_ref, qrg_ref, kvg_ref, krg_ref,
     q_ref, ckv_ref, kpe_ref, kpep_ref, cqn_ref) = refs[n_cq:]
    cos = cos_ref[...]
    sin = sin_ref[...]

    @pl.when(pl.program_id(1) == 0)
    def _():
        cq = jnp.concatenate([r[...].astype(F32) for r in cq_refs], axis=1)
        cqn_ref[...] = (_rms(cq, cq.shape[-1]) * qan_ref[...]).astype(BF16)
        pf = pf_ref[...]
        ckv_raw = pf[:, :kv_lora]
        ckv_ref[...] = _rms(ckv_raw, kv_lora) * kvg_ref[...]
        tail = pf[:, kv_lora:kv_lora + LANE]
        lane = lax.broadcasted_iota(jnp.int32, tail.shape, 1)
        kraw = jnp.where(lane < QK_ROPE, tail, 0.0)
        kpe = _rope_pad(_rms(kraw, QK_ROPE) * krg_ref[...], cos, sin)
        kpe_ref[...] = kpe[:, :QK_ROPE]
        kpep_ref[...] = kpe.astype(BF16)

    qf = _dot(cqn_ref[...], wq_ref[...])
    for h in range(heads_per_step):
        base = h * QK_PAD
        nope = qf[:, base:base + QK_NOPE]
        pe = qf[:, base + QK_NOPE:base + QK_PAD]
        q_ref[:, base:base + QK_NOPE] = (_rms(nope, QK_NOPE) * qng_ref[...]).astype(BF16)
        q_ref[:, base + QK_NOPE:base + QK_PAD] = _rope_pad(_rms(pe, QK_ROPE) * qrg_ref[...], cos, sin).astype(BF16)


def _latent_q(p_bf, p_f32, cq_off, q_lora, kv_lora, cos, sin, qan, wq, qng, qrg, kvg, krg, tm, heads_per_step):
    m = p_bf.shape[0]
    n_heads = wq.shape[1] // QK_PAD
    n_cq = q_lora // COL_TILE
    cq0 = cq_off // COL_TILE
    ns = p_f32.shape[1]
    wstep = heads_per_step * QK_PAD
    row = lambda i, j: (i, 0)
    const = lambda i, j: (0, 0)
    pos_blocks = cos.shape[0] // tm
    pos = lambda i, j: (i % pos_blocks, 0)
    in_specs = [pl.BlockSpec((tm, COL_TILE), functools.partial(lambda i, j, c: (i, c), c=cq0 + c))
                for c in range(n_cq)]
    in_specs += [
        pl.BlockSpec((tm, ns), row),
        pl.BlockSpec((tm, LANE), pos),
        pl.BlockSpec((tm, LANE), pos),
        pl.BlockSpec((1, q_lora), const),
        pl.BlockSpec((q_lora, wstep), lambda i, j: (0, j)),
        pl.BlockSpec((1, LANE), const),
        pl.BlockSpec((1, LANE), const),
        pl.BlockSpec((1, kv_lora), const),
        pl.BlockSpec((1, LANE), const),
    ]
    return pl.pallas_call(
        functools.partial(_latent_q_kernel, n_cq=n_cq, kv_lora=kv_lora, heads_per_step=heads_per_step),
        grid=(m // tm, n_heads // heads_per_step),
        in_specs=in_specs,
        out_specs=[
            pl.BlockSpec((tm, wstep), lambda i, j: (i, j)),
            pl.BlockSpec((tm, kv_lora), row),
            pl.BlockSpec((tm, QK_ROPE), row),
            pl.BlockSpec((tm, LANE), row),
        ],
        out_shape=[
            jax.ShapeDtypeStruct((m, n_heads * QK_PAD), BF16),
            jax.ShapeDtypeStruct((m, kv_lora), F32),
            jax.ShapeDtypeStruct((m, QK_ROPE), F32),
            jax.ShapeDtypeStruct((m, LANE), BF16),
        ],
        scratch_shapes=[pltpu.VMEM((tm, q_lora), BF16)],
        compiler_params=_cparams(("parallel", "arbitrary")),
        name="latent_q",
    )(*([p_bf] * n_cq), p_f32, cos, sin, qan, wq, qng, qrg, kvg, krg)


def _kv_expand_kernel(ckv_ref, kpep_ref, wk_ref, wv_ref, kng_ref, k_ref, v_ref, *, n_heads):
    c = ckv_ref[...].astype(BF16)
    kf = _dot(c, wk_ref[...])
    v_ref[...] = _dot(c, wv_ref[...]).astype(BF16)
    kpep = kpep_ref[...]
    for h in range(n_heads):
        kn = _rms(kf[:, h * QK_NOPE:(h + 1) * QK_NOPE], QK_NOPE) * kng_ref[...]
        k_ref[:, h * QK_PAD:h * QK_PAD + QK_NOPE] = kn.astype(BF16)
        k_ref[:, h * QK_PAD + QK_NOPE:(h + 1) * QK_PAD] = kpep


def _kv_expand(ckv, kpep, wk, wv, kng, tm):
    r, kv_lora = ckv.shape
    n_heads = wk.shape[1] // QK_NOPE
    row = lambda i: (i, 0)
    const = lambda i: (0, 0)
    return pl.pallas_call(
        functools.partial(_kv_expand_kernel, n_heads=n_heads),
        grid=(r // tm,),
        in_specs=[
            pl.BlockSpec((tm, kv_lora), row),
            pl.BlockSpec((tm, LANE), row),
            pl.BlockSpec((kv_lora, n_heads * QK_NOPE), const),
            pl.BlockSpec((kv_lora, n_heads * V_HEAD), const),
            pl.BlockSpec((1, LANE), const),
        ],
        out_specs=[
            pl.BlockSpec((tm, n_heads * QK_PAD), row),
            pl.BlockSpec((tm, n_heads * V_HEAD), row),
        ],
        out_shape=[
            jax.ShapeDtypeStruct((r, n_heads * QK_PAD), BF16),
            jax.ShapeDtypeStruct((r, n_heads * V_HEAD), BF16),
        ],
        compiler_params=_cparams(("parallel",)),
        name="kv_expand",
    )(ckv, kpep, wk, wv, kng)


def _lane_chunk_sum(p):
    out = p[:, :LANE]
    for c in range(1, p.shape[1] // LANE):
        out = out + p[:, c * LANE:(c + 1) * LANE]
    return out


def _gated(lvec, acc, za):
    za = za.astype(F32)
    return acc / jnp.sum(lvec, axis=-1, keepdims=True) * (za * jax.nn.sigmoid(za))


def _softmax_scale():
    return float((QK_NOPE + QK_ROPE) ** -0.5 * 1.4426950408889634)


def _attn_prompt_block(n_full, q_ref, k_ref, km_ref, za_ref, o_ref, s_ref, p_ref, va_ref, vma_ref, *,
                       tq, heads, c, n_meta):
    td = min(tq, 256)
    pieces = [(i * tq, tq, 0, False) for i in range(n_full)]
    pieces += [(n_full * tq + j * td, td, j * td, True) for j in range(tq // td)]

    def upd(full, r0, fn):
        return fn(full) if r0 == 0 else jnp.concatenate([full[:r0], fn(full[r0:])], axis=0)

    def scores(h):
        qs = slice(h * QK_PAD, (h + 1) * QK_PAD)
        q = q_ref[:, qs]
        col = lax.broadcasted_iota(jnp.int32, (tq, LANE), 1)
        t = jnp.where(col < n_meta, _dot_nt(q, km_ref[:, qs]) * c, -jnp.inf)
        s_ref[h, :, :LANE] = t
        mx = t
        off = LANE
        for start, rows, r0, masked in pieces:
            t = _dot_nt(q[r0:], k_ref[start:start + rows, qs]) * c
            if masked:
                rq = lax.broadcasted_iota(jnp.int32, t.shape, 0) // CHUNK
                ck = lax.broadcasted_iota(jnp.int32, t.shape, 1) // CHUNK
                t = jnp.where(rq >= ck, t, -jnp.inf)
            s_ref[h, r0:, off:off + rows] = t
            tmax = t[:, :LANE]
            for ch in range(1, rows // LANE):
                tmax = jnp.maximum(tmax, t[:, ch * LANE:(ch + 1) * LANE])
            mx = upd(mx, r0, lambda a: jnp.maximum(a, tmax))
            off += rows
        return jnp.max(mx, axis=-1, keepdims=True)

    n_keys = (n_full + 1) * tq

    def values(h, m):
        vs = slice(h * V_HEAD, (h + 1) * V_HEAD)
        va = slice(2 * h * V_HEAD, 2 * (h + 1) * V_HEAD)
        p_ref[h, :, :LANE] = jnp.exp2(s_ref[h, :, :LANE] - m).astype(BF16)
        off = LANE
        for start, rows, r0, _ in pieces:
            if r0:
                p_ref[h, :r0, off:off + rows] = jnp.zeros((r0, rows), BF16)
            p_ref[h, r0:, off:off + rows] = jnp.exp2(s_ref[h, r0:, off:off + rows] - m[r0:]).astype(BF16)
            off += rows
        acc = (_dot(p_ref[h, :, :LANE], vma_ref[:, va])
               + _dot(p_ref[h, :, LANE:LANE + n_keys], va_ref[:n_keys, va]))
        za = za_ref[:, vs].astype(F32)
        out = acc[:, :V_HEAD] / acc[:, V_HEAD:V_HEAD + 1] * (za * jax.nn.sigmoid(za))
        o_ref[:, vs] = out.astype(o_ref.dtype)

    maxes = [scores(h) for h in range(heads)]
    for h in range(heads):
        values(h, maxes[h])


def _attn_prompt_kernel(q_ref, k_ref, v_ref, km_ref, vm_ref, za_ref, o_ref, s_ref, p_ref, va_ref, vma_ref, *,
                        nqb, heads, **kw):
    qi = pl.program_id(2)

    @pl.when(qi == 0)
    def _():
        for h in range(heads):
            vs = slice(h * V_HEAD, (h + 1) * V_HEAD)
            va_ref[:, 2 * h * V_HEAD:(2 * h + 1) * V_HEAD] = v_ref[:, vs]
            va_ref[:, (2 * h + 1) * V_HEAD:(2 * h + 2) * V_HEAD] = jnp.ones((va_ref.shape[0], V_HEAD), BF16)
            vma_ref[:, 2 * h * V_HEAD:(2 * h + 1) * V_HEAD] = vm_ref[:, vs]
            vma_ref[:, (2 * h + 1) * V_HEAD:(2 * h + 2) * V_HEAD] = jnp.ones((vma_ref.shape[0], V_HEAD), BF16)

    for n_full in range(nqb):
        pl.when(qi == n_full)(
            functools.partial(_attn_prompt_block, n_full, q_ref, k_ref, km_ref, za_ref, o_ref, s_ref, p_ref, va_ref,
                              vma_ref, heads=heads, **kw))


def _attn_prompt(q, k, v, k_small, v_small, meta_blk, p_bf, za_off, batch, seq, tq, heads, n_meta):
    n_heads = q.shape[1] // QK_PAD
    nqb = seq // tq
    za0 = za_off // (V_HEAD * heads)
    assert za_off % (V_HEAD * heads) == 0 and n_heads % heads == 0 and tq % CHUNK == 0
    return pl.pallas_call(
        functools.partial(_attn_prompt_kernel, nqb=nqb, tq=tq, heads=heads, c=_softmax_scale(), n_meta=n_meta),
        grid=(batch, n_heads // heads, nqb),
        in_specs=[
            pl.BlockSpec((tq, heads * QK_PAD), lambda b, h, i: (b * nqb + i, h)),
            pl.BlockSpec((seq, heads * QK_PAD), lambda b, h, i: (b, h)),
            pl.BlockSpec((seq, heads * V_HEAD), lambda b, h, i: (b, h)),
            pl.BlockSpec((LANE, heads * QK_PAD), lambda b, h, i: (meta_blk, h)),
            pl.BlockSpec((LANE, heads * V_HEAD), lambda b, h, i: (meta_blk, h)),
            pl.BlockSpec((tq, heads * V_HEAD), lambda b, h, i: (b * nqb + i, za0 + h)),
        ],
        out_specs=pl.BlockSpec((tq, heads * V_HEAD), lambda b, h, i: (b * nqb + i, h)),
        out_shape=jax.ShapeDtypeStruct((batch * seq, n_heads * V_HEAD), BF16),
        scratch_shapes=[pltpu.VMEM((heads, tq, LANE + seq), F32), pltpu.VMEM((heads, tq, LANE + seq), BF16),
                        pltpu.VMEM((seq, 2 * heads * V_HEAD), BF16), pltpu.VMEM((LANE, 2 * heads * V_HEAD), BF16)],
        compiler_params=_cparams(("parallel", "parallel", "arbitrary")),
        name="attn_prompt",
    )(q, k, v, k_small, v_small, p_bf)


def _attn_sample_kernel(q_ref, kc_ref, vc_ref, ks_ref, vs_ref, za_ref, o_ref, *, heads, lq, c, n_meta, meta_row):
    b = pl.program_id(0)
    col = lax.broadcasted_iota(jnp.int32, (lq, ks_ref.shape[0]), 1)
    own = (col >= b * lq) & (col < (b + 1) * lq)
    meta = (col >= meta_row) & (col < meta_row + n_meta)
    visible = own | meta
    for h in range(heads):
        qs = slice(h * QK_PAD, (h + 1) * QK_PAD)
        vs = slice(h * V_HEAD, (h + 1) * V_HEAD)
        q = q_ref[:, qs]
        t_cache = _dot_nt(q, kc_ref[:, qs]) * c
        t_new = jnp.where(visible, _dot_nt(q, ks_ref[:, qs]) * c, -jnp.inf)
        m = jnp.maximum(jnp.max(t_cache, axis=-1, keepdims=True), jnp.max(t_new, axis=-1, keepdims=True))
        p_cache = jnp.exp2(t_cache - m)
        p_new = jnp.exp2(t_new - m)
        lvec = _lane_chunk_sum(p_cache) + _lane_chunk_sum(p_new)
        acc = _dot(p_cache.astype(BF16), vc_ref[:, vs]) + _dot(p_new.astype(BF16), vs_ref[:, vs])
        o_ref[:, vs] = _gated(lvec, acc, za_ref[:, vs]).astype(o_ref.dtype)


def _attn_sample(q, kc, vc, ks, vs, p_bf, za_off, batch, lq, past, n_meta, heads):
    n_heads = q.shape[1] // QK_PAD
    rows = ks.shape[0]
    za0 = za_off // (V_HEAD * heads)
    assert za_off % (V_HEAD * heads) == 0 and n_heads % heads == 0
    return pl.pallas_call(
        functools.partial(_attn_sample_kernel, heads=heads, lq=lq, c=_softmax_scale(), n_meta=n_meta,
                          meta_row=batch * lq),
        grid=(batch, n_heads // heads),
        in_specs=[
            pl.BlockSpec((lq, heads * QK_PAD), lambda b, h: (b, h)),
            pl.BlockSpec((past, heads * QK_PAD), lambda b, h: (b, h)),
            pl.BlockSpec((past, heads * V_HEAD), lambda b, h: (b, h)),
            pl.BlockSpec((rows, heads * QK_PAD), lambda b, h: (0, h)),
            pl.BlockSpec((rows, heads * V_HEAD), lambda b, h: (0, h)),
            pl.BlockSpec((lq, heads * V_HEAD), lambda b, h: (b, za0 + h)),
        ],
        out_specs=pl.BlockSpec((lq, heads * V_HEAD), lambda b, h: (b, h)),
        out_shape=jax.ShapeDtypeStruct((batch * lq, n_heads * V_HEAD), BF16),
        compiler_params=_cparams(("parallel", "arbitrary")),
        name="attn_sample",
    )(q, kc, vc, ks, vs, p_bf)


def _log_sigmoid(x):
    return jnp.minimum(x, 0.0) - jnp.log1p(jnp.exp(-jnp.abs(x)))


def _pad_rows(a, rows):
    if a.shape[0] == rows:
        return a
    return jnp.concatenate([a, jnp.zeros((rows - a.shape[0], a.shape[1]), a.dtype)], axis=0)


def _mlstm_kernel(q_ref, k_ref, v_ref, mo_ref, zb_ref, g_ref, gb_ref, gain_ref, c0_ref, n0_ref, m0_ref,
                  h_ref, c_ref, n_ref, m_ref, *, n_heads, lc, lp, dk, dv):
    @pl.when(pl.program_id(1) == 0)
    def _():
        c_ref[...] = c0_ref[...]
        n_ref[...] = n0_ref[...]
        m_ref[...] = m0_ref[...]

    hi = lax.Precision.HIGHEST
    gates = _pad_rows(g_ref[...] + gb_ref[...], lp)
    gates_t = gates.T
    t_col = lax.broadcasted_iota(jnp.int32, (lp, 1), 0)
    t_row = lax.broadcasted_iota(jnp.int32, (1, lp), 1)
    r_idx = lax.broadcasted_iota(jnp.int32, (lp, lp), 0)
    c_idx = lax.broadcasted_iota(jnp.int32, (lp, lp), 1)
    causal = r_idx >= c_idx
    lower = causal.astype(F32)
    upper = (c_idx >= r_idx).astype(F32)
    b_cols = jnp.dot(lower, jnp.where(t_col < lc, _log_sigmoid(gates), 0.0), precision=hi,
                     preferred_element_type=F32)
    b_rows = jnp.dot(jnp.where(t_row < lc, _log_sigmoid(gates_t), 0.0), upper, precision=hi,
                     preferred_element_type=F32)

    for h in range(n_heads):
        li = QK_ROPE + h
        lf_lane = QK_ROPE + n_heads + h
        ig_col = jnp.where(t_col < lc, gates[:, li:li + 1], NEG_BIG)
        ig_row = jnp.where(t_row < lc, gates_t[li:li + 1, :], NEG_BIG)
        b_col = b_cols[:, lf_lane:lf_lane + 1]
        b_row = b_rows[lf_lane:lf_lane + 1, :]
        m0 = m_ref[0, 0:1, h:h + 1]
        d = jnp.where(causal, b_col - b_row + ig_row, -jnp.inf)
        a_col = b_col + m0
        m = jnp.maximum(a_col, jnp.max(d, axis=-1, keepdims=True))
        w_inter = jnp.exp(a_col - m)

        q = _pad_rows(q_ref[:, h * dk:(h + 1) * dk], lp)
        k = _pad_rows(k_ref[:, h * dk:(h + 1) * dk], lp) * (dk ** -0.5)
        v = _pad_rows(v_ref[:, h * dv:(h + 1) * dv], lp)
        qk = _dot_nt(q, k) * jnp.exp(d - m)
        c_old = c_ref[0, h]
        n_old = n_ref[0, h:h + 1, :]
        num = w_inter * _dot(q, c_old.astype(BF16)) + _dot(qk.astype(BF16), v)
        den = (w_inter * jnp.sum(q.astype(F32) * n_old, axis=-1, keepdims=True)
               + jnp.sum(qk, axis=-1, keepdims=True))
        hid = num / jnp.maximum(jnp.abs(den), jnp.exp(-m))

        b_last = b_col[lp - 1:lp, :]
        g_row = b_last - b_row + ig_row
        g_col = b_last - b_col + ig_col
        m_new = jnp.maximum(b_last + m0, jnp.max(g_row, axis=-1, keepdims=True))
        decay = jnp.exp(b_last + m0 - m_new)
        kw = k.astype(F32) * jnp.exp(g_col - m_new)
        c_ref[0, h] = decay * c_old + _dot(kw.T.astype(BF16), v)
        n_ref[0, h:h + 1, :] = decay * n_old + jnp.sum(kw, axis=0, keepdims=True)
        m_ref[0, 0:1, h:h + 1] = m_new

        hid = hid[:lc]
        mo = mo_ref[:, h * dv:(h + 1) * dv].astype(F32)
        zb = zb_ref[:, h * dv:(h + 1) * dv].astype(F32)
        out = jax.nn.sigmoid(mo) * (_rms(hid, dv) * gain_ref[:, h * dv:(h + 1) * dv])
        h_ref[:, h * dv:(h + 1) * dv] = (out * (zb * jax.nn.sigmoid(zb))).astype(h_ref.dtype)


def _mlstm(p_bf, p_f32, offs, gate_bias, gain, c0, n0, m0, batch, seq, lc, row0, share_state, n_heads, dk, dv):
    lp = -(-lc // LANE) * LANE
    nch = seq // lc
    blk0 = row0 // lc
    wqk = n_heads * dk
    wv = n_heads * dv
    gate_blk = p_f32.shape[1] // LANE - 1
    rows = lambda col: (lambda b, c: (blk0 + b * nch + c, col))
    state = (lambda b, c: (0, 0, 0, 0)) if share_state else (lambda b, c: (b, 0, 0, 0))
    state3 = (lambda b, c: (0, 0, 0)) if share_state else (lambda b, c: (b, 0, 0))
    out_rows = batch * seq
    return pl.pallas_call(
        functools.partial(_mlstm_kernel, n_heads=n_heads, lc=lc, lp=lp, dk=dk, dv=dv),
        grid=(batch, nch),
        in_specs=[
            pl.BlockSpec((lc, wqk), rows(offs["mq"] // wqk)),
            pl.BlockSpec((lc, wqk), rows(offs["mk"] // wqk)),
            pl.BlockSpec((lc, wv), rows(offs["mv"] // wv)),
            pl.BlockSpec((lc, wv), rows(offs["mo"] // wv)),
            pl.BlockSpec((lc, wv), rows(offs["zb"] // wv)),
            pl.BlockSpec((lc, LANE), rows(gate_blk)),
            pl.BlockSpec((1, LANE), lambda b, c: (0, 0)),
            pl.BlockSpec((1, wv), lambda b, c: (0, 0)),
            pl.BlockSpec((1, n_heads, dk, dv), state),
            pl.BlockSpec((1, n_heads, dk), state3),
            pl.BlockSpec((1, 1, n_heads), state3),
        ],
        out_specs=[
            pl.BlockSpec((lc, wv), lambda b, c: (b * nch + c, 0)),
            pl.BlockSpec((1, n_heads, dk, dv), lambda b, c: (b, 0, 0, 0)),
            pl.BlockSpec((1, n_heads, dk), lambda b, c: (b, 0, 0)),
            pl.BlockSpec((1, 1, n_heads), lambda b, c: (b, 0, 0)),
        ],
        out_shape=[
            jax.ShapeDtypeStruct((out_rows, wv), BF16),
            jax.ShapeDtypeStruct((batch, n_heads, dk, dv), F32),
            jax.ShapeDtypeStruct((batch, n_heads, dk), F32),
            jax.ShapeDtypeStruct((batch, 1, n_heads), F32),
        ],
        compiler_params=_cparams(("parallel", "arbitrary")),
        name="mlstm",
    )(p_bf, p_bf, p_bf, p_bf, p_bf, p_f32, gate_bias, gain, c0, n0, m0)


def _out_proj_kernel(a_ref, m_ref, wa_ref, wm_ref, x_ref, o_ref):
    o_ref[...] = (x_ref[...] + _dot(a_ref[...], wa_ref[...].astype(BF16))
                  + _dot(m_ref[...], wm_ref[...].astype(BF16)))


def _out_proj(a, ml, w_out, x, tm, tn):
    rows, wa = a.shape
    wm = ml.shape[1]
    d = w_out.shape[1]
    assert wa == wm
    return pl.pallas_call(
        _out_proj_kernel,
        grid=(rows // tm, d // tn),
        in_specs=[
            pl.BlockSpec((tm, wa), lambda i, j: (i, 0)),
            pl.BlockSpec((tm, wm), lambda i, j: (i, 0)),
            pl.BlockSpec((wa, tn), lambda i, j: (0, j)),
            pl.BlockSpec((wm, tn), lambda i, j: (1, j)),
            pl.BlockSpec((tm, tn), lambda i, j: (i, j)),
        ],
        out_specs=pl.BlockSpec((tm, tn), lambda i, j: (i, j)),
        out_shape=jax.ShapeDtypeStruct((rows, d), F32),
        compiler_params=_cparams(("parallel", "arbitrary")),
        name="out_proj",
    )(a, ml, w_out, w_out, x)


def _rope_tables(pos):
    inv_freq = ROPE_THETA ** (-jnp.arange(HALF_ROPE, dtype=F32) / HALF_ROPE)
    ang = pos.astype(F32)[:, None] * inv_freq[None, :]
    cos, sin = jnp.cos(ang), jnp.sin(ang)
    zero = jnp.zeros((pos.shape[0], LANE - QK_ROPE), F32)
    return jnp.concatenate([cos, cos, zero], axis=1), jnp.concatenate([-sin, sin, zero], axis=1)


def _pad_lanes(vec):
    return jnp.pad(vec.astype(F32), (0, LANE - vec.shape[0]))[None, :]


def _row_tile(rows, target):
    t = min(rows, target)
    while rows % t:
        t //= 2
    return t


def kernel(x_prompt, x_sample, cache_ckv, cache_kpe, state_C, state_n, state_m, meta_tokens, norm_gain, w_in,
           b_igate, b_fgate, q_a_norm, w_q_up, q_nope_norm, q_rope_norm, kv_a_norm, k_rope_norm, w_kv_up,
           k_nope_norm, ml_out_norm, w_out):
    batch, seq, d = x_prompt.shape
    dbatch, dseq, _ = x_sample.shape
    past = cache_ckv.shape[1]
    n_meta = meta_tokens.shape[0]
    q_lora = q_a_norm.shape[0]
    kv_lora = kv_a_norm.shape[0]
    ml_heads, dv = ml_out_norm.shape
    dk = state_n.shape[-1]
    mla_heads = w_kv_up.shape[1] // (QK_NOPE + V_HEAD)
    mla_w = mla_heads * V_HEAD
    ml_w = ml_heads * dv
    mqk_w = ml_heads * dk
    assert q_lora % COL_TILE == 0 and kv_lora % LANE == 0 and 2 * ml_heads <= LANE - QK_ROPE
    assert mla_w == ml_w and w_out.shape[0] == mla_w + ml_w

    o_cq = 0
    o_ckv = o_cq + q_lora
    o_kpe = o_ckv + kv_lora
    o_mq = o_kpe + QK_ROPE
    o_mk = o_mq + mqk_w
    o_mv = o_mk + mqk_w
    o_mo = o_mv + ml_w
    o_mi = o_mo + ml_w
    o_mf = o_mi + ml_heads
    o_za = o_mf + ml_heads
    o_zb = o_za + mla_w

    offs = {"mq": 0, "mk": mqk_w, "mv": 2 * mqk_w, "mo": 2 * mqk_w + ml_w, "za": 2 * mqk_w + 2 * ml_w,
            "zb": 2 * mqk_w + 2 * ml_w + mla_w, "cq": 2 * mqk_w + 3 * ml_w + mla_w}
    gate_pad = LANE - QK_ROPE - 2 * ml_heads
    w_main, w_small = _regroup_rows(
        w_in.T,
        [(o_mq, mqk_w, offs["mq"]), (o_mk, mqk_w, offs["mk"]), (o_mv, ml_w, offs["mv"]), (o_mo, ml_w, offs["mo"]),
         (o_za, mla_w, offs["za"]), (o_zb, ml_w, offs["zb"]), (o_cq, q_lora, offs["cq"])], offs["cq"] + q_lora,
        [(o_ckv, kv_lora), (o_kpe, QK_ROPE), (o_mi, 2 * ml_heads)], kv_lora + LANE, _row_tile(d, 256))
    gate_bias = jnp.concatenate([jnp.zeros((QK_ROPE,), F32), b_igate.astype(F32), b_fgate.astype(F32),
                                 jnp.zeros((gate_pad,), F32)])[None, :]
    qk_w = QK_NOPE + QK_ROPE
    kv_w = QK_NOPE + V_HEAD
    wq = _regroup(w_q_up, [(h * qk_w, qk_w, h * QK_PAD) for h in range(mla_heads)], mla_heads * QK_PAD,
                  _row_tile(q_lora, 512))
    wk = _regroup(w_kv_up, [(h * kv_w, QK_NOPE, h * QK_NOPE) for h in range(mla_heads)], mla_heads * QK_NOPE,
                  kv_lora)
    wv = _regroup(w_kv_up, [(h * kv_w + QK_NOPE, V_HEAD, h * V_HEAD) for h in range(mla_heads)],
                  mla_heads * V_HEAD, kv_lora)
    gain_row = norm_gain.astype(F32)[None, :]
    qan = q_a_norm.astype(F32)[None, :]
    kvg = kv_a_norm.astype(F32)[None, :]
    qng = q_nope_norm.astype(F32)[None, :]
    kng = k_nope_norm.astype(F32)[None, :]
    qrg = _pad_lanes(q_rope_norm)
    krg = _pad_lanes(k_rope_norm)
    ml_gain = ml_out_norm.astype(F32).reshape(1, ml_w)

    n_p = batch * seq
    n_s = dbatch * dseq
    assert n_s % LANE == 0 and n_meta <= LANE and n_meta % 16 == 0
    rows_s = n_s + n_meta
    xp = x_prompt.reshape(n_p, d)
    xs = jnp.concatenate([x_sample.reshape(n_s, d), meta_tokens.astype(x_sample.dtype)], axis=0)

    pos_s = jnp.concatenate([jnp.tile(n_meta + past + jnp.arange(dseq), dbatch), jnp.arange(n_meta)])
    cos_p, sin_p = _rope_tables(n_meta + jnp.arange(seq))
    cos_s, sin_s = _rope_tables(pos_s)

    tm_p = _row_tile(seq, 512)
    proj_sub = 2 if n_p % (2 * tm_p) == 0 else 1
    pbf_p, pf_p = _proj(xp, gain_row, w_main, w_small, proj_sub * tm_p, proj_sub)
    pbf_s, pf_s = _proj(xs, gain_row, w_main, w_small, rows_s, 1)

    hps = max(1, mla_heads // 2)
    q_p, ckv_p, kpe_p, kpep_p = _latent_q(pbf_p, pf_p, offs["cq"], q_lora, kv_lora, cos_p, sin_p, qan, wq,
                                          qng, qrg, kvg, krg, tm_p, hps)
    q_s, ckv_s, kpe_s, kpep_s = _latent_q(pbf_s, pf_s, offs["cq"], q_lora, kv_lora, cos_s, sin_s, qan, wq,
                                          qng, qrg, kvg, krg, rows_s, hps)

    small_rows = n_s + LANE
    meta_blk = n_s // LANE
    tail = ((0, small_rows - rows_s), (0, 0))
    k_p, v_p = _kv_expand(ckv_p, kpep_p, wk, wv, kng, tm_p)
    k_s, v_s = _kv_expand(jnp.pad(ckv_s, tail), jnp.pad(kpep_s, tail), wk, wv, kng, small_rows)
    cache_rows = dbatch * past
    cache_kpep = jnp.pad(cache_kpe.reshape(cache_rows, QK_ROPE), ((0, 0), (0, LANE - QK_ROPE))).astype(BF16)
    k_c, v_c = _kv_expand(cache_ckv.reshape(cache_rows, kv_lora).astype(F32), cache_kpep, wk, wv, kng,
                          _row_tile(cache_rows, 512))

    tq = _row_tile(seq, ATTN_TQ)
    attn_p = _attn_prompt(q_p, k_p, v_p, k_s, v_s, meta_blk, pbf_p, offs["za"], batch, seq, tq,
                          min(ATTN_HEADS, mla_heads), n_meta)
    attn_s = _attn_sample(q_s, k_c, v_c, k_s, v_s, pbf_s, offs["za"], dbatch, dseq, past, n_meta,
                          min(4, mla_heads))

    zc = jnp.zeros((1, ml_heads, dk, dv), F32)
    zn = jnp.zeros((1, ml_heads, dk), F32)
    zm = jnp.zeros((1, 1, ml_heads), F32)
    _, c_m, n_m, m_m = _mlstm(pbf_s, pf_s, offs, gate_bias, ml_gain, zc, zn, zm, 1, n_meta, n_meta, n_s, True,
                              ml_heads, dk, dv)
    lc_p = _row_tile(seq, 256)
    ml_p, c_p, n_pr, m_p = _mlstm(pbf_p, pf_p, offs, gate_bias, ml_gain, c_m, n_m, m_m, batch, seq, lc_p, 0,
                                  True, ml_heads, dk, dv)
    ml_s, c_s, n_sm, m_s = _mlstm(pbf_s, pf_s, offs, gate_bias, ml_gain, state_C.astype(F32),
                                  state_n.astype(F32), state_m.astype(F32).reshape(dbatch, 1, ml_heads),
                                  dbatch, dseq, dseq, 0, False, ml_heads, dk, dv)

    y_p = _out_proj(attn_p, ml_p, w_out, xp, _row_tile(n_p, 1024), _row_tile(d, 512))
    y_s = _out_proj(attn_s, ml_s, w_out, xs, n_s, _row_tile(d, 512))

    meta_ckv = ckv_s[n_s:n_s + n_meta]
    meta_kpe = kpe_s[n_s:n_s + n_meta]
    ckv_prompt = jnp.concatenate([jnp.broadcast_to(meta_ckv[None], (batch, n_meta, kv_lora)),
                                  ckv_p.reshape(batch, seq, kv_lora)], axis=1)
    kpe_prompt = jnp.concatenate([jnp.broadcast_to(meta_kpe[None], (batch, n_meta, QK_ROPE)),
                                  kpe_p.reshape(batch, seq, QK_ROPE)], axis=1)
    return (y_p.reshape(batch, seq, d), y_s.reshape(dbatch, dseq, d), ckv_prompt, kpe_prompt,
            c_p, n_pr, m_p.reshape(batch, ml_heads),
            ckv_s[:n_s].reshape(dbatch, dseq, kv_lora), kpe_s[:n_s].reshape(dbatch, dseq, QK_ROPE),
            c_s, n_sm, m_s.reshape(dbatch, ml_heads))
```

```python
import functools
import math

import jax
import jax.numpy as jnp
from jax import lax
from jax.experimental import pallas as pl
from jax.experimental.pallas import tpu as pltpu

CHUNK = 64
EPS = 1e-6
ROPE_THETA = 10000.0
V_HEAD = 128
QK_NOPE = 128
QK_ROPE = 64
HALF_ROPE = QK_ROPE // 2
LANE = 128
QK_PAD = QK_NOPE + LANE
COL_TILE = 512
NEG_BIG = -1e30
ATTN_TQ = 512
ATTN_HEADS = 4
VMEM_LIMIT = 56 * 1024 * 1024

F32 = jnp.float32
BF16 = jnp.bfloat16


def _cparams(sem):
    return pltpu.CompilerParams(dimension_semantics=sem, vmem_limit_bytes=VMEM_LIMIT)


def _dot(a, b):
    return jnp.dot(a, b, preferred_element_type=F32)


def _dot_nt(a, b):
    return lax.dot_general(a, b, (((1,), (1,)), ((), ())), preferred_element_type=F32)


def _rms(x, n):
    return x * lax.rsqrt(jnp.sum(x * x, axis=-1, keepdims=True) * (1.0 / n) + EPS)


def _rope_pad(x, cos, sin_signed):
    lane = lax.broadcasted_iota(jnp.int32, x.shape, 1)
    rot = jnp.where(lane < HALF_ROPE, pltpu.roll(x, LANE - HALF_ROPE, 1), pltpu.roll(x, HALF_ROPE, 1))
    return x * cos + rot * sin_signed


def _regroup_kernel(w_ref, o_ref, *, segments, zero_ranges):
    for src, n, dst in segments:
        o_ref[:, dst:dst + n] = w_ref[:, src:src + n].astype(o_ref.dtype)
    for dst, n in zero_ranges:
        o_ref[:, dst:dst + n] = jnp.zeros((o_ref.shape[0], n), o_ref.dtype)


def _regroup(w, segments, out_cols, tr):
    rows, cols = w.shape
    covered = sorted((dst, n) for _, n, dst in segments)
    zero_ranges, pos = [], 0
    for dst, n in covered + [(out_cols, 0)]:
        if dst > pos:
            zero_ranges.append((pos, dst - pos))
        pos = dst + n
    return pl.pallas_call(
        functools.partial(_regroup_kernel, segments=tuple(segments), zero_ranges=tuple(zero_ranges)),
        grid=(rows // tr,),
        in_specs=[pl.BlockSpec((tr, cols), lambda i: (i, 0))],
        out_specs=pl.BlockSpec((tr, out_cols), lambda i: (i, 0)),
        out_shape=jax.ShapeDtypeStruct((rows, out_cols), BF16),
        compiler_params=_cparams(("parallel",)),
        name="regroup",
    )(w)


def _stack_rows_kernel(*refs):
    *w_refs, o_ref = refs
    parts = [r[...] for r in w_refs]
    used = sum(p.shape[0] for p in parts)
    parts.append(jnp.zeros((o_ref.shape[0] - used, o_ref.shape[1]), parts[0].dtype))
    o_ref[...] = jnp.concatenate(parts, axis=0).astype(o_ref.dtype)


def _stack_rows(wt, segments, out_rows, tc):
    cols = wt.shape[1]
    return pl.pallas_call(
        _stack_rows_kernel,
        grid=(cols // tc,),
        in_specs=[pl.BlockSpec((pl.Element(n), pl.Element(tc)), functools.partial(lambda i, s: (s, i * tc), s=src))
                  for src, n in segments],
        out_specs=pl.BlockSpec((out_rows, tc), lambda i: (0, i)),
        out_shape=jax.ShapeDtypeStruct((out_rows, cols), BF16),
        compiler_params=_cparams(("parallel",)),
        name="stack_rows",
    )(*([wt] * len(segments)))


def _proj_regroup_kernel(x_ref, g_ref, w_ref, ws_ref, om_ref, o_ref, os_ref, hn_ref):
    @pl.when(pl.program_id(0) == 0)
    def _():
        x = x_ref[...]
        hn = (_rms(x, x.shape[-1]) * g_ref[...]).astype(BF16)
        hn_ref[...] = hn
        os_ref[...] = _dot_nt(hn, ws_ref[...])

    wb = w_ref[...].astype(BF16)
    om_ref[...] = wb
    o_ref[...] = _dot_nt(hn_ref[...], wb).astype(o_ref.dtype)


def _proj_regroup(x, gain, wt, segments, w_small):
    m, d = x.shape
    ns = w_small.shape[0]
    n = sum(sz for _, sz, _ in segments)
    merged = []
    for src, sz, dst in sorted(segments, key=lambda s: s[2]):
        if merged and merged[-1][0] + merged[-1][1] == src and merged[-1][2] + merged[-1][1] == dst:
            merged[-1] = (merged[-1][0], merged[-1][1] + sz, merged[-1][2])
        else:
            merged.append((src, sz, dst))
    segments = merged
    assert all(sz % COL_TILE == 0 and dst % COL_TILE == 0 for _, sz, dst in segments)
    align = functools.reduce(math.gcd, [src for src, _, _ in segments], COL_TILE)

    def src_row(j):
        r = j * COL_TILE
        out = 0
        for src, sz, dst in segments:
            out = jnp.where((r >= dst) & (r < dst + sz), src + r - dst, out)
        return out

    const = lambda j: (0, 0)
    return pl.pallas_call(
        _proj_regroup_kernel,
        grid=(n // COL_TILE,),
        in_specs=[
            pl.BlockSpec((m, d), const),
            pl.BlockSpec((1, d), const),
            pl.BlockSpec((pl.Element(COL_TILE), pl.Element(d)), lambda j: (pl.multiple_of(src_row(j), align), 0)),
            pl.BlockSpec((ns, d), const),
        ],
        out_specs=[
            pl.BlockSpec((COL_TILE, d), lambda j: (j, 0)),
            pl.BlockSpec((m, COL_TILE), lambda j: (0, j)),
            pl.BlockSpec((m, ns), const),
        ],
        out_shape=[jax.ShapeDtypeStruct((n, d), BF16), jax.ShapeDtypeStruct((m, n), BF16),
                   jax.ShapeDtypeStruct((m, ns), F32)],
        scratch_shapes=[pltpu.VMEM((m, d), BF16)],
        compiler_params=_cparams(("arbitrary",)),
        name="proj_regroup",
    )(x, gain, wt, w_small)


def _proj_kernel(x_ref, g_ref, w_ref, ws_ref, o_ref, os_ref, hn_ref, *, n_sub):
    j = pl.program_id(1)
    ts = x_ref.shape[0]

    @pl.when(j < n_sub)
    def _():
        x = x_ref[...]
        hn = (_rms(x, x.shape[-1]) * g_ref[...]).astype(BF16)
        hn_ref[pl.ds(pl.multiple_of(j * ts, ts), ts), :] = hn
        os_ref[...] = _dot_nt(hn, ws_ref[...])

    @pl.when(j >= n_sub)
    def _():
        o_ref[...] = _dot_nt(hn_ref[...], w_ref[...]).astype(o_ref.dtype)


def _proj(x, gain, w_main, w_small, tm, n_sub):
    m, d = x.shape
    n = w_main.shape[0]
    ns = w_small.shape[0]
    ts = tm // n_sub
    sub = lambda i, j: (i * n_sub + jnp.minimum(j, n_sub - 1), 0)
    feat = lambda j: jnp.maximum(j - n_sub, 0)
    return pl.pallas_call(
        functools.partial(_proj_kernel, n_sub=n_sub),
        grid=(m // tm, n_sub + n // COL_TILE),
        in_specs=[
            pl.BlockSpec((ts, d), sub),
            pl.BlockSpec((1, d), lambda i, j: (0, 0)),
            pl.BlockSpec((COL_TILE, d), lambda i, j: (feat(j), 0)),
            pl.BlockSpec((ns, d), lambda i, j: (0, 0)),
        ],
        out_specs=[
            pl.BlockSpec((tm, COL_TILE), lambda i, j: (i, feat(j))),
            pl.BlockSpec((ts, ns), sub),
        ],
        out_shape=[jax.ShapeDtypeStruct((m, n), BF16), jax.ShapeDtypeStruct((m, ns), F32)],
        scratch_shapes=[pltpu.VMEM((tm, d), BF16)],
        compiler_params=_cparams(("parallel", "arbitrary")),
        name="proj",
    )(x, gain, w_main, w_small)


def _latent_q_kernel(*refs, n_cq, kv_lora, heads_per_step):
    cq_refs = refs[:n_cq]
    (pf_ref, cos_ref, sin_ref, qan_ref, wq_ref, qng_ref, qrg_ref, kvg_ref, krg_ref,
     q_ref, ckv_ref, kpe_ref, kpep_ref, cqn_ref) = refs[n_cq:]
    cos = cos_ref[...]
    sin = sin_ref[...]

    @pl.when(pl.program_id(1) == 0)
    def _():
        cq = jnp.concatenate([r[...].astype(F32) for r in cq_refs], axis=1)
        cqn_ref[...] = (_rms(cq, cq.shape[-1]) * qan_ref[...]).astype(BF16)
        pf = pf_ref[...]
        ckv_raw = pf[:, :kv_lora]
        ckv_ref[...] = _rms(ckv_raw, kv_lora) * kvg_ref[...]
        tail = pf[:, kv_lora:kv_lora + LANE]
        lane = lax.broadcasted_iota(jnp.int32, tail.shape, 1)
        kraw = jnp.where(lane < QK_ROPE, tail, 0.0)
        kpe = _rope_pad(_rms(kraw, QK_ROPE) * krg_ref[...], cos, sin)
        kpe_ref[...] = kpe[:, :QK_ROPE]
        kpep_ref[...] = kpe.astype(BF16)

    qf = _dot(cqn_ref[...], wq_ref[...])
    for h in range(heads_per_step):
        base = h * QK_PAD
        nope = qf[:, base:base + QK_NOPE]
        pe = qf[:, base + QK_NOPE:base + QK_PAD]
        q_ref[:, base:base + QK_NOPE] = (_rms(nope, QK_NOPE) * qng_ref[...]).astype(BF16)
        q_ref[:, base + QK_NOPE:base + QK_PAD] = _rope_pad(_rms(pe, QK_ROPE) * qrg_ref[...], cos, sin).astype(BF16)


def _latent_q(p_bf, p_f32, cq_off, q_lora, kv_lora, cos, sin, qan, wq, qng, qrg, kvg, krg, tm, heads_per_step):
    m = p_bf.shape[0]
    n_heads = wq.shape[1] // QK_PAD
    n_cq = q_lora // COL_TILE
    cq0 = cq_off // COL_TILE
    ns = p_f32.shape[1]
    wstep = heads_per_step * QK_PAD
    row = lambda i, j: (i, 0)
    const = lambda i, j: (0, 0)
    pos_blocks = cos.shape[0] // tm
    pos = lambda i, j: (i % pos_blocks, 0)
    in_specs = [pl.BlockSpec((tm, COL_TILE), functools.partial(lambda i, j, c: (i, c), c=cq0 + c))
                for c in range(n_cq)]
    in_specs += [
        pl.BlockSpec((tm, ns), row),
        pl.BlockSpec((tm, LANE), pos),
        pl.BlockSpec((tm, LANE), pos),
        pl.BlockSpec((1, q_lora), const),
        pl.BlockSpec((q_lora, wstep), lambda i, j: (0, j)),
        pl.BlockSpec((1, LANE), const),
        pl.BlockSpec((1, LANE), const),
        pl.BlockSpec((1, kv_lora), const),
        pl.BlockSpec((1, LANE), const),
    ]
    return pl.pallas_call(
        functools.partial(_latent_q_kernel, n_cq=n_cq, kv_lora=kv_lora, heads_per_step=heads_per_step),
        grid=(m // tm, n_heads // heads_per_step),
        in_specs=in_specs,
        out_specs=[
            pl.BlockSpec((tm, wstep), lambda i, j: (i, j)),
            pl.BlockSpec((tm, kv_lora), row),
            pl.BlockSpec((tm, QK_ROPE), row),
            pl.BlockSpec((tm, LANE), row),
        ],
        out_shape=[
            jax.ShapeDtypeStruct((m, n_heads * QK_PAD), BF16),
            jax.ShapeDtypeStruct((m, kv_lora), F32),
            jax.ShapeDtypeStruct((m, QK_ROPE), F32),
            jax.ShapeDtypeStruct((m, LANE), BF16),
        ],
        scratch_shapes=[pltpu.VMEM((tm, q_lora), BF16)],
        compiler_params=_cparams(("parallel", "arbitrary")),
        name="latent_q",
    )(*([p_bf] * n_cq), p_f32, cos, sin, qan, wq, qng, qrg, kvg, krg)


def _kv_expand_kernel(ckv_ref, kpep_ref, wk_ref, wv_ref, kng_ref, k_ref, v_ref, *, n_heads):
    c = ckv_ref[...].astype(BF16)
    kf = _dot(c, wk_ref[...])
    v_ref[...] = _dot(c, wv_ref[...]).astype(BF16)
    kpep = kpep_ref[...]
    for h in range(n_heads):
        kn = _rms(kf[:, h * QK_NOPE:(h + 1) * QK_NOPE], QK_NOPE) * kng_ref[...]
        k_ref[:, h * QK_PAD:h * QK_PAD + QK_NOPE] = kn.astype(BF16)
        k_ref[:, h * QK_PAD + QK_NOPE:(h + 1) * QK_PAD] = kpep


def _kv_expand(ckv, kpep, wk, wv, kng, tm):
    r, kv_lora = ckv.shape
    n_heads = wk.shape[1] // QK_NOPE
    row = lambda i: (i, 0)
    const = lambda i: (0, 0)
    return pl.pallas_call(
        functools.partial(_kv_expand_kernel, n_heads=n_heads),
        grid=(r // tm,),
        in_specs=[
            pl.BlockSpec((tm, kv_lora), row),
            pl.BlockSpec((tm, LANE), row),
            pl.BlockSpec((kv_lora, n_heads * QK_NOPE), const),
            pl.BlockSpec((kv_lora, n_heads * V_HEAD), const),
            pl.BlockSpec((1, LANE), const),
        ],
        out_specs=[
            pl.BlockSpec((tm, n_heads * QK_PAD), row),
            pl.BlockSpec((tm, n_heads * V_HEAD), row),
        ],
        out_shape=[
            jax.ShapeDtypeStruct((r, n_heads * QK_PAD), BF16),
            jax.ShapeDtypeStruct((r, n_heads * V_HEAD), BF16),
        ],
        compiler_params=_cparams(("parallel",)),
        name="kv_expand",
    )(ckv, kpep, wk, wv, kng)


def _lane_chunk_sum(p):
    out = p[:, :LANE]
    for c in range(1, p.shape[1] // LANE):
        out = out + p[:, c * LANE:(c + 1) * LANE]
    return out


def _gated(lvec, acc, za):
    za = za.astype(F32)
    return acc / jnp.sum(lvec, axis=-1, keepdims=True) * (za * jax.nn.sigmoid(za))


def _softmax_scale():
    return float((QK_NOPE + QK_ROPE) ** -0.5 * 1.4426950408889634)


def _attn_prompt_block(n_full, q_ref, k_ref, km_ref, za_ref, o_ref, s_ref, p_ref, va_ref, vma_ref, *,
                       tq, heads, c, n_meta):
    td = min(tq, 256)
    pieces = [(i * tq, tq, 0, False) for i in range(n_full)]
    pieces += [(n_full * tq + j * td, td, j * td, True) for j in range(tq // td)]

    def upd(full, r0, fn):
        return fn(full) if r0 == 0 else jnp.concatenate([full[:r0], fn(full[r0:])], axis=0)

    def scores(h):
        qs = slice(h * QK_PAD, (h + 1) * QK_PAD)
        q = q_ref[:, qs]
        col = lax.broadcasted_iota(jnp.int32, (tq, LANE), 1)
        t = jnp.where(col < n_meta, _dot_nt(q, km_ref[:, qs]) * c, -jnp.inf)
        s_ref[h, :, :LANE] = t
        mx = t
        off = LANE
        for start, rows, r0, masked in pieces:
            t = _dot_nt(q[r0:], k_ref[start:start + rows, qs]) * c
            if masked:
                rq = lax.broadcasted_iota(jnp.int32, t.shape, 0) // CHUNK
                ck = lax.broadcasted_iota(jnp.int32, t.shape, 1) // CHUNK
                t = jnp.where(rq >= ck, t, -jnp.inf)
            s_ref[h, r0:, off:off + rows] = t
            tmax = t[:, :LANE]
            for ch in range(1, rows // LANE):
                tmax = jnp.maximum(tmax, t[:, ch * LANE:(ch + 1) * LANE])
            mx = upd(mx, r0, lambda a: jnp.maximum(a, tmax))
            off += rows
        return jnp.max(mx, axis=-1, keepdims=True)

    def values(h, m):
        vs = slice(h * V_HEAD, (h + 1) * V_HEAD)
        va = slice(2 * h * V_HEAD, 2 * (h + 1) * V_HEAD)
        p_ref[h, :, :LANE] = jnp.exp2(s_ref[h, :, :LANE] - m).astype(BF16)
        off = LANE
        for start, rows, r0, _ in pieces:
            p_ref[h, r0:, off:off + rows] = jnp.exp2(s_ref[h, r0:, off:off + rows] - m[r0:]).astype(BF16)
            off += rows
        acc = _dot(p_ref[h, :, :LANE], vma_ref[:, va])
        n_main = n_full * tq
        if n_main:
            acc = acc + _dot(p_ref[h, :, LANE:LANE + n_main], va_ref[:n_main, va])
        off = LANE + n_main
        for start, rows, r0, _ in pieces[n_full:]:
            pv = _dot(p_ref[h, r0:, off:off + rows], va_ref[start:start + rows, va])
            acc = upd(acc, r0, lambda a: a + pv)
            off += rows
        za = za_ref[:, vs].astype(F32)
        out = acc[:, :V_HEAD] / acc[:, V_HEAD:V_HEAD + 1] * (za * jax.nn.sigmoid(za))
        o_ref[:, vs] = out.astype(o_ref.dtype)

    maxes = [scores(h) for h in range(heads)]
    for h in range(heads):
        values(h, maxes[h])


def _attn_prompt_kernel(q_ref, k_ref, v_ref, km_ref, vm_ref, za_ref, o_ref, s_ref, p_ref, va_ref, vma_ref, *,
                        nqb, heads, **kw):
    qi = pl.program_id(2)

    @pl.when(qi == 0)
    def _():
        for h in range(heads):
            vs = slice(h * V_HEAD, (h + 1) * V_HEAD)
            va_ref[:, 2 * h * V_HEAD:(2 * h + 1) * V_HEAD] = v_ref[:, vs]
            va_ref[:, (2 * h + 1) * V_HEAD:(2 * h + 2) * V_HEAD] = jnp.ones((va_ref.shape[0], V_HEAD), BF16)
            vma_ref[:, 2 * h * V_HEAD:(2 * h + 1) * V_HEAD] = vm_ref[:, vs]
            vma_ref[:, (2 * h + 1) * V_HEAD:(2 * h + 2) * V_HEAD] = jnp.ones((vma_ref.shape[0], V_HEAD), BF16)

    for n_full in range(nqb):
        pl.when(qi == n_full)(
            functools.partial(_attn_prompt_block, n_full, q_ref, k_ref, km_ref, za_ref, o_ref, s_ref, p_ref, va_ref,
                              vma_ref, heads=heads, **kw))


def _attn_prompt(q, k, v, k_small, v_small, meta_blk, p_bf, za_off, batch, seq, tq, heads, n_meta):
    n_heads = q.shape[1] // QK_PAD
    nqb = seq // tq
    za0 = za_off // (V_HEAD * heads)
    assert za_off % (V_HEAD * heads) == 0 and n_heads % heads == 0 and tq % CHUNK == 0
    return pl.pallas_call(
        functools.partial(_attn_prompt_kernel, nqb=nqb, tq=tq, heads=heads, c=_softmax_scale(), n_meta=n_meta),
        grid=(batch, n_heads // heads, nqb),
        in_specs=[
            pl.BlockSpec((tq, heads * QK_PAD), lambda b, h, i: (b * nqb + i, h)),
            pl.BlockSpec((seq, heads * QK_PAD), lambda b, h, i: (b, h)),
            pl.BlockSpec((seq, heads * V_HEAD), lambda b, h, i: (b, h)),
            pl.BlockSpec((LANE, heads * QK_PAD), lambda b, h, i: (meta_blk, h)),
            pl.BlockSpec((LANE, heads * V_HEAD), lambda b, h, i: (meta_blk, h)),
            pl.BlockSpec((tq, heads * V_HEAD), lambda b, h, i: (b * nqb + i, za0 + h)),
        ],
        out_specs=pl.BlockSpec((tq, heads * V_HEAD), lambda b, h, i: (b * nqb + i, h)),
        out_shape=jax.ShapeDtypeStruct((batch * seq, n_heads * V_HEAD), BF16),
        scratch_shapes=[pltpu.VMEM((heads, tq, LANE + seq), F32), pltpu.VMEM((heads, tq, LANE + seq), BF16),
                        pltpu.VMEM((seq, 2 * heads * V_HEAD), BF16), pltpu.VMEM((LANE, 2 * heads * V_HEAD), BF16)],
        compiler_params=_cparams(("parallel", "parallel", "arbitrary")),
        name="attn_prompt",
    )(q, k, v, k_small, v_small, p_bf)


def _attn_sample_kernel(q_ref, kc_ref, vc_ref, ks_ref, vs_ref, za_ref, o_ref, *, heads, lq, c, n_meta, meta_row):
    b = pl.program_id(0)
    col = lax.broadcasted_iota(jnp.int32, (lq, ks_ref.shape[0]), 1)
    own = (col >= b * lq) & (col < (b + 1) * lq)
    meta = (col >= meta_row) & (col < meta_row + n_meta)
    visible = own | meta
    for h in range(heads):
        qs = slice(h * QK_PAD, (h + 1) * QK_PAD)
        vs = slice(h * V_HEAD, (h + 1) * V_HEAD)
        q = q_ref[:, qs]
        t_cache = _dot_nt(q, kc_ref[:, qs]) * c
        t_new = jnp.where(visible, _dot_nt(q, ks_ref[:, qs]) * c, -jnp.inf)
        m = jnp.maximum(jnp.max(t_cache, axis=-1, keepdims=True), jnp.max(t_new, axis=-1, keepdims=True))
        p_cache = jnp.exp2(t_cache - m)
        p_new = jnp.exp2(t_new - m)
        lvec = _lane_chunk_sum(p_cache) + _lane_chunk_sum(p_new)
        acc = _dot(p_cache.astype(BF16), vc_ref[:, vs]) + _dot(p_new.astype(BF16), vs_ref[:, vs])
        o_ref[:, vs] = _gated(lvec, acc, za_ref[:, vs]).astype(o_ref.dtype)


def _attn_sample(q, kc, vc, ks, vs, p_bf, za_off, batch, lq, past, n_meta, heads):
    n_heads = q.shape[1] // QK_PAD
    rows = ks.shape[0]
    za0 = za_off // (V_HEAD * heads)
    assert za_off % (V_HEAD * heads) == 0 and n_heads % heads == 0
    return pl.pallas_call(
        functools.partial(_attn_sample_kernel, heads=heads, lq=lq, c=_softmax_scale(), n_meta=n_meta,
                          meta_row=batch * lq),
        grid=(batch, n_heads // heads),
        in_specs=[
            pl.BlockSpec((lq, heads * QK_PAD), lambda b, h: (b, h)),
            pl.BlockSpec((past, heads * QK_PAD), lambda b, h: (b, h)),
            pl.BlockSpec((past, heads * V_HEAD), lambda b, h: (b, h)),
            pl.BlockSpec((rows, heads * QK_PAD), lambda b, h: (0, h)),
            pl.BlockSpec((rows, heads * V_HEAD), lambda b, h: (0, h)),
            pl.BlockSpec((lq, heads * V_HEAD), lambda b, h: (b, za0 + h)),
        ],
        out_specs=pl.BlockSpec((lq, heads * V_HEAD), lambda b, h: (b, h)),
        out_shape=jax.ShapeDtypeStruct((batch * lq, n_heads * V_HEAD), BF16),
        compiler_params=_cparams(("parallel", "arbitrary")),
        name="attn_sample",
    )(q, kc, vc, ks, vs, p_bf)


def _log_sigmoid(x):
    return jnp.minimum(x, 0.0) - jnp.log1p(jnp.exp(-jnp.abs(x)))


def _pad_rows(a, rows):
    if a.shape[0] == rows:
        return a
    return jnp.concatenate([a, jnp.zeros((rows - a.shape[0], a.shape[1]), a.dtype)], axis=0)


def _mlstm_kernel(q_ref, k_ref, v_ref, mo_ref, zb_ref, g_ref, gb_ref, gain_ref, c0_ref, n0_ref, m0_ref,
                  h_ref, c_ref, n_ref, m_ref, *, n_heads, lc, lp, dk, dv):
    @pl.when(pl.program_id(1) == 0)
    def _():
        c_ref[...] = c0_ref[...]
        n_ref[...] = n0_ref[...]
        m_ref[...] = m0_ref[...]

    hi = lax.Precision.HIGHEST
    gates = _pad_rows(g_ref[...] + gb_ref[...], lp)
    gates_t = gates.T
    t_col = lax.broadcasted_iota(jnp.int32, (lp, 1), 0)
    t_row = lax.broadcasted_iota(jnp.int32, (1, lp), 1)
    r_idx = lax.broadcasted_iota(jnp.int32, (lp, lp), 0)
    c_idx = lax.broadcasted_iota(jnp.int32, (lp, lp), 1)
    causal = r_idx >= c_idx
    lower = causal.astype(F32)
    upper = (c_idx >= r_idx).astype(F32)
    b_cols = jnp.dot(lower, jnp.where(t_col < lc, _log_sigmoid(gates), 0.0), precision=hi,
                     preferred_element_type=F32)
    b_rows = jnp.dot(jnp.where(t_row < lc, _log_sigmoid(gates_t), 0.0), upper, precision=hi,
                     preferred_element_type=F32)

    for h in range(n_heads):
        li = QK_ROPE + h
        lf_lane = QK_ROPE + n_heads + h
        ig_col = jnp.where(t_col < lc, gates[:, li:li + 1], NEG_BIG)
        ig_row = jnp.where(t_row < lc, gates_t[li:li + 1, :], NEG_BIG)
        b_col = b_cols[:, lf_lane:lf_lane + 1]
        b_row = b_rows[lf_lane:lf_lane + 1, :]
        m0 = m_ref[0, 0:1, h:h + 1]
        d = jnp.where(causal, b_col - b_row + ig_row, -jnp.inf)
        a_col = b_col + m0
        m = jnp.maximum(a_col, jnp.max(d, axis=-1, keepdims=True))
        w_inter = jnp.exp(a_col - m)

        q = _pad_rows(q_ref[:, h * dk:(h + 1) * dk], lp)
        k = _pad_rows(k_ref[:, h * dk:(h + 1) * dk], lp) * (dk ** -0.5)
        v = _pad_rows(v_ref[:, h * dv:(h + 1) * dv], lp)
        qk = _dot_nt(q, k) * jnp.exp(d - m)
        c_old = c_ref[0, h]
        n_old = n_ref[0, h:h + 1, :]
        num = w_inter * _dot(q, c_old.astype(BF16)) + _dot(qk.astype(BF16), v)
        den = (w_inter * jnp.sum(q.astype(F32) * n_old, axis=-1, keepdims=True)
               + jnp.sum(qk, axis=-1, keepdims=True))
        hid = num / jnp.maximum(jnp.abs(den), jnp.exp(-m))

        b_last = b_col[lp - 1:lp, :]
        g_row = b_last - b_row + ig_row
        g_col = b_last - b_col + ig_col
        m_new = jnp.maximum(b_last + m0, jnp.max(g_row, axis=-1, keepdims=True))
        decay = jnp.exp(b_last + m0 - m_new)
        kw = k.astype(F32) * jnp.exp(g_col - m_new)
        c_ref[0, h] = decay * c_old + _dot(kw.T.astype(BF16), v)
        n_ref[0, h:h + 1, :] = decay * n_old + jnp.sum(kw, axis=0, keepdims=True)
        m_ref[0, 0:1, h:h + 1] = m_new

        hid = hid[:lc]
        mo = mo_ref[:, h * dv:(h + 1) * dv].astype(F32)
        zb = zb_ref[:, h * dv:(h + 1) * dv].astype(F32)
        out = jax.nn.sigmoid(mo) * (_rms(hid, dv) * gain_ref[:, h * dv:(h + 1) * dv])
        h_ref[:, h * dv:(h + 1) * dv] = (out * (zb * jax.nn.sigmoid(zb))).astype(h_ref.dtype)


def _mlstm(p_bf, p_f32, offs, gate_bias, gain, c0, n0, m0, batch, seq, lc, row0, share_state, n_heads, dk, dv):
    lp = -(-lc // LANE) * LANE
    nch = seq // lc
    blk0 = row0 // lc
    wqk = n_heads * dk
    wv = n_heads * dv
    gate_blk = p_f32.shape[1] // LANE - 1
    rows = lambda col: (lambda b, c: (blk0 + b * nch + c, col))
    state = (lambda b, c: (0, 0, 0, 0)) if share_state else (lambda b, c: (b, 0, 0, 0))
    state3 = (lambda b, c: (0, 0, 0)) if share_state else (lambda b, c: (b, 0, 0))
    out_rows = batch * seq
    return pl.pallas_call(
        functools.partial(_mlstm_kernel, n_heads=n_heads, lc=lc, lp=lp, dk=dk, dv=dv),
        grid=(batch, nch),
        in_specs=[
            pl.BlockSpec((lc, wqk), rows(offs["mq"] // wqk)),
            pl.BlockSpec((lc, wqk), rows(offs["mk"] // wqk)),
            pl.BlockSpec((lc, wv), rows(offs["mv"] // wv)),
            pl.BlockSpec((lc, wv), rows(offs["mo"] // wv)),
            pl.BlockSpec((lc, wv), rows(offs["zb"] // wv)),
            pl.BlockSpec((lc, LANE), rows(gate_blk)),
            pl.BlockSpec((1, LANE), lambda b, c: (0, 0)),
            pl.BlockSpec((1, wv), lambda b, c: (0, 0)),
            pl.BlockSpec((1, n_heads, dk, dv), state),
            pl.BlockSpec((1, n_heads, dk), state3),
            pl.BlockSpec((1, 1, n_heads), state3),
        ],
        out_specs=[
            pl.BlockSpec((lc, wv), lambda b, c: (b * nch + c, 0)),
            pl.BlockSpec((1, n_heads, dk, dv), lambda b, c: (b, 0, 0, 0)),
            pl.BlockSpec((1, n_heads, dk), lambda b, c: (b, 0, 0)),
            pl.BlockSpec((1, 1, n_heads), lambda b, c: (b, 0, 0)),
        ],
        out_shape=[
            jax.ShapeDtypeStruct((out_rows, wv), BF16),
            jax.ShapeDtypeStruct((batch, n_heads, dk, dv), F32),
            jax.ShapeDtypeStruct((batch, n_heads, dk), F32),
            jax.ShapeDtypeStruct((batch, 1, n_heads), F32),
        ],
        compiler_params=_cparams(("parallel", "arbitrary")),
        name="mlstm",
    )(p_bf, p_bf, p_bf, p_bf, p_bf, p_f32, gate_bias, gain, c0, n0, m0)


def _out_proj_kernel(a_ref, m_ref, wa_ref, wm_ref, x_ref, o_ref):
    o_ref[...] = (x_ref[...] + _dot(a_ref[...], wa_ref[...].astype(BF16))
                  + _dot(m_ref[...], wm_ref[...].astype(BF16)))


def _out_proj(a, ml, w_out, x, tm, tn):
    rows, wa = a.shape
    wm = ml.shape[1]
    d = w_out.shape[1]
    assert wa == wm
    return pl.pallas_call(
        _out_proj_kernel,
        grid=(rows // tm, d // tn),
        in_specs=[
            pl.BlockSpec((tm, wa), lambda i, j: (i, 0)),
            pl.BlockSpec((tm, wm), lambda i, j: (i, 0)),
            pl.BlockSpec((wa, tn), lambda i, j: (0, j)),
            pl.BlockSpec((wm, tn), lambda i, j: (1, j)),
            pl.BlockSpec((tm, tn), lambda i, j: (i, j)),
        ],
        out_specs=pl.BlockSpec((tm, tn), lambda i, j: (i, j)),
        out_shape=jax.ShapeDtypeStruct((rows, d), F32),
        compiler_params=_cparams(("parallel", "arbitrary")),
        name="out_proj",
    )(a, ml, w_out, w_out, x)


def _rope_tables(pos):
    inv_freq = ROPE_THETA ** (-jnp.arange(HALF_ROPE, dtype=F32) / HALF_ROPE)
    ang = pos.astype(F32)[:, None] * inv_freq[None, :]
    cos, sin = jnp.cos(ang), jnp.sin(ang)
    zero = jnp.zeros((pos.shape[0], LANE - QK_ROPE), F32)
    return jnp.concatenate([cos, cos, zero], axis=1), jnp.concatenate([-sin, sin, zero], axis=1)


def _pad_lanes(vec):
    return jnp.pad(vec.astype(F32), (0, LANE - vec.shape[0]))[None, :]


def _row_tile(rows, target):
    t = min(rows, target)
    while rows % t:
        t //= 2
    return t


def kernel(x_prompt, x_sample, cache_ckv, cache_kpe, state_C, state_n, state_m, meta_tokens, norm_gain, w_in,
           b_igate, b_fgate, q_a_norm, w_q_up, q_nope_norm, q_rope_norm, kv_a_norm, k_rope_norm, w_kv_up,
           k_nope_norm, ml_out_norm, w_out):
    batch, seq, d = x_prompt.shape
    dbatch, dseq, _ = x_sample.shape
    past = cache_ckv.shape[1]
    n_meta = meta_tokens.shape[0]
    q_lora = q_a_norm.shape[0]
    kv_lora = kv_a_norm.shape[0]
    ml_heads, dv = ml_out_norm.shape
    dk = state_n.shape[-1]
    mla_heads = w_kv_up.shape[1] // (QK_NOPE + V_HEAD)
    mla_w = mla_heads * V_HEAD
    ml_w = ml_heads * dv
    mqk_w = ml_heads * dk
    assert q_lora % COL_TILE == 0 and kv_lora % LANE == 0 and 2 * ml_heads <= LANE - QK_ROPE
    assert mla_w == ml_w and w_out.shape[0] == mla_w + ml_w

    o_cq = 0
    o_ckv = o_cq + q_lora
    o_kpe = o_ckv + kv_lora
    o_mq = o_kpe + QK_ROPE
    o_mk = o_mq + mqk_w
    o_mv = o_mk + mqk_w
    o_mo = o_mv + ml_w
    o_mi = o_mo + ml_w
    o_mf = o_mi + ml_heads
    o_za = o_mf + ml_heads
    o_zb = o_za + mla_w

    offs = {"mq": 0, "mk": mqk_w, "mv": 2 * mqk_w, "mo": 2 * mqk_w + ml_w, "za": 2 * mqk_w + 2 * ml_w,
            "zb": 2 * mqk_w + 2 * ml_w + mla_w, "cq": 2 * mqk_w + 3 * ml_w + mla_w}
    gate_pad = LANE - QK_ROPE - 2 * ml_heads
    w_in_t = w_in.T
    main_segments = [(o_mq, mqk_w, offs["mq"]), (o_mk, mqk_w, offs["mk"]), (o_mv, ml_w, offs["mv"]),
                     (o_mo, ml_w, offs["mo"]), (o_za, mla_w, offs["za"]), (o_zb, ml_w, offs["zb"]),
                     (o_cq, q_lora, offs["cq"])]
    w_small = _stack_rows(w_in_t, [(o_ckv, kv_lora), (o_kpe, QK_ROPE), (o_mi, 2 * ml_heads)], kv_lora + LANE,
                          _row_tile(d, 512))
    gate_bias = jnp.concatenate([jnp.zeros((QK_ROPE,), F32), b_igate.astype(F32), b_fgate.astype(F32),
                                 jnp.zeros((gate_pad,), F32)])[None, :]
    qk_w = QK_NOPE + QK_ROPE
    kv_w = QK_NOPE + V_HEAD
    wq = _regroup(w_q_up, [(h * qk_w, qk_w, h * QK_PAD) for h in range(mla_heads)], mla_heads * QK_PAD,
                  _row_tile(q_lora, 512))
    wk = _regroup(w_kv_up, [(h * kv_w, QK_NOPE, h * QK_NOPE) for h in range(mla_heads)], mla_heads * QK_NOPE,
                  kv_lora)
    wv = _regroup(w_kv_up, [(h * kv_w + QK_NOPE, V_HEAD, h * V_HEAD) for h in range(mla_heads)],
                  mla_heads * V_HEAD, kv_lora)
    gain_row = norm_gain.astype(F32)[None, :]
    qan = q_a_norm.astype(F32)[None, :]
    kvg = kv_a_norm.astype(F32)[None, :]
    qng = q_nope_norm.astype(F32)[None, :]
    kng = k_nope_norm.astype(F32)[None, :]
    qrg = _pad_lanes(q_rope_norm)
    krg = _pad_lanes(k_rope_norm)
    ml_gain = ml_out_norm.astype(F32).reshape(1, ml_w)

    n_p = batch * seq
    n_s = dbatch * dseq
    assert n_s % LANE == 0 and n_meta <= LANE and n_meta % 16 == 0
    rows_s = n_s + n_meta
    xp = x_prompt.reshape(n_p, d)
    xs = jnp.concatenate([x_sample.reshape(n_s, d), meta_tokens.astype(x_sample.dtype)], axis=0)

    pos_s = jnp.concatenate([jnp.tile(n_meta + past + jnp.arange(dseq), dbatch), jnp.arange(n_meta)])
    cos_p, sin_p = _rope_tables(n_meta + jnp.arange(seq))
    cos_s, sin_s = _rope_tables(pos_s)

    tm_p = _row_tile(seq, 512)
    proj_sub = 2 if n_p % (2 * tm_p) == 0 else 1
    w_main, pbf_s, pf_s = _proj_regroup(xs, gain_row, w_in_t, main_segments, w_small)
    pbf_p, pf_p = _proj(xp, gain_row, w_main, w_small, proj_sub * tm_p, proj_sub)

    hps = max(1, mla_heads // 2)
    q_p, ckv_p, kpe_p, kpep_p = _latent_q(pbf_p, pf_p, offs["cq"], q_lora, kv_lora, cos_p, sin_p, qan, wq,
                                          qng, qrg, kvg, krg, tm_p, hps)
    q_s, ckv_s, kpe_s, kpep_s = _latent_q(pbf_s, pf_s, offs["cq"], q_lora, kv_lora, cos_s, sin_s, qan, wq,
                                          qng, qrg, kvg, krg, rows_s, hps)

    small_rows = n_s + LANE
    meta_blk = n_s // LANE
    tail = ((0, small_rows - rows_s), (0, 0))
    k_p, v_p = _kv_expand(ckv_p, kpep_p, wk, wv, kng, tm_p)
    k_s, v_s = _kv_expand(jnp.pad(ckv_s, tail), jnp.pad(kpep_s, tail), wk, wv, kng, small_rows)
    cache_rows = dbatch * past
    cache_kpep = jnp.pad(cache_kpe.reshape(cache_rows, QK_ROPE), ((0, 0), (0, LANE - QK_ROPE))).astype(BF16)
    k_c, v_c = _kv_expand(cache_ckv.reshape(cache_rows, kv_lora).astype(F32), cache_kpep, wk, wv, kng,
                          _row_tile(cache_rows, 512))

    tq = _row_tile(seq, ATTN_TQ)
    attn_p = _attn_prompt(q_p, k_p, v_p, k_s, v_s, meta_blk, pbf_p, offs["za"], batch, seq, tq,
                          min(ATTN_HEADS, mla_heads), n_meta)
    attn_s = _attn_sample(q_s, k_c, v_c, k_s, v_s, pbf_s, offs["za"], dbatch, dseq, past, n_meta,
                          min(4, mla_heads))

    zc = jnp.zeros((1, ml_heads, dk, dv), F32)
    zn = jnp.zeros((1, ml_heads, dk), F32)
    zm = jnp.zeros((1, 1, ml_heads), F32)
    _, c_m, n_m, m_m = _mlstm(pbf_s, pf_s, offs, gate_bias, ml_gain, zc, zn, zm, 1, n_meta, n_meta, n_s, True,
                              ml_heads, dk, dv)
    lc_p = _row_tile(seq, 256)
    ml_p, c_p, n_pr, m_p = _mlstm(pbf_p, pf_p, offs, gate_bias, ml_gain, c_m, n_m, m_m, batch, seq, lc_p, 0,
                                  True, ml_heads, dk, dv)
    ml_s, c_s, n_sm, m_s = _mlstm(pbf_s, pf_s, offs, gate_bias, ml_gain, state_C.astype(F32),
                                  state_n.astype(F32), state_m.astype(F32).reshape(dbatch, 1, ml_heads),
                                  dbatch, dseq, dseq, 0, False, ml_heads, dk, dv)

    y_p = _out_proj(attn_p, ml_p, w_out, xp, _row_tile(n_p, 1024), _row_tile(d, 512))
    y_s = _out_proj(attn_s, ml_s, w_out, xs, n_s, _row_tile(d, 512))

    meta_ckv = ckv_s[n_s:n_s + n_meta]
    meta_kpe = kpe_s[n_s:n_s + n_meta]
    ckv_prompt = jnp.concatenate([jnp.broadcast_to(meta_ckv[None], (batch, n_meta, kv_lora)),
                                  ckv_p.reshape(batch, seq, kv_lora)], axis=1)
    kpe_prompt = jnp.concatenate([jnp.broadcast_to(meta_kpe[None], (batch, n_meta, QK_ROPE)),
                                  kpe_p.reshape(batch, seq, QK_ROPE)], axis=1)
    return (y_p.reshape(batch, seq, d), y_s.reshape(dbatch, dseq, d), ckv_prompt, kpe_prompt,
            c_p, n_pr, m_p.reshape(batch, ml_heads),
            ckv_s[:n_s].reshape(dbatch, dseq, kv_lora), kpe_s[:n_s].reshape(dbatch, dseq, QK_ROPE),
            c_s, n_sm, m_s.reshape(dbatch, ml_heads))
```

```python
import functools
import math

import jax
import jax.numpy as jnp
from jax import lax
from jax.experimental import pallas as pl
from jax.experimental.pallas import tpu as pltpu

CHUNK = 64
EPS = 1e-6
ROPE_THETA = 10000.0
V_HEAD = 128
QK_NOPE = 128
QK_ROPE = 64
HALF_ROPE = QK_ROPE // 2
LANE = 128
QK_PAD = QK_NOPE + LANE
COL_TILE = 512
NEG_BIG = -1e30
ATTN_TQ = 512
ATTN_HEADS = 4
VMEM_LIMIT = 56 * 1024 * 1024

F32 = jnp.float32
BF16 = jnp.bfloat16


def _cparams(sem):
    return pltpu.CompilerParams(dimension_semantics=sem, vmem_limit_bytes=VMEM_LIMIT)


def _dot(a, b):
    return jnp.dot(a, b, preferred_element_type=F32)


def _dot_nt(a, b):
    return lax.dot_general(a, b, (((1,), (1,)), ((), ())), preferred_element_type=F32)


def _rms(x, n):
    return x * lax.rsqrt(jnp.sum(x * x, axis=-1, keepdims=True) * (1.0 / n) + EPS)


def _rope_lanes(x, cos, sin_signed):
    lane = lax.broadcasted_iota(jnp.int32, x.shape, 1)
    first_half = (lane & (QK_ROPE - 1)) < HALF_ROPE
    rot = jnp.where(first_half, pltpu.roll(x, LANE - HALF_ROPE, 1), pltpu.roll(x, HALF_ROPE, 1))
    return x * cos + rot * sin_signed


def _regroup_kernel(w_ref, o_ref, *, segments, zero_ranges):
    for src, n, dst in segments:
        o_ref[:, dst:dst + n] = w_ref[:, src:src + n].astype(o_ref.dtype)
    for dst, n in zero_ranges:
        o_ref[:, dst:dst + n] = jnp.zeros((o_ref.shape[0], n), o_ref.dtype)


def _regroup(w, segments, out_cols, tr):
    rows, cols = w.shape
    covered = sorted((dst, n) for _, n, dst in segments)
    zero_ranges, pos = [], 0
    for dst, n in covered + [(out_cols, 0)]:
        if dst > pos:
            zero_ranges.append((pos, dst - pos))
        pos = dst + n
    return pl.pallas_call(
        functools.partial(_regroup_kernel, segments=tuple(segments), zero_ranges=tuple(zero_ranges)),
        grid=(rows // tr,),
        in_specs=[pl.BlockSpec((tr, cols), lambda i: (i, 0))],
        out_specs=pl.BlockSpec((tr, out_cols), lambda i: (i, 0)),
        out_shape=jax.ShapeDtypeStruct((rows, out_cols), BF16),
        compiler_params=_cparams(("parallel",)),
        name="regroup",
    )(w)


def _stack_rows_kernel(*refs):
    *w_refs, o_ref = refs
    parts = [r[...] for r in w_refs]
    used = sum(p.shape[0] for p in parts)
    parts.append(jnp.zeros((o_ref.shape[0] - used, o_ref.shape[1]), parts[0].dtype))
    o_ref[...] = jnp.concatenate(parts, axis=0).astype(o_ref.dtype)


def _stack_rows(wt, segments, out_rows, tc):
    cols = wt.shape[1]
    return pl.pallas_call(
        _stack_rows_kernel,
        grid=(cols // tc,),
        in_specs=[pl.BlockSpec((pl.Element(n), pl.Element(tc)), functools.partial(lambda i, s: (s, i * tc), s=src))
                  for src, n in segments],
        out_specs=pl.BlockSpec((out_rows, tc), lambda i: (0, i)),
        out_shape=jax.ShapeDtypeStruct((out_rows, cols), BF16),
        compiler_params=_cparams(("parallel",)),
        name="stack_rows",
    )(*([wt] * len(segments)))


def _proj_regroup_kernel(x_ref, g_ref, w_ref, ws_ref, om_ref, o_ref, os_ref, hn_ref):
    @pl.when(pl.program_id(0) == 0)
    def _():
        x = x_ref[...]
        hn = (_rms(x, x.shape[-1]) * g_ref[...]).astype(BF16)
        hn_ref[...] = hn
        os_ref[...] = _dot_nt(hn, ws_ref[...])

    wb = w_ref[...].astype(BF16)
    om_ref[...] = wb
    o_ref[...] = _dot_nt(hn_ref[...], wb).astype(o_ref.dtype)


def _proj_regroup(x, gain, wt, segments, w_small):
    m, d = x.shape
    ns = w_small.shape[0]
    n = sum(sz for _, sz, _ in segments)
    merged = []
    for src, sz, dst in sorted(segments, key=lambda s: s[2]):
        if merged and merged[-1][0] + merged[-1][1] == src and merged[-1][2] + merged[-1][1] == dst:
            merged[-1] = (merged[-1][0], merged[-1][1] + sz, merged[-1][2])
        else:
            merged.append((src, sz, dst))
    segments = merged
    assert all(sz % COL_TILE == 0 and dst % COL_TILE == 0 for _, sz, dst in segments)
    align = functools.reduce(math.gcd, [src for src, _, _ in segments], COL_TILE)

    def src_row(j):
        r = j * COL_TILE
        out = 0
        for src, sz, dst in segments:
            out = jnp.where((r >= dst) & (r < dst + sz), src + r - dst, out)
        return out

    const = lambda j: (0, 0)
    return pl.pallas_call(
        _proj_regroup_kernel,
        grid=(n // COL_TILE,),
        in_specs=[
            pl.BlockSpec((m, d), const),
            pl.BlockSpec((1, d), const),
            pl.BlockSpec((pl.Element(COL_TILE), pl.Element(d)), lambda j: (pl.multiple_of(src_row(j), align), 0)),
            pl.BlockSpec((ns, d), const),
        ],
        out_specs=[
            pl.BlockSpec((COL_TILE, d), lambda j: (j, 0)),
            pl.BlockSpec((m, COL_TILE), lambda j: (0, j)),
            pl.BlockSpec((m, ns), const),
        ],
        out_shape=[jax.ShapeDtypeStruct((n, d), BF16), jax.ShapeDtypeStruct((m, n), BF16),
                   jax.ShapeDtypeStruct((m, ns), F32)],
        scratch_shapes=[pltpu.VMEM((m, d), BF16)],
        compiler_params=_cparams(("arbitrary",)),
        name="proj_regroup",
    )(x, gain, wt, w_small)


def _proj_kernel(x_ref, g_ref, w_ref, ws_ref, o_ref, os_ref, hn_ref, *, n_sub):
    j = pl.program_id(1)
    ts = x_ref.shape[0]

    @pl.when(j < n_sub)
    def _():
        x = x_ref[...]
        hn = (_rms(x, x.shape[-1]) * g_ref[...]).astype(BF16)
        hn_ref[pl.ds(pl.multiple_of(j * ts, ts), ts), :] = hn
        os_ref[...] = _dot_nt(hn, ws_ref[...])

    @pl.when(j >= n_sub)
    def _():
        o_ref[...] = _dot_nt(hn_ref[...], w_ref[...]).astype(o_ref.dtype)


def _proj(x, gain, w_main, w_small, tm, n_sub):
    m, d = x.shape
    n = w_main.shape[0]
    ns = w_small.shape[0]
    ts = tm // n_sub
    sub = lambda i, j: (i * n_sub + jnp.minimum(j, n_sub - 1), 0)
    feat = lambda j: jnp.maximum(j - n_sub, 0)
    return pl.pallas_call(
        functools.partial(_proj_kernel, n_sub=n_sub),
        grid=(m // tm, n_sub + n // COL_TILE),
        in_specs=[
            pl.BlockSpec((ts, d), sub),
            pl.BlockSpec((1, d), lambda i, j: (0, 0)),
            pl.BlockSpec((COL_TILE, d), lambda i, j: (feat(j), 0)),
            pl.BlockSpec((ns, d), lambda i, j: (0, 0)),
        ],
        out_specs=[
            pl.BlockSpec((tm, COL_TILE), lambda i, j: (i, feat(j))),
            pl.BlockSpec((ts, ns), sub),
        ],
        out_shape=[jax.ShapeDtypeStruct((m, n), BF16), jax.ShapeDtypeStruct((m, ns), F32)],
        scratch_shapes=[pltpu.VMEM((tm, d), BF16)],
        compiler_params=_cparams(("parallel", "arbitrary")),
        name="proj",
    )(x, gain, w_main, w_small)


def _latent_q_kernel(*refs, n_cq, kv_lora, heads_per_step):
    cq_refs = refs[:n_cq]
    (pf_ref, cos_ref, sin_ref, qan_ref, wqn_ref, wqp_ref, qng_ref, qrg_ref, kvg_ref, krg_ref,
     q_ref, ckv_ref, kpe_ref, kpep_ref, cqn_ref) = refs[n_cq:]
    cos = cos_ref[...]
    sin = sin_ref[...]
    lane = lax.broadcasted_iota(jnp.int32, cos.shape, 1)
    low = lane < QK_ROPE

    @pl.when(pl.program_id(1) == 0)
    def _():
        cq = jnp.concatenate([r[...].astype(F32) for r in cq_refs], axis=1)
        cqn_ref[...] = (_rms(cq, cq.shape[-1]) * qan_ref[...]).astype(BF16)
        pf = pf_ref[...]
        ckv_raw = pf[:, :kv_lora]
        ckv_ref[...] = _rms(ckv_raw, kv_lora) * kvg_ref[...]
        kraw = jnp.where(low, pf[:, kv_lora:kv_lora + LANE], 0.0)
        kpe = _rope_lanes(_rms(kraw, QK_ROPE) * krg_ref[...], cos, sin)
        kpe_ref[...] = kpe[:, :QK_ROPE]
        kpep_ref[...] = kpe.astype(BF16)

    cqn = cqn_ref[...]
    group = 4 if heads_per_step % 4 == 0 else 2
    for g0 in range(0, heads_per_step, group):
        qn = _dot(cqn, wqn_ref[:, g0 * QK_NOPE:(g0 + group) * QK_NOPE])
        qp = _dot(cqn, wqp_ref[:, g0 * QK_ROPE:(g0 + group) * QK_ROPE])
        for pair in range(group // 2):
            x = qp[:, pair * LANE:(pair + 1) * LANE]
            x2 = x * x
            ss_a = jnp.sum(jnp.where(low, x2, 0.0), axis=-1, keepdims=True)
            ss_b = jnp.sum(jnp.where(low, 0.0, x2), axis=-1, keepdims=True)
            scale = jnp.where(low, lax.rsqrt(ss_a * (1.0 / QK_ROPE) + EPS),
                              lax.rsqrt(ss_b * (1.0 / QK_ROPE) + EPS))
            r = _rope_lanes(x * scale * qrg_ref[...], cos, sin)
            halves = (jnp.where(low, r, 0.0), jnp.where(low, pltpu.roll(r, QK_ROPE, 1), 0.0))
            for h in (0, 1):
                base = (g0 + 2 * pair + h) * QK_PAD
                nope = qn[:, (2 * pair + h) * QK_NOPE:(2 * pair + h + 1) * QK_NOPE]
                q_ref[:, base:base + QK_NOPE] = (_rms(nope, QK_NOPE) * qng_ref[...]).astype(BF16)
                q_ref[:, base + QK_NOPE:base + QK_PAD] = halves[h].astype(BF16)


def _latent_q(p_bf, p_f32, cq_off, q_lora, kv_lora, cos, sin, qan, wqn, wqp, qng, qrg, kvg, krg, tm,
              heads_per_step):
    m = p_bf.shape[0]
    n_heads = wqn.shape[1] // QK_NOPE
    n_cq = q_lora // COL_TILE
    cq0 = cq_off // COL_TILE
    ns = p_f32.shape[1]
    wstep = heads_per_step * QK_PAD
    assert heads_per_step % 2 == 0
    row = lambda i, j: (i, 0)
    const = lambda i, j: (0, 0)
    pos_blocks = cos.shape[0] // tm
    pos = lambda i, j: (i % pos_blocks, 0)
    in_specs = [pl.BlockSpec((tm, COL_TILE), functools.partial(lambda i, j, c: (i, c), c=cq0 + c))
                for c in range(n_cq)]
    in_specs += [
        pl.BlockSpec((tm, ns), row),
        pl.BlockSpec((tm, LANE), pos),
        pl.BlockSpec((tm, LANE), pos),
        pl.BlockSpec((1, q_lora), const),
        pl.BlockSpec((q_lora, heads_per_step * QK_NOPE), lambda i, j: (0, j)),
        pl.BlockSpec((q_lora, heads_per_step * QK_ROPE), lambda i, j: (0, j)),
        pl.BlockSpec((1, LANE), const),
        pl.BlockSpec((1, LANE), const),
        pl.BlockSpec((1, kv_lora), const),
        pl.BlockSpec((1, LANE), const),
    ]
    return pl.pallas_call(
        functools.partial(_latent_q_kernel, n_cq=n_cq, kv_lora=kv_lora, heads_per_step=heads_per_step),
        grid=(m // tm, n_heads // heads_per_step),
        in_specs=in_specs,
        out_specs=[
            pl.BlockSpec((tm, wstep), lambda i, j: (i, j)),
            pl.BlockSpec((tm, kv_lora), row),
            pl.BlockSpec((tm, QK_ROPE), row),
            pl.BlockSpec((tm, LANE), row),
        ],
        out_shape=[
            jax.ShapeDtypeStruct((m, n_heads * QK_PAD), BF16),
            jax.ShapeDtypeStruct((m, kv_lora), F32),
            jax.ShapeDtypeStruct((m, QK_ROPE), F32),
            jax.ShapeDtypeStruct((m, LANE), BF16),
        ],
        scratch_shapes=[pltpu.VMEM((tm, q_lora), BF16)],
        compiler_params=_cparams(("parallel", "arbitrary")),
        name="latent_q",
    )(*([p_bf] * n_cq), p_f32, cos, sin, qan, wqn, wqp, qng, qrg, kvg, krg)


def _kv_expand_kernel(ckv_ref, kpep_ref, wk_ref, wv_ref, kng_ref, k_ref, v_ref, *, n_heads):
    c = ckv_ref[...].astype(BF16)
    kf = _dot(c, wk_ref[...])
    v_ref[...] = _dot(c, wv_ref[...]).astype(BF16)
    kpep = kpep_ref[...]
    for h in range(n_heads):
        kn = _rms(kf[:, h * QK_NOPE:(h + 1) * QK_NOPE], QK_NOPE) * kng_ref[...]
        k_ref[:, h * QK_PAD:h * QK_PAD + QK_NOPE] = kn.astype(BF16)
        k_ref[:, h * QK_PAD + QK_NOPE:(h + 1) * QK_PAD] = kpep


def _kv_expand(ckv, kpep, wk, wv, kng, tm):
    r, kv_lora = ckv.shape
    n_heads = wk.shape[1] // QK_NOPE
    row = lambda i: (i, 0)
    const = lambda i: (0, 0)
    return pl.pallas_call(
        functools.partial(_kv_expand_kernel, n_heads=n_heads),
        grid=(r // tm,),
        in_specs=[
            pl.BlockSpec((tm, kv_lora), row),
            pl.BlockSpec((tm, LANE), row),
            pl.BlockSpec((kv_lora, n_heads * QK_NOPE), const),
            pl.BlockSpec((kv_lora, n_heads * V_HEAD), const),
            pl.BlockSpec((1, LANE), const),
        ],
        out_specs=[
            pl.BlockSpec((tm, n_heads * QK_PAD), row),
            pl.BlockSpec((tm, n_heads * V_HEAD), row),
        ],
        out_shape=[
            jax.ShapeDtypeStruct((r, n_heads * QK_PAD), BF16),
            jax.ShapeDtypeStruct((r, n_heads * V_HEAD), BF16),
        ],
        compiler_params=_cparams(("parallel",)),
        name="kv_expand",
    )(ckv, kpep, wk, wv, kng)


def _lane_chunk_sum(p):
    out = p[:, :LANE]
    for c in range(1, p.shape[1] // LANE):
        out = out + p[:, c * LANE:(c + 1) * LANE]
    return out


def _gated(lvec, acc, za):
    za = za.astype(F32)
    return acc / jnp.sum(lvec, axis=-1, keepdims=True) * (za * jax.nn.sigmoid(za))


def _softmax_scale():
    return float((QK_NOPE + QK_ROPE) ** -0.5 * 1.4426950408889634)


def _attn_prompt_block(n_full, q_ref, k_ref, km_ref, za_ref, o_ref, s_ref, p_ref, va_ref, vma_ref, *,
                       tq, heads, c, n_meta):
    td = min(tq, 256)
    pieces = [(i * tq, tq, 0, False) for i in range(n_full)]
    pieces += [(n_full * tq + j * td, td, j * td, True) for j in range(tq // td)]

    def upd(full, r0, fn):
        return fn(full) if r0 == 0 else jnp.concatenate([full[:r0], fn(full[r0:])], axis=0)

    def scores(h):
        qs = slice(h * QK_PAD, (h + 1) * QK_PAD)
        q = q_ref[:, qs]
        col = lax.broadcasted_iota(jnp.int32, (tq, LANE), 1)
        t = jnp.where(col < n_meta, _dot_nt(q, km_ref[:, qs]) * c, -jnp.inf)
        s_ref[h, :, :LANE] = t
        mx = t
        off = LANE
        for start, rows, r0, masked in pieces:
            t = _dot_nt(q[r0:], k_ref[start:start + rows, qs]) * c
            if masked:
                rq = lax.broadcasted_iota(jnp.int32, t.shape, 0) // CHUNK
                ck = lax.broadcasted_iota(jnp.int32, t.shape, 1) // CHUNK
                t = jnp.where(rq >= ck, t, -jnp.inf)
            s_ref[h, r0:, off:off + rows] = t
            tmax = t[:, :LANE]
            for ch in range(1, rows // LANE):
                tmax = jnp.maximum(tmax, t[:, ch * LANE:(ch + 1) * LANE])
            mx = upd(mx, r0, lambda a: jnp.maximum(a, tmax))
            off += rows
        return jnp.max(mx, axis=-1, keepdims=True)

    def values(h, m):
        vs = slice(h * V_HEAD, (h + 1) * V_HEAD)
        va = slice(2 * h * V_HEAD, 2 * (h + 1) * V_HEAD)
        p_ref[h, :, :LANE] = jnp.exp2(s_ref[h, :, :LANE] - m).astype(BF16)
        off = LANE
        for start, rows, r0, _ in pieces:
            p_ref[h, r0:, off:off + rows] = jnp.exp2(s_ref[h, r0:, off:off + rows] - m[r0:]).astype(BF16)
            off += rows
        acc = _dot(p_ref[h, :, :LANE], vma_ref[:, va])
        n_main = n_full * tq
        if n_main:
            acc = acc + _dot(p_ref[h, :, LANE:LANE + n_main], va_ref[:n_main, va])
        off = LANE + n_main
        for start, rows, r0, _ in pieces[n_full:]:
            pv = _dot(p_ref[h, r0:, off:off + rows], va_ref[start:start + rows, va])
            acc = upd(acc, r0, lambda a: a + pv)
            off += rows
        za = za_ref[:, vs].astype(F32)
        out = acc[:, :V_HEAD] / acc[:, V_HEAD:V_HEAD + 1] * (za * jax.nn.sigmoid(za))
        o_ref[:, vs] = out.astype(o_ref.dtype)

    maxes = [scores(h) for h in range(heads)]
    for h in range(heads):
        values(h, maxes[h])


def _attn_prompt_kernel(q_ref, k_ref, v_ref, km_ref, vm_ref, za_ref, o_ref, s_ref, p_ref, va_ref, vma_ref, *,
                        nqb, heads, **kw):
    qi = pl.program_id(2)

    @pl.when(qi == 0)
    def _():
        for h in range(heads):
            vs = slice(h * V_HEAD, (h + 1) * V_HEAD)
            va_ref[:, 2 * h * V_HEAD:(2 * h + 1) * V_HEAD] = v_ref[:, vs]
            va_ref[:, (2 * h + 1) * V_HEAD:(2 * h + 2) * V_HEAD] = jnp.ones((va_ref.shape[0], V_HEAD), BF16)
            vma_ref[:, 2 * h * V_HEAD:(2 * h + 1) * V_HEAD] = vm_ref[:, vs]
            vma_ref[:, (2 * h + 1) * V_HEAD:(2 * h + 2) * V_HEAD] = jnp.ones((vma_ref.shape[0], V_HEAD), BF16)

    for n_full in range(nqb):
        pl.when(qi == n_full)(
            functools.partial(_attn_prompt_block, n_full, q_ref, k_ref, km_ref, za_ref, o_ref, s_ref, p_ref, va_ref,
                              vma_ref, heads=heads, **kw))


def _attn_prompt(q, k, v, k_small, v_small, meta_blk, p_bf, za_off, batch, seq, tq, heads, n_meta):
    n_heads = q.shape[1] // QK_PAD
    nqb = seq // tq
    za0 = za_off // (V_HEAD * heads)
    assert za_off % (V_HEAD * heads) == 0 and n_heads % heads == 0 and tq % CHUNK == 0
    return pl.pallas_call(
        functools.partial(_attn_prompt_kernel, nqb=nqb, tq=tq, heads=heads, c=_softmax_scale(), n_meta=n_meta),
        grid=(batch, n_heads // heads, nqb),
        in_specs=[
            pl.BlockSpec((tq, heads * QK_PAD), lambda b, h, i: (b * nqb + i, h)),
            pl.BlockSpec((seq, heads * QK_PAD), lambda b, h, i: (b, h)),
            pl.BlockSpec((seq, heads * V_HEAD), lambda b, h, i: (b, h)),
            pl.BlockSpec((LANE, heads * QK_PAD), lambda b, h, i: (meta_blk, h)),
            pl.BlockSpec((LANE, heads * V_HEAD), lambda b, h, i: (meta_blk, h)),
            pl.BlockSpec((tq, heads * V_HEAD), lambda b, h, i: (b * nqb + i, za0 + h)),
        ],
        out_specs=pl.BlockSpec((tq, heads * V_HEAD), lambda b, h, i: (b * nqb + i, h)),
        out_shape=jax.ShapeDtypeStruct((batch * seq, n_heads * V_HEAD), BF16),
        scratch_shapes=[pltpu.VMEM((heads, tq, LANE + seq), F32), pltpu.VMEM((heads, tq, LANE + seq), BF16),
                        pltpu.VMEM((seq, 2 * heads * V_HEAD), BF16), pltpu.VMEM((LANE, 2 * heads * V_HEAD), BF16)],
        compiler_params=_cparams(("parallel", "parallel", "arbitrary")),
        name="attn_prompt",
    )(q, k, v, k_small, v_small, p_bf)


def _attn_sample_kernel(q_ref, kc_ref, vc_ref, ks_ref, vs_ref, za_ref, o_ref, *, heads, lq, c, n_meta, meta_row):
    b = pl.program_id(0)
    col = lax.broadcasted_iota(jnp.int32, (lq, ks_ref.shape[0]), 1)
    own = (col >= b * lq) & (col < (b + 1) * lq)
    meta = (col >= meta_row) & (col < meta_row + n_meta)
    visible = own | meta
    for h in range(heads):
        qs = slice(h * QK_PAD, (h + 1) * QK_PAD)
        vs = slice(h * V_HEAD, (h + 1) * V_HEAD)
        q = q_ref[:, qs]
        t_cache = _dot_nt(q, kc_ref[:, qs]) * c
        t_new = jnp.where(visible, _dot_nt(q, ks_ref[:, qs]) * c, -jnp.inf)
        m = jnp.maximum(jnp.max(t_cache, axis=-1, keepdims=True), jnp.max(t_new, axis=-1, keepdims=True))
        p_cache = jnp.exp2(t_cache - m)
        p_new = jnp.exp2(t_new - m)
        lvec = _lane_chunk_sum(p_cache) + _lane_chunk_sum(p_new)
        acc = _dot(p_cache.astype(BF16), vc_ref[:, vs]) + _dot(p_new.astype(BF16), vs_ref[:, vs])
        o_ref[:, vs] = _gated(lvec, acc, za_ref[:, vs]).astype(o_ref.dtype)


def _attn_sample(q, kc, vc, ks, vs, p_bf, za_off, batch, lq, past, n_meta, heads):
    n_heads = q.shape[1] // QK_PAD
    rows = ks.shape[0]
    za0 = za_off // (V_HEAD * heads)
    assert za_off % (V_HEAD * heads) == 0 and n_heads % heads == 0
    return pl.pallas_call(
        functools.partial(_attn_sample_kernel, heads=heads, lq=lq, c=_softmax_scale(), n_meta=n_meta,
                          meta_row=batch * lq),
        grid=(batch, n_heads // heads),
        in_specs=[
            pl.BlockSpec((lq, heads * QK_PAD), lambda b, h: (b, h)),
            pl.BlockSpec((past, heads * QK_PAD), lambda b, h: (b, h)),
            pl.BlockSpec((past, heads * V_HEAD), lambda b, h: (b, h)),
            pl.BlockSpec((rows, heads * QK_PAD), lambda b, h: (0, h)),
            pl.BlockSpec((rows, heads * V_HEAD), lambda b, h: (0, h)),
            pl.BlockSpec((lq, heads * V_HEAD), lambda b, h: (b, za0 + h)),
        ],
        out_specs=pl.BlockSpec((lq, heads * V_HEAD), lambda b, h: (b, h)),
        out_shape=jax.ShapeDtypeStruct((batch * lq, n_heads * V_HEAD), BF16),
        compiler_params=_cparams(("parallel", "arbitrary")),
        name="attn_sample",
    )(q, kc, vc, ks, vs, p_bf)


def _log_sigmoid(x):
    return jnp.minimum(x, 0.0) - jnp.log1p(jnp.exp(-jnp.abs(x)))


def _pad_rows(a, rows):
    if a.shape[0] == rows:
        return a
    return jnp.concatenate([a, jnp.zeros((rows - a.shape[0], a.shape[1]), a.dtype)], axis=0)


def _mlstm_kernel(q_ref, k_ref, v_ref, mo_ref, zb_ref, g_ref, gb_ref, gain_ref, c0_ref, n0_ref, m0_ref,
                  h_ref, c_ref, n_ref, m_ref, *, n_heads, lc, lp, dk, dv):
    @pl.when(pl.program_id(1) == 0)
    def _():
        c_ref[...] = c0_ref[...]
        n_ref[...] = n0_ref[...]
        m_ref[...] = m0_ref[...]

    hi = lax.Precision.HIGHEST
    gates = _pad_rows(g_ref[...] + gb_ref[...], lp)
    gates_t = gates.T
    t_col = lax.broadcasted_iota(jnp.int32, (lp, 1), 0)
    t_row = lax.broadcasted_iota(jnp.int32, (1, lp), 1)
    r_idx = lax.broadcasted_iota(jnp.int32, (lp, lp), 0)
    c_idx = lax.broadcasted_iota(jnp.int32, (lp, lp), 1)
    causal = r_idx >= c_idx
    lower = causal.astype(F32)
    upper = (c_idx >= r_idx).astype(F32)
    b_cols = jnp.dot(lower, jnp.where(t_col < lc, _log_sigmoid(gates), 0.0), precision=hi,
                     preferred_element_type=F32)
    b_rows = jnp.dot(jnp.where(t_row < lc, _log_sigmoid(gates_t), 0.0), upper, precision=hi,
                     preferred_element_type=F32)

    for h in range(n_heads):
        li = QK_ROPE + h
        lf_lane = QK_ROPE + n_heads + h
        ig_col = jnp.where(t_col < lc, gates[:, li:li + 1], NEG_BIG)
        ig_row = jnp.where(t_row < lc, gates_t[li:li + 1, :], NEG_BIG)
        b_col = b_cols[:, lf_lane:lf_lane + 1]
        b_row = b_rows[lf_lane:lf_lane + 1, :]
        m0 = m_ref[0, 0:1, h:h + 1]
        d = jnp.where(causal, b_col - b_row + ig_row, -jnp.inf)
        a_col = b_col + m0
        m = jnp.maximum(a_col, jnp.max(d, axis=-1, keepdims=True))
        w_inter = jnp.exp(a_col - m)

        q = _pad_rows(q_ref[:, h * dk:(h + 1) * dk], lp)
        k = _pad_rows(k_ref[:, h * dk:(h + 1) * dk], lp) * (dk ** -0.5)
        v = _pad_rows(v_ref[:, h * dv:(h + 1) * dv], lp)
        qk = _dot_nt(q, k) * jnp.exp(d - m)
        c_old = c_ref[0, h]
        n_old = n_ref[0, h:h + 1, :]
        num = w_inter * _dot(q, c_old.astype(BF16)) + _dot(qk.astype(BF16), v)
        den = (w_inter * jnp.sum(q.astype(F32) * n_old, axis=-1, keepdims=True)
               + jnp.sum(qk, axis=-1, keepdims=True))
        denc = jnp.maximum(jnp.abs(den), jnp.exp(-m))

        b_last = b_col[lp - 1:lp, :]
        g_row = b_last - b_row + ig_row
        g_col = b_last - b_col + ig_col
        m_new = jnp.maximum(b_last + m0, jnp.max(g_row, axis=-1, keepdims=True))
        decay = jnp.exp(b_last + m0 - m_new)
        kw = k.astype(F32) * jnp.exp(g_col - m_new)
        c_ref[0, h] = decay * c_old + _dot(kw.T.astype(BF16), v)
        n_ref[0, h:h + 1, :] = decay * n_old + jnp.sum(kw, axis=0, keepdims=True)
        m_ref[0, 0:1, h:h + 1] = m_new

        num = num[:lc]
        denc = denc[:lc]
        normed = num * lax.rsqrt(jnp.sum(num * num, axis=-1, keepdims=True) * (1.0 / dv) + EPS * denc * denc)
        mo = mo_ref[:, h * dv:(h + 1) * dv].astype(F32)
        zb = zb_ref[:, h * dv:(h + 1) * dv].astype(F32)
        out = jax.nn.sigmoid(mo) * (normed * gain_ref[:, h * dv:(h + 1) * dv])
        h_ref[:, h * dv:(h + 1) * dv] = (out * (zb * jax.nn.sigmoid(zb))).astype(h_ref.dtype)


def _mlstm(p_bf, p_f32, offs, gate_bias, gain, c0, n0, m0, batch, seq, lc, row0, share_state, n_heads, dk, dv):
    lp = -(-lc // LANE) * LANE
    nch = seq // lc
    blk0 = row0 // lc
    wqk = n_heads * dk
    wv = n_heads * dv
    gate_blk = p_f32.shape[1] // LANE - 1
    rows = lambda col: (lambda b, c: (blk0 + b * nch + c, col))
    state = (lambda b, c: (0, 0, 0, 0)) if share_state else (lambda b, c: (b, 0, 0, 0))
    state3 = (lambda b, c: (0, 0, 0)) if share_state else (lambda b, c: (b, 0, 0))
    out_rows = batch * seq
    return pl.pallas_call(
        functools.partial(_mlstm_kernel, n_heads=n_heads, lc=lc, lp=lp, dk=dk, dv=dv),
        grid=(batch, nch),
        in_specs=[
            pl.BlockSpec((lc, wqk), rows(offs["mq"] // wqk)),
            pl.BlockSpec((lc, wqk), rows(offs["mk"] // wqk)),
            pl.BlockSpec((lc, wv), rows(offs["mv"] // wv)),
            pl.BlockSpec((lc, wv), rows(offs["mo"] // wv)),
            pl.BlockSpec((lc, wv), rows(offs["zb"] // wv)),
            pl.BlockSpec((lc, LANE), rows(gate_blk)),
            pl.BlockSpec((1, LANE), lambda b, c: (0, 0)),
            pl.BlockSpec((1, wv), lambda b, c: (0, 0)),
            pl.BlockSpec((1, n_heads, dk, dv), state),
            pl.BlockSpec((1, n_heads, dk), state3),
            pl.BlockSpec((1, 1, n_heads), state3),
        ],
        out_specs=[
            pl.BlockSpec((lc, wv), lambda b, c: (b * nch + c, 0)),
            pl.BlockSpec((1, n_heads, dk, dv), lambda b, c: (b, 0, 0, 0)),
            pl.BlockSpec((1, n_heads, dk), lambda b, c: (b, 0, 0)),
            pl.BlockSpec((1, 1, n_heads), lambda b, c: (b, 0, 0)),
        ],
        out_shape=[
            jax.ShapeDtypeStruct((out_rows, wv), BF16),
            jax.ShapeDtypeStruct((batch, n_heads, dk, dv), F32),
            jax.ShapeDtypeStruct((batch, n_heads, dk), F32),
            jax.ShapeDtypeStruct((batch, 1, n_heads), F32),
        ],
        compiler_params=_cparams(("parallel", "arbitrary")),
        name="mlstm",
    )(p_bf, p_bf, p_bf, p_bf, p_bf, p_f32, gate_bias, gain, c0, n0, m0)


def _out_proj_kernel(a_ref, m_ref, wa_ref, wm_ref, x_ref, o_ref):
    o_ref[...] = (x_ref[...] + _dot(a_ref[...], wa_ref[...].astype(BF16))
                  + _dot(m_ref[...], wm_ref[...].astype(BF16)))


def _out_proj(a, ml, w_out, x, tm, tn):
    rows, wa = a.shape
    wm = ml.shape[1]
    d = w_out.shape[1]
    assert wa == wm
    return pl.pallas_call(
        _out_proj_kernel,
        grid=(rows // tm, d // tn),
        in_specs=[
            pl.BlockSpec((tm, wa), lambda i, j: (i, 0)),
            pl.BlockSpec((tm, wm), lambda i, j: (i, 0)),
            pl.BlockSpec((wa, tn), lambda i, j: (0, j)),
            pl.BlockSpec((wm, tn), lambda i, j: (1, j)),
            pl.BlockSpec((tm, tn), lambda i, j: (i, j)),
        ],
        out_specs=pl.BlockSpec((tm, tn), lambda i, j: (i, j)),
        out_shape=jax.ShapeDtypeStruct((rows, d), F32),
        compiler_params=_cparams(("parallel", "arbitrary")),
        name="out_proj",
    )(a, ml, w_out, w_out, x)


def _rope_tables(pos):
    inv_freq = ROPE_THETA ** (-jnp.arange(HALF_ROPE, dtype=F32) / HALF_ROPE)
    ang = pos.astype(F32)[:, None] * inv_freq[None, :]
    cos, sin = jnp.cos(ang), jnp.sin(ang)
    return jnp.concatenate([cos, cos, cos, cos], axis=1), jnp.concatenate([-sin, sin, -sin, sin], axis=1)


def _pad_lanes(vec):
    return jnp.pad(vec.astype(F32), (0, LANE - vec.shape[0]))[None, :]


def _row_tile(rows, target):
    t = min(rows, target)
    while rows % t:
        t //= 2
    return t


def kernel(x_prompt, x_sample, cache_ckv, cache_kpe, state_C, state_n, state_m, meta_tokens, norm_gain, w_in,
           b_igate, b_fgate, q_a_norm, w_q_up, q_nope_norm, q_rope_norm, kv_a_norm, k_rope_norm, w_kv_up,
           k_nope_norm, ml_out_norm, w_out):
    batch, seq, d = x_prompt.shape
    dbatch, dseq, _ = x_sample.shape
    past = cache_ckv.shape[1]
    n_meta = meta_tokens.shape[0]
    q_lora = q_a_norm.shape[0]
    kv_lora = kv_a_norm.shape[0]
    ml_heads, dv = ml_out_norm.shape
    dk = state_n.shape[-1]
    mla_heads = w_kv_up.shape[1] // (QK_NOPE + V_HEAD)
    mla_w = mla_heads * V_HEAD
    ml_w = ml_heads * dv
    mqk_w = ml_heads * dk
    assert q_lora % COL_TILE == 0 and kv_lora % LANE == 0 and 2 * ml_heads <= LANE - QK_ROPE
    assert mla_w == ml_w and w_out.shape[0] == mla_w + ml_w

    o_cq = 0
    o_ckv = o_cq + q_lora
    o_kpe = o_ckv + kv_lora
    o_mq = o_kpe + QK_ROPE
    o_mk = o_mq + mqk_w
    o_mv = o_mk + mqk_w
    o_mo = o_mv + ml_w
    o_mi = o_mo + ml_w
    o_mf = o_mi + ml_heads
    o_za = o_mf + ml_heads
    o_zb = o_za + mla_w

    offs = {"mq": 0, "mk": mqk_w, "mv": 2 * mqk_w, "mo": 2 * mqk_w + ml_w, "za": 2 * mqk_w + 2 * ml_w,
            "zb": 2 * mqk_w + 2 * ml_w + mla_w, "cq": 2 * mqk_w + 3 * ml_w + mla_w}
    gate_pad = LANE - QK_ROPE - 2 * ml_heads
    w_in_t = w_in.T
    main_segments = [(o_mq, mqk_w, offs["mq"]), (o_mk, mqk_w, offs["mk"]), (o_mv, ml_w, offs["mv"]),
                     (o_mo, ml_w, offs["mo"]), (o_za, mla_w, offs["za"]), (o_zb, ml_w, offs["zb"]),
                     (o_cq, q_lora, offs["cq"])]
    w_small = _stack_rows(w_in_t, [(o_ckv, kv_lora), (o_kpe, QK_ROPE), (o_mi, 2 * ml_heads)], kv_lora + LANE,
                          _row_tile(d, 512))
    gate_bias = jnp.concatenate([jnp.zeros((QK_ROPE,), F32), b_igate.astype(F32), b_fgate.astype(F32),
                                 jnp.zeros((gate_pad,), F32)])[None, :]
    qk_w = QK_NOPE + QK_ROPE
    kv_w = QK_NOPE + V_HEAD
    wqn = _regroup(w_q_up, [(h * qk_w, QK_NOPE, h * QK_NOPE) for h in range(mla_heads)], mla_heads * QK_NOPE,
                   _row_tile(q_lora, 512))
    wqp = _regroup(w_q_up, [(h * qk_w + QK_NOPE, QK_ROPE, h * QK_ROPE) for h in range(mla_heads)],
                   mla_heads * QK_ROPE, _row_tile(q_lora, 512))
    wk = _regroup(w_kv_up, [(h * kv_w, QK_NOPE, h * QK_NOPE) for h in range(mla_heads)], mla_heads * QK_NOPE,
                  kv_lora)
    wv = _regroup(w_kv_up, [(h * kv_w + QK_NOPE, V_HEAD, h * V_HEAD) for h in range(mla_heads)],
                  mla_heads * V_HEAD, kv_lora)
    gain_row = norm_gain.astype(F32)[None, :]
    qan = q_a_norm.astype(F32)[None, :]
    kvg = kv_a_norm.astype(F32)[None, :]
    qng = q_nope_norm.astype(F32)[None, :]
    kng = k_nope_norm.astype(F32)[None, :]
    qrg = jnp.tile(q_rope_norm.astype(F32), LANE // QK_ROPE)[None, :]
    krg = _pad_lanes(k_rope_norm)
    ml_gain = ml_out_norm.astype(F32).reshape(1, ml_w)

    n_p = batch * seq
    n_s = dbatch * dseq
    assert n_s % LANE == 0 and n_meta <= LANE and n_meta % 16 == 0
    rows_s = n_s + n_meta
    xp = x_prompt.reshape(n_p, d)
    xs = jnp.concatenate([x_sample.reshape(n_s, d), meta_tokens.astype(x_sample.dtype)], axis=0)

    pos_s = jnp.concatenate([jnp.tile(n_meta + past + jnp.arange(dseq), dbatch), jnp.arange(n_meta)])
    cos_p, sin_p = _rope_tables(n_meta + jnp.arange(seq))
    cos_s, sin_s = _rope_tables(pos_s)

    tm_p = _row_tile(seq, 512)
    proj_sub = 2 if n_p % (2 * tm_p) == 0 else 1
    w_main, pbf_s, pf_s = _proj_regroup(xs, gain_row, w_in_t, main_segments, w_small)
    pbf_p, pf_p = _proj(xp, gain_row, w_main, w_small, proj_sub * tm_p, proj_sub)

    hps = max(1, mla_heads // 2)
    q_p, ckv_p, kpe_p, kpep_p = _latent_q(pbf_p, pf_p, offs["cq"], q_lora, kv_lora, cos_p, sin_p, qan, wqn, wqp,
                                          qng, qrg, kvg, krg, tm_p, hps)
    q_s, ckv_s, kpe_s, kpep_s = _latent_q(pbf_s, pf_s, offs["cq"], q_lora, kv_lora, cos_s, sin_s, qan, wqn, wqp,
                                          qng, qrg, kvg, krg, rows_s, hps)

    small_rows = n_s + LANE
    meta_blk = n_s // LANE
    tail = ((0, small_rows - rows_s), (0, 0))
    k_p, v_p = _kv_expand(ckv_p, kpep_p, wk, wv, kng, tm_p)
    k_s, v_s = _kv_expand(jnp.pad(ckv_s, tail), jnp.pad(kpep_s, tail), wk, wv, kng, small_rows)
    cache_rows = dbatch * past
    cache_kpep = jnp.pad(cache_kpe.reshape(cache_rows, QK_ROPE), ((0, 0), (0, LANE - QK_ROPE))).astype(BF16)
    k_c, v_c = _kv_expand(cache_ckv.reshape(cache_rows, kv_lora).astype(F32), cache_kpep, wk, wv, kng,
                          _row_tile(cache_rows, 512))

    tq = _row_tile(seq, ATTN_TQ)
    attn_p = _attn_prompt(q_p, k_p, v_p, k_s, v_s, meta_blk, pbf_p, offs["za"], batch, seq, tq,
                          min(ATTN_HEADS, mla_heads), n_meta)
    attn_s = _attn_sample(q_s, k_c, v_c, k_s, v_s, pbf_s, offs["za"], dbatch, dseq, past, n_meta,
                          min(4, mla_heads))

    zc = jnp.zeros((1, ml_heads, dk, dv), F32)
    zn = jnp.zeros((1, ml_heads, dk), F32)
    zm = jnp.zeros((1, 1, ml_heads), F32)
    _, c_m, n_m, m_m = _mlstm(pbf_s, pf_s, offs, gate_bias, ml_gain, zc, zn, zm, 1, n_meta, n_meta, n_s, True,
                              ml_heads, dk, dv)
    lc_p = _row_tile(seq, 256)
    ml_p, c_p, n_pr, m_p = _mlstm(pbf_p, pf_p, offs, gate_bias, ml_gain, c_m, n_m, m_m, batch, seq, lc_p, 0,
                                  True, ml_heads, dk, dv)
    ml_s, c_s, n_sm, m_s = _mlstm(pbf_s, pf_s, offs, gate_bias, ml_gain, state_C.astype(F32),
                                  state_n.astype(F32), state_m.astype(F32).reshape(dbatch, 1, ml_heads),
                                  dbatch, dseq, dseq, 0, False, ml_heads, dk, dv)

    y_p = _out_proj(attn_p, ml_p, w_out, xp, _row_tile(n_p, 1024), _row_tile(d, 512))
    y_s = _out_proj(attn_s, ml_s, w_out, xs, n_s, _row_tile(d, 512))

    meta_ckv = ckv_s[n_s:n_s + n_meta]
    meta_kpe = kpe_s[n_s:n_s + n_meta]
    ckv_prompt = jnp.concatenate([jnp.broadcast_to(meta_ckv[None], (batch, n_meta, kv_lora)),
                                  ckv_p.reshape(batch, seq, kv_lora)], axis=1)
    kpe_prompt = jnp.concatenate([jnp.broadcast_to(meta_kpe[None], (batch, n_meta, QK_ROPE)),
                                  kpe_p.reshape(batch, seq, QK_ROPE)], axis=1)
    return (y_p.reshape(batch, seq, d), y_s.reshape(dbatch, dseq, d), ckv_prompt, kpe_prompt,
            c_p, n_pr, m_p.reshape(batch, ml_heads),
            ckv_s[:n_s].reshape(dbatch, dseq, kv_lora), kpe_s[:n_s].reshape(dbatch, dseq, QK_ROPE),
            c_s, n_sm, m_s.reshape(dbatch, ml_heads))
```

```python
import functools
import math

import jax
import jax.numpy as jnp
from jax import lax
from jax.experimental import pallas as pl
from jax.experimental.pallas import tpu as pltpu

CHUNK = 64
EPS = 1e-6
ROPE_THETA = 10000.0
V_HEAD = 128
QK_NOPE = 128
QK_ROPE = 64
HALF_ROPE = QK_ROPE // 2
LANE = 128
QK_PAD = QK_NOPE + LANE
COL_TILE = 512
NEG_BIG = -1e30
ATTN_TQ = 512
ATTN_HEADS = 4
VMEM_LIMIT = 56 * 1024 * 1024

F32 = jnp.float32
BF16 = jnp.bfloat16


def _cparams(sem):
    return pltpu.CompilerParams(dimension_semantics=sem, vmem_limit_bytes=VMEM_LIMIT)


def _dot(a, b):
    return jnp.dot(a, b, preferred_element_type=F32)


def _dot_nt(a, b):
    return lax.dot_general(a, b, (((1,), (1,)), ((), ())), preferred_element_type=F32)


def _rms(x, n):
    return x * lax.rsqrt(jnp.sum(x * x, axis=-1, keepdims=True) * (1.0 / n) + EPS)


def _rope_lanes(x, cos, sin_signed):
    lane = lax.broadcasted_iota(jnp.int32, x.shape, 1)
    first_half = (lane & (QK_ROPE - 1)) < HALF_ROPE
    rot = jnp.where(first_half, pltpu.roll(x, LANE - HALF_ROPE, 1), pltpu.roll(x, HALF_ROPE, 1))
    return x * cos + rot * sin_signed


def _regroup_kernel(w_ref, o_ref, *, segments, zero_ranges):
    for src, n, dst in segments:
        o_ref[:, dst:dst + n] = w_ref[:, src:src + n].astype(o_ref.dtype)
    for dst, n in zero_ranges:
        o_ref[:, dst:dst + n] = jnp.zeros((o_ref.shape[0], n), o_ref.dtype)


def _regroup(w, segments, out_cols, tr):
    rows, cols = w.shape
    covered = sorted((dst, n) for _, n, dst in segments)
    zero_ranges, pos = [], 0
    for dst, n in covered + [(out_cols, 0)]:
        if dst > pos:
            zero_ranges.append((pos, dst - pos))
        pos = dst + n
    return pl.pallas_call(
        functools.partial(_regroup_kernel, segments=tuple(segments), zero_ranges=tuple(zero_ranges)),
        grid=(rows // tr,),
        in_specs=[pl.BlockSpec((tr, cols), lambda i: (i, 0))],
        out_specs=pl.BlockSpec((tr, out_cols), lambda i: (i, 0)),
        out_shape=jax.ShapeDtypeStruct((rows, out_cols), BF16),
        compiler_params=_cparams(("parallel",)),
        name="regroup",
    )(w)


def _stack_rows_kernel(*refs):
    *w_refs, o_ref = refs
    parts = [r[...] for r in w_refs]
    used = sum(p.shape[0] for p in parts)
    parts.append(jnp.zeros((o_ref.shape[0] - used, o_ref.shape[1]), parts[0].dtype))
    o_ref[...] = jnp.concatenate(parts, axis=0).astype(o_ref.dtype)


def _stack_rows(wt, segments, out_rows, tc):
    cols = wt.shape[1]
    return pl.pallas_call(
        _stack_rows_kernel,
        grid=(cols // tc,),
        in_specs=[pl.BlockSpec((pl.Element(n), pl.Element(tc)), functools.partial(lambda i, s: (s, i * tc), s=src))
                  for src, n in segments],
        out_specs=pl.BlockSpec((out_rows, tc), lambda i: (0, i)),
        out_shape=jax.ShapeDtypeStruct((out_rows, cols), BF16),
        compiler_params=_cparams(("parallel",)),
        name="stack_rows",
    )(*([wt] * len(segments)))


def _proj_regroup_kernel(x_ref, g_ref, w_ref, ws_ref, om_ref, o_ref, os_ref, hn_ref):
    @pl.when(pl.program_id(0) == 0)
    def _():
        x = x_ref[...]
        hn = (_rms(x, x.shape[-1]) * g_ref[...]).astype(BF16)
        hn_ref[...] = hn
        os_ref[...] = _dot_nt(hn, ws_ref[...])

    wb = w_ref[...].astype(BF16)
    om_ref[...] = wb
    o_ref[...] = _dot_nt(hn_ref[...], wb).astype(o_ref.dtype)


def _proj_regroup(x, gain, wt, segments, w_small):
    m, d = x.shape
    ns = w_small.shape[0]
    n = sum(sz for _, sz, _ in segments)
    merged = []
    for src, sz, dst in sorted(segments, key=lambda s: s[2]):
        if merged and merged[-1][0] + merged[-1][1] == src and merged[-1][2] + merged[-1][1] == dst:
            merged[-1] = (merged[-1][0], merged[-1][1] + sz, merged[-1][2])
        else:
            merged.append((src, sz, dst))
    segments = merged
    assert all(sz % COL_TILE == 0 and dst % COL_TILE == 0 for _, sz, dst in segments)
    align = functools.reduce(math.gcd, [src for src, _, _ in segments], COL_TILE)

    def src_row(j):
        r = j * COL_TILE
        out = 0
        for src, sz, dst in segments:
            out = jnp.where((r >= dst) & (r < dst + sz), src + r - dst, out)
        return out

    const = lambda j: (0, 0)
    return pl.pallas_call(
        _proj_regroup_kernel,
        grid=(n // COL_TILE,),
        in_specs=[
            pl.BlockSpec((m, d), const),
            pl.BlockSpec((1, d), const),
            pl.BlockSpec((pl.Element(COL_TILE), pl.Element(d)), lambda j: (pl.multiple_of(src_row(j), align), 0)),
            pl.BlockSpec((ns, d), const),
        ],
        out_specs=[
            pl.BlockSpec((COL_TILE, d), lambda j: (j, 0)),
            pl.BlockSpec((m, COL_TILE), lambda j: (0, j)),
            pl.BlockSpec((m, ns), const),
        ],
        out_shape=[jax.ShapeDtypeStruct((n, d), BF16), jax.ShapeDtypeStruct((m, n), BF16),
                   jax.ShapeDtypeStruct((m, ns), F32)],
        scratch_shapes=[pltpu.VMEM((m, d), BF16)],
        compiler_params=_cparams(("arbitrary",)),
        name="proj_regroup",
    )(x, gain, wt, w_small)


def _proj_kernel(x_ref, g_ref, w_ref, ws_ref, o_ref, os_ref, hn_ref, *, n_sub):
    j = pl.program_id(1)
    ts = x_ref.shape[0]

    @pl.when(j < n_sub)
    def _():
        x = x_ref[...]
        hn = (_rms(x, x.shape[-1]) * g_ref[...]).astype(BF16)
        hn_ref[pl.ds(pl.multiple_of(j * ts, ts), ts), :] = hn
        os_ref[...] = _dot_nt(hn, ws_ref[...])

    @pl.when(j >= n_sub)
    def _():
        o_ref[...] = _dot_nt(hn_ref[...], w_ref[...]).astype(o_ref.dtype)


def _proj(x, gain, w_main, w_small, tm, n_sub):
    m, d = x.shape
    n = w_main.shape[0]
    ns = w_small.shape[0]
    ts = tm // n_sub
    sub = lambda i, j: (i * n_sub + jnp.minimum(j, n_sub - 1), 0)
    feat = lambda j: jnp.maximum(j - n_sub, 0)
    return pl.pallas_call(
        functools.partial(_proj_kernel, n_sub=n_sub),
        grid=(m // tm, n_sub + n // COL_TILE),
        in_specs=[
            pl.BlockSpec((ts, d), sub),
            pl.BlockSpec((1, d), lambda i, j: (0, 0)),
            pl.BlockSpec((COL_TILE, d), lambda i, j: (feat(j), 0)),
            pl.BlockSpec((ns, d), lambda i, j: (0, 0)),
        ],
        out_specs=[
            pl.BlockSpec((tm, COL_TILE), lambda i, j: (i, feat(j))),
            pl.BlockSpec((ts, ns), sub),
        ],
        out_shape=[jax.ShapeDtypeStruct((m, n), BF16), jax.ShapeDtypeStruct((m, ns), F32)],
        scratch_shapes=[pltpu.VMEM((tm, d), BF16)],
        compiler_params=_cparams(("parallel", "arbitrary")),
        name="proj",
    )(x, gain, w_main, w_small)


def _latent_q_kernel(*refs, n_cq, kv_lora, heads_per_step):
    cq_refs = refs[:n_cq]
    (pf_ref, cos_ref, sin_ref, qan_ref, wqn_ref, wqp_ref, qng_ref, qrg_ref, kvg_ref, krg_ref,
     q_ref, ckv_ref, kpe_ref, kpep_ref, cqn_ref) = refs[n_cq:]
    cos = cos_ref[...]
    sin = sin_ref[...]
    lane = lax.broadcasted_iota(jnp.int32, cos.shape, 1)
    low = lane < QK_ROPE

    @pl.when(pl.program_id(1) == 0)
    def _():
        cq = jnp.concatenate([r[...].astype(F32) for r in cq_refs], axis=1)
        cqn_ref[...] = (_rms(cq, cq.shape[-1]) * qan_ref[...]).astype(BF16)
        pf = pf_ref[...]
        ckv_raw = pf[:, :kv_lora]
        ckv_ref[...] = _rms(ckv_raw, kv_lora) * kvg_ref[...]
        kraw = jnp.where(low, pf[:, kv_lora:kv_lora + LANE], 0.0)
        kpe = _rope_lanes(_rms(kraw, QK_ROPE) * krg_ref[...], cos, sin)
        kpe_ref[...] = kpe[:, :QK_ROPE]
        kpep_ref[...] = kpe.astype(BF16)

    cqn = cqn_ref[...]
    group = 4 if heads_per_step % 4 == 0 else 2
    for g0 in range(0, heads_per_step, group):
        qn = _dot(cqn, wqn_ref[:, g0 * QK_NOPE:(g0 + group) * QK_NOPE])
        qp = _dot(cqn, wqp_ref[:, g0 * QK_ROPE:(g0 + group) * QK_ROPE])
        for pair in range(group // 2):
            x = qp[:, pair * LANE:(pair + 1) * LANE]
            x2 = x * x
            ss_a = jnp.sum(jnp.where(low, x2, 0.0), axis=-1, keepdims=True)
            ss_b = jnp.sum(jnp.where(low, 0.0, x2), axis=-1, keepdims=True)
            scale = jnp.where(low, lax.rsqrt(ss_a * (1.0 / QK_ROPE) + EPS),
                              lax.rsqrt(ss_b * (1.0 / QK_ROPE) + EPS))
            r = _rope_lanes(x * scale * qrg_ref[...], cos, sin)
            halves = (jnp.where(low, r, 0.0), jnp.where(low, pltpu.roll(r, QK_ROPE, 1), 0.0))
            for h in (0, 1):
                base = (g0 + 2 * pair + h) * QK_PAD
                nope = qn[:, (2 * pair + h) * QK_NOPE:(2 * pair + h + 1) * QK_NOPE]
                q_ref[:, base:base + QK_NOPE] = (_rms(nope, QK_NOPE) * qng_ref[...]).astype(BF16)
                q_ref[:, base + QK_NOPE:base + QK_PAD] = halves[h].astype(BF16)


def _latent_q(p_bf, p_f32, cq_off, q_lora, kv_lora, cos, sin, qan, wqn, wqp, qng, qrg, kvg, krg, tm,
              heads_per_step):
    m = p_bf.shape[0]
    n_heads = wqn.shape[1] // QK_NOPE
    n_cq = q_lora // COL_TILE
    cq0 = cq_off // COL_TILE
    ns = p_f32.shape[1]
    wstep = heads_per_step * QK_PAD
    assert heads_per_step % 2 == 0
    row = lambda i, j: (i, 0)
    const = lambda i, j: (0, 0)
    pos_blocks = cos.shape[0] // tm
    pos = lambda i, j: (i % pos_blocks, 0)
    in_specs = [pl.BlockSpec((tm, COL_TILE), functools.partial(lambda i, j, c: (i, c), c=cq0 + c))
                for c in range(n_cq)]
    in_specs += [
        pl.BlockSpec((tm, ns), row),
        pl.BlockSpec((tm, LANE), pos),
        pl.BlockSpec((tm, LANE), pos),
        pl.BlockSpec((1, q_lora), const),
        pl.BlockSpec((q_lora, heads_per_step * QK_NOPE), lambda i, j: (0, j)),
        pl.BlockSpec((q_lora, heads_per_step * QK_ROPE), lambda i, j: (0, j)),
        pl.BlockSpec((1, LANE), const),
        pl.BlockSpec((1, LANE), const),
        pl.BlockSpec((1, kv_lora), const),
        pl.BlockSpec((1, LANE), const),
    ]
    return pl.pallas_call(
        functools.partial(_latent_q_kernel, n_cq=n_cq, kv_lora=kv_lora, heads_per_step=heads_per_step),
        grid=(m // tm, n_heads // heads_per_step),
        in_specs=in_specs,
        out_specs=[
            pl.BlockSpec((tm, wstep), lambda i, j: (i, j)),
            pl.BlockSpec((tm, kv_lora), row),
            pl.BlockSpec((tm, QK_ROPE), row),
            pl.BlockSpec((tm, LANE), row),
        ],
        out_shape=[
            jax.ShapeDtypeStruct((m, n_heads * QK_PAD), BF16),
            jax.ShapeDtypeStruct((m, kv_lora), F32),
            jax.ShapeDtypeStruct((m, QK_ROPE), F32),
            jax.ShapeDtypeStruct((m, LANE), BF16),
        ],
        scratch_shapes=[pltpu.VMEM((tm, q_lora), BF16)],
        compiler_params=_cparams(("parallel", "arbitrary")),
        name="latent_q",
    )(*([p_bf] * n_cq), p_f32, cos, sin, qan, wqn, wqp, qng, qrg, kvg, krg)


def _kv_expand_kernel(ckv_ref, kpep_ref, wk_ref, wv_ref, kng_ref, k_ref, v_ref, *, n_heads, kw):
    c = ckv_ref[...].astype(BF16)
    kf = _dot(c, wk_ref[...])
    v_ref[...] = _dot(c, wv_ref[...]).astype(BF16)
    kpep = kpep_ref[...]
    for h in range(n_heads):
        kn = _rms(kf[:, h * QK_NOPE:(h + 1) * QK_NOPE], QK_NOPE) * kng_ref[...]
        k_ref[:, h * kw:h * kw + QK_NOPE] = kn.astype(BF16)
        if kw == QK_PAD:
            k_ref[:, h * kw + QK_NOPE:(h + 1) * kw] = kpep


def _kv_expand(ckv, kpep, wk, wv, kng, tm, append_rope=True):
    r, kv_lora = ckv.shape
    n_heads = wk.shape[1] // QK_NOPE
    kw = QK_PAD if append_rope else QK_NOPE
    row = lambda i: (i, 0)
    const = lambda i: (0, 0)
    return pl.pallas_call(
        functools.partial(_kv_expand_kernel, n_heads=n_heads, kw=kw),
        grid=(r // tm,),
        in_specs=[
            pl.BlockSpec((tm, kv_lora), row),
            pl.BlockSpec((tm, LANE), row),
            pl.BlockSpec((kv_lora, n_heads * QK_NOPE), const),
            pl.BlockSpec((kv_lora, n_heads * V_HEAD), const),
            pl.BlockSpec((1, LANE), const),
        ],
        out_specs=[
            pl.BlockSpec((tm, n_heads * kw), row),
            pl.BlockSpec((tm, n_heads * V_HEAD), row),
        ],
        out_shape=[
            jax.ShapeDtypeStruct((r, n_heads * kw), BF16),
            jax.ShapeDtypeStruct((r, n_heads * V_HEAD), BF16),
        ],
        compiler_params=_cparams(("parallel",)),
        name="kv_expand",
    )(ckv, kpep, wk, wv, kng)


def _lane_chunk_sum(p):
    out = p[:, :LANE]
    for c in range(1, p.shape[1] // LANE):
        out = out + p[:, c * LANE:(c + 1) * LANE]
    return out


def _gated(lvec, acc, za):
    za = za.astype(F32)
    return acc / jnp.sum(lvec, axis=-1, keepdims=True) * (za * jax.nn.sigmoid(za))


def _softmax_scale():
    return float((QK_NOPE + QK_ROPE) ** -0.5 * 1.4426950408889634)


def _attn_prompt_block(n_full, q_ref, k_ref, km_ref, za_ref, o_ref, s_ref, p_ref, va_ref, vma_ref, *,
                       tq, heads, c, n_meta):
    td = min(tq, 256)
    pieces = [(i * tq, tq, 0, False) for i in range(n_full)]
    pieces += [(n_full * tq + j * td, td, j * td, True) for j in range(tq // td)]

    def upd(full, r0, fn):
        return fn(full) if r0 == 0 else jnp.concatenate([full[:r0], fn(full[r0:])], axis=0)

    def scores(h):
        qs = slice(h * QK_PAD, (h + 1) * QK_PAD)
        q = q_ref[:, qs]
        col = lax.broadcasted_iota(jnp.int32, (tq, LANE), 1)
        t = jnp.where(col < n_meta, _dot_nt(q, km_ref[:, qs]) * c, -jnp.inf)
        s_ref[h, :, :LANE] = t
        mx = t
        off = LANE
        for start, rows, r0, masked in pieces:
            t = _dot_nt(q[r0:], k_ref[start:start + rows, qs]) * c
            if masked:
                rq = lax.broadcasted_iota(jnp.int32, t.shape, 0) // CHUNK
                ck = lax.broadcasted_iota(jnp.int32, t.shape, 1) // CHUNK
                t = jnp.where(rq >= ck, t, -jnp.inf)
            s_ref[h, r0:, off:off + rows] = t
            tmax = t[:, :LANE]
            for ch in range(1, rows // LANE):
                tmax = jnp.maximum(tmax, t[:, ch * LANE:(ch + 1) * LANE])
            mx = upd(mx, r0, lambda a: jnp.maximum(a, tmax))
            off += rows
        return jnp.max(mx, axis=-1, keepdims=True)

    def values(h, m):
        vs = slice(h * V_HEAD, (h + 1) * V_HEAD)
        va = slice(2 * h * V_HEAD, 2 * (h + 1) * V_HEAD)
        p_ref[h, :, :LANE] = jnp.exp2(s_ref[h, :, :LANE] - m).astype(BF16)
        off = LANE
        for start, rows, r0, _ in pieces:
            p_ref[h, r0:, off:off + rows] = jnp.exp2(s_ref[h, r0:, off:off + rows] - m[r0:]).astype(BF16)
            off += rows
        acc = _dot(p_ref[h, :, :LANE], vma_ref[:, va])
        n_main = n_full * tq
        if n_main:
            acc = acc + _dot(p_ref[h, :, LANE:LANE + n_main], va_ref[:n_main, va])
        off = LANE + n_main
        for start, rows, r0, _ in pieces[n_full:]:
            pv = _dot(p_ref[h, r0:, off:off + rows], va_ref[start:start + rows, va])
            acc = upd(acc, r0, lambda a: a + pv)
            off += rows
        za = za_ref[:, vs].astype(F32)
        out = acc[:, :V_HEAD] / acc[:, V_HEAD:V_HEAD + 1] * (za * jax.nn.sigmoid(za))
        o_ref[:, vs] = out.astype(o_ref.dtype)

    maxes = [scores(h) for h in range(heads)]
    for h in range(heads):
        values(h, maxes[h])


def _attn_prompt_kernel(q_ref, k_ref, kpe_ref, v_ref, km_ref, vm_ref, za_ref, o_ref, s_ref, p_ref, kc_ref, va_ref,
                        vma_ref, *, nqb, heads, **kw):
    qi = pl.program_id(2)

    @pl.when(qi == 0)
    def _():
        kpe = kpe_ref[...]
        for h in range(heads):
            kc_ref[:, h * QK_PAD:h * QK_PAD + QK_NOPE] = k_ref[:, h * QK_NOPE:(h + 1) * QK_NOPE]
            kc_ref[:, h * QK_PAD + QK_NOPE:(h + 1) * QK_PAD] = kpe
            vs = slice(h * V_HEAD, (h + 1) * V_HEAD)
            va_ref[:, 2 * h * V_HEAD:(2 * h + 1) * V_HEAD] = v_ref[:, vs]
            va_ref[:, (2 * h + 1) * V_HEAD:(2 * h + 2) * V_HEAD] = jnp.ones((va_ref.shape[0], V_HEAD), BF16)
            vma_ref[:, 2 * h * V_HEAD:(2 * h + 1) * V_HEAD] = vm_ref[:, vs]
            vma_ref[:, (2 * h + 1) * V_HEAD:(2 * h + 2) * V_HEAD] = jnp.ones((vma_ref.shape[0], V_HEAD), BF16)

    for n_full in range(nqb):
        pl.when(qi == n_full)(
            functools.partial(_attn_prompt_block, n_full, q_ref, kc_ref, km_ref, za_ref, o_ref, s_ref, p_ref, va_ref,
                              vma_ref, heads=heads, **kw))


def _attn_prompt(q, k, kpep, v, k_small, v_small, meta_blk, p_bf, za_off, batch, seq, tq, heads, n_meta):
    n_heads = q.shape[1] // QK_PAD
    nqb = seq // tq
    za0 = za_off // (V_HEAD * heads)
    assert za_off % (V_HEAD * heads) == 0 and n_heads % heads == 0 and tq % CHUNK == 0
    return pl.pallas_call(
        functools.partial(_attn_prompt_kernel, nqb=nqb, tq=tq, heads=heads, c=_softmax_scale(), n_meta=n_meta),
        grid=(batch, n_heads // heads, nqb),
        in_specs=[
            pl.BlockSpec((tq, heads * QK_PAD), lambda b, h, i: (b * nqb + i, h)),
            pl.BlockSpec((seq, heads * QK_NOPE), lambda b, h, i: (b, h)),
            pl.BlockSpec((seq, LANE), lambda b, h, i: (b, 0)),
            pl.BlockSpec((seq, heads * V_HEAD), lambda b, h, i: (b, h)),
            pl.BlockSpec((LANE, heads * QK_PAD), lambda b, h, i: (meta_blk, h)),
            pl.BlockSpec((LANE, heads * V_HEAD), lambda b, h, i: (meta_blk, h)),
            pl.BlockSpec((tq, heads * V_HEAD), lambda b, h, i: (b * nqb + i, za0 + h)),
        ],
        out_specs=pl.BlockSpec((tq, heads * V_HEAD), lambda b, h, i: (b * nqb + i, h)),
        out_shape=jax.ShapeDtypeStruct((batch * seq, n_heads * V_HEAD), BF16),
        scratch_shapes=[pltpu.VMEM((heads, tq, LANE + seq), F32), pltpu.VMEM((heads, tq, LANE + seq), BF16),
                        pltpu.VMEM((seq, heads * QK_PAD), BF16),
                        pltpu.VMEM((seq, 2 * heads * V_HEAD), BF16), pltpu.VMEM((LANE, 2 * heads * V_HEAD), BF16)],
        compiler_params=_cparams(("parallel", "parallel", "arbitrary")),
        name="attn_prompt",
    )(q, k, kpep, v, k_small, v_small, p_bf)


def _attn_sample_kernel(q_ref, kc_ref, vc_ref, ks_ref, vs_ref, za_ref, o_ref, *, heads, lq, c, n_meta, meta_row):
    b = pl.program_id(0)
    col = lax.broadcasted_iota(jnp.int32, (lq, ks_ref.shape[0]), 1)
    own = (col >= b * lq) & (col < (b + 1) * lq)
    meta = (col >= meta_row) & (col < meta_row + n_meta)
    visible = own | meta
    for h in range(heads):
        qs = slice(h * QK_PAD, (h + 1) * QK_PAD)
        vs = slice(h * V_HEAD, (h + 1) * V_HEAD)
        q = q_ref[:, qs]
        t_cache = _dot_nt(q, kc_ref[:, qs]) * c
        t_new = jnp.where(visible, _dot_nt(q, ks_ref[:, qs]) * c, -jnp.inf)
        m = jnp.maximum(jnp.max(t_cache, axis=-1, keepdims=True), jnp.max(t_new, axis=-1, keepdims=True))
        p_cache = jnp.exp2(t_cache - m)
        p_new = jnp.exp2(t_new - m)
        lvec = _lane_chunk_sum(p_cache) + _lane_chunk_sum(p_new)
        acc = _dot(p_cache.astype(BF16), vc_ref[:, vs]) + _dot(p_new.astype(BF16), vs_ref[:, vs])
        o_ref[:, vs] = _gated(lvec, acc, za_ref[:, vs]).astype(o_ref.dtype)


def _attn_sample(q, kc, vc, ks, vs, p_bf, za_off, batch, lq, past, n_meta, heads):
    n_heads = q.shape[1] // QK_PAD
    rows = ks.shape[0]
    za0 = za_off // (V_HEAD * heads)
    assert za_off % (V_HEAD * heads) == 0 and n_heads % heads == 0
    return pl.pallas_call(
        functools.partial(_attn_sample_kernel, heads=heads, lq=lq, c=_softmax_scale(), n_meta=n_meta,
                          meta_row=batch * lq),
        grid=(batch, n_heads // heads),
        in_specs=[
            pl.BlockSpec((lq, heads * QK_PAD), lambda b, h: (b, h)),
            pl.BlockSpec((past, heads * QK_PAD), lambda b, h: (b, h)),
            pl.BlockSpec((past, heads * V_HEAD), lambda b, h: (b, h)),
            pl.BlockSpec((rows, heads * QK_PAD), lambda b, h: (0, h)),
            pl.BlockSpec((rows, heads * V_HEAD), lambda b, h: (0, h)),
            pl.BlockSpec((lq, heads * V_HEAD), lambda b, h: (b, za0 + h)),
        ],
        out_specs=pl.BlockSpec((lq, heads * V_HEAD), lambda b, h: (b, h)),
        out_shape=jax.ShapeDtypeStruct((batch * lq, n_heads * V_HEAD), BF16),
        compiler_params=_cparams(("parallel", "arbitrary")),
        name="attn_sample",
    )(q, kc, vc, ks, vs, p_bf)


def _log_sigmoid(x):
    return jnp.minimum(x, 0.0) - jnp.log1p(jnp.exp(-jnp.abs(x)))


def _pad_rows(a, rows):
    if a.shape[0] == rows:
        return a
    return jnp.concatenate([a, jnp.zeros((rows - a.shape[0], a.shape[1]), a.dtype)], axis=0)


def _mlstm_kernel(q_ref, k_ref, v_ref, mo_ref, zb_ref, g_ref, gb_ref, gain_ref, c0_ref, n0_ref, m0_ref,
                  h_ref, c_ref, n_ref, m_ref, *, n_heads, lc, lp, dk, dv):
    @pl.when(pl.program_id(1) == 0)
    def _():
        c_ref[...] = c0_ref[...]
        n_ref[...] = n0_ref[...]
        m_ref[...] = m0_ref[...]

    hi = lax.Precision.HIGHEST
    gates = _pad_rows(g_ref[...] + gb_ref[...], lp)
    gates_t = gates.T
    t_col = lax.broadcasted_iota(jnp.int32, (lp, 1), 0)
    t_row = lax.broadcasted_iota(jnp.int32, (1, lp), 1)
    r_idx = lax.broadcasted_iota(jnp.int32, (lp, lp), 0)
    c_idx = lax.broadcasted_iota(jnp.int32, (lp, lp), 1)
    causal = r_idx >= c_idx
    lower = causal.astype(F32)
    upper = (c_idx >= r_idx).astype(F32)
    b_cols = jnp.dot(lower, jnp.where(t_col < lc, _log_sigmoid(gates), 0.0), precision=hi,
                     preferred_element_type=F32)
    b_rows = jnp.dot(jnp.where(t_row < lc, _log_sigmoid(gates_t), 0.0), upper, precision=hi,
                     preferred_element_type=F32)

    for h in range(n_heads):
        li = QK_ROPE + h
        lf_lane = QK_ROPE + n_heads + h
        ig_col = jnp.where(t_col < lc, gates[:, li:li + 1], NEG_BIG)
        ig_row = jnp.where(t_row < lc, gates_t[li:li + 1, :], NEG_BIG)
        b_col = b_cols[:, lf_lane:lf_lane + 1]
        b_row = b_rows[lf_lane:lf_lane + 1, :]
        m0 = m_ref[0, 0:1, h:h + 1]
        d = jnp.where(causal, b_col - b_row + ig_row, -jnp.inf)
        a_col = b_col + m0
        m = jnp.maximum(a_col, jnp.max(d, axis=-1, keepdims=True))
        w_inter = jnp.exp(a_col - m)

        q = _pad_rows(q_ref[:, h * dk:(h + 1) * dk], lp)
        k = _pad_rows(k_ref[:, h * dk:(h + 1) * dk], lp) * (dk ** -0.5)
        v = _pad_rows(v_ref[:, h * dv:(h + 1) * dv], lp)
        qk = _dot_nt(q, k) * jnp.exp(d - m)
        c_old = c_ref[0, h]
        n_old = n_ref[0, h:h + 1, :]
        num = w_inter * _dot(q, c_old.astype(BF16)) + _dot(qk.astype(BF16), v)
        den = (w_inter * jnp.sum(q.astype(F32) * n_old, axis=-1, keepdims=True)
               + jnp.sum(qk, axis=-1, keepdims=True))
        denc = jnp.maximum(jnp.abs(den), jnp.exp(-m))

        b_last = b_col[lp - 1:lp, :]
        g_row = b_last - b_row + ig_row
        g_col = b_last - b_col + ig_col
        m_new = jnp.maximum(b_last + m0, jnp.max(g_row, axis=-1, keepdims=True))
        decay = jnp.exp(b_last + m0 - m_new)
        kw = k.astype(F32) * jnp.exp(g_col - m_new)
        c_ref[0, h] = decay * c_old + _dot(kw.T.astype(BF16), v)
        n_ref[0, h:h + 1, :] = decay * n_old + jnp.sum(kw, axis=0, keepdims=True)
        m_ref[0, 0:1, h:h + 1] = m_new

        num = num[:lc]
        denc = denc[:lc]
        normed = num * lax.rsqrt(jnp.sum(num * num, axis=-1, keepdims=True) * (1.0 / dv) + EPS * denc * denc)
        mo = mo_ref[:, h * dv:(h + 1) * dv].astype(F32)
        zb = zb_ref[:, h * dv:(h + 1) * dv].astype(F32)
        out = jax.nn.sigmoid(mo) * (normed * gain_ref[:, h * dv:(h + 1) * dv])
        h_ref[:, h * dv:(h + 1) * dv] = (out * (zb * jax.nn.sigmoid(zb))).astype(h_ref.dtype)


def _mlstm(p_bf, p_f32, offs, gate_bias, gain, c0, n0, m0, batch, seq, lc, row0, share_state, n_heads, dk, dv):
    lp = -(-lc // LANE) * LANE
    nch = seq // lc
    blk0 = row0 // lc
    wqk = n_heads * dk
    wv = n_heads * dv
    gate_blk = p_f32.shape[1] // LANE - 1
    rows = lambda col: (lambda b, c: (blk0 + b * nch + c, col))
    state = (lambda b, c: (0, 0, 0, 0)) if share_state else (lambda b, c: (b, 0, 0, 0))
    state3 = (lambda b, c: (0, 0, 0)) if share_state else (lambda b, c: (b, 0, 0))
    out_rows = batch * seq
    return pl.pallas_call(
        functools.partial(_mlstm_kernel, n_heads=n_heads, lc=lc, lp=lp, dk=dk, dv=dv),
        grid=(batch, nch),
        in_specs=[
            pl.BlockSpec((lc, wqk), rows(offs["mq"] // wqk)),
            pl.BlockSpec((lc, wqk), rows(offs["mk"] // wqk)),
            pl.BlockSpec((lc, wv), rows(offs["mv"] // wv)),
            pl.BlockSpec((lc, wv), rows(offs["mo"] // wv)),
            pl.BlockSpec((lc, wv), rows(offs["zb"] // wv)),
            pl.BlockSpec((lc, LANE), rows(gate_blk)),
            pl.BlockSpec((1, LANE), lambda b, c: (0, 0)),
            pl.BlockSpec((1, wv), lambda b, c: (0, 0)),
            pl.BlockSpec((1, n_heads, dk, dv), state),
            pl.BlockSpec((1, n_heads, dk), state3),
            pl.BlockSpec((1, 1, n_heads), state3),
        ],
        out_specs=[
            pl.BlockSpec((lc, wv), lambda b, c: (b * nch + c, 0)),
            pl.BlockSpec((1, n_heads, dk, dv), lambda b, c: (b, 0, 0, 0)),
            pl.BlockSpec((1, n_heads, dk), lambda b, c: (b, 0, 0)),
            pl.BlockSpec((1, 1, n_heads), lambda b, c: (b, 0, 0)),
        ],
        out_shape=[
            jax.ShapeDtypeStruct((out_rows, wv), BF16),
            jax.ShapeDtypeStruct((batch, n_heads, dk, dv), F32),
            jax.ShapeDtypeStruct((batch, n_heads, dk), F32),
            jax.ShapeDtypeStruct((batch, 1, n_heads), F32),
        ],
        compiler_params=_cparams(("parallel", "arbitrary")),
        name="mlstm",
    )(p_bf, p_bf, p_bf, p_bf, p_bf, p_f32, gate_bias, gain, c0, n0, m0)


def _out_proj_kernel(a_ref, m_ref, wa_ref, wm_ref, x_ref, o_ref, *wb_refs):
    wa = wa_ref[...].astype(BF16)
    wm = wm_ref[...].astype(BF16)
    o_ref[...] = x_ref[...] + _dot(a_ref[...], wa) + _dot(m_ref[...], wm)
    if wb_refs:
        wb_refs[0][...] = wa
        wb_refs[1][...] = wm


def _out_proj(a, ml, w_a, w_m, wm_blk, x, tm, tn, emit_bf16=False):
    rows, wa = a.shape
    wm = ml.shape[1]
    d = w_a.shape[1]
    assert wa == wm and (not emit_bf16 or rows == tm)
    out_specs = [pl.BlockSpec((tm, tn), lambda i, j: (i, j))]
    out_shape = [jax.ShapeDtypeStruct((rows, d), F32)]
    if emit_bf16:
        out_specs += [pl.BlockSpec((wa, tn), lambda i, j: (0, j)), pl.BlockSpec((wm, tn), lambda i, j: (0, j))]
        out_shape += [jax.ShapeDtypeStruct((wa, d), BF16), jax.ShapeDtypeStruct((wm, d), BF16)]
    return pl.pallas_call(
        _out_proj_kernel,
        grid=(rows // tm, d // tn),
        in_specs=[
            pl.BlockSpec((tm, wa), lambda i, j: (i, 0)),
            pl.BlockSpec((tm, wm), lambda i, j: (i, 0)),
            pl.BlockSpec((wa, tn), lambda i, j: (0, j)),
            pl.BlockSpec((wm, tn), lambda i, j: (wm_blk, j)),
            pl.BlockSpec((tm, tn), lambda i, j: (i, j)),
        ],
        out_specs=out_specs,
        out_shape=out_shape,
        compiler_params=_cparams(("parallel", "arbitrary")),
        name="out_proj",
    )(a, ml, w_a, w_m, x)


def _rope_tables(pos):
    inv_freq = ROPE_THETA ** (-jnp.arange(HALF_ROPE, dtype=F32) / HALF_ROPE)
    ang = pos.astype(F32)[:, None] * inv_freq[None, :]
    cos, sin = jnp.cos(ang), jnp.sin(ang)
    return jnp.concatenate([cos, cos, cos, cos], axis=1), jnp.concatenate([-sin, sin, -sin, sin], axis=1)


def _pad_lanes(vec):
    return jnp.pad(vec.astype(F32), (0, LANE - vec.shape[0]))[None, :]


def _row_tile(rows, target):
    t = min(rows, target)
    while rows % t:
        t //= 2
    return t


def kernel(x_prompt, x_sample, cache_ckv, cache_kpe, state_C, state_n, state_m, meta_tokens, norm_gain, w_in,
           b_igate, b_fgate, q_a_norm, w_q_up, q_nope_norm, q_rope_norm, kv_a_norm, k_rope_norm, w_kv_up,
           k_nope_norm, ml_out_norm, w_out):
    batch, seq, d = x_prompt.shape
    dbatch, dseq, _ = x_sample.shape
    past = cache_ckv.shape[1]
    n_meta = meta_tokens.shape[0]
    q_lora = q_a_norm.shape[0]
    kv_lora = kv_a_norm.shape[0]
    ml_heads, dv = ml_out_norm.shape
    dk = state_n.shape[-1]
    mla_heads = w_kv_up.shape[1] // (QK_NOPE + V_HEAD)
    mla_w = mla_heads * V_HEAD
    ml_w = ml_heads * dv
    mqk_w = ml_heads * dk
    assert q_lora % COL_TILE == 0 and kv_lora % LANE == 0 and 2 * ml_heads <= LANE - QK_ROPE
    assert mla_w == ml_w and w_out.shape[0] == mla_w + ml_w

    o_cq = 0
    o_ckv = o_cq + q_lora
    o_kpe = o_ckv + kv_lora
    o_mq = o_kpe + QK_ROPE
    o_mk = o_mq + mqk_w
    o_mv = o_mk + mqk_w
    o_mo = o_mv + ml_w
    o_mi = o_mo + ml_w
    o_mf = o_mi + ml_heads
    o_za = o_mf + ml_heads
    o_zb = o_za + mla_w

    offs = {"mq": 0, "mk": mqk_w, "mv": 2 * mqk_w, "mo": 2 * mqk_w + ml_w, "za": 2 * mqk_w + 2 * ml_w,
            "zb": 2 * mqk_w + 2 * ml_w + mla_w, "cq": 2 * mqk_w + 3 * ml_w + mla_w}
    gate_pad = LANE - QK_ROPE - 2 * ml_heads
    w_in_t = w_in.T
    main_segments = [(o_mq, mqk_w, offs["mq"]), (o_mk, mqk_w, offs["mk"]), (o_mv, ml_w, offs["mv"]),
                     (o_mo, ml_w, offs["mo"]), (o_za, mla_w, offs["za"]), (o_zb, ml_w, offs["zb"]),
                     (o_cq, q_lora, offs["cq"])]
    w_small = _stack_rows(w_in_t, [(o_ckv, kv_lora), (o_kpe, QK_ROPE), (o_mi, 2 * ml_heads)], kv_lora + LANE,
                          _row_tile(d, 512))
    gate_bias = jnp.concatenate([jnp.zeros((QK_ROPE,), F32), b_igate.astype(F32), b_fgate.astype(F32),
                                 jnp.zeros((gate_pad,), F32)])[None, :]
    qk_w = QK_NOPE + QK_ROPE
    kv_w = QK_NOPE + V_HEAD
    wqn = _regroup(w_q_up, [(h * qk_w, QK_NOPE, h * QK_NOPE) for h in range(mla_heads)], mla_heads * QK_NOPE,
                   _row_tile(q_lora, 512))
    wqp = _regroup(w_q_up, [(h * qk_w + QK_NOPE, QK_ROPE, h * QK_ROPE) for h in range(mla_heads)],
                   mla_heads * QK_ROPE, _row_tile(q_lora, 512))
    wk = _regroup(w_kv_up, [(h * kv_w, QK_NOPE, h * QK_NOPE) for h in range(mla_heads)], mla_heads * QK_NOPE,
                  kv_lora)
    wv = _regroup(w_kv_up, [(h * kv_w + QK_NOPE, V_HEAD, h * V_HEAD) for h in range(mla_heads)],
                  mla_heads * V_HEAD, kv_lora)
    gain_row = norm_gain.astype(F32)[None, :]
    qan = q_a_norm.astype(F32)[None, :]
    kvg = kv_a_norm.astype(F32)[None, :]
    qng = q_nope_norm.astype(F32)[None, :]
    kng = k_nope_norm.astype(F32)[None, :]
    qrg = jnp.tile(q_rope_norm.astype(F32), LANE // QK_ROPE)[None, :]
    krg = _pad_lanes(k_rope_norm)
    ml_gain = ml_out_norm.astype(F32).reshape(1, ml_w)

    n_p = batch * seq
    n_s = dbatch * dseq
    assert n_s % LANE == 0 and n_meta <= LANE and n_meta % 16 == 0
    rows_s = n_s + n_meta
    xp = x_prompt.reshape(n_p, d)
    xs = jnp.concatenate([x_sample.reshape(n_s, d), meta_tokens.astype(x_sample.dtype)], axis=0)

    pos_s = jnp.concatenate([jnp.tile(n_meta + past + jnp.arange(dseq), dbatch), jnp.arange(n_meta)])
    cos_p, sin_p = _rope_tables(n_meta + jnp.arange(seq))
    cos_s, sin_s = _rope_tables(pos_s)

    tm_p = _row_tile(seq, 512)
    proj_sub = 2 if n_p % (2 * tm_p) == 0 else 1
    w_main, pbf_s, pf_s = _proj_regroup(xs, gain_row, w_in_t, main_segments, w_small)
    pbf_p, pf_p = _proj(xp, gain_row, w_main, w_small, proj_sub * tm_p, proj_sub)

    hps = max(1, mla_heads // 2)
    q_p, ckv_p, kpe_p, kpep_p = _latent_q(pbf_p, pf_p, offs["cq"], q_lora, kv_lora, cos_p, sin_p, qan, wqn, wqp,
                                          qng, qrg, kvg, krg, tm_p, hps)
    q_s, ckv_s, kpe_s, kpep_s = _latent_q(pbf_s, pf_s, offs["cq"], q_lora, kv_lora, cos_s, sin_s, qan, wqn, wqp,
                                          qng, qrg, kvg, krg, rows_s, hps)

    small_rows = n_s + LANE
    meta_blk = n_s // LANE
    tail = ((0, small_rows - rows_s), (0, 0))
    k_p, v_p = _kv_expand(ckv_p, kpep_p, wk, wv, kng, tm_p, append_rope=False)
    k_s, v_s = _kv_expand(jnp.pad(ckv_s, tail), jnp.pad(kpep_s, tail), wk, wv, kng, small_rows)
    cache_rows = dbatch * past
    cache_kpep = jnp.pad(cache_kpe.reshape(cache_rows, QK_ROPE), ((0, 0), (0, LANE - QK_ROPE))).astype(BF16)
    k_c, v_c = _kv_expand(cache_ckv.reshape(cache_rows, kv_lora).astype(F32), cache_kpep, wk, wv, kng,
                          _row_tile(cache_rows, 512))

    tq = _row_tile(seq, ATTN_TQ)
    attn_p = _attn_prompt(q_p, k_p, kpep_p, v_p, k_s, v_s, meta_blk, pbf_p, offs["za"], batch, seq, tq,
                          min(ATTN_HEADS, mla_heads), n_meta)
    attn_s = _attn_sample(q_s, k_c, v_c, k_s, v_s, pbf_s, offs["za"], dbatch, dseq, past, n_meta,
                          min(4, mla_heads))

    zc = jnp.zeros((1, ml_heads, dk, dv), F32)
    zn = jnp.zeros((1, ml_heads, dk), F32)
    zm = jnp.zeros((1, 1, ml_heads), F32)
    _, c_m, n_m, m_m = _mlstm(pbf_s, pf_s, offs, gate_bias, ml_gain, zc, zn, zm, 1, n_meta, n_meta, n_s, True,
                              ml_heads, dk, dv)
    lc_p = _row_tile(seq, 256)
    ml_p, c_p, n_pr, m_p = _mlstm(pbf_p, pf_p, offs, gate_bias, ml_gain, c_m, n_m, m_m, batch, seq, lc_p, 0,
                                  True, ml_heads, dk, dv)
    ml_s, c_s, n_sm, m_s = _mlstm(pbf_s, pf_s, offs, gate_bias, ml_gain, state_C.astype(F32),
                                  state_n.astype(F32), state_m.astype(F32).reshape(dbatch, 1, ml_heads),
                                  dbatch, dseq, dseq, 0, False, ml_heads, dk, dv)

    y_s, w_oa, w_om = _out_proj(attn_s, ml_s, w_out, w_out, 1, xs, n_s, _row_tile(d, 512), emit_bf16=True)
    y_p, = _out_proj(attn_p, ml_p, w_oa, w_om, 0, xp, _row_tile(n_p, 1024), _row_tile(d, 512))

    meta_ckv = ckv_s[n_s:n_s + n_meta]
    meta_kpe = kpe_s[n_s:n_s + n_meta]
    ckv_prompt = jnp.concatenate([jnp.broadcast_to(meta_ckv[None], (batch, n_meta, kv_lora)),
                                  ckv_p.reshape(batch, seq, kv_lora)], axis=1)
    kpe_prompt = jnp.concatenate([jnp.broadcast_to(meta_kpe[None], (batch, n_meta, QK_ROPE)),
                                  kpe_p.reshape(batch, seq, QK_ROPE)], axis=1)
    return (y_p.reshape(batch, seq, d), y_s.reshape(dbatch, dseq, d), ckv_prompt, kpe_prompt,
            c_p, n_pr, m_p.reshape(batch, ml_heads),
            ckv_s[:n_s].reshape(dbatch, dseq, kv_lora), kpe_s[:n_s].reshape(dbatch, dseq, QK_ROPE),
            c_s, n_sm, m_s.reshape(dbatch, ml_heads))
```

```python
import functools
import math

import jax
import jax.numpy as jnp
from jax import lax
from jax.experimental import pallas as pl
from jax.experimental.pallas import tpu as pltpu

CHUNK = 64
EPS = 1e-6
ROPE_THETA = 10000.0
V_HEAD = 128
QK_NOPE = 128
QK_ROPE = 64
HALF_ROPE = QK_ROPE // 2
LANE = 128
QK_PAD = QK_NOPE + LANE
COL_TILE = 512
NEG_BIG = -1e30
ATTN_TQ = 512
ATTN_HEADS = 4
VMEM_LIMIT = 56 * 1024 * 1024

F32 = jnp.float32
BF16 = jnp.bfloat16


def _cparams(sem):
    return pltpu.CompilerParams(dimension_semantics=sem, vmem_limit_bytes=VMEM_LIMIT)


def _dot(a, b):
    return jnp.dot(a, b, preferred_element_type=F32)


def _dot_nt(a, b):
    return lax.dot_general(a, b, (((1,), (1,)), ((), ())), preferred_element_type=F32)


def _rms(x, n):
    return x * lax.rsqrt(jnp.sum(x * x, axis=-1, keepdims=True) * (1.0 / n) + EPS)


def _rope_lanes(x, cos, sin_signed):
    lane = lax.broadcasted_iota(jnp.int32, x.shape, 1)
    first_half = (lane & (QK_ROPE - 1)) < HALF_ROPE
    rot = jnp.where(first_half, pltpu.roll(x, LANE - HALF_ROPE, 1), pltpu.roll(x, HALF_ROPE, 1))
    return x * cos + rot * sin_signed


def _regroup_kernel(w_ref, *o_refs, plans):
    for o_ref, segments in zip(o_refs, plans):
        for src, n, dst in segments:
            o_ref[:, dst:dst + n] = w_ref[:, src:src + n].astype(o_ref.dtype)


def _regroup(w, plans, tr):
    rows, cols = w.shape
    widths = [sum(n for _, n, _ in segments) for segments in plans]
    for segments in plans:
        pos = 0
        for _, n, dst in sorted(segments, key=lambda s: s[2]):
            assert dst == pos
            pos += n
    return pl.pallas_call(
        functools.partial(_regroup_kernel, plans=tuple(tuple(s) for s in plans)),
        grid=(rows // tr,),
        in_specs=[pl.BlockSpec((tr, cols), lambda i: (i, 0))],
        out_specs=[pl.BlockSpec((tr, width), lambda i: (i, 0)) for width in widths],
        out_shape=[jax.ShapeDtypeStruct((rows, width), BF16) for width in widths],
        compiler_params=_cparams(("parallel",)),
        name="regroup",
    )(w)


def _stack_rows_kernel(*refs):
    *w_refs, o_ref = refs
    parts = [r[...] for r in w_refs]
    used = sum(p.shape[0] for p in parts)
    parts.append(jnp.zeros((o_ref.shape[0] - used, o_ref.shape[1]), parts[0].dtype))
    o_ref[...] = jnp.concatenate(parts, axis=0).astype(o_ref.dtype)


def _stack_rows(wt, segments, out_rows, tc):
    cols = wt.shape[1]
    return pl.pallas_call(
        _stack_rows_kernel,
        grid=(cols // tc,),
        in_specs=[pl.BlockSpec((pl.Element(n), pl.Element(tc)), functools.partial(lambda i, s: (s, i * tc), s=src))
                  for src, n in segments],
        out_specs=pl.BlockSpec((out_rows, tc), lambda i: (0, i)),
        out_shape=jax.ShapeDtypeStruct((out_rows, cols), BF16),
        compiler_params=_cparams(("parallel",)),
        name="stack_rows",
    )(*([wt] * len(segments)))


def _proj_regroup_kernel(x_ref, g_ref, w_ref, ws_ref, om_ref, o_ref, os_ref, hn_ref):
    @pl.when(pl.program_id(0) == 0)
    def _():
        x = x_ref[...]
        hn = (_rms(x, x.shape[-1]) * g_ref[...]).astype(BF16)
        hn_ref[...] = hn
        os_ref[...] = _dot_nt(hn, ws_ref[...])

    wb = w_ref[...].astype(BF16)
    om_ref[...] = wb
    o_ref[...] = _dot_nt(hn_ref[...], wb).astype(o_ref.dtype)


def _proj_regroup(x, gain, wt, segments, w_small):
    m, d = x.shape
    ns = w_small.shape[0]
    n = sum(sz for _, sz, _ in segments)
    merged = []
    for src, sz, dst in sorted(segments, key=lambda s: s[2]):
        if merged and merged[-1][0] + merged[-1][1] == src and merged[-1][2] + merged[-1][1] == dst:
            merged[-1] = (merged[-1][0], merged[-1][1] + sz, merged[-1][2])
        else:
            merged.append((src, sz, dst))
    segments = merged
    assert all(sz % COL_TILE == 0 and dst % COL_TILE == 0 for _, sz, dst in segments)
    align = functools.reduce(math.gcd, [src for src, _, _ in segments], COL_TILE)

    def src_row(j):
        r = j * COL_TILE
        out = 0
        for src, sz, dst in segments:
            out = jnp.where((r >= dst) & (r < dst + sz), src + r - dst, out)
        return out

    const = lambda j: (0, 0)
    return pl.pallas_call(
        _proj_regroup_kernel,
        grid=(n // COL_TILE,),
        in_specs=[
            pl.BlockSpec((m, d), const),
            pl.BlockSpec((1, d), const),
            pl.BlockSpec((pl.Element(COL_TILE), pl.Element(d)), lambda j: (pl.multiple_of(src_row(j), align), 0)),
            pl.BlockSpec((ns, d), const),
        ],
        out_specs=[
            pl.BlockSpec((COL_TILE, d), lambda j: (j, 0)),
            pl.BlockSpec((m, COL_TILE), lambda j: (0, j)),
            pl.BlockSpec((m, ns), const),
        ],
        out_shape=[jax.ShapeDtypeStruct((n, d), BF16), jax.ShapeDtypeStruct((m, n), BF16),
                   jax.ShapeDtypeStruct((m, ns), F32)],
        scratch_shapes=[pltpu.VMEM((m, d), BF16)],
        compiler_params=_cparams(("arbitrary",)),
        name="proj_regroup",
    )(x, gain, wt, w_small)


def _proj_kernel(x_ref, g_ref, w_ref, ws_ref, o_ref, os_ref, hn_ref, *, n_sub):
    j = pl.program_id(1)
    ts = x_ref.shape[0]

    @pl.when(j < n_sub)
    def _():
        x = x_ref[...]
        hn = (_rms(x, x.shape[-1]) * g_ref[...]).astype(BF16)
        hn_ref[pl.ds(pl.multiple_of(j * ts, ts), ts), :] = hn
        os_ref[...] = _dot_nt(hn, ws_ref[...])

    @pl.when(j >= n_sub)
    def _():
        o_ref[...] = _dot_nt(hn_ref[...], w_ref[...]).astype(o_ref.dtype)


def _proj(x, gain, w_main, w_small, tm, n_sub):
    m, d = x.shape
    n = w_main.shape[0]
    ns = w_small.shape[0]
    ts = tm // n_sub
    sub = lambda i, j: (i * n_sub + jnp.minimum(j, n_sub - 1), 0)
    feat = lambda j: jnp.maximum(j - n_sub, 0)
    return pl.pallas_call(
        functools.partial(_proj_kernel, n_sub=n_sub),
        grid=(m // tm, n_sub + n // COL_TILE),
        in_specs=[
            pl.BlockSpec((ts, d), sub),
            pl.BlockSpec((1, d), lambda i, j: (0, 0)),
            pl.BlockSpec((COL_TILE, d), lambda i, j: (feat(j), 0)),
            pl.BlockSpec((ns, d), lambda i, j: (0, 0)),
        ],
        out_specs=[
            pl.BlockSpec((tm, COL_TILE), lambda i, j: (i, feat(j))),
            pl.BlockSpec((ts, ns), sub),
        ],
        out_shape=[jax.ShapeDtypeStruct((m, n), BF16), jax.ShapeDtypeStruct((m, ns), F32)],
        scratch_shapes=[pltpu.VMEM((tm, d), BF16)],
        compiler_params=_cparams(("parallel", "arbitrary")),
        name="proj",
    )(x, gain, w_main, w_small)


def _latent_q_kernel(*refs, n_cq, kv_lora, heads_per_step):
    cq_refs = refs[:n_cq]
    (pf_ref, cos_ref, sin_ref, qan_ref, wqn_ref, wqp_ref, qng_ref, qrg_ref, kvg_ref, krg_ref,
     q_ref, ckv_ref, kpe_ref, kpep_ref, cqn_ref, rawn_ref, rawp_ref) = refs[n_cq:]
    j = pl.program_id(1)
    n_groups = pl.num_programs(1) - 1
    cos = cos_ref[...]
    sin = sin_ref[...]
    lane = lax.broadcasted_iota(jnp.int32, cos.shape, 1)
    low = lane < QK_ROPE

    @pl.when(j == 0)
    def _():
        cq = jnp.concatenate([r[...].astype(F32) for r in cq_refs], axis=1)
        cqn_ref[...] = (_rms(cq, cq.shape[-1]) * qan_ref[...]).astype(BF16)
        pf = pf_ref[...]
        ckv_raw = pf[:, :kv_lora]
        ckv_ref[...] = _rms(ckv_raw, kv_lora) * kvg_ref[...]
        kraw = jnp.where(low, pf[:, kv_lora:kv_lora + LANE], 0.0)
        kpe = _rope_lanes(_rms(kraw, QK_ROPE) * krg_ref[...], cos, sin)
        kpe_ref[...] = kpe[:, :QK_ROPE]
        kpep_ref[...] = kpe.astype(BF16)

    def multiply(slot):
        cqn = cqn_ref[...]
        rawn_ref[slot] = _dot(cqn, wqn_ref[...])
        rawp_ref[slot] = _dot(cqn, wqp_ref[...])

    def finish(slot):
        for pair in range(heads_per_step // 2):
            x = rawp_ref[slot, :, pair * LANE:(pair + 1) * LANE]
            x2 = x * x
            ss_a = jnp.sum(jnp.where(low, x2, 0.0), axis=-1, keepdims=True)
            ss_b = jnp.sum(jnp.where(low, 0.0, x2), axis=-1, keepdims=True)
            scale = jnp.where(low, lax.rsqrt(ss_a * (1.0 / QK_ROPE) + EPS),
                              lax.rsqrt(ss_b * (1.0 / QK_ROPE) + EPS))
            r = _rope_lanes(x * scale * qrg_ref[...], cos, sin)
            halves = (jnp.where(low, r, 0.0), jnp.where(low, pltpu.roll(r, QK_ROPE, 1), 0.0))
            for h in (0, 1):
                head = 2 * pair + h
                nope = rawn_ref[slot, :, head * QK_NOPE:(head + 1) * QK_NOPE]
                q_ref[:, head * QK_PAD:head * QK_PAD + QK_NOPE] = (_rms(nope, QK_NOPE) * qng_ref[...]).astype(BF16)
                q_ref[:, head * QK_PAD + QK_NOPE:(head + 1) * QK_PAD] = halves[h].astype(BF16)

    for parity in (0, 1):
        mine = (j % 2) == parity

        @pl.when(mine & (j == 0))
        def _():
            multiply(parity)

        @pl.when(mine & (j > 0) & (j < n_groups))
        def _():
            multiply(parity)
            finish(1 - parity)

        @pl.when(mine & (j == n_groups))
        def _():
            finish(1 - parity)


def _latent_q(p_bf, p_f32, cq_off, q_lora, kv_lora, cos, sin, qan, wqn, wqp, qng, qrg, kvg, krg, tm,
              heads_per_step):
    m = p_bf.shape[0]
    n_heads = wqn.shape[1] // QK_NOPE
    n_cq = q_lora // COL_TILE
    cq0 = cq_off // COL_TILE
    ns = p_f32.shape[1]
    wstep = heads_per_step * QK_PAD
    n_groups = n_heads // heads_per_step
    assert heads_per_step % 2 == 0 and n_heads % heads_per_step == 0
    row = lambda i, j: (i, 0)
    const = lambda i, j: (0, 0)
    pos_blocks = cos.shape[0] // tm
    pos = lambda i, j: (i % pos_blocks, 0)
    in_specs = [pl.BlockSpec((tm, COL_TILE), functools.partial(lambda i, j, c: (i, c), c=cq0 + c))
                for c in range(n_cq)]
    in_specs += [
        pl.BlockSpec((tm, ns), row),
        pl.BlockSpec((tm, LANE), pos),
        pl.BlockSpec((tm, LANE), pos),
        pl.BlockSpec((1, q_lora), const),
        pl.BlockSpec((q_lora, heads_per_step * QK_NOPE), lambda i, j: (0, jnp.minimum(j, n_groups - 1))),
        pl.BlockSpec((q_lora, heads_per_step * QK_ROPE), lambda i, j: (0, jnp.minimum(j, n_groups - 1))),
        pl.BlockSpec((1, LANE), const),
        pl.BlockSpec((1, LANE), const),
        pl.BlockSpec((1, kv_lora), const),
        pl.BlockSpec((1, LANE), const),
    ]
    return pl.pallas_call(
        functools.partial(_latent_q_kernel, n_cq=n_cq, kv_lora=kv_lora, heads_per_step=heads_per_step),
        grid=(m // tm, n_groups + 1),
        in_specs=in_specs,
        out_specs=[
            pl.BlockSpec((tm, wstep), lambda i, j: (i, jnp.maximum(j - 1, 0))),
            pl.BlockSpec((tm, kv_lora), row),
            pl.BlockSpec((tm, QK_ROPE), row),
            pl.BlockSpec((tm, LANE), row),
        ],
        out_shape=[
            jax.ShapeDtypeStruct((m, n_heads * QK_PAD), BF16),
            jax.ShapeDtypeStruct((m, kv_lora), F32),
            jax.ShapeDtypeStruct((m, QK_ROPE), F32),
            jax.ShapeDtypeStruct((m, LANE), BF16),
        ],
        scratch_shapes=[pltpu.VMEM((tm, q_lora), BF16), pltpu.VMEM((2, tm, heads_per_step * QK_NOPE), F32),
                        pltpu.VMEM((2, tm, heads_per_step * QK_ROPE), F32)],
        compiler_params=_cparams(("parallel", "arbitrary")),
        name="latent_q",
    )(*([p_bf] * n_cq), p_f32, cos, sin, qan, wqn, wqp, qng, qrg, kvg, krg)


def _kv_expand_kernel(ckv_ref, kpep_ref, wk_ref, wv_ref, kng_ref, k_ref, v_ref, *, n_heads):
    c = ckv_ref[...].astype(BF16)
    kf = _dot(c, wk_ref[...])
    v_ref[...] = _dot(c, wv_ref[...]).astype(BF16)
    kpep = kpep_ref[...]
    for h in range(n_heads):
        kn = _rms(kf[:, h * QK_NOPE:(h + 1) * QK_NOPE], QK_NOPE) * kng_ref[...]
        k_ref[:, h * QK_PAD:h * QK_PAD + QK_NOPE] = kn.astype(BF16)
        k_ref[:, h * QK_PAD + QK_NOPE:(h + 1) * QK_PAD] = kpep


def _kv_expand(ckv, kpep, wk, wv, kng, tm):
    r, kv_lora = ckv.shape
    n_heads = wk.shape[1] // QK_NOPE
    row = lambda i: (i, 0)
    const = lambda i: (0, 0)
    return pl.pallas_call(
        functools.partial(_kv_expand_kernel, n_heads=n_heads),
        grid=(r // tm,),
        in_specs=[
            pl.BlockSpec((tm, kv_lora), row),
            pl.BlockSpec((tm, LANE), row),
            pl.BlockSpec((kv_lora, n_heads * QK_NOPE), const),
            pl.BlockSpec((kv_lora, n_heads * V_HEAD), const),
            pl.BlockSpec((1, LANE), const),
        ],
        out_specs=[
            pl.BlockSpec((tm, n_heads * QK_PAD), row),
            pl.BlockSpec((tm, n_heads * V_HEAD), row),
        ],
        out_shape=[
            jax.ShapeDtypeStruct((r, n_heads * QK_PAD), BF16),
            jax.ShapeDtypeStruct((r, n_heads * V_HEAD), BF16),
        ],
        compiler_params=_cparams(("parallel",)),
        name="kv_expand",
    )(ckv, kpep, wk, wv, kng)


def _lane_chunk_sum(p):
    out = p[:, :LANE]
    for c in range(1, p.shape[1] // LANE):
        out = out + p[:, c * LANE:(c + 1) * LANE]
    return out


def _gated(lvec, acc, za):
    za = za.astype(F32)
    return acc / jnp.sum(lvec, axis=-1, keepdims=True) * (za * jax.nn.sigmoid(za))


def _softmax_scale():
    return float((QK_NOPE + QK_ROPE) ** -0.5 * 1.4426950408889634)


def _attn_prompt_block(n_full, q_ref, k_ref, km_ref, za_ref, o_ref, s_ref, p_ref, va_ref, vma_ref, *,
                       tq, heads, c, n_meta):
    td = min(tq, 256)
    pieces = [(i * tq, tq, 0, False) for i in range(n_full)]
    pieces += [(n_full * tq + j * td, td, j * td, True) for j in range(tq // td)]

    def upd(full, r0, fn):
        return fn(full) if r0 == 0 else jnp.concatenate([full[:r0], fn(full[r0:])], axis=0)

    def scores(h):
        qs = slice(h * QK_PAD, (h + 1) * QK_PAD)
        q = q_ref[:, qs]
        col = lax.broadcasted_iota(jnp.int32, (tq, LANE), 1)
        t = jnp.where(col < n_meta, _dot_nt(q, km_ref[:, qs]) * c, -jnp.inf)
        s_ref[h, :, :LANE] = t
        mx = t
        off = LANE
        for start, rows, r0, masked in pieces:
            t = _dot_nt(q[r0:], k_ref[start:start + rows, qs]) * c
            if masked:
                rq = lax.broadcasted_iota(jnp.int32, t.shape, 0) // CHUNK
                ck = lax.broadcasted_iota(jnp.int32, t.shape, 1) // CHUNK
                t = jnp.where(rq >= ck, t, -jnp.inf)
            s_ref[h, r0:, off:off + rows] = t
            tmax = t[:, :LANE]
            for ch in range(1, rows // LANE):
                tmax = jnp.maximum(tmax, t[:, ch * LANE:(ch + 1) * LANE])
            mx = upd(mx, r0, lambda a: jnp.maximum(a, tmax))
            off += rows
        return jnp.max(mx, axis=-1, keepdims=True)

    def values(h, m):
        vs = slice(h * V_HEAD, (h + 1) * V_HEAD)
        va = slice(2 * h * V_HEAD, 2 * (h + 1) * V_HEAD)
        p_ref[h, :, :LANE] = jnp.exp2(s_ref[h, :, :LANE] - m).astype(BF16)
        off = LANE
        for start, rows, r0, _ in pieces:
            p_ref[h, r0:, off:off + rows] = jnp.exp2(s_ref[h, r0:, off:off + rows] - m[r0:]).astype(BF16)
            off += rows
        acc = _dot(p_ref[h, :, :LANE], vma_ref[:, va])
        n_main = n_full * tq
        if n_main:
            acc = acc + _dot(p_ref[h, :, LANE:LANE + n_main], va_ref[:n_main, va])
        off = LANE + n_main
        for start, rows, r0, _ in pieces[n_full:]:
            pv = _dot(p_ref[h, r0:, off:off + rows], va_ref[start:start + rows, va])
            acc = upd(acc, r0, lambda a: a + pv)
            off += rows
        za = za_ref[:, vs].astype(F32)
        out = acc[:, :V_HEAD] / acc[:, V_HEAD:V_HEAD + 1] * (za * jax.nn.sigmoid(za))
        o_ref[:, vs] = out.astype(o_ref.dtype)

    maxes = [scores(h) for h in range(heads)]
    for h in range(heads):
        values(h, maxes[h])


def _attn_prompt_kernel(q_ref, k_ref, v_ref, km_ref, vm_ref, za_ref, o_ref, s_ref, p_ref, va_ref, vma_ref, *,
                        nqb, heads, **kw):
    qi = pl.program_id(2)

    @pl.when(qi == 0)
    def _():
        for h in range(heads):
            vs = slice(h * V_HEAD, (h + 1) * V_HEAD)
            va_ref[:, 2 * h * V_HEAD:(2 * h + 1) * V_HEAD] = v_ref[:, vs]
            va_ref[:, (2 * h + 1) * V_HEAD:(2 * h + 2) * V_HEAD] = jnp.ones((va_ref.shape[0], V_HEAD), BF16)
            vma_ref[:, 2 * h * V_HEAD:(2 * h + 1) * V_HEAD] = vm_ref[:, vs]
            vma_ref[:, (2 * h + 1) * V_HEAD:(2 * h + 2) * V_HEAD] = jnp.ones((vma_ref.shape[0], V_HEAD), BF16)

    for n_full in range(nqb):
        pl.when(qi == n_full)(
            functools.partial(_attn_prompt_block, n_full, q_ref, k_ref, km_ref, za_ref, o_ref, s_ref, p_ref, va_ref,
                              vma_ref, heads=heads, **kw))


def _attn_prompt(q, k, v, k_small, v_small, meta_blk, p_bf, za_off, batch, seq, tq, heads, n_meta):
    n_heads = q.shape[1] // QK_PAD
    nqb = seq // tq
    za0 = za_off // (V_HEAD * heads)
    assert za_off % (V_HEAD * heads) == 0 and n_heads % heads == 0 and tq % CHUNK == 0
    return pl.pallas_call(
        functools.partial(_attn_prompt_kernel, nqb=nqb, tq=tq, heads=heads, c=_softmax_scale(), n_meta=n_meta),
        grid=(batch, n_heads // heads, nqb),
        in_specs=[
            pl.BlockSpec((tq, heads * QK_PAD), lambda b, h, i: (b * nqb + i, h)),
            pl.BlockSpec((seq, heads * QK_PAD), lambda b, h, i: (b, h)),
            pl.BlockSpec((seq, heads * V_HEAD), lambda b, h, i: (b, h)),
            pl.BlockSpec((LANE, heads * QK_PAD), lambda b, h, i: (meta_blk, h)),
            pl.BlockSpec((LANE, heads * V_HEAD), lambda b, h, i: (meta_blk, h)),
            pl.BlockSpec((tq, heads * V_HEAD), lambda b, h, i: (b * nqb + i, za0 + h)),
        ],
        out_specs=pl.BlockSpec((tq, heads * V_HEAD), lambda b, h, i: (b * nqb + i, h)),
        out_shape=jax.ShapeDtypeStruct((batch * seq, n_heads * V_HEAD), BF16),
        scratch_shapes=[pltpu.VMEM((heads, tq, LANE + seq), F32), pltpu.VMEM((heads, tq, LANE + seq), BF16),
                        pltpu.VMEM((seq, 2 * heads * V_HEAD), BF16), pltpu.VMEM((LANE, 2 * heads * V_HEAD), BF16)],
        compiler_params=_cparams(("parallel", "parallel", "arbitrary")),
        name="attn_prompt",
    )(q, k, v, k_small, v_small, p_bf)


def _attn_sample_kernel(q_ref, kc_ref, vc_ref, ks_ref, vs_ref, za_ref, o_ref, *, heads, lq, c, n_meta, meta_row):
    b = pl.program_id(0)
    col = lax.broadcasted_iota(jnp.int32, (lq, ks_ref.shape[0]), 1)
    own = (col >= b * lq) & (col < (b + 1) * lq)
    meta = (col >= meta_row) & (col < meta_row + n_meta)
    visible = own | meta
    for h in range(heads):
        qs = slice(h * QK_PAD, (h + 1) * QK_PAD)
        vs = slice(h * V_HEAD, (h + 1) * V_HEAD)
        q = q_ref[:, qs]
        t_cache = _dot_nt(q, kc_ref[:, qs]) * c
        t_new = jnp.where(visible, _dot_nt(q, ks_ref[:, qs]) * c, -jnp.inf)
        m = jnp.maximum(jnp.max(t_cache, axis=-1, keepdims=True), jnp.max(t_new, axis=-1, keepdims=True))
        p_cache = jnp.exp2(t_cache - m)
        p_new = jnp.exp2(t_new - m)
        lvec = _lane_chunk_sum(p_cache) + _lane_chunk_sum(p_new)
        acc = _dot(p_cache.astype(BF16), vc_ref[:, vs]) + _dot(p_new.astype(BF16), vs_ref[:, vs])
        o_ref[:, vs] = _gated(lvec, acc, za_ref[:, vs]).astype(o_ref.dtype)


def _attn_sample(q, kc, vc, ks, vs, p_bf, za_off, batch, lq, past, n_meta, heads):
    n_heads = q.shape[1] // QK_PAD
    rows = ks.shape[0]
    za0 = za_off // (V_HEAD * heads)
    assert za_off % (V_HEAD * heads) == 0 and n_heads % heads == 0
    return pl.pallas_call(
        functools.partial(_attn_sample_kernel, heads=heads, lq=lq, c=_softmax_scale(), n_meta=n_meta,
                          meta_row=batch * lq),
        grid=(batch, n_heads // heads),
        in_specs=[
            pl.BlockSpec((lq, heads * QK_PAD), lambda b, h: (b, h)),
            pl.BlockSpec((past, heads * QK_PAD), lambda b, h: (b, h)),
            pl.BlockSpec((past, heads * V_HEAD), lambda b, h: (b, h)),
            pl.BlockSpec((rows, heads * QK_PAD), lambda b, h: (0, h)),
            pl.BlockSpec((rows, heads * V_HEAD), lambda b, h: (0, h)),
            pl.BlockSpec((lq, heads * V_HEAD), lambda b, h: (b, za0 + h)),
        ],
        out_specs=pl.BlockSpec((lq, heads * V_HEAD), lambda b, h: (b, h)),
        out_shape=jax.ShapeDtypeStruct((batch * lq, n_heads * V_HEAD), BF16),
        compiler_params=_cparams(("parallel", "arbitrary")),
        name="attn_sample",
    )(q, kc, vc, ks, vs, p_bf)


def _log_sigmoid(x):
    return jnp.minimum(x, 0.0) - jnp.log1p(jnp.exp(-jnp.abs(x)))


def _pad_rows(a, rows):
    if a.shape[0] == rows:
        return a
    return jnp.concatenate([a, jnp.zeros((rows - a.shape[0], a.shape[1]), a.dtype)], axis=0)


def _mlstm_kernel(q_ref, k_ref, v_ref, mo_ref, zb_ref, g_ref, gb_ref, gain_ref, c0_ref, n0_ref, m0_ref,
                  h_ref, c_ref, n_ref, m_ref, *, n_heads, lc, lp, dk, dv):
    @pl.when(pl.program_id(1) == 0)
    def _():
        c_ref[...] = c0_ref[...]
        n_ref[...] = n0_ref[...]
        m_ref[...] = m0_ref[...]

    hi = lax.Precision.HIGHEST
    gates = _pad_rows(g_ref[...] + gb_ref[...], lp)
    gates_t = gates.T
    t_col = lax.broadcasted_iota(jnp.int32, (lp, 1), 0)
    t_row = lax.broadcasted_iota(jnp.int32, (1, lp), 1)
    r_idx = lax.broadcasted_iota(jnp.int32, (lp, lp), 0)
    c_idx = lax.broadcasted_iota(jnp.int32, (lp, lp), 1)
    causal = r_idx >= c_idx
    lower = causal.astype(F32)
    upper = (c_idx >= r_idx).astype(F32)
    b_cols = jnp.dot(lower, jnp.where(t_col < lc, _log_sigmoid(gates), 0.0), precision=hi,
                     preferred_element_type=F32)
    b_rows = jnp.dot(jnp.where(t_row < lc, _log_sigmoid(gates_t), 0.0), upper, precision=hi,
                     preferred_element_type=F32)

    for h in range(n_heads):
        li = QK_ROPE + h
        lf_lane = QK_ROPE + n_heads + h
        ig_col = jnp.where(t_col < lc, gates[:, li:li + 1], NEG_BIG)
        ig_row = jnp.where(t_row < lc, gates_t[li:li + 1, :], NEG_BIG)
        b_col = b_cols[:, lf_lane:lf_lane + 1]
        b_row = b_rows[lf_lane:lf_lane + 1, :]
        m0 = m_ref[0, 0:1, h:h + 1]
        d = jnp.where(causal, b_col - b_row + ig_row, -jnp.inf)
        a_col = b_col + m0
        m = jnp.maximum(a_col, jnp.max(d, axis=-1, keepdims=True))
        w_inter = jnp.exp(a_col - m)

        q = _pad_rows(q_ref[:, h * dk:(h + 1) * dk], lp)
        k = _pad_rows(k_ref[:, h * dk:(h + 1) * dk], lp) * (dk ** -0.5)
        v = _pad_rows(v_ref[:, h * dv:(h + 1) * dv], lp)
        qk = _dot_nt(q, k) * jnp.exp(d - m)
        c_old = c_ref[0, h]
        n_old = n_ref[0, h:h + 1, :]
        num = w_inter * _dot(q, c_old.astype(BF16)) + _dot(qk.astype(BF16), v)
        den = (w_inter * jnp.sum(q.astype(F32) * n_old, axis=-1, keepdims=True)
               + jnp.sum(qk, axis=-1, keepdims=True))
        denc = jnp.maximum(jnp.abs(den), jnp.exp(-m))

        b_last = b_col[lp - 1:lp, :]
        g_row = b_last - b_row + ig_row
        g_col = b_last - b_col + ig_col
        m_new = jnp.maximum(b_last + m0, jnp.max(g_row, axis=-1, keepdims=True))
        decay = jnp.exp(b_last + m0 - m_new)
        kw = k.astype(F32) * jnp.exp(g_col - m_new)
        c_ref[0, h] = decay * c_old + _dot(kw.T.astype(BF16), v)
        n_ref[0, h:h + 1, :] = decay * n_old + jnp.sum(kw, axis=0, keepdims=True)
        m_ref[0, 0:1, h:h + 1] = m_new

        num = num[:lc]
        denc = denc[:lc]
        normed = num * lax.rsqrt(jnp.sum(num * num, axis=-1, keepdims=True) * (1.0 / dv) + EPS * denc * denc)
        mo = mo_ref[:, h * dv:(h + 1) * dv].astype(F32)
        zb = zb_ref[:, h * dv:(h + 1) * dv].astype(F32)
        out = jax.nn.sigmoid(mo) * (normed * gain_ref[:, h * dv:(h + 1) * dv])
        h_ref[:, h * dv:(h + 1) * dv] = (out * (zb * jax.nn.sigmoid(zb))).astype(h_ref.dtype)


def _mlstm(p_bf, p_f32, offs, gate_bias, gain, c0, n0, m0, batch, seq, lc, row0, share_state, n_heads, dk, dv):
    lp = -(-lc // LANE) * LANE
    nch = seq // lc
    blk0 = row0 // lc
    wqk = n_heads * dk
    wv = n_heads * dv
    gate_blk = p_f32.shape[1] // LANE - 1
    rows = lambda col: (lambda b, c: (blk0 + b * nch + c, col))
    state = (lambda b, c: (0, 0, 0, 0)) if share_state else (lambda b, c: (b, 0, 0, 0))
    state3 = (lambda b, c: (0, 0, 0)) if share_state else (lambda b, c: (b, 0, 0))
    out_rows = batch * seq
    return pl.pallas_call(
        functools.partial(_mlstm_kernel, n_heads=n_heads, lc=lc, lp=lp, dk=dk, dv=dv),
        grid=(batch, nch),
        in_specs=[
            pl.BlockSpec((lc, wqk), rows(offs["mq"] // wqk)),
            pl.BlockSpec((lc, wqk), rows(offs["mk"] // wqk)),
            pl.BlockSpec((lc, wv), rows(offs["mv"] // wv)),
            pl.BlockSpec((lc, wv), rows(offs["mo"] // wv)),
            pl.BlockSpec((lc, wv), rows(offs["zb"] // wv)),
            pl.BlockSpec((lc, LANE), rows(gate_blk)),
            pl.BlockSpec((1, LANE), lambda b, c: (0, 0)),
            pl.BlockSpec((1, wv), lambda b, c: (0, 0)),
            pl.BlockSpec((1, n_heads, dk, dv), state),
            pl.BlockSpec((1, n_heads, dk), state3),
            pl.BlockSpec((1, 1, n_heads), state3),
        ],
        out_specs=[
            pl.BlockSpec((lc, wv), lambda b, c: (b * nch + c, 0)),
            pl.BlockSpec((1, n_heads, dk, dv), lambda b, c: (b, 0, 0, 0)),
            pl.BlockSpec((1, n_heads, dk), lambda b, c: (b, 0, 0)),
            pl.BlockSpec((1, 1, n_heads), lambda b, c: (b, 0, 0)),
        ],
        out_shape=[
            jax.ShapeDtypeStruct((out_rows, wv), BF16),
            jax.ShapeDtypeStruct((batch, n_heads, dk, dv), F32),
            jax.ShapeDtypeStruct((batch, n_heads, dk), F32),
            jax.ShapeDtypeStruct((batch, 1, n_heads), F32),
        ],
        compiler_params=_cparams(("parallel", "arbitrary")),
        name="mlstm",
    )(p_bf, p_bf, p_bf, p_bf, p_bf, p_f32, gate_bias, gain, c0, n0, m0)


def _out_proj_kernel(a_ref, m_ref, wa_ref, wm_ref, x_ref, o_ref, *wb_refs):
    wa = wa_ref[...].astype(BF16)
    wm = wm_ref[...].astype(BF16)
    o_ref[...] = x_ref[...] + _dot(a_ref[...], wa) + _dot(m_ref[...], wm)
    if wb_refs:
        wb_refs[0][...] = wa
        wb_refs[1][...] = wm


def _out_proj(a, ml, w_a, w_m, wm_blk, x, tm, tn, emit_bf16=False):
    rows, wa = a.shape
    wm = ml.shape[1]
    d = w_a.shape[1]
    assert wa == wm and (not emit_bf16 or rows == tm)
    out_specs = [pl.BlockSpec((tm, tn), lambda i, j: (i, j))]
    out_shape = [jax.ShapeDtypeStruct((rows, d), F32)]
    if emit_bf16:
        out_specs += [pl.BlockSpec((wa, tn), lambda i, j: (0, j)), pl.BlockSpec((wm, tn), lambda i, j: (0, j))]
        out_shape += [jax.ShapeDtypeStruct((wa, d), BF16), jax.ShapeDtypeStruct((wm, d), BF16)]
    return pl.pallas_call(
        _out_proj_kernel,
        grid=(rows // tm, d // tn),
        in_specs=[
            pl.BlockSpec((tm, wa), lambda i, j: (i, 0)),
            pl.BlockSpec((tm, wm), lambda i, j: (i, 0)),
            pl.BlockSpec((wa, tn), lambda i, j: (0, j)),
            pl.BlockSpec((wm, tn), lambda i, j: (wm_blk, j)),
            pl.BlockSpec((tm, tn), lambda i, j: (i, j)),
        ],
        out_specs=out_specs,
        out_shape=out_shape,
        compiler_params=_cparams(("parallel", "arbitrary")),
        name="out_proj",
    )(a, ml, w_a, w_m, x)


def _rope_tables(pos):
    inv_freq = ROPE_THETA ** (-jnp.arange(HALF_ROPE, dtype=F32) / HALF_ROPE)
    ang = pos.astype(F32)[:, None] * inv_freq[None, :]
    cos, sin = jnp.cos(ang), jnp.sin(ang)
    return jnp.concatenate([cos, cos, cos, cos], axis=1), jnp.concatenate([-sin, sin, -sin, sin], axis=1)


def _pad_lanes(vec):
    return jnp.pad(vec.astype(F32), (0, LANE - vec.shape[0]))[None, :]


def _row_tile(rows, target):
    t = min(rows, target)
    while rows % t:
        t //= 2
    return t


def kernel(x_prompt, x_sample, cache_ckv, cache_kpe, state_C, state_n, state_m, meta_tokens, norm_gain, w_in,
           b_igate, b_fgate, q_a_norm, w_q_up, q_nope_norm, q_rope_norm, kv_a_norm, k_rope_norm, w_kv_up,
           k_nope_norm, ml_out_norm, w_out):
    batch, seq, d = x_prompt.shape
    dbatch, dseq, _ = x_sample.shape
    past = cache_ckv.shape[1]
    n_meta = meta_tokens.shape[0]
    q_lora = q_a_norm.shape[0]
    kv_lora = kv_a_norm.shape[0]
    ml_heads, dv = ml_out_norm.shape
    dk = state_n.shape[-1]
    mla_heads = w_kv_up.shape[1] // (QK_NOPE + V_HEAD)
    mla_w = mla_heads * V_HEAD
    ml_w = ml_heads * dv
    mqk_w = ml_heads * dk
    assert q_lora % COL_TILE == 0 and kv_lora % LANE == 0 and 2 * ml_heads <= LANE - QK_ROPE
    assert mla_w == ml_w and w_out.shape[0] == mla_w + ml_w

    o_cq = 0
    o_ckv = o_cq + q_lora
    o_kpe = o_ckv + kv_lora
    o_mq = o_kpe + QK_ROPE
    o_mk = o_mq + mqk_w
    o_mv = o_mk + mqk_w
    o_mo = o_mv + ml_w
    o_mi = o_mo + ml_w
    o_mf = o_mi + ml_heads
    o_za = o_mf + ml_heads
    o_zb = o_za + mla_w

    offs = {"mq": 0, "mk": mqk_w, "mv": 2 * mqk_w, "mo": 2 * mqk_w + ml_w, "za": 2 * mqk_w + 2 * ml_w,
            "zb": 2 * mqk_w + 2 * ml_w + mla_w, "cq": 2 * mqk_w + 3 * ml_w + mla_w}
    gate_pad = LANE - QK_ROPE - 2 * ml_heads
    w_in_t = w_in.T
    main_segments = [(o_mq, mqk_w, offs["mq"]), (o_mk, mqk_w, offs["mk"]), (o_mv, ml_w, offs["mv"]),
                     (o_mo, ml_w, offs["mo"]), (o_za, mla_w, offs["za"]), (o_zb, ml_w, offs["zb"]),
                     (o_cq, q_lora, offs["cq"])]
    w_small = _stack_rows(w_in_t, [(o_ckv, kv_lora), (o_kpe, QK_ROPE), (o_mi, 2 * ml_heads)], kv_lora + LANE,
                          _row_tile(d, 512))
    gate_bias = jnp.concatenate([jnp.zeros((QK_ROPE,), F32), b_igate.astype(F32), b_fgate.astype(F32),
                                 jnp.zeros((gate_pad,), F32)])[None, :]
    qk_w = QK_NOPE + QK_ROPE
    kv_w = QK_NOPE + V_HEAD
    wqn, wqp = _regroup(w_q_up, [[(h * qk_w, QK_NOPE, h * QK_NOPE) for h in range(mla_heads)],
                                 [(h * qk_w + QK_NOPE, QK_ROPE, h * QK_ROPE) for h in range(mla_heads)]],
                        _row_tile(q_lora, 512))
    wk, wv = _regroup(w_kv_up, [[(h * kv_w, QK_NOPE, h * QK_NOPE) for h in range(mla_heads)],
                                [(h * kv_w + QK_NOPE, V_HEAD, h * V_HEAD) for h in range(mla_heads)]], kv_lora)
    gain_row = norm_gain.astype(F32)[None, :]
    qan = q_a_norm.astype(F32)[None, :]
    kvg = kv_a_norm.astype(F32)[None, :]
    qng = q_nope_norm.astype(F32)[None, :]
    kng = k_nope_norm.astype(F32)[None, :]
    qrg = jnp.tile(q_rope_norm.astype(F32), LANE // QK_ROPE)[None, :]
    krg = _pad_lanes(k_rope_norm)
    ml_gain = ml_out_norm.astype(F32).reshape(1, ml_w)

    n_p = batch * seq
    n_s = dbatch * dseq
    assert n_s % LANE == 0 and n_meta <= LANE and n_meta % 16 == 0
    rows_s = n_s + n_meta
    xp = x_prompt.reshape(n_p, d)
    xs = jnp.concatenate([x_sample.reshape(n_s, d), meta_tokens.astype(x_sample.dtype)], axis=0)

    pos_s = jnp.concatenate([jnp.tile(n_meta + past + jnp.arange(dseq), dbatch), jnp.arange(n_meta)])
    cos_p, sin_p = _rope_tables(n_meta + jnp.arange(seq))
    cos_s, sin_s = _rope_tables(pos_s)

    tm_p = _row_tile(seq, 512)
    proj_sub = 2 if n_p % (2 * tm_p) == 0 else 1
    w_main, pbf_s, pf_s = _proj_regroup(xs, gain_row, w_in_t, main_segments, w_small)
    pbf_p, pf_p = _proj(xp, gain_row, w_main, w_small, proj_sub * tm_p, proj_sub)

    hps = 4 if mla_heads % 4 == 0 and mla_heads > 4 else 2
    q_p, ckv_p, kpe_p, kpep_p = _latent_q(pbf_p, pf_p, offs["cq"], q_lora, kv_lora, cos_p, sin_p, qan, wqn, wqp,
                                          qng, qrg, kvg, krg, tm_p, hps)
    q_s, ckv_s, kpe_s, kpep_s = _latent_q(pbf_s, pf_s, offs["cq"], q_lora, kv_lora, cos_s, sin_s, qan, wqn, wqp,
                                          qng, qrg, kvg, krg, rows_s, hps)

    small_rows = n_s + LANE
    meta_blk = n_s // LANE
    tail = ((0, small_rows - rows_s), (0, 0))
    k_p, v_p = _kv_expand(ckv_p, kpep_p, wk, wv, kng, tm_p)
    k_s, v_s = _kv_expand(jnp.pad(ckv_s, tail), jnp.pad(kpep_s, tail), wk, wv, kng, small_rows)
    cache_rows = dbatch * past
    cache_kpep = jnp.pad(cache_kpe.reshape(cache_rows, QK_ROPE), ((0, 0), (0, LANE - QK_ROPE))).astype(BF16)
    k_c, v_c = _kv_expand(cache_ckv.reshape(cache_rows, kv_lora).astype(F32), cache_kpep, wk, wv, kng,
                          _row_tile(cache_rows, 512))

    tq = _row_tile(seq, ATTN_TQ)
    attn_p = _attn_prompt(q_p, k_p, v_p, k_s, v_s, meta_blk, pbf_p, offs["za"], batch, seq, tq,
                          min(ATTN_HEADS, mla_heads), n_meta)
    attn_s = _attn_sample(q_s, k_c, v_c, k_s, v_s, pbf_s, offs["za"], dbatch, dseq, past, n_meta,
                          min(4, mla_heads))

    zc = jnp.zeros((1, ml_heads, dk, dv), F32)
    zn = jnp.zeros((1, ml_heads, dk), F32)
    zm = jnp.zeros((1, 1, ml_heads), F32)
    _, c_m, n_m, m_m = _mlstm(pbf_s, pf_s, offs, gate_bias, ml_gain, zc, zn, zm, 1, n_meta, n_meta, n_s, True,
                              ml_heads, dk, dv)
    lc_p = _row_tile(seq, 256)
    ml_p, c_p, n_pr, m_p = _mlstm(pbf_p, pf_p, offs, gate_bias, ml_gain, c_m, n_m, m_m, batch, seq, lc_p, 0,
                                  True, ml_heads, dk, dv)
    ml_s, c_s, n_sm, m_s = _mlstm(pbf_s, pf_s, offs, gate_bias, ml_gain, state_C.astype(F32),
                                  state_n.astype(F32), state_m.astype(F32).reshape(dbatch, 1, ml_heads),
                                  dbatch, dseq, dseq, 0, False, ml_heads, dk, dv)

    y_s, w_oa, w_om = _out_proj(attn_s, ml_s, w_out, w_out, 1, xs, n_s, _row_tile(d, 512), emit_bf16=True)
    y_p, = _out_proj(attn_p, ml_p, w_oa, w_om, 0, xp, _row_tile(n_p, 1024), _row_tile(d, 512))

    meta_ckv = ckv_s[n_s:n_s + n_meta]
    meta_kpe = kpe_s[n_s:n_s + n_meta]
    ckv_prompt = jnp.concatenate([jnp.broadcast_to(meta_ckv[None], (batch, n_meta, kv_lora)),
                                  ckv_p.reshape(batch, seq, kv_lora)], axis=1)
    kpe_prompt = jnp.concatenate([jnp.broadcast_to(meta_kpe[None], (batch, n_meta, QK_ROPE)),
                                  kpe_p.reshape(batch, seq, QK_ROPE)], axis=1)
    return (y_p.reshape(batch, seq, d), y_s.reshape(dbatch, dseq, d), ckv_prompt, kpe_prompt,
            c_p, n_pr, m_p.reshape(batch, ml_heads),
            ckv_s[:n_s].reshape(dbatch, dseq, kv_lora), kpe_s[:n_s].reshape(dbatch, dseq, QK_ROPE),
            c_s, n_sm, m_s.reshape(dbatch, ml_heads))
```

```python
import functools
import math

import jax
import jax.numpy as jnp
from jax import lax
from jax.experimental import pallas as pl
from jax.experimental.pallas import tpu as pltpu

CHUNK = 64
EPS = 1e-6
ROPE_THETA = 10000.0
V_HEAD = 128
QK_NOPE = 128
QK_ROPE = 64
HALF_ROPE = QK_ROPE // 2
LANE = 128
QK_PAD = QK_NOPE + LANE
COL_TILE = 512
NEG_BIG = -1e30
ATTN_TQ = 512
ATTN_HEADS = 4
VMEM_LIMIT = 56 * 1024 * 1024

F32 = jnp.float32
BF16 = jnp.bfloat16


def _cparams(sem):
    return pltpu.CompilerParams(dimension_semantics=sem, vmem_limit_bytes=VMEM_LIMIT)


def _dot(a, b):
    return jnp.dot(a, b, preferred_element_type=F32)


def _dot_nt(a, b):
    return lax.dot_general(a, b, (((1,), (1,)), ((), ())), preferred_element_type=F32)


def _rms(x, n):
    return x * lax.rsqrt(jnp.sum(x * x, axis=-1, keepdims=True) * (1.0 / n) + EPS)


def _rope_lanes(x, cos, sin_signed):
    lane = lax.broadcasted_iota(jnp.int32, x.shape, 1)
    first_half = (lane & (QK_ROPE - 1)) < HALF_ROPE
    rot = jnp.where(first_half, pltpu.roll(x, LANE - HALF_ROPE, 1), pltpu.roll(x, HALF_ROPE, 1))
    return x * cos + rot * sin_signed


def _regroup_kernel(w_ref, *o_refs, plans):
    for o_ref, segments in zip(o_refs, plans):
        for src, n, dst in segments:
            o_ref[:, dst:dst + n] = w_ref[:, src:src + n].astype(o_ref.dtype)


def _regroup(w, plans, tr):
    rows, cols = w.shape
    widths = [sum(n for _, n, _ in segments) for segments in plans]
    for segments in plans:
        pos = 0
        for _, n, dst in sorted(segments, key=lambda s: s[2]):
            assert dst == pos
            pos += n
    return pl.pallas_call(
        functools.partial(_regroup_kernel, plans=tuple(tuple(s) for s in plans)),
        grid=(rows // tr,),
        in_specs=[pl.BlockSpec((tr, cols), lambda i: (i, 0))],
        out_specs=[pl.BlockSpec((tr, width), lambda i: (i, 0)) for width in widths],
        out_shape=[jax.ShapeDtypeStruct((rows, width), BF16) for width in widths],
        compiler_params=_cparams(("parallel",)),
        name="regroup",
    )(w)


def _stack_rows_kernel(*refs):
    *w_refs, o_ref = refs
    parts = [r[...] for r in w_refs]
    used = sum(p.shape[0] for p in parts)
    parts.append(jnp.zeros((o_ref.shape[0] - used, o_ref.shape[1]), parts[0].dtype))
    o_ref[...] = jnp.concatenate(parts, axis=0).astype(o_ref.dtype)


def _stack_rows(wt, segments, out_rows, tc):
    cols = wt.shape[1]
    return pl.pallas_call(
        _stack_rows_kernel,
        grid=(cols // tc,),
        in_specs=[pl.BlockSpec((pl.Element(n), pl.Element(tc)), functools.partial(lambda i, s: (s, i * tc), s=src))
                  for src, n in segments],
        out_specs=pl.BlockSpec((out_rows, tc), lambda i: (0, i)),
        out_shape=jax.ShapeDtypeStruct((out_rows, cols), BF16),
        compiler_params=_cparams(("parallel",)),
        name="stack_rows",
    )(*([wt] * len(segments)))


def _proj_regroup_kernel(x_ref, g_ref, w_ref, ws_ref, om_ref, o_ref, os_ref, hn_ref):
    @pl.when(pl.program_id(0) == 0)
    def _():
        x = x_ref[...]
        hn = (_rms(x, x.shape[-1]) * g_ref[...]).astype(BF16)
        hn_ref[...] = hn
        os_ref[...] = _dot_nt(hn, ws_ref[...])

    wb = w_ref[...].astype(BF16)
    om_ref[...] = wb
    o_ref[...] = _dot_nt(hn_ref[...], wb).astype(o_ref.dtype)


def _proj_regroup(x, gain, wt, segments, w_small):
    m, d = x.shape
    ns = w_small.shape[0]
    n = sum(sz for _, sz, _ in segments)
    merged = []
    for src, sz, dst in sorted(segments, key=lambda s: s[2]):
        if merged and merged[-1][0] + merged[-1][1] == src and merged[-1][2] + merged[-1][1] == dst:
            merged[-1] = (merged[-1][0], merged[-1][1] + sz, merged[-1][2])
        else:
            merged.append((src, sz, dst))
    segments = merged
    assert all(sz % COL_TILE == 0 and dst % COL_TILE == 0 for _, sz, dst in segments)
    align = functools.reduce(math.gcd, [src for src, _, _ in segments], COL_TILE)

    def src_row(j):
        r = j * COL_TILE
        out = 0
        for src, sz, dst in segments:
            out = jnp.where((r >= dst) & (r < dst + sz), src + r - dst, out)
        return out

    const = lambda j: (0, 0)
    return pl.pallas_call(
        _proj_regroup_kernel,
        grid=(n // COL_TILE,),
        in_specs=[
            pl.BlockSpec((m, d), const),
            pl.BlockSpec((1, d), const),
            pl.BlockSpec((pl.Element(COL_TILE), pl.Element(d)), lambda j: (pl.multiple_of(src_row(j), align), 0)),
            pl.BlockSpec((ns, d), const),
        ],
        out_specs=[
            pl.BlockSpec((COL_TILE, d), lambda j: (j, 0)),
            pl.BlockSpec((m, COL_TILE), lambda j: (0, j)),
            pl.BlockSpec((m, ns), const),
        ],
        out_shape=[jax.ShapeDtypeStruct((n, d), BF16), jax.ShapeDtypeStruct((m, n), BF16),
                   jax.ShapeDtypeStruct((m, ns), F32)],
        scratch_shapes=[pltpu.VMEM((m, d), BF16)],
        compiler_params=_cparams(("arbitrary",)),
        name="proj_regroup",
    )(x, gain, wt, w_small)


def _proj_kernel(x_ref, g_ref, w_ref, ws_ref, o_ref, os_ref, hn_ref, *, n_sub):
    j = pl.program_id(1)
    ts = x_ref.shape[0]

    @pl.when(j < n_sub)
    def _():
        x = x_ref[...]
        hn = (_rms(x, x.shape[-1]) * g_ref[...]).astype(BF16)
        hn_ref[pl.ds(pl.multiple_of(j * ts, ts), ts), :] = hn
        os_ref[...] = _dot_nt(hn, ws_ref[...])

    @pl.when(j >= n_sub)
    def _():
        o_ref[...] = _dot_nt(hn_ref[...], w_ref[...]).astype(o_ref.dtype)


def _proj(x, gain, w_main, w_small, tm, n_sub):
    m, d = x.shape
    n = w_main.shape[0]
    ns = w_small.shape[0]
    ts = tm // n_sub
    sub = lambda i, j: (i * n_sub + jnp.minimum(j, n_sub - 1), 0)
    feat = lambda j: jnp.maximum(j - n_sub, 0)
    return pl.pallas_call(
        functools.partial(_proj_kernel, n_sub=n_sub),
        grid=(m // tm, n_sub + n // COL_TILE),
        in_specs=[
            pl.BlockSpec((ts, d), sub),
            pl.BlockSpec((1, d), lambda i, j: (0, 0)),
            pl.BlockSpec((COL_TILE, d), lambda i, j: (feat(j), 0)),
            pl.BlockSpec((ns, d), lambda i, j: (0, 0)),
        ],
        out_specs=[
            pl.BlockSpec((tm, COL_TILE), lambda i, j: (i, feat(j))),
            pl.BlockSpec((ts, ns), sub),
        ],
        out_shape=[jax.ShapeDtypeStruct((m, n), BF16), jax.ShapeDtypeStruct((m, ns), F32)],
        scratch_shapes=[pltpu.VMEM((tm, d), BF16)],
        compiler_params=_cparams(("parallel", "arbitrary")),
        name="proj",
    )(x, gain, w_main, w_small)


def _latent_q_kernel(*refs, n_cq, kv_lora, heads_per_step):
    cq_refs = refs[:n_cq]
    (pf_ref, cos_ref, sin_ref, qan_ref, wqn_ref, wqp_ref, qng_ref, qrg_ref, kvg_ref, krg_ref,
     q_ref, ckv_ref, kpe_ref, kpep_ref, cqn_ref) = refs[n_cq:]
    cos = cos_ref[...]
    sin = sin_ref[...]
    lane = lax.broadcasted_iota(jnp.int32, cos.shape, 1)
    low = lane < QK_ROPE

    @pl.when(pl.program_id(1) == 0)
    def _():
        cq = jnp.concatenate([r[...].astype(F32) for r in cq_refs], axis=1)
        cqn_ref[...] = (_rms(cq, cq.shape[-1]) * qan_ref[...]).astype(BF16)
        pf = pf_ref[...]
        ckv_raw = pf[:, :kv_lora]
        ckv_ref[...] = _rms(ckv_raw, kv_lora) * kvg_ref[...]
        kraw = jnp.where(low, pf[:, kv_lora:kv_lora + LANE], 0.0)
        kpe = _rope_lanes(_rms(kraw, QK_ROPE) * krg_ref[...], cos, sin)
        kpe_ref[...] = kpe[:, :QK_ROPE]
        kpep_ref[...] = kpe.astype(BF16)

    cqn = cqn_ref[...]
    group = 4 if heads_per_step % 4 == 0 else 2
    for g0 in range(0, heads_per_step, group):
        qn = _dot(cqn, wqn_ref[:, g0 * QK_NOPE:(g0 + group) * QK_NOPE])
        qp = _dot(cqn, wqp_ref[:, g0 * QK_ROPE:(g0 + group) * QK_ROPE])
        for pair in range(group // 2):
            x = qp[:, pair * LANE:(pair + 1) * LANE]
            x2 = x * x
            ss_a = jnp.sum(jnp.where(low, x2, 0.0), axis=-1, keepdims=True)
            ss_b = jnp.sum(jnp.where(low, 0.0, x2), axis=-1, keepdims=True)
            scale = jnp.where(low, lax.rsqrt(ss_a * (1.0 / QK_ROPE) + EPS),
                              lax.rsqrt(ss_b * (1.0 / QK_ROPE) + EPS))
            r = _rope_lanes(x * scale * qrg_ref[...], cos, sin)
            halves = (jnp.where(low, r, 0.0), jnp.where(low, pltpu.roll(r, QK_ROPE, 1), 0.0))
            for h in (0, 1):
                base = (g0 + 2 * pair + h) * QK_PAD
                nope = qn[:, (2 * pair + h) * QK_NOPE:(2 * pair + h + 1) * QK_NOPE]
                q_ref[:, base:base + QK_NOPE] = (_rms(nope, QK_NOPE) * qng_ref[...]).astype(BF16)
                q_ref[:, base + QK_NOPE:base + QK_PAD] = halves[h].astype(BF16)


def _latent_q(p_bf, p_f32, cq_off, q_lora, kv_lora, cos, sin, qan, wqn, wqp, qng, qrg, kvg, krg, tm,
              heads_per_step):
    m = p_bf.shape[0]
    n_heads = wqn.shape[1] // QK_NOPE
    n_cq = q_lora // COL_TILE
    cq0 = cq_off // COL_TILE
    ns = p_f32.shape[1]
    wstep = heads_per_step * QK_PAD
    assert heads_per_step % 2 == 0
    row = lambda i, j: (i, 0)
    const = lambda i, j: (0, 0)
    pos_blocks = cos.shape[0] // tm
    pos = lambda i, j: (i % pos_blocks, 0)
    in_specs = [pl.BlockSpec((tm, COL_TILE), functools.partial(lambda i, j, c: (i, c), c=cq0 + c))
                for c in range(n_cq)]
    in_specs += [
        pl.BlockSpec((tm, ns), row),
        pl.BlockSpec((tm, LANE), pos),
        pl.BlockSpec((tm, LANE), pos),
        pl.BlockSpec((1, q_lora), const),
        pl.BlockSpec((q_lora, heads_per_step * QK_NOPE), lambda i, j: (0, j)),
        pl.BlockSpec((q_lora, heads_per_step * QK_ROPE), lambda i, j: (0, j)),
        pl.BlockSpec((1, LANE), const),
        pl.BlockSpec((1, LANE), const),
        pl.BlockSpec((1, kv_lora), const),
        pl.BlockSpec((1, LANE), const),
    ]
    return pl.pallas_call(
        functools.partial(_latent_q_kernel, n_cq=n_cq, kv_lora=kv_lora, heads_per_step=heads_per_step),
        grid=(m // tm, n_heads // heads_per_step),
        in_specs=in_specs,
        out_specs=[
            pl.BlockSpec((tm, wstep), lambda i, j: (i, j)),
            pl.BlockSpec((tm, kv_lora), row),
            pl.BlockSpec((tm, QK_ROPE), row),
            pl.BlockSpec((tm, LANE), row),
        ],
        out_shape=[
            jax.ShapeDtypeStruct((m, n_heads * QK_PAD), BF16),
            jax.ShapeDtypeStruct((m, kv_lora), F32),
            jax.ShapeDtypeStruct((m, QK_ROPE), F32),
            jax.ShapeDtypeStruct((m, LANE), BF16),
        ],
        scratch_shapes=[pltpu.VMEM((tm, q_lora), BF16)],
        compiler_params=_cparams(("parallel", "arbitrary")),
        name="latent_q",
    )(*([p_bf] * n_cq), p_f32, cos, sin, qan, wqn, wqp, qng, qrg, kvg, krg)


def _kv_expand_kernel(ckv_ref, kpep_ref, wk_ref, wv_ref, kng_ref, k_ref, v_ref, *, n_heads):
    c = ckv_ref[...].astype(BF16)
    kf = _dot(c, wk_ref[...])
    v_ref[...] = _dot(c, wv_ref[...]).astype(BF16)
    kpep = kpep_ref[...]
    for h in range(n_heads):
        kn = _rms(kf[:, h * QK_NOPE:(h + 1) * QK_NOPE], QK_NOPE) * kng_ref[...]
        k_ref[:, h * QK_PAD:h * QK_PAD + QK_NOPE] = kn.astype(BF16)
        k_ref[:, h * QK_PAD + QK_NOPE:(h + 1) * QK_PAD] = kpep


def _kv_expand(ckv, kpep, wk, wv, kng, tm):
    r, kv_lora = ckv.shape
    n_heads = wk.shape[1] // QK_NOPE
    row = lambda i: (i, 0)
    const = lambda i: (0, 0)
    return pl.pallas_call(
        functools.partial(_kv_expand_kernel, n_heads=n_heads),
        grid=(r // tm,),
        in_specs=[
            pl.BlockSpec((tm, kv_lora), row),
            pl.BlockSpec((tm, LANE), row),
            pl.BlockSpec((kv_lora, n_heads * QK_NOPE), const),
            pl.BlockSpec((kv_lora, n_heads * V_HEAD), const),
            pl.BlockSpec((1, LANE), const),
        ],
        out_specs=[
            pl.BlockSpec((tm, n_heads * QK_PAD), row),
            pl.BlockSpec((tm, n_heads * V_HEAD), row),
        ],
        out_shape=[
            jax.ShapeDtypeStruct((r, n_heads * QK_PAD), BF16),
            jax.ShapeDtypeStruct((r, n_heads * V_HEAD), BF16),
        ],
        compiler_params=_cparams(("parallel",)),
        name="kv_expand",
    )(ckv, kpep, wk, wv, kng)


def _lane_chunk_sum(p):
    out = p[:, :LANE]
    for c in range(1, p.shape[1] // LANE):
        out = out + p[:, c * LANE:(c + 1) * LANE]
    return out


def _gated(lvec, acc, za):
    za = za.astype(F32)
    return acc / jnp.sum(lvec, axis=-1, keepdims=True) * (za * jax.nn.sigmoid(za))


def _softmax_scale():
    return float((QK_NOPE + QK_ROPE) ** -0.5 * 1.4426950408889634)


def _attn_prompt_block(n_full, q_ref, k_ref, km_ref, za_ref, o_ref, s_ref, p_ref, va_ref, vma_ref, *,
                       tq, heads, c, n_meta):
    td = min(tq, 256)
    pieces = [(i * tq, tq, 0, False) for i in range(n_full)]
    pieces += [(n_full * tq + j * td, td, j * td, True) for j in range(tq // td)]

    def upd(full, r0, fn):
        return fn(full) if r0 == 0 else jnp.concatenate([full[:r0], fn(full[r0:])], axis=0)

    def scores(h):
        qs = slice(h * QK_PAD, (h + 1) * QK_PAD)
        q = q_ref[:, qs]
        col = lax.broadcasted_iota(jnp.int32, (tq, LANE), 1)
        t = jnp.where(col < n_meta, _dot_nt(q, km_ref[:, qs]) * c, -jnp.inf)
        s_ref[h, :, :LANE] = t
        mx = t
        off = LANE
        for start, rows, r0, masked in pieces:
            t = _dot_nt(q[r0:], k_ref[start:start + rows, qs]) * c
            if masked:
                rq = lax.broadcasted_iota(jnp.int32, t.shape, 0) // CHUNK
                ck = lax.broadcasted_iota(jnp.int32, t.shape, 1) // CHUNK
                t = jnp.where(rq >= ck, t, -jnp.inf)
            s_ref[h, r0:, off:off + rows] = t
            tmax = t[:, :LANE]
            for ch in range(1, rows // LANE):
                tmax = jnp.maximum(tmax, t[:, ch * LANE:(ch + 1) * LANE])
            mx = upd(mx, r0, lambda a: jnp.maximum(a, tmax))
            off += rows
        return jnp.max(mx, axis=-1, keepdims=True)

    def values(h, m):
        vs = slice(h * V_HEAD, (h + 1) * V_HEAD)
        va = slice(2 * h * V_HEAD, 2 * (h + 1) * V_HEAD)
        p_ref[h, :, :LANE] = jnp.exp2(s_ref[h, :, :LANE] - m).astype(BF16)
        off = LANE
        for start, rows, r0, _ in pieces:
            p_ref[h, r0:, off:off + rows] = jnp.exp2(s_ref[h, r0:, off:off + rows] - m[r0:]).astype(BF16)
            off += rows
        acc = _dot(p_ref[h, :, :LANE], vma_ref[:, va])
        n_main = n_full * tq
        if n_main:
            acc = acc + _dot(p_ref[h, :, LANE:LANE + n_main], va_ref[:n_main, va])
        off = LANE + n_main
        for start, rows, r0, _ in pieces[n_full:]:
            pv = _dot(p_ref[h, r0:, off:off + rows], va_ref[start:start + rows, va])
            acc = upd(acc, r0, lambda a: a + pv)
            off += rows
        za = za_ref[:, vs].astype(F32)
        out = acc[:, :V_HEAD] / acc[:, V_HEAD:V_HEAD + 1] * (za * jax.nn.sigmoid(za))
        o_ref[:, vs] = out.astype(o_ref.dtype)

    maxes = [scores(h) for h in range(heads)]
    for h in range(heads):
        values(h, maxes[h])


def _attn_prompt_kernel(q_ref, k_ref, v_ref, km_ref, vm_ref, za_ref, o_ref, s_ref, p_ref, va_ref, vma_ref, *,
                        nqb, heads, **kw):
    qi = pl.program_id(2)

    @pl.when(qi == 0)
    def _():
        for h in range(heads):
            vs = slice(h * V_HEAD, (h + 1) * V_HEAD)
            va_ref[:, 2 * h * V_HEAD:(2 * h + 1) * V_HEAD] = v_ref[:, vs]
            va_ref[:, (2 * h + 1) * V_HEAD:(2 * h + 2) * V_HEAD] = jnp.ones((va_ref.shape[0], V_HEAD), BF16)
            vma_ref[:, 2 * h * V_HEAD:(2 * h + 1) * V_HEAD] = vm_ref[:, vs]
            vma_ref[:, (2 * h + 1) * V_HEAD:(2 * h + 2) * V_HEAD] = jnp.ones((vma_ref.shape[0], V_HEAD), BF16)

    for n_full in range(nqb):
        pl.when(qi == n_full)(
            functools.partial(_attn_prompt_block, n_full, q_ref, k_ref, km_ref, za_ref, o_ref, s_ref, p_ref, va_ref,
                              vma_ref, heads=heads, **kw))


def _attn_prompt(q, k, v, k_small, v_small, meta_blk, p_bf, za_off, batch, seq, tq, heads, n_meta):
    n_heads = q.shape[1] // QK_PAD
    nqb = seq // tq
    za0 = za_off // (V_HEAD * heads)
    assert za_off % (V_HEAD * heads) == 0 and n_heads % heads == 0 and tq % CHUNK == 0
    return pl.pallas_call(
        functools.partial(_attn_prompt_kernel, nqb=nqb, tq=tq, heads=heads, c=_softmax_scale(), n_meta=n_meta),
        grid=(batch, n_heads // heads, nqb),
        in_specs=[
            pl.BlockSpec((tq, heads * QK_PAD), lambda b, h, i: (b * nqb + i, h)),
            pl.BlockSpec((seq, heads * QK_PAD), lambda b, h, i: (b, h)),
            pl.BlockSpec((seq, heads * V_HEAD), lambda b, h, i: (b, h)),
            pl.BlockSpec((LANE, heads * QK_PAD), lambda b, h, i: (meta_blk, h)),
            pl.BlockSpec((LANE, heads * V_HEAD), lambda b, h, i: (meta_blk, h)),
            pl.BlockSpec((tq, heads * V_HEAD), lambda b, h, i: (b * nqb + i, za0 + h)),
        ],
        out_specs=pl.BlockSpec((tq, heads * V_HEAD), lambda b, h, i: (b * nqb + i, h)),
        out_shape=jax.ShapeDtypeStruct((batch * seq, n_heads * V_HEAD), BF16),
        scratch_shapes=[pltpu.VMEM((heads, tq, LANE + seq), F32), pltpu.VMEM((heads, tq, LANE + seq), BF16),
                        pltpu.VMEM((seq, 2 * heads * V_HEAD), BF16), pltpu.VMEM((LANE, 2 * heads * V_HEAD), BF16)],
        compiler_params=_cparams(("parallel", "parallel", "arbitrary")),
        name="attn_prompt",
    )(q, k, v, k_small, v_small, p_bf)


def _attn_sample_kernel(q_ref, kc_ref, vc_ref, ks_ref, vs_ref, za_ref, o_ref, *, heads, lq, c, n_meta, meta_row):
    b = pl.program_id(0)
    col = lax.broadcasted_iota(jnp.int32, (lq, ks_ref.shape[0]), 1)
    own = (col >= b * lq) & (col < (b + 1) * lq)
    meta = (col >= meta_row) & (col < meta_row + n_meta)
    visible = own | meta
    for h in range(heads):
        qs = slice(h * QK_PAD, (h + 1) * QK_PAD)
        vs = slice(h * V_HEAD, (h + 1) * V_HEAD)
        q = q_ref[:, qs]
        t_cache = _dot_nt(q, kc_ref[:, qs]) * c
        t_new = jnp.where(visible, _dot_nt(q, ks_ref[:, qs]) * c, -jnp.inf)
        m = jnp.maximum(jnp.max(t_cache, axis=-1, keepdims=True), jnp.max(t_new, axis=-1, keepdims=True))
        p_cache = jnp.exp2(t_cache - m)
        p_new = jnp.exp2(t_new - m)
        lvec = _lane_chunk_sum(p_cache) + _lane_chunk_sum(p_new)
        acc = _dot(p_cache.astype(BF16), vc_ref[:, vs]) + _dot(p_new.astype(BF16), vs_ref[:, vs])
        o_ref[:, vs] = _gated(lvec, acc, za_ref[:, vs]).astype(o_ref.dtype)


def _attn_sample(q, kc, vc, ks, vs, p_bf, za_off, batch, lq, past, n_meta, heads):
    n_heads = q.shape[1] // QK_PAD
    rows = ks.shape[0]
    za0 = za_off // (V_HEAD * heads)
    assert za_off % (V_HEAD * heads) == 0 and n_heads % heads == 0
    return pl.pallas_call(
        functools.partial(_attn_sample_kernel, heads=heads, lq=lq, c=_softmax_scale(), n_meta=n_meta,
                          meta_row=batch * lq),
        grid=(batch, n_heads // heads),
        in_specs=[
            pl.BlockSpec((lq, heads * QK_PAD), lambda b, h: (b, h)),
            pl.BlockSpec((past, heads * QK_PAD), lambda b, h: (b, h)),
            pl.BlockSpec((past, heads * V_HEAD), lambda b, h: (b, h)),
            pl.BlockSpec((rows, heads * QK_PAD), lambda b, h: (0, h)),
            pl.BlockSpec((rows, heads * V_HEAD), lambda b, h: (0, h)),
            pl.BlockSpec((lq, heads * V_HEAD), lambda b, h: (b, za0 + h)),
        ],
        out_specs=pl.BlockSpec((lq, heads * V_HEAD), lambda b, h: (b, h)),
        out_shape=jax.ShapeDtypeStruct((batch * lq, n_heads * V_HEAD), BF16),
        compiler_params=_cparams(("parallel", "arbitrary")),
        name="attn_sample",
    )(q, kc, vc, ks, vs, p_bf)


def _log_sigmoid(x):
    return jnp.minimum(x, 0.0) - jnp.log1p(jnp.exp(-jnp.abs(x)))


def _pad_rows(a, rows):
    if a.shape[0] == rows:
        return a
    return jnp.concatenate([a, jnp.zeros((rows - a.shape[0], a.shape[1]), a.dtype)], axis=0)


def _mlstm_kernel(q_ref, k_ref, v_ref, mo_ref, zb_ref, g_ref, gb_ref, gain_ref, c0_ref, n0_ref, m0_ref,
                  h_ref, c_ref, n_ref, m_ref, *, n_heads, lc, lp, dk, dv):
    @pl.when(pl.program_id(1) == 0)
    def _():
        c_ref[...] = c0_ref[...]
        n_ref[...] = n0_ref[...]
        m_ref[...] = m0_ref[...]

    hi = lax.Precision.HIGHEST
    gates = _pad_rows(g_ref[...] + gb_ref[...], lp)
    gates_t = gates.T
    t_col = lax.broadcasted_iota(jnp.int32, (lp, 1), 0)
    t_row = lax.broadcasted_iota(jnp.int32, (1, lp), 1)
    r_idx = lax.broadcasted_iota(jnp.int32, (lp, lp), 0)
    c_idx = lax.broadcasted_iota(jnp.int32, (lp, lp), 1)
    causal = r_idx >= c_idx
    lower = causal.astype(F32)
    upper = (c_idx >= r_idx).astype(F32)
    b_cols = jnp.dot(lower, jnp.where(t_col < lc, _log_sigmoid(gates), 0.0), precision=hi,
                     preferred_element_type=F32)
    b_rows = jnp.dot(jnp.where(t_row < lc, _log_sigmoid(gates_t), 0.0), upper, precision=hi,
                     preferred_element_type=F32)

    for h in range(n_heads):
        li = QK_ROPE + h
        lf_lane = QK_ROPE + n_heads + h
        ig_col = jnp.where(t_col < lc, gates[:, li:li + 1], NEG_BIG)
        ig_row = jnp.where(t_row < lc, gates_t[li:li + 1, :], NEG_BIG)
        b_col = b_cols[:, lf_lane:lf_lane + 1]
        b_row = b_rows[lf_lane:lf_lane + 1, :]
        m0 = m_ref[0, 0:1, h:h + 1]
        d = jnp.where(causal, b_col - b_row + ig_row, -jnp.inf)
        a_col = b_col + m0
        m = jnp.maximum(a_col, jnp.max(d, axis=-1, keepdims=True))
        w_inter = jnp.exp(a_col - m)

        q = _pad_rows(q_ref[:, h * dk:(h + 1) * dk], lp)
        k = _pad_rows(k_ref[:, h * dk:(h + 1) * dk], lp) * (dk ** -0.5)
        v = _pad_rows(v_ref[:, h * dv:(h + 1) * dv], lp)
        qk = _dot_nt(q, k) * jnp.exp(d - m)
        c_old = c_ref[0, h]
        n_old = n_ref[0, h:h + 1, :]
        num = w_inter * _dot(q, c_old.astype(BF16)) + _dot(qk.astype(BF16), v)
        den = (w_inter * jnp.sum(q.astype(F32) * n_old, axis=-1, keepdims=True)
               + jnp.sum(qk, axis=-1, keepdims=True))
        denc = jnp.maximum(jnp.abs(den), jnp.exp(-m))

        b_last = b_col[lp - 1:lp, :]
        g_row = b_last - b_row + ig_row
        g_col = b_last - b_col + ig_col
        m_new = jnp.maximum(b_last + m0, jnp.max(g_row, axis=-1, keepdims=True))
        decay = jnp.exp(b_last + m0 - m_new)
        kw = k.astype(F32) * jnp.exp(g_col - m_new)
        c_ref[0, h] = decay * c_old + _dot(kw.T.astype(BF16), v)
        n_ref[0, h:h + 1, :] = decay * n_old + jnp.sum(kw, axis=0, keepdims=True)
        m_ref[0, 0:1, h:h + 1] = m_new

        num = num[:lc]
        denc = denc[:lc]
        normed = num * lax.rsqrt(jnp.sum(num * num, axis=-1, keepdims=True) * (1.0 / dv) + EPS * denc * denc)
        mo = mo_ref[:, h * dv:(h + 1) * dv].astype(F32)
        zb = zb_ref[:, h * dv:(h + 1) * dv].astype(F32)
        out = jax.nn.sigmoid(mo) * (normed * gain_ref[:, h * dv:(h + 1) * dv])
        h_ref[:, h * dv:(h + 1) * dv] = (out * (zb * jax.nn.sigmoid(zb))).astype(h_ref.dtype)


def _mlstm(p_bf, p_f32, offs, gate_bias, gain, c0, n0, m0, batch, seq, lc, row0, share_state, n_heads, dk, dv):
    lp = -(-lc // LANE) * LANE
    nch = seq // lc
    blk0 = row0 // lc
    wqk = n_heads * dk
    wv = n_heads * dv
    gate_blk = p_f32.shape[1] // LANE - 1
    rows = lambda col: (lambda b, c: (blk0 + b * nch + c, col))
    state = (lambda b, c: (0, 0, 0, 0)) if share_state else (lambda b, c: (b, 0, 0, 0))
    state3 = (lambda b, c: (0, 0, 0)) if share_state else (lambda b, c: (b, 0, 0))
    out_rows = batch * seq
    return pl.pallas_call(
        functools.partial(_mlstm_kernel, n_heads=n_heads, lc=lc, lp=lp, dk=dk, dv=dv),
        grid=(batch, nch),
        in_specs=[
            pl.BlockSpec((lc, wqk), rows(offs["mq"] // wqk)),
            pl.BlockSpec((lc, wqk), rows(offs["mk"] // wqk)),
            pl.BlockSpec((lc, wv), rows(offs["mv"] // wv)),
            pl.BlockSpec((lc, wv), rows(offs["mo"] // wv)),
            pl.BlockSpec((lc, wv), rows(offs["zb"] // wv)),
            pl.BlockSpec((lc, LANE), rows(gate_blk)),
            pl.BlockSpec((1, LANE), lambda b, c: (0, 0)),
            pl.BlockSpec((1, wv), lambda b, c: (0, 0)),
            pl.BlockSpec((1, n_heads, dk, dv), state),
            pl.BlockSpec((1, n_heads, dk), state3),
            pl.BlockSpec((1, 1, n_heads), state3),
        ],
        out_specs=[
            pl.BlockSpec((lc, wv), lambda b, c: (b * nch + c, 0)),
            pl.BlockSpec((1, n_heads, dk, dv), lambda b, c: (b, 0, 0, 0)),
            pl.BlockSpec((1, n_heads, dk), lambda b, c: (b, 0, 0)),
            pl.BlockSpec((1, 1, n_heads), lambda b, c: (b, 0, 0)),
        ],
        out_shape=[
            jax.ShapeDtypeStruct((out_rows, wv), BF16),
            jax.ShapeDtypeStruct((batch, n_heads, dk, dv), F32),
            jax.ShapeDtypeStruct((batch, n_heads, dk), F32),
            jax.ShapeDtypeStruct((batch, 1, n_heads), F32),
        ],
        compiler_params=_cparams(("parallel", "arbitrary")),
        name="mlstm",
    )(p_bf, p_bf, p_bf, p_bf, p_bf, p_f32, gate_bias, gain, c0, n0, m0)


def _out_proj_kernel(a_ref, m_ref, wa_ref, wm_ref, x_ref, o_ref, *wb_refs):
    wa = wa_ref[...].astype(BF16)
    wm = wm_ref[...].astype(BF16)
    o_ref[...] = x_ref[...] + _dot(a_ref[...], wa) + _dot(m_ref[...], wm)
    if wb_refs:
        wb_refs[0][...] = wa
        wb_refs[1][...] = wm


def _out_proj(a, ml, w_a, w_m, wm_blk, x, tm, tn, emit_bf16=False):
    rows, wa = a.shape
    wm = ml.shape[1]
    d = w_a.shape[1]
    assert wa == wm and (not emit_bf16 or rows == tm)
    out_specs = [pl.BlockSpec((tm, tn), lambda i, j: (i, j))]
    out_shape = [jax.ShapeDtypeStruct((rows, d), F32)]
    if emit_bf16:
        out_specs += [pl.BlockSpec((wa, tn), lambda i, j: (0, j)), pl.BlockSpec((wm, tn), lambda i, j: (0, j))]
        out_shape += [jax.ShapeDtypeStruct((wa, d), BF16), jax.ShapeDtypeStruct((wm, d), BF16)]
    return pl.pallas_call(
        _out_proj_kernel,
        grid=(rows // tm, d // tn),
        in_specs=[
            pl.BlockSpec((tm, wa), lambda i, j: (i, 0)),
            pl.BlockSpec((tm, wm), lambda i, j: (i, 0)),
            pl.BlockSpec((wa, tn), lambda i, j: (0, j)),
            pl.BlockSpec((wm, tn), lambda i, j: (wm_blk, j)),
            pl.BlockSpec((tm, tn), lambda i, j: (i, j)),
        ],
        out_specs=out_specs,
        out_shape=out_shape,
        compiler_params=_cparams(("parallel", "arbitrary")),
        name="out_proj",
    )(a, ml, w_a, w_m, x)


def _rope_tables(pos):
    inv_freq = ROPE_THETA ** (-jnp.arange(HALF_ROPE, dtype=F32) / HALF_ROPE)
    ang = pos.astype(F32)[:, None] * inv_freq[None, :]
    cos, sin = jnp.cos(ang), jnp.sin(ang)
    return jnp.concatenate([cos, cos, cos, cos], axis=1), jnp.concatenate([-sin, sin, -sin, sin], axis=1)


def _pad_lanes(vec):
    return jnp.pad(vec.astype(F32), (0, LANE - vec.shape[0]))[None, :]


def _row_tile(rows, target):
    t = min(rows, target)
    while rows % t:
        t //= 2
    return t


def kernel(x_prompt, x_sample, cache_ckv, cache_kpe, state_C, state_n, state_m, meta_tokens, norm_gain, w_in,
           b_igate, b_fgate, q_a_norm, w_q_up, q_nope_norm, q_rope_norm, kv_a_norm, k_rope_norm, w_kv_up,
           k_nope_norm, ml_out_norm, w_out):
    batch, seq, d = x_prompt.shape
    dbatch, dseq, _ = x_sample.shape
    past = cache_ckv.shape[1]
    n_meta = meta_tokens.shape[0]
    q_lora = q_a_norm.shape[0]
    kv_lora = kv_a_norm.shape[0]
    ml_heads, dv = ml_out_norm.shape
    dk = state_n.shape[-1]
    mla_heads = w_kv_up.shape[1] // (QK_NOPE + V_HEAD)
    mla_w = mla_heads * V_HEAD
    ml_w = ml_heads * dv
    mqk_w = ml_heads * dk
    assert q_lora % COL_TILE == 0 and kv_lora % LANE == 0 and 2 * ml_heads <= LANE - QK_ROPE
    assert mla_w == ml_w and w_out.shape[0] == mla_w + ml_w

    o_cq = 0
    o_ckv = o_cq + q_lora
    o_kpe = o_ckv + kv_lora
    o_mq = o_kpe + QK_ROPE
    o_mk = o_mq + mqk_w
    o_mv = o_mk + mqk_w
    o_mo = o_mv + ml_w
    o_mi = o_mo + ml_w
    o_mf = o_mi + ml_heads
    o_za = o_mf + ml_heads
    o_zb = o_za + mla_w

    offs = {"mq": 0, "mk": mqk_w, "mv": 2 * mqk_w, "mo": 2 * mqk_w + ml_w, "za": 2 * mqk_w + 2 * ml_w,
            "zb": 2 * mqk_w + 2 * ml_w + mla_w, "cq": 2 * mqk_w + 3 * ml_w + mla_w}
    gate_pad = LANE - QK_ROPE - 2 * ml_heads
    w_in_t = w_in.T
    main_segments = [(o_mq, mqk_w, offs["mq"]), (o_mk, mqk_w, offs["mk"]), (o_mv, ml_w, offs["mv"]),
                     (o_mo, ml_w, offs["mo"]), (o_za, mla_w, offs["za"]), (o_zb, ml_w, offs["zb"]),
                     (o_cq, q_lora, offs["cq"])]
    w_small = _stack_rows(w_in_t, [(o_ckv, kv_lora), (o_kpe, QK_ROPE), (o_mi, 2 * ml_heads)], kv_lora + LANE,
                          _row_tile(d, 512))
    gate_bias = jnp.concatenate([jnp.zeros((QK_ROPE,), F32), b_igate.astype(F32), b_fgate.astype(F32),
                                 jnp.zeros((gate_pad,), F32)])[None, :]
    qk_w = QK_NOPE + QK_ROPE
    kv_w = QK_NOPE + V_HEAD
    wqn, wqp = _regroup(w_q_up, [[(h * qk_w, QK_NOPE, h * QK_NOPE) for h in range(mla_heads)],
                                 [(h * qk_w + QK_NOPE, QK_ROPE, h * QK_ROPE) for h in range(mla_heads)]],
                        _row_tile(q_lora, 512))
    wk, wv = _regroup(w_kv_up, [[(h * kv_w, QK_NOPE, h * QK_NOPE) for h in range(mla_heads)],
                                [(h * kv_w + QK_NOPE, V_HEAD, h * V_HEAD) for h in range(mla_heads)]], kv_lora)
    gain_row = norm_gain.astype(F32)[None, :]
    qan = q_a_norm.astype(F32)[None, :]
    kvg = kv_a_norm.astype(F32)[None, :]
    qng = q_nope_norm.astype(F32)[None, :]
    kng = k_nope_norm.astype(F32)[None, :]
    qrg = jnp.tile(q_rope_norm.astype(F32), LANE // QK_ROPE)[None, :]
    krg = _pad_lanes(k_rope_norm)
    ml_gain = ml_out_norm.astype(F32).reshape(1, ml_w)

    n_p = batch * seq
    n_s = dbatch * dseq
    assert n_s % LANE == 0 and n_meta <= LANE and n_meta % 16 == 0
    rows_s = n_s + n_meta
    xp = x_prompt.reshape(n_p, d)
    xs = jnp.concatenate([x_sample.reshape(n_s, d), meta_tokens.astype(x_sample.dtype)], axis=0)

    pos_s = jnp.concatenate([jnp.tile(n_meta + past + jnp.arange(dseq), dbatch), jnp.arange(n_meta)])
    cos_p, sin_p = _rope_tables(n_meta + jnp.arange(seq))
    cos_s, sin_s = _rope_tables(pos_s)

    tm_p = _row_tile(seq, 512)
    proj_sub = 2 if n_p % (2 * tm_p) == 0 else 1
    w_main, pbf_s, pf_s = _proj_regroup(xs, gain_row, w_in_t, main_segments, w_small)
    pbf_p, pf_p = _proj(xp, gain_row, w_main, w_small, proj_sub * tm_p, proj_sub)

    hps = max(2, mla_heads // 2)
    q_p, ckv_p, kpe_p, kpep_p = _latent_q(pbf_p, pf_p, offs["cq"], q_lora, kv_lora, cos_p, sin_p, qan, wqn, wqp,
                                          qng, qrg, kvg, krg, tm_p, hps)
    q_s, ckv_s, kpe_s, kpep_s = _latent_q(pbf_s, pf_s, offs["cq"], q_lora, kv_lora, cos_s, sin_s, qan, wqn, wqp,
                                          qng, qrg, kvg, krg, rows_s, hps)

    small_rows = n_s + LANE
    meta_blk = n_s // LANE
    tail = ((0, small_rows - rows_s), (0, 0))
    k_p, v_p = _kv_expand(ckv_p, kpep_p, wk, wv, kng, tm_p)
    k_s, v_s = _kv_expand(jnp.pad(ckv_s, tail), jnp.pad(kpep_s, tail), wk, wv, kng, small_rows)
    cache_rows = dbatch * past
    cache_kpep = jnp.pad(cache_kpe.reshape(cache_rows, QK_ROPE), ((0, 0), (0, LANE - QK_ROPE))).astype(BF16)
    k_c, v_c = _kv_expand(cache_ckv.reshape(cache_rows, kv_lora).astype(F32), cache_kpep, wk, wv, kng,
                          _row_tile(cache_rows, 512))

    tq = _row_tile(seq, ATTN_TQ)
    attn_p = _attn_prompt(q_p, k_p, v_p, k_s, v_s, meta_blk, pbf_p, offs["za"], batch, seq, tq,
                          min(ATTN_HEADS, mla_heads), n_meta)
    attn_s = _attn_sample(q_s, k_c, v_c, k_s, v_s, pbf_s, offs["za"], dbatch, dseq, past, n_meta,
                          min(4, mla_heads))

    zc = jnp.zeros((1, ml_heads, dk, dv), F32)
    zn = jnp.zeros((1, ml_heads, dk), F32)
    zm = jnp.zeros((1, 1, ml_heads), F32)
    _, c_m, n_m, m_m = _mlstm(pbf_s, pf_s, offs, gate_bias, ml_gain, zc, zn, zm, 1, n_meta, n_meta, n_s, True,
                              ml_heads, dk, dv)
    lc_p = _row_tile(seq, 256)
    ml_p, c_p, n_pr, m_p = _mlstm(pbf_p, pf_p, offs, gate_bias, ml_gain, c_m, n_m, m_m, batch, seq, lc_p, 0,
                                  True, ml_heads, dk, dv)
    ml_s, c_s, n_sm, m_s = _mlstm(pbf_s, pf_s, offs, gate_bias, ml_gain, state_C.astype(F32),
                                  state_n.astype(F32), state_m.astype(F32).reshape(dbatch, 1, ml_heads),
                                  dbatch, dseq, dseq, 0, False, ml_heads, dk, dv)

    y_s, w_oa, w_om = _out_proj(attn_s, ml_s, w_out, w_out, 1, xs, n_s, _row_tile(d, 512), emit_bf16=True)
    y_p, = _out_proj(attn_p, ml_p, w_oa, w_om, 0, xp, _row_tile(n_p, 1024), _row_tile(d, 512))

    meta_ckv = ckv_s[n_s:n_s + n_meta]
    meta_kpe = kpe_s[n_s:n_s + n_meta]
    ckv_prompt = jnp.concatenate([jnp.broadcast_to(meta_ckv[None], (batch, n_meta, kv_lora)),
                                  ckv_p.reshape(batch, seq, kv_lora)], axis=1)
    kpe_prompt = jnp.concatenate([jnp.broadcast_to(meta_kpe[None], (batch, n_meta, QK_ROPE)),
                                  kpe_p.reshape(batch, seq, QK_ROPE)], axis=1)
    return (y_p.reshape(batch, seq, d), y_s.reshape(dbatch, dseq, d), ckv_prompt, kpe_prompt,
            c_p, n_pr, m_p.reshape(batch, ml_heads),
            ckv_s[:n_s].reshape(dbatch, dseq, kv_lora), kpe_s[:n_s].reshape(dbatch, dseq, QK_ROPE),
            c_s, n_sm, m_s.reshape(dbatch, ml_heads))
```

```python
import functools
import math

import jax
import jax.numpy as jnp
from jax import lax
from jax.experimental import pallas as pl
from jax.experimental.pallas import tpu as pltpu

CHUNK = 64
EPS = 1e-6
ROPE_THETA = 10000.0
V_HEAD = 128
QK_NOPE = 128
QK_ROPE = 64
HALF_ROPE = QK_ROPE // 2
LANE = 128
MXU_EDGE = 256
QK_PAD = QK_NOPE + LANE
COL_TILE = 512
ROW_TILE = 512
NEG_BIG = -1e30
ATTN_TQ = 512
ATTN_HEADS = 4
MLSTM_CHUNK = 256
VMEM_LIMIT = 56 * 1024 * 1024

F32 = jnp.float32
BF16 = jnp.bfloat16


def _cparams(sem):
    return pltpu.CompilerParams(dimension_semantics=sem, vmem_limit_bytes=VMEM_LIMIT)


def _dot(a, b):
    return jnp.dot(a, b, preferred_element_type=F32)


def _dot_nt(a, b):
    return lax.dot_general(a, b, (((1,), (1,)), ((), ())), preferred_element_type=F32)


def _rms(x, n):
    return x * lax.rsqrt(jnp.sum(x * x, axis=-1, keepdims=True) * (1.0 / n) + EPS)


def _rope_lanes(x, cos, sin_signed):
    lane = lax.broadcasted_iota(jnp.int32, x.shape, 1)
    first_half = (lane & (QK_ROPE - 1)) < HALF_ROPE
    rot = jnp.where(first_half, pltpu.roll(x, LANE - HALF_ROPE, 1), pltpu.roll(x, HALF_ROPE, 1))
    return x * cos + rot * sin_signed


def _regroup_kernel(w_ref, *o_refs, plans):
    for o_ref, segments in zip(o_refs, plans):
        for src, n, dst in segments:
            o_ref[:, dst:dst + n] = w_ref[:, src:src + n].astype(o_ref.dtype)


def _regroup(w, plans, tr):
    rows, cols = w.shape
    widths = [sum(n for _, n, _ in segments) for segments in plans]
    for segments in plans:
        pos = 0
        for _, n, dst in sorted(segments, key=lambda s: s[2]):
            assert dst == pos
            pos += n
    return pl.pallas_call(
        functools.partial(_regroup_kernel, plans=tuple(tuple(s) for s in plans)),
        grid=(rows // tr,),
        in_specs=[pl.BlockSpec((tr, cols), lambda i: (i, 0))],
        out_specs=[pl.BlockSpec((tr, width), lambda i: (i, 0)) for width in widths],
        out_shape=[jax.ShapeDtypeStruct((rows, width), BF16) for width in widths],
        compiler_params=_cparams(("parallel",)),
        name="regroup",
    )(w)


def _stack_rows_kernel(*refs):
    *w_refs, o_ref = refs
    parts = [r[...] for r in w_refs]
    used = sum(p.shape[0] for p in parts)
    parts.append(jnp.zeros((o_ref.shape[0] - used, o_ref.shape[1]), parts[0].dtype))
    o_ref[...] = jnp.concatenate(parts, axis=0).astype(o_ref.dtype)


def _stack_rows(wt, segments, out_rows, tc):
    cols = wt.shape[1]
    return pl.pallas_call(
        _stack_rows_kernel,
        grid=(cols // tc,),
        in_specs=[pl.BlockSpec((pl.Element(n), pl.Element(tc)), functools.partial(lambda i, s: (s, i * tc), s=src))
                  for src, n in segments],
        out_specs=pl.BlockSpec((out_rows, tc), lambda i: (0, i)),
        out_shape=jax.ShapeDtypeStruct((out_rows, cols), BF16),
        compiler_params=_cparams(("parallel",)),
        name="stack_rows",
    )(*([wt] * len(segments)))


def _proj_regroup_kernel(x_ref, g_ref, w_ref, ws_ref, om_ref, o_ref, os_ref, hn_ref):
    @pl.when(pl.program_id(0) == 0)
    def _():
        x = x_ref[...]
        hn = (_rms(x, x.shape[-1]) * g_ref[...]).astype(BF16)
        hn_ref[...] = hn
        os_ref[...] = _dot_nt(hn, ws_ref[...])

    wb = w_ref[...].astype(BF16)
    om_ref[...] = wb
    o_ref[...] = _dot_nt(hn_ref[...], wb).astype(o_ref.dtype)


def _proj_regroup(x, gain, wt, segments, w_small):
    m, d = x.shape
    ns = w_small.shape[0]
    n = sum(sz for _, sz, _ in segments)
    merged = []
    for src, sz, dst in sorted(segments, key=lambda s: s[2]):
        if merged and merged[-1][0] + merged[-1][1] == src and merged[-1][2] + merged[-1][1] == dst:
            merged[-1] = (merged[-1][0], merged[-1][1] + sz, merged[-1][2])
        else:
            merged.append((src, sz, dst))
    segments = merged
    assert all(sz % COL_TILE == 0 and dst % COL_TILE == 0 for _, sz, dst in segments)
    align = functools.reduce(math.gcd, [src for src, _, _ in segments], COL_TILE)

    def src_row(j):
        r = j * COL_TILE
        out = 0
        for src, sz, dst in segments:
            out = jnp.where((r >= dst) & (r < dst + sz), src + r - dst, out)
        return out

    const = lambda j: (0, 0)
    return pl.pallas_call(
        _proj_regroup_kernel,
        grid=(n // COL_TILE,),
        in_specs=[
            pl.BlockSpec((m, d), const),
            pl.BlockSpec((1, d), const),
            pl.BlockSpec((pl.Element(COL_TILE), pl.Element(d)), lambda j: (pl.multiple_of(src_row(j), align), 0)),
            pl.BlockSpec((ns, d), const),
        ],
        out_specs=[
            pl.BlockSpec((COL_TILE, d), lambda j: (j, 0)),
            pl.BlockSpec((m, COL_TILE), lambda j: (0, j)),
            pl.BlockSpec((m, ns), const),
        ],
        out_shape=[jax.ShapeDtypeStruct((n, d), BF16), jax.ShapeDtypeStruct((m, n), BF16),
                   jax.ShapeDtypeStruct((m, ns), F32)],
        scratch_shapes=[pltpu.VMEM((m, d), BF16)],
        compiler_params=_cparams(("arbitrary",)),
        name="proj_regroup",
    )(x, gain, wt, w_small)


def _proj_kernel(x_ref, g_ref, w_ref, ws_ref, o_ref, os_ref, hn_ref, *, n_sub):
    j = pl.program_id(1)
    ts = x_ref.shape[0]

    @pl.when(j < n_sub)
    def _():
        x = x_ref[...]
        hn = (_rms(x, x.shape[-1]) * g_ref[...]).astype(BF16)
        hn_ref[pl.ds(pl.multiple_of(j * ts, ts), ts), :] = hn
        os_ref[...] = _dot_nt(hn, ws_ref[...])

    @pl.when(j >= n_sub)
    def _():
        o_ref[...] = _dot_nt(hn_ref[...], w_ref[...]).astype(o_ref.dtype)


def _proj(x, gain, w_main, w_small, tm, n_sub):
    m, d = x.shape
    n = w_main.shape[0]
    ns = w_small.shape[0]
    ts = tm // n_sub
    sub = lambda i, j: (i * n_sub + jnp.minimum(j, n_sub - 1), 0)
    feat = lambda j: jnp.maximum(j - n_sub, 0)
    return pl.pallas_call(
        functools.partial(_proj_kernel, n_sub=n_sub),
        grid=(m // tm, n_sub + n // COL_TILE),
        in_specs=[
            pl.BlockSpec((ts, d), sub),
            pl.BlockSpec((1, d), lambda i, j: (0, 0)),
            pl.BlockSpec((COL_TILE, d), lambda i, j: (feat(j), 0)),
            pl.BlockSpec((ns, d), lambda i, j: (0, 0)),
        ],
        out_specs=[
            pl.BlockSpec((tm, COL_TILE), lambda i, j: (i, feat(j))),
            pl.BlockSpec((ts, ns), sub),
        ],
        out_shape=[jax.ShapeDtypeStruct((m, n), BF16), jax.ShapeDtypeStruct((m, ns), F32)],
        scratch_shapes=[pltpu.VMEM((tm, d), BF16)],
        compiler_params=_cparams(("parallel", "arbitrary")),
        name="proj",
    )(x, gain, w_main, w_small)


def _latent_q_kernel(*refs, n_cq, kv_lora, heads_per_step):
    cq_refs = refs[:n_cq]
    (pf_ref, cos_ref, sin_ref, qan_ref, wqn_ref, wqp_ref, qng_ref, qrg_ref, kvg_ref, krg_ref,
     q_ref, ckv_ref, kpe_ref, kpep_ref, cqn_ref) = refs[n_cq:]
    cos = cos_ref[...]
    sin = sin_ref[...]
    lane = lax.broadcasted_iota(jnp.int32, cos.shape, 1)
    low = lane < QK_ROPE

    @pl.when(pl.program_id(1) == 0)
    def _():
        cq = jnp.concatenate([r[...].astype(F32) for r in cq_refs], axis=1)
        cqn_ref[...] = (_rms(cq, cq.shape[-1]) * qan_ref[...]).astype(BF16)
        pf = pf_ref[...]
        ckv_raw = pf[:, :kv_lora]
        ckv_ref[...] = _rms(ckv_raw, kv_lora) * kvg_ref[...]
        kraw = jnp.where(low, pf[:, kv_lora:kv_lora + LANE], 0.0)
        kpe = _rope_lanes(_rms(kraw, QK_ROPE) * krg_ref[...], cos, sin)
        kpe_ref[...] = kpe[:, :QK_ROPE]
        kpep_ref[...] = kpe.astype(BF16)

    cqn = cqn_ref[...]
    group = 4 if heads_per_step % 4 == 0 else 2
    for g0 in range(0, heads_per_step, group):
        qn = _dot(cqn, wqn_ref[:, g0 * QK_NOPE:(g0 + group) * QK_NOPE])
        qp = _dot(cqn, wqp_ref[:, g0 * QK_ROPE:(g0 + group) * QK_ROPE])
        for pair in range(group // 2):
            x = qp[:, pair * LANE:(pair + 1) * LANE]
            x2 = x * x
            ss_a = jnp.sum(jnp.where(low, x2, 0.0), axis=-1, keepdims=True)
            ss_b = jnp.sum(jnp.where(low, 0.0, x2), axis=-1, keepdims=True)
            scale = jnp.where(low, lax.rsqrt(ss_a * (1.0 / QK_ROPE) + EPS),
                              lax.rsqrt(ss_b * (1.0 / QK_ROPE) + EPS))
            r = _rope_lanes(x * scale * qrg_ref[...], cos, sin)
            halves = (jnp.where(low, r, 0.0), jnp.where(low, pltpu.roll(r, QK_ROPE, 1), 0.0))
            for h in (0, 1):
                base = (g0 + 2 * pair + h) * QK_PAD
                nope = qn[:, (2 * pair + h) * QK_NOPE:(2 * pair + h + 1) * QK_NOPE]
                q_ref[:, base:base + QK_NOPE] = (_rms(nope, QK_NOPE) * qng_ref[...]).astype(BF16)
                q_ref[:, base + QK_NOPE:base + QK_PAD] = halves[h].astype(BF16)


def _latent_q(p_bf, p_f32, cq_off, q_lora, kv_lora, cos, sin, qan, wqn, wqp, qng, qrg, kvg, krg, tm,
              heads_per_step):
    m = p_bf.shape[0]
    n_heads = wqn.shape[1] // QK_NOPE
    n_cq = q_lora // COL_TILE
    cq0 = cq_off // COL_TILE
    ns = p_f32.shape[1]
    wstep = heads_per_step * QK_PAD
    assert heads_per_step % 2 == 0
    row = lambda i, j: (i, 0)
    const = lambda i, j: (0, 0)
    pos_blocks = cos.shape[0] // tm
    pos = lambda i, j: (i % pos_blocks, 0)
    in_specs = [pl.BlockSpec((tm, COL_TILE), functools.partial(lambda i, j, c: (i, c), c=cq0 + c))
                for c in range(n_cq)]
    in_specs += [
        pl.BlockSpec((tm, ns), row),
        pl.BlockSpec((tm, LANE), pos),
        pl.BlockSpec((tm, LANE), pos),
        pl.BlockSpec((1, q_lora), const),
        pl.BlockSpec((q_lora, heads_per_step * QK_NOPE), lambda i, j: (0, j)),
        pl.BlockSpec((q_lora, heads_per_step * QK_ROPE), lambda i, j: (0, j)),
        pl.BlockSpec((1, LANE), const),
        pl.BlockSpec((1, LANE), const),
        pl.BlockSpec((1, kv_lora), const),
        pl.BlockSpec((1, LANE), const),
    ]
    return pl.pallas_call(
        functools.partial(_latent_q_kernel, n_cq=n_cq, kv_lora=kv_lora, heads_per_step=heads_per_step),
        grid=(m // tm, n_heads // heads_per_step),
        in_specs=in_specs,
        out_specs=[
            pl.BlockSpec((tm, wstep), lambda i, j: (i, j)),
            pl.BlockSpec((tm, kv_lora), row),
            pl.BlockSpec((tm, QK_ROPE), row),
            pl.BlockSpec((tm, LANE), row),
        ],
        out_shape=[
            jax.ShapeDtypeStruct((m, n_heads * QK_PAD), BF16),
            jax.ShapeDtypeStruct((m, kv_lora), F32),
            jax.ShapeDtypeStruct((m, QK_ROPE), F32),
            jax.ShapeDtypeStruct((m, LANE), BF16),
        ],
        scratch_shapes=[pltpu.VMEM((tm, q_lora), BF16)],
        compiler_params=_cparams(("parallel", "arbitrary")),
        name="latent_q",
    )(*([p_bf] * n_cq), p_f32, cos, sin, qan, wqn, wqp, qng, qrg, kvg, krg)


def _kv_expand_kernel(ckv_ref, kpep_ref, wk_ref, wv_ref, kng_ref, k_ref, *v_refs, n_heads):
    c = ckv_ref[...].astype(BF16)
    kf = _dot(c, wk_ref[...])
    if v_refs:
        v_refs[0][...] = _dot(c, wv_ref[...]).astype(BF16)
    kpep = kpep_ref[...]
    for h in range(n_heads):
        kn = _rms(kf[:, h * QK_NOPE:(h + 1) * QK_NOPE], QK_NOPE) * kng_ref[...]
        k_ref[:, h * QK_PAD:h * QK_PAD + QK_NOPE] = kn.astype(BF16)
        k_ref[:, h * QK_PAD + QK_NOPE:(h + 1) * QK_PAD] = kpep


def _kv_expand(ckv, kpep, wk, wv, kng, tm, with_values=True):
    r, kv_lora = ckv.shape
    n_heads = wk.shape[1] // QK_NOPE
    row = lambda i: (i, 0)
    const = lambda i: (0, 0)
    n_out = 2 if with_values else 1
    return pl.pallas_call(
        functools.partial(_kv_expand_kernel, n_heads=n_heads),
        grid=(r // tm,),
        in_specs=[
            pl.BlockSpec((tm, kv_lora), row),
            pl.BlockSpec((tm, LANE), row),
            pl.BlockSpec((kv_lora, n_heads * QK_NOPE), const),
            pl.BlockSpec((kv_lora, n_heads * V_HEAD), const),
            pl.BlockSpec((1, LANE), const),
        ],
        out_specs=[
            pl.BlockSpec((tm, n_heads * QK_PAD), row),
            pl.BlockSpec((tm, n_heads * V_HEAD), row),
        ][:n_out],
        out_shape=[
            jax.ShapeDtypeStruct((r, n_heads * QK_PAD), BF16),
            jax.ShapeDtypeStruct((r, n_heads * V_HEAD), BF16),
        ][:n_out],
        compiler_params=_cparams(("parallel",)),
        name="kv_expand",
    )(ckv, kpep, wk, wv, kng)


def _lane_chunk_sum(p):
    out = p[:, :LANE]
    for c in range(1, p.shape[1] // LANE):
        out = out + p[:, c * LANE:(c + 1) * LANE]
    return out


def _gated(lvec, acc, za):
    za = za.astype(F32)
    return acc / jnp.sum(lvec, axis=-1, keepdims=True) * (za * jax.nn.sigmoid(za))


def _softmax_scale():
    return float((QK_NOPE + QK_ROPE) ** -0.5 * 1.4426950408889634)


def _attn_prompt_block(n_full, q_ref, k_ref, km_ref, za_ref, o_ref, s_ref, p_ref, va_ref, vma_ref, *,
                       tq, heads, c, n_meta):
    td = min(tq, MXU_EDGE)
    pieces = [(i * tq, tq, 0, False) for i in range(n_full)]
    pieces += [(n_full * tq + j * td, td, j * td, True) for j in range(tq // td)]

    def upd(full, r0, fn):
        return fn(full) if r0 == 0 else jnp.concatenate([full[:r0], fn(full[r0:])], axis=0)

    def scores(h):
        qs = slice(h * QK_PAD, (h + 1) * QK_PAD)
        q = q_ref[:, qs]
        col = lax.broadcasted_iota(jnp.int32, (tq, LANE), 1)
        t = jnp.where(col < n_meta, _dot_nt(q, km_ref[:, qs]) * c, -jnp.inf)
        s_ref[h, :, :LANE] = t
        mx = t
        off = LANE
        for start, rows, r0, masked in pieces:
            t = _dot_nt(q[r0:], k_ref[start:start + rows, qs]) * c
            if masked:
                rq = lax.broadcasted_iota(jnp.int32, t.shape, 0) // CHUNK
                ck = lax.broadcasted_iota(jnp.int32, t.shape, 1) // CHUNK
                t = jnp.where(rq >= ck, t, -jnp.inf)
            s_ref[h, r0:, off:off + rows] = t
            tmax = t[:, :LANE]
            for ch in range(1, rows // LANE):
                tmax = jnp.maximum(tmax, t[:, ch * LANE:(ch + 1) * LANE])
            mx = upd(mx, r0, lambda a: jnp.maximum(a, tmax))
            off += rows
        return jnp.max(mx, axis=-1, keepdims=True)

    def values(h, m):
        vs = slice(h * V_HEAD, (h + 1) * V_HEAD)
        va = slice(2 * h * V_HEAD, 2 * (h + 1) * V_HEAD)
        p_ref[h, :, :LANE] = jnp.exp2(s_ref[h, :, :LANE] - m).astype(BF16)
        off = LANE
        for start, rows, r0, _ in pieces:
            p_ref[h, r0:, off:off + rows] = jnp.exp2(s_ref[h, r0:, off:off + rows] - m[r0:]).astype(BF16)
            off += rows
        acc = _dot(p_ref[h, :, :LANE], vma_ref[:, va])
        n_main = n_full * tq
        if n_main:
            acc = acc + _dot(p_ref[h, :, LANE:LANE + n_main], va_ref[:n_main, va])
        off = LANE + n_main
        for start, rows, r0, _ in pieces[n_full:]:
            pv = _dot(p_ref[h, r0:, off:off + rows], va_ref[start:start + rows, va])
            acc = upd(acc, r0, lambda a: a + pv)
            off += rows
        za = za_ref[:, vs].astype(F32)
        out = acc[:, :V_HEAD] / acc[:, V_HEAD:V_HEAD + 1] * (za * jax.nn.sigmoid(za))
        o_ref[:, vs] = out.astype(o_ref.dtype)

    maxes = [scores(h) for h in range(heads)]
    for h in range(heads):
        values(h, maxes[h])


def _attn_prompt_kernel(q_ref, k_ref, v_ref, km_ref, vm_ref, za_ref, o_ref, s_ref, p_ref, va_ref, vma_ref, *,
                        nqb, heads, **kw):
    qi = pl.program_id(2)

    @pl.when(qi == 0)
    def _():
        for h in range(heads):
            vs = slice(h * V_HEAD, (h + 1) * V_HEAD)
            va_ref[:, 2 * h * V_HEAD:(2 * h + 1) * V_HEAD] = v_ref[:, vs]
            va_ref[:, (2 * h + 1) * V_HEAD:(2 * h + 2) * V_HEAD] = jnp.ones((va_ref.shape[0], V_HEAD), BF16)
            vma_ref[:, 2 * h * V_HEAD:(2 * h + 1) * V_HEAD] = vm_ref[:, vs]
            vma_ref[:, (2 * h + 1) * V_HEAD:(2 * h + 2) * V_HEAD] = jnp.ones((vma_ref.shape[0], V_HEAD), BF16)

    for n_full in range(nqb):
        pl.when(qi == n_full)(
            functools.partial(_attn_prompt_block, n_full, q_ref, k_ref, km_ref, za_ref, o_ref, s_ref, p_ref, va_ref,
                              vma_ref, heads=heads, **kw))


def _attn_prompt(q, k, v, k_small, v_small, meta_blk, p_bf, za_off, batch, seq, tq, heads, n_meta):
    n_heads = q.shape[1] // QK_PAD
    nqb = seq // tq
    za0 = za_off // (V_HEAD * heads)
    assert za_off % (V_HEAD * heads) == 0 and n_heads % heads == 0 and tq % CHUNK == 0
    return pl.pallas_call(
        functools.partial(_attn_prompt_kernel, nqb=nqb, tq=tq, heads=heads, c=_softmax_scale(), n_meta=n_meta),
        grid=(batch, n_heads // heads, nqb),
        in_specs=[
            pl.BlockSpec((tq, heads * QK_PAD), lambda b, h, i: (b * nqb + i, h)),
            pl.BlockSpec((seq, heads * QK_PAD), lambda b, h, i: (b, h)),
            pl.BlockSpec((seq, heads * V_HEAD), lambda b, h, i: (b, h)),
            pl.BlockSpec((LANE, heads * QK_PAD), lambda b, h, i: (meta_blk, h)),
            pl.BlockSpec((LANE, heads * V_HEAD), lambda b, h, i: (meta_blk, h)),
            pl.BlockSpec((tq, heads * V_HEAD), lambda b, h, i: (b * nqb + i, za0 + h)),
        ],
        out_specs=pl.BlockSpec((tq, heads * V_HEAD), lambda b, h, i: (b * nqb + i, h)),
        out_shape=jax.ShapeDtypeStruct((batch * seq, n_heads * V_HEAD), BF16),
        scratch_shapes=[pltpu.VMEM((heads, tq, LANE + seq), F32), pltpu.VMEM((heads, tq, LANE + seq), BF16),
                        pltpu.VMEM((seq, 2 * heads * V_HEAD), BF16), pltpu.VMEM((LANE, 2 * heads * V_HEAD), BF16)],
        compiler_params=_cparams(("parallel", "parallel", "arbitrary")),
        name="attn_prompt",
    )(q, k, v, k_small, v_small, p_bf)


def _attn_sample_kernel(q_ref, kc_ref, cc_ref, ks_ref, cs_ref, wv_ref, za_ref, o_ref, pc_ref, ps_ref, li_ref, *,
                        heads, lq, c, n_meta, meta_row):
    b = pl.program_id(0)
    col = lax.broadcasted_iota(jnp.int32, (lq, ks_ref.shape[0]), 1)
    own = (col >= b * lq) & (col < (b + 1) * lq)
    meta = (col >= meta_row) & (col < meta_row + n_meta)
    visible = own | meta
    for h in range(heads):
        qs = slice(h * QK_PAD, (h + 1) * QK_PAD)
        rows = slice(h * lq, (h + 1) * lq)
        q = q_ref[:, qs]
        t_cache = _dot_nt(q, kc_ref[:, qs]) * c
        t_new = jnp.where(visible, _dot_nt(q, ks_ref[:, qs]) * c, -jnp.inf)
        m = jnp.maximum(jnp.max(t_cache, axis=-1, keepdims=True), jnp.max(t_new, axis=-1, keepdims=True))
        p_cache = jnp.exp2(t_cache - m)
        p_new = jnp.exp2(t_new - m)
        lsum = jnp.sum(_lane_chunk_sum(p_cache) + _lane_chunk_sum(p_new), axis=-1, keepdims=True)
        pc_ref[rows, :] = p_cache.astype(BF16)
        ps_ref[rows, :] = p_new.astype(BF16)
        li_ref[rows, :] = jnp.broadcast_to(1.0 / lsum, (lq, LANE))
    mixed = (_dot(pc_ref[...], cc_ref[...].astype(BF16)) + _dot(ps_ref[...], cs_ref[...].astype(BF16)))
    mixed = (mixed * li_ref[:, 0:1]).astype(BF16)
    for h in range(heads):
        vs = slice(h * V_HEAD, (h + 1) * V_HEAD)
        za = za_ref[:, vs].astype(F32)
        out = _dot(mixed[h * lq:(h + 1) * lq], wv_ref[:, vs]) * (za * jax.nn.sigmoid(za))
        o_ref[:, vs] = out.astype(o_ref.dtype)


def _attn_sample(q, kc, ckv_cache, ks, ckv_small, wv, p_bf, za_off, batch, lq, past, n_meta):
    n_heads = q.shape[1] // QK_PAD
    rows = ks.shape[0]
    kv_lora = ckv_cache.shape[1]
    za0 = za_off // (V_HEAD * n_heads)
    assert za_off % (V_HEAD * n_heads) == 0
    const = lambda b: (0, 0)
    return pl.pallas_call(
        functools.partial(_attn_sample_kernel, heads=n_heads, lq=lq, c=_softmax_scale(), n_meta=n_meta,
                          meta_row=batch * lq),
        grid=(batch,),
        in_specs=[
            pl.BlockSpec((lq, n_heads * QK_PAD), lambda b: (b, 0)),
            pl.BlockSpec((past, n_heads * QK_PAD), lambda b: (b, 0)),
            pl.BlockSpec((past, kv_lora), lambda b: (b, 0)),
            pl.BlockSpec((rows, n_heads * QK_PAD), const),
            pl.BlockSpec((rows, kv_lora), const),
            pl.BlockSpec((kv_lora, n_heads * V_HEAD), const),
            pl.BlockSpec((lq, n_heads * V_HEAD), lambda b: (b, za0)),
        ],
        out_specs=pl.BlockSpec((lq, n_heads * V_HEAD), lambda b: (b, 0)),
        out_shape=jax.ShapeDtypeStruct((batch * lq, n_heads * V_HEAD), BF16),
        scratch_shapes=[pltpu.VMEM((n_heads * lq, past), BF16), pltpu.VMEM((n_heads * lq, rows), BF16),
                        pltpu.VMEM((n_heads * lq, LANE), F32)],
        compiler_params=_cparams(("parallel",)),
        name="attn_sample",
    )(q, kc, ckv_cache, ks, ckv_small, wv, p_bf)


def _log_sigmoid(x):
    return jnp.minimum(x, 0.0) - jnp.log1p(jnp.exp(-jnp.abs(x)))


def _pad_rows(a, rows):
    if a.shape[0] == rows:
        return a
    return jnp.concatenate([a, jnp.zeros((rows - a.shape[0], a.shape[1]), a.dtype)], axis=0)


def _mlstm_kernel(q_ref, k_ref, v_ref, mo_ref, zb_ref, g_ref, gb_ref, gain_ref, c0_ref, n0_ref, m0_ref,
                  h_ref, c_ref, n_ref, m_ref, *, n_heads, lc, lp, dk, dv):
    @pl.when(pl.program_id(1) == 0)
    def _():
        c_ref[...] = c0_ref[...]
        n_ref[...] = n0_ref[...]
        m_ref[...] = m0_ref[...]

    hi = lax.Precision.HIGHEST
    gates = _pad_rows(g_ref[...] + gb_ref[...], lp)
    gates_t = gates.T
    t_col = lax.broadcasted_iota(jnp.int32, (lp, 1), 0)
    t_row = lax.broadcasted_iota(jnp.int32, (1, lp), 1)
    r_idx = lax.broadcasted_iota(jnp.int32, (lp, lp), 0)
    c_idx = lax.broadcasted_iota(jnp.int32, (lp, lp), 1)
    causal = r_idx >= c_idx
    lower = causal.astype(F32)
    upper = (c_idx >= r_idx).astype(F32)
    b_cols = jnp.dot(lower, jnp.where(t_col < lc, _log_sigmoid(gates), 0.0), precision=hi,
                     preferred_element_type=F32)
    b_rows = jnp.dot(jnp.where(t_row < lc, _log_sigmoid(gates_t), 0.0), upper, precision=hi,
                     preferred_element_type=F32)

    for h in range(n_heads):
        li = QK_ROPE + h
        lf_lane = QK_ROPE + n_heads + h
        ig_col = jnp.where(t_col < lc, gates[:, li:li + 1], NEG_BIG)
        ig_row = jnp.where(t_row < lc, gates_t[li:li + 1, :], NEG_BIG)
        b_col = b_cols[:, lf_lane:lf_lane + 1]
        b_row = b_rows[lf_lane:lf_lane + 1, :]
        m0 = m_ref[0, 0:1, h:h + 1]
        d = jnp.where(causal, b_col - b_row + ig_row, -jnp.inf)
        a_col = b_col + m0
        m = jnp.maximum(a_col, jnp.max(d, axis=-1, keepdims=True))
        w_inter = jnp.exp(a_col - m)

        q = _pad_rows(q_ref[:, h * dk:(h + 1) * dk], lp)
        k = _pad_rows(k_ref[:, h * dk:(h + 1) * dk], lp) * (dk ** -0.5)
        v = _pad_rows(v_ref[:, h * dv:(h + 1) * dv], lp)
        qk = _dot_nt(q, k) * jnp.exp(d - m)
        c_old = c_ref[0, h]
        n_old = n_ref[0, h:h + 1, :]
        num = w_inter * _dot(q, c_old.astype(BF16)) + _dot(qk.astype(BF16), v)
        den = (w_inter * jnp.sum(q.astype(F32) * n_old, axis=-1, keepdims=True)
               + jnp.sum(qk, axis=-1, keepdims=True))
        denc = jnp.maximum(jnp.abs(den), jnp.exp(-m))

        b_last = b_col[lp - 1:lp, :]
        g_row = b_last - b_row + ig_row
        g_col = b_last - b_col + ig_col
        m_new = jnp.maximum(b_last + m0, jnp.max(g_row, axis=-1, keepdims=True))
        decay = jnp.exp(b_last + m0 - m_new)
        kw = k.astype(F32) * jnp.exp(g_col - m_new)
        c_ref[0, h] = decay * c_old + _dot(kw.T.astype(BF16), v)
        n_ref[0, h:h + 1, :] = decay * n_old + jnp.sum(kw, axis=0, keepdims=True)
        m_ref[0, 0:1, h:h + 1] = m_new

        num = num[:lc]
        denc = denc[:lc]
        normed = num * lax.rsqrt(jnp.sum(num * num, axis=-1, keepdims=True) * (1.0 / dv) + EPS * denc * denc)
        mo = mo_ref[:, h * dv:(h + 1) * dv].astype(F32)
        zb = zb_ref[:, h * dv:(h + 1) * dv].astype(F32)
        out = jax.nn.sigmoid(mo) * (normed * gain_ref[:, h * dv:(h + 1) * dv])
        h_ref[:, h * dv:(h + 1) * dv] = (out * (zb * jax.nn.sigmoid(zb))).astype(h_ref.dtype)


def _mlstm(p_bf, p_f32, offs, gate_bias, gain, c0, n0, m0, batch, seq, lc, row0, share_state, n_heads, dk, dv):
    lp = -(-lc // LANE) * LANE
    nch = seq // lc
    blk0 = row0 // lc
    wqk = n_heads * dk
    wv = n_heads * dv
    gate_blk = p_f32.shape[1] // LANE - 1
    rows = lambda col: (lambda b, c: (blk0 + b * nch + c, col))
    state = (lambda b, c: (0, 0, 0, 0)) if share_state else (lambda b, c: (b, 0, 0, 0))
    state3 = (lambda b, c: (0, 0, 0)) if share_state else (lambda b, c: (b, 0, 0))
    out_rows = batch * seq
    return pl.pallas_call(
        functools.partial(_mlstm_kernel, n_heads=n_heads, lc=lc, lp=lp, dk=dk, dv=dv),
        grid=(batch, nch),
        in_specs=[
            pl.BlockSpec((lc, wqk), rows(offs["mq"] // wqk)),
            pl.BlockSpec((lc, wqk), rows(offs["mk"] // wqk)),
            pl.BlockSpec((lc, wv), rows(offs["mv"] // wv)),
            pl.BlockSpec((lc, wv), rows(offs["mo"] // wv)),
            pl.BlockSpec((lc, wv), rows(offs["zb"] // wv)),
            pl.BlockSpec((lc, LANE), rows(gate_blk)),
            pl.BlockSpec((1, LANE), lambda b, c: (0, 0)),
            pl.BlockSpec((1, wv), lambda b, c: (0, 0)),
            pl.BlockSpec((1, n_heads, dk, dv), state),
            pl.BlockSpec((1, n_heads, dk), state3),
            pl.BlockSpec((1, 1, n_heads), state3),
        ],
        out_specs=[
            pl.BlockSpec((lc, wv), lambda b, c: (b * nch + c, 0)),
            pl.BlockSpec((1, n_heads, dk, dv), lambda b, c: (b, 0, 0, 0)),
            pl.BlockSpec((1, n_heads, dk), lambda b, c: (b, 0, 0)),
            pl.BlockSpec((1, 1, n_heads), lambda b, c: (b, 0, 0)),
        ],
        out_shape=[
            jax.ShapeDtypeStruct((out_rows, wv), BF16),
            jax.ShapeDtypeStruct((batch, n_heads, dk, dv), F32),
            jax.ShapeDtypeStruct((batch, n_heads, dk), F32),
            jax.ShapeDtypeStruct((batch, 1, n_heads), F32),
        ],
        compiler_params=_cparams(("parallel", "arbitrary")),
        name="mlstm",
    )(p_bf, p_bf, p_bf, p_bf, p_bf, p_f32, gate_bias, gain, c0, n0, m0)


def _out_proj_kernel(a_ref, m_ref, wa_ref, wm_ref, x_ref, o_ref, *wb_refs):
    wa = wa_ref[...].astype(BF16)
    wm = wm_ref[...].astype(BF16)
    o_ref[...] = x_ref[...] + _dot(a_ref[...], wa) + _dot(m_ref[...], wm)
    if wb_refs:
        wb_refs[0][...] = wa
        wb_refs[1][...] = wm


def _out_proj(a, ml, w_a, w_m, wm_blk, x, tm, tn, emit_bf16=False):
    rows, wa = a.shape
    wm = ml.shape[1]
    d = w_a.shape[1]
    assert wa == wm and (not emit_bf16 or rows == tm)
    out_specs = [pl.BlockSpec((tm, tn), lambda i, j: (i, j))]
    out_shape = [jax.ShapeDtypeStruct((rows, d), F32)]
    if emit_bf16:
        out_specs += [pl.BlockSpec((wa, tn), lambda i, j: (0, j)), pl.BlockSpec((wm, tn), lambda i, j: (0, j))]
        out_shape += [jax.ShapeDtypeStruct((wa, d), BF16), jax.ShapeDtypeStruct((wm, d), BF16)]
    return pl.pallas_call(
        _out_proj_kernel,
        grid=(rows // tm, d // tn),
        in_specs=[
            pl.BlockSpec((tm, wa), lambda i, j: (i, 0)),
            pl.BlockSpec((tm, wm), lambda i, j: (i, 0)),
            pl.BlockSpec((wa, tn), lambda i, j: (0, j)),
            pl.BlockSpec((wm, tn), lambda i, j: (wm_blk, j)),
            pl.BlockSpec((tm, tn), lambda i, j: (i, j)),
        ],
        out_specs=out_specs,
        out_shape=out_shape,
        compiler_params=_cparams(("parallel", "arbitrary")),
        name="out_proj",
    )(a, ml, w_a, w_m, x)


def _rope_tables(pos):
    inv_freq = ROPE_THETA ** (-jnp.arange(HALF_ROPE, dtype=F32) / HALF_ROPE)
    ang = pos.astype(F32)[:, None] * inv_freq[None, :]
    cos, sin = jnp.cos(ang), jnp.sin(ang)
    return jnp.concatenate([cos, cos, cos, cos], axis=1), jnp.concatenate([-sin, sin, -sin, sin], axis=1)


def _pad_lanes(vec):
    return jnp.pad(vec.astype(F32), (0, LANE - vec.shape[0]))[None, :]


def _row_tile(rows, target):
    t = min(rows, target)
    while rows % t:
        t //= 2
    return t


def kernel(x_prompt, x_sample, cache_ckv, cache_kpe, state_C, state_n, state_m, meta_tokens, norm_gain, w_in,
           b_igate, b_fgate, q_a_norm, w_q_up, q_nope_norm, q_rope_norm, kv_a_norm, k_rope_norm, w_kv_up,
           k_nope_norm, ml_out_norm, w_out):
    batch, seq, d = x_prompt.shape
    dbatch, dseq, _ = x_sample.shape
    past = cache_ckv.shape[1]
    n_meta = meta_tokens.shape[0]
    q_lora = q_a_norm.shape[0]
    kv_lora = kv_a_norm.shape[0]
    ml_heads, dv = ml_out_norm.shape
    dk = state_n.shape[-1]
    mla_heads = w_kv_up.shape[1] // (QK_NOPE + V_HEAD)
    mla_w = mla_heads * V_HEAD
    ml_w = ml_heads * dv
    mqk_w = ml_heads * dk
    assert q_lora % COL_TILE == 0 and kv_lora % LANE == 0 and 2 * ml_heads <= LANE - QK_ROPE
    assert mla_w == ml_w and w_out.shape[0] == mla_w + ml_w

    o_cq = 0
    o_ckv = o_cq + q_lora
    o_kpe = o_ckv + kv_lora
    o_mq = o_kpe + QK_ROPE
    o_mk = o_mq + mqk_w
    o_mv = o_mk + mqk_w
    o_mo = o_mv + ml_w
    o_mi = o_mo + ml_w
    o_mf = o_mi + ml_heads
    o_za = o_mf + ml_heads
    o_zb = o_za + mla_w

    offs = {"mq": 0, "mk": mqk_w, "mv": 2 * mqk_w, "mo": 2 * mqk_w + ml_w, "za": 2 * mqk_w + 2 * ml_w,
            "zb": 2 * mqk_w + 2 * ml_w + mla_w, "cq": 2 * mqk_w + 3 * ml_w + mla_w}
    gate_pad = LANE - QK_ROPE - 2 * ml_heads
    w_in_t = w_in.T
    main_segments = [(o_mq, mqk_w, offs["mq"]), (o_mk, mqk_w, offs["mk"]), (o_mv, ml_w, offs["mv"]),
                     (o_mo, ml_w, offs["mo"]), (o_za, mla_w, offs["za"]), (o_zb, ml_w, offs["zb"]),
                     (o_cq, q_lora, offs["cq"])]
    w_small = _stack_rows(w_in_t, [(o_ckv, kv_lora), (o_kpe, QK_ROPE), (o_mi, 2 * ml_heads)], kv_lora + LANE,
                          _row_tile(d, COL_TILE))
    gate_bias = jnp.concatenate([jnp.zeros((QK_ROPE,), F32), b_igate.astype(F32), b_fgate.astype(F32),
                                 jnp.zeros((gate_pad,), F32)])[None, :]
    qk_w = QK_NOPE + QK_ROPE
    kv_w = QK_NOPE + V_HEAD
    wqn, wqp = _regroup(w_q_up, [[(h * qk_w, QK_NOPE, h * QK_NOPE) for h in range(mla_heads)],
                                 [(h * qk_w + QK_NOPE, QK_ROPE, h * QK_ROPE) for h in range(mla_heads)]],
                        _row_tile(q_lora, ROW_TILE))
    wk, wv = _regroup(w_kv_up, [[(h * kv_w, QK_NOPE, h * QK_NOPE) for h in range(mla_heads)],
                                [(h * kv_w + QK_NOPE, V_HEAD, h * V_HEAD) for h in range(mla_heads)]], kv_lora)
    gain_row = norm_gain.astype(F32)[None, :]
    qan = q_a_norm.astype(F32)[None, :]
    kvg = kv_a_norm.astype(F32)[None, :]
    qng = q_nope_norm.astype(F32)[None, :]
    kng = k_nope_norm.astype(F32)[None, :]
    qrg = jnp.tile(q_rope_norm.astype(F32), LANE // QK_ROPE)[None, :]
    krg = _pad_lanes(k_rope_norm)
    ml_gain = ml_out_norm.astype(F32).reshape(1, ml_w)

    n_p = batch * seq
    n_s = dbatch * dseq
    assert n_s % LANE == 0 and n_meta <= LANE and n_meta % 16 == 0
    rows_s = n_s + n_meta
    xp = x_prompt.reshape(n_p, d)
    xs = jnp.concatenate([x_sample.reshape(n_s, d), meta_tokens.astype(x_sample.dtype)], axis=0)

    pos_s = jnp.concatenate([jnp.tile(n_meta + past + jnp.arange(dseq), dbatch), jnp.arange(n_meta)])
    cos_p, sin_p = _rope_tables(n_meta + jnp.arange(seq))
    cos_s, sin_s = _rope_tables(pos_s)

    tm_p = _row_tile(seq, ROW_TILE)
    proj_sub = 2 if n_p % (2 * tm_p) == 0 else 1
    w_main, pbf_s, pf_s = _proj_regroup(xs, gain_row, w_in_t, main_segments, w_small)
    pbf_p, pf_p = _proj(xp, gain_row, w_main, w_small, proj_sub * tm_p, proj_sub)

    hps = max(2, mla_heads // 2)
    q_p, ckv_p, kpe_p, kpep_p = _latent_q(pbf_p, pf_p, offs["cq"], q_lora, kv_lora, cos_p, sin_p, qan, wqn, wqp,
                                          qng, qrg, kvg, krg, tm_p, hps)
    q_s, ckv_s, kpe_s, kpep_s = _latent_q(pbf_s, pf_s, offs["cq"], q_lora, kv_lora, cos_s, sin_s, qan, wqn, wqp,
                                          qng, qrg, kvg, krg, rows_s, hps)

    small_rows = n_s + LANE
    meta_blk = n_s // LANE
    tail = ((0, small_rows - rows_s), (0, 0))
    k_p, v_p = _kv_expand(ckv_p, kpep_p, wk, wv, kng, tm_p)
    ckv_small = jnp.pad(ckv_s, tail)
    k_s, v_s = _kv_expand(ckv_small, jnp.pad(kpep_s, tail), wk, wv, kng, small_rows)
    cache_rows = dbatch * past
    cache_kpep = jnp.pad(cache_kpe.reshape(cache_rows, QK_ROPE), ((0, 0), (0, LANE - QK_ROPE))).astype(BF16)
    ckv_cache = cache_ckv.reshape(cache_rows, kv_lora).astype(F32)
    k_c, = _kv_expand(ckv_cache, cache_kpep, wk, wv, kng, _row_tile(cache_rows, ROW_TILE), with_values=False)

    tq = _row_tile(seq, ATTN_TQ)
    attn_p = _attn_prompt(q_p, k_p, v_p, k_s, v_s, meta_blk, pbf_p, offs["za"], batch, seq, tq,
                          min(ATTN_HEADS, mla_heads), n_meta)
    attn_s = _attn_sample(q_s, k_c, ckv_cache, k_s, ckv_small, wv, pbf_s, offs["za"], dbatch, dseq, past, n_meta)

    zc = jnp.zeros((1, ml_heads, dk, dv), F32)
    zn = jnp.zeros((1, ml_heads, dk), F32)
    zm = jnp.zeros((1, 1, ml_heads), F32)
    _, c_m, n_m, m_m = _mlstm(pbf_s, pf_s, offs, gate_bias, ml_gain, zc, zn, zm, 1, n_meta, n_meta, n_s, True,
                              ml_heads, dk, dv)
    lc_p = _row_tile(seq, MLSTM_CHUNK)
    ml_p, c_p, n_pr, m_p = _mlstm(pbf_p, pf_p, offs, gate_bias, ml_gain, c_m, n_m, m_m, batch, seq, lc_p, 0,
                                  True, ml_heads, dk, dv)
    ml_s, c_s, n_sm, m_s = _mlstm(pbf_s, pf_s, offs, gate_bias, ml_gain, state_C.astype(F32),
                                  state_n.astype(F32), state_m.astype(F32).reshape(dbatch, 1, ml_heads),
                                  dbatch, dseq, dseq, 0, False, ml_heads, dk, dv)

    y_s, w_oa, w_om = _out_proj(attn_s, ml_s, w_out, w_out, 1, xs, n_s, _row_tile(d, COL_TILE), emit_bf16=True)
    y_p, = _out_proj(attn_p, ml_p, w_oa, w_om, 0, xp, _row_tile(n_p, 2 * ROW_TILE), _row_tile(d, COL_TILE))

    meta_ckv = ckv_s[n_s:n_s + n_meta]
    meta_kpe = kpe_s[n_s:n_s + n_meta]
    ckv_prompt = jnp.concatenate([jnp.broadcast_to(meta_ckv[None], (batch, n_meta, kv_lora)),
                                  ckv_p.reshape(batch, seq, kv_lora)], axis=1)
    kpe_prompt = jnp.concatenate([jnp.broadcast_to(meta_kpe[None], (batch, n_meta, QK_ROPE)),
                                  kpe_p.reshape(batch, seq, QK_ROPE)], axis=1)
    return (y_p.reshape(batch, seq, d), y_s.reshape(dbatch, dseq, d), ckv_prompt, kpe_prompt,
            c_p, n_pr, m_p.reshape(batch, ml_heads),
            ckv_s[:n_s].reshape(dbatch, dseq, kv_lora), kpe_s[:n_s].reshape(dbatch, dseq, QK_ROPE),
            c_s, n_sm, m_s.reshape(dbatch, ml_heads))
```

```python
import functools
import math

import jax
import jax.numpy as jnp
from jax import lax
from jax.experimental import pallas as pl
from jax.experimental.pallas import tpu as pltpu

CHUNK = 64
EPS = 1e-6
ROPE_THETA = 10000.0
V_HEAD = 128
QK_NOPE = 128
QK_ROPE = 64
HALF_ROPE = QK_ROPE // 2
LANE = 128
MXU_EDGE = 256
QK_PAD = QK_NOPE + LANE
COL_TILE = 512
ROW_TILE = 512
NEG_BIG = -1e30
ATTN_TQ = 512
ATTN_HEADS = 4
MLSTM_CHUNK = 256
VMEM_LIMIT = 62 * 1024 * 1024

F32 = jnp.float32
BF16 = jnp.bfloat16


def _cparams(sem):
    return pltpu.CompilerParams(dimension_semantics=sem, vmem_limit_bytes=VMEM_LIMIT)


def _dot(a, b):
    return jnp.dot(a, b, preferred_element_type=F32)


def _dot_nt(a, b):
    return lax.dot_general(a, b, (((1,), (1,)), ((), ())), preferred_element_type=F32)


def _rms(x, n):
    return x * lax.rsqrt(jnp.sum(x * x, axis=-1, keepdims=True) * (1.0 / n) + EPS)


def _rope_lanes(x, cos, sin_signed):
    lane = lax.broadcasted_iota(jnp.int32, x.shape, 1)
    first_half = (lane & (QK_ROPE - 1)) < HALF_ROPE
    rot = jnp.where(first_half, pltpu.roll(x, LANE - HALF_ROPE, 1), pltpu.roll(x, HALF_ROPE, 1))
    return x * cos + rot * sin_signed


def _regroup_kernel(w_ref, *o_refs, plans):
    for o_ref, segments in zip(o_refs, plans):
        for src, n, dst in segments:
            o_ref[:, dst:dst + n] = w_ref[:, src:src + n].astype(o_ref.dtype)


def _regroup(w, plans, tr):
    rows, cols = w.shape
    widths = [sum(n for _, n, _ in segments) for segments in plans]
    for segments in plans:
        pos = 0
        for _, n, dst in sorted(segments, key=lambda s: s[2]):
            assert dst == pos
            pos += n
    return pl.pallas_call(
        functools.partial(_regroup_kernel, plans=tuple(tuple(s) for s in plans)),
        grid=(rows // tr,),
        in_specs=[pl.BlockSpec((tr, cols), lambda i: (i, 0))],
        out_specs=[pl.BlockSpec((tr, width), lambda i: (i, 0)) for width in widths],
        out_shape=[jax.ShapeDtypeStruct((rows, width), BF16) for width in widths],
        compiler_params=_cparams(("parallel",)),
        name="regroup",
    )(w)


def _stack_rows_kernel(*refs):
    *w_refs, o_ref = refs
    parts = [r[...] for r in w_refs]
    used = sum(p.shape[0] for p in parts)
    parts.append(jnp.zeros((o_ref.shape[0] - used, o_ref.shape[1]), parts[0].dtype))
    o_ref[...] = jnp.concatenate(parts, axis=0).astype(o_ref.dtype)


def _stack_rows(wt, segments, out_rows, tc):
    cols = wt.shape[1]
    return pl.pallas_call(
        _stack_rows_kernel,
        grid=(cols // tc,),
        in_specs=[pl.BlockSpec((pl.Element(n), pl.Element(tc)), functools.partial(lambda i, s: (s, i * tc), s=src))
                  for src, n in segments],
        out_specs=pl.BlockSpec((out_rows, tc), lambda i: (0, i)),
        out_shape=jax.ShapeDtypeStruct((out_rows, cols), BF16),
        compiler_params=_cparams(("parallel",)),
        name="stack_rows",
    )(*([wt] * len(segments)))


def _proj_regroup_kernel(x_ref, g_ref, w_ref, ws_ref, om_ref, o_ref, os_ref, hn_ref):
    @pl.when(pl.program_id(0) == 0)
    def _():
        x = x_ref[...]
        hn = (_rms(x, x.shape[-1]) * g_ref[...]).astype(BF16)
        hn_ref[...] = hn
        os_ref[...] = _dot_nt(hn, ws_ref[...])

    wb = w_ref[...].astype(BF16)
    om_ref[...] = wb
    o_ref[...] = _dot_nt(hn_ref[...], wb).astype(o_ref.dtype)


def _proj_regroup(x, gain, wt, segments, w_small):
    m, d = x.shape
    ns = w_small.shape[0]
    n = sum(sz for _, sz, _ in segments)
    merged = []
    for src, sz, dst in sorted(segments, key=lambda s: s[2]):
        if merged and merged[-1][0] + merged[-1][1] == src and merged[-1][2] + merged[-1][1] == dst:
            merged[-1] = (merged[-1][0], merged[-1][1] + sz, merged[-1][2])
        else:
            merged.append((src, sz, dst))
    segments = merged
    assert all(sz % COL_TILE == 0 and dst % COL_TILE == 0 for _, sz, dst in segments)
    align = functools.reduce(math.gcd, [src for src, _, _ in segments], COL_TILE)

    def src_row(j):
        r = j * COL_TILE
        out = 0
        for src, sz, dst in segments:
            out = jnp.where((r >= dst) & (r < dst + sz), src + r - dst, out)
        return out

    const = lambda j: (0, 0)
    return pl.pallas_call(
        _proj_regroup_kernel,
        grid=(n // COL_TILE,),
        in_specs=[
            pl.BlockSpec((m, d), const),
            pl.BlockSpec((1, d), const),
            pl.BlockSpec((pl.Element(COL_TILE), pl.Element(d)), lambda j: (pl.multiple_of(src_row(j), align), 0)),
            pl.BlockSpec((ns, d), const),
        ],
        out_specs=[
            pl.BlockSpec((COL_TILE, d), lambda j: (j, 0)),
            pl.BlockSpec((m, COL_TILE), lambda j: (0, j)),
            pl.BlockSpec((m, ns), const),
        ],
        out_shape=[jax.ShapeDtypeStruct((n, d), BF16), jax.ShapeDtypeStruct((m, n), BF16),
                   jax.ShapeDtypeStruct((m, ns), F32)],
        scratch_shapes=[pltpu.VMEM((m, d), BF16)],
        compiler_params=_cparams(("arbitrary",)),
        name="proj_regroup",
    )(x, gain, wt, w_small)


def _proj_kernel(x_ref, g_ref, w_ref, ws_ref, o_ref, os_ref, hn_ref, *, n_sub):
    j = pl.program_id(1)
    ts = x_ref.shape[0]

    @pl.when(j < n_sub)
    def _():
        x = x_ref[...]
        hn = (_rms(x, x.shape[-1]) * g_ref[...]).astype(BF16)
        hn_ref[pl.ds(pl.multiple_of(j * ts, ts), ts), :] = hn
        os_ref[...] = _dot_nt(hn, ws_ref[...])

    @pl.when(j >= n_sub)
    def _():
        o_ref[...] = _dot_nt(hn_ref[...], w_ref[...]).astype(o_ref.dtype)


def _proj(x, gain, w_main, w_small, tm, n_sub, tn):
    m, d = x.shape
    n = w_main.shape[0]
    ns = w_small.shape[0]
    ts = tm // n_sub
    sub = lambda i, j: (i * n_sub + jnp.minimum(j, n_sub - 1), 0)
    feat = lambda j: jnp.maximum(j - n_sub, 0)
    return pl.pallas_call(
        functools.partial(_proj_kernel, n_sub=n_sub),
        grid=(m // tm, n_sub + n // tn),
        in_specs=[
            pl.BlockSpec((ts, d), sub),
            pl.BlockSpec((1, d), lambda i, j: (0, 0)),
            pl.BlockSpec((tn, d), lambda i, j: (feat(j), 0)),
            pl.BlockSpec((ns, d), lambda i, j: (0, 0), pipeline_mode=pl.Buffered(1)),
        ],
        out_specs=[
            pl.BlockSpec((tm, tn), lambda i, j: (i, feat(j))),
            pl.BlockSpec((ts, ns), sub),
        ],
        out_shape=[jax.ShapeDtypeStruct((m, n), BF16), jax.ShapeDtypeStruct((m, ns), F32)],
        scratch_shapes=[pltpu.VMEM((tm, d), BF16)],
        compiler_params=_cparams(("parallel", "arbitrary")),
        name="proj",
    )(x, gain, w_main, w_small)


def _latent_q_kernel(*refs, n_cq, kv_lora, heads_per_step):
    cq_refs = refs[:n_cq]
    (pf_ref, cos_ref, sin_ref, qan_ref, wqn_ref, wqp_ref, qng_ref, qrg_ref, kvg_ref, krg_ref,
     q_ref, ckv_ref, kpe_ref, kpep_ref, cqn_ref) = refs[n_cq:]
    cos = cos_ref[...]
    sin = sin_ref[...]
    lane = lax.broadcasted_iota(jnp.int32, cos.shape, 1)
    low = lane < QK_ROPE

    @pl.when(pl.program_id(1) == 0)
    def _():
        cq = jnp.concatenate([r[...].astype(F32) for r in cq_refs], axis=1)
        cqn_ref[...] = (_rms(cq, cq.shape[-1]) * qan_ref[...]).astype(BF16)
        pf = pf_ref[...]
        ckv_raw = pf[:, :kv_lora]
        ckv_ref[...] = _rms(ckv_raw, kv_lora) * kvg_ref[...]
        kraw = jnp.where(low, pf[:, kv_lora:kv_lora + LANE], 0.0)
        kpe = _rope_lanes(_rms(kraw, QK_ROPE) * krg_ref[...], cos, sin)
        kpe_ref[...] = kpe[:, :QK_ROPE]
        kpep_ref[...] = kpe.astype(BF16)

    cqn = cqn_ref[...]
    group = 4 if heads_per_step % 4 == 0 else 2
    for g0 in range(0, heads_per_step, group):
        qn = _dot(cqn, wqn_ref[:, g0 * QK_NOPE:(g0 + group) * QK_NOPE])
        qp = _dot(cqn, wqp_ref[:, g0 * QK_ROPE:(g0 + group) * QK_ROPE])
        for pair in range(group // 2):
            x = qp[:, pair * LANE:(pair + 1) * LANE]
            x2 = x * x
            ss_a = jnp.sum(jnp.where(low, x2, 0.0), axis=-1, keepdims=True)
            ss_b = jnp.sum(jnp.where(low, 0.0, x2), axis=-1, keepdims=True)
            scale = jnp.where(low, lax.rsqrt(ss_a * (1.0 / QK_ROPE) + EPS),
                              lax.rsqrt(ss_b * (1.0 / QK_ROPE) + EPS))
            r = _rope_lanes(x * scale * qrg_ref[...], cos, sin)
            halves = (jnp.where(low, r, 0.0), jnp.where(low, pltpu.roll(r, QK_ROPE, 1), 0.0))
            for h in (0, 1):
                base = (g0 + 2 * pair + h) * QK_PAD
                nope = qn[:, (2 * pair + h) * QK_NOPE:(2 * pair + h + 1) * QK_NOPE]
                q_ref[:, base:base + QK_NOPE] = (_rms(nope, QK_NOPE) * qng_ref[...]).astype(BF16)
                q_ref[:, base + QK_NOPE:base + QK_PAD] = halves[h].astype(BF16)


def _latent_q(p_bf, p_f32, cq_off, q_lora, kv_lora, cos, sin, qan, wqn, wqp, qng, qrg, kvg, krg, tm,
              heads_per_step):
    m = p_bf.shape[0]
    n_heads = wqn.shape[1] // QK_NOPE
    n_cq = q_lora // COL_TILE
    cq0 = cq_off // COL_TILE
    ns = p_f32.shape[1]
    wstep = heads_per_step * QK_PAD
    assert heads_per_step % 2 == 0
    row = lambda i, j: (i, 0)
    const = lambda i, j: (0, 0)
    pos_blocks = cos.shape[0] // tm
    pos = lambda i, j: (i % pos_blocks, 0)
    in_specs = [pl.BlockSpec((tm, COL_TILE), functools.partial(lambda i, j, c: (i, c), c=cq0 + c))
                for c in range(n_cq)]
    in_specs += [
        pl.BlockSpec((tm, ns), row),
        pl.BlockSpec((tm, LANE), pos),
        pl.BlockSpec((tm, LANE), pos),
        pl.BlockSpec((1, q_lora), const),
        pl.BlockSpec((q_lora, heads_per_step * QK_NOPE), lambda i, j: (0, j)),
        pl.BlockSpec((q_lora, heads_per_step * QK_ROPE), lambda i, j: (0, j)),
        pl.BlockSpec((1, LANE), const),
        pl.BlockSpec((1, LANE), const),
        pl.BlockSpec((1, kv_lora), const),
        pl.BlockSpec((1, LANE), const),
    ]
    return pl.pallas_call(
        functools.partial(_latent_q_kernel, n_cq=n_cq, kv_lora=kv_lora, heads_per_step=heads_per_step),
        grid=(m // tm, n_heads // heads_per_step),
        in_specs=in_specs,
        out_specs=[
            pl.BlockSpec((tm, wstep), lambda i, j: (i, j)),
            pl.BlockSpec((tm, kv_lora), row),
            pl.BlockSpec((tm, QK_ROPE), row),
            pl.BlockSpec((tm, LANE), row),
        ],
        out_shape=[
            jax.ShapeDtypeStruct((m, n_heads * QK_PAD), BF16),
            jax.ShapeDtypeStruct((m, kv_lora), F32),
            jax.ShapeDtypeStruct((m, QK_ROPE), F32),
            jax.ShapeDtypeStruct((m, LANE), BF16),
        ],
        scratch_shapes=[pltpu.VMEM((tm, q_lora), BF16)],
        compiler_params=_cparams(("parallel", "arbitrary")),
        name="latent_q",
    )(*([p_bf] * n_cq), p_f32, cos, sin, qan, wqn, wqp, qng, qrg, kvg, krg)


def _kv_expand_kernel(ckv_ref, kpep_ref, wk_ref, wv_ref, kng_ref, k_ref, *v_refs, n_heads):
    c = ckv_ref[...].astype(BF16)
    kf = _dot(c, wk_ref[...])
    if v_refs:
        v_refs[0][...] = _dot(c, wv_ref[...]).astype(BF16)
    kpep = kpep_ref[...]
    for h in range(n_heads):
        kn = _rms(kf[:, h * QK_NOPE:(h + 1) * QK_NOPE], QK_NOPE) * kng_ref[...]
        k_ref[:, h * QK_PAD:h * QK_PAD + QK_NOPE] = kn.astype(BF16)
        k_ref[:, h * QK_PAD + QK_NOPE:(h + 1) * QK_PAD] = kpep


def _kv_expand(ckv, kpep, wk, wv, kng, tm, with_values=True):
    r, kv_lora = ckv.shape
    n_heads = wk.shape[1] // QK_NOPE
    row = lambda i: (i, 0)
    const = lambda i: (0, 0)
    n_out = 2 if with_values else 1
    return pl.pallas_call(
        functools.partial(_kv_expand_kernel, n_heads=n_heads),
        grid=(r // tm,),
        in_specs=[
            pl.BlockSpec((tm, kv_lora), row),
            pl.BlockSpec((tm, LANE), row),
            pl.BlockSpec((kv_lora, n_heads * QK_NOPE), const),
            pl.BlockSpec((kv_lora, n_heads * V_HEAD), const),
            pl.BlockSpec((1, LANE), const),
        ],
        out_specs=[
            pl.BlockSpec((tm, n_heads * QK_PAD), row),
            pl.BlockSpec((tm, n_heads * V_HEAD), row),
        ][:n_out],
        out_shape=[
            jax.ShapeDtypeStruct((r, n_heads * QK_PAD), BF16),
            jax.ShapeDtypeStruct((r, n_heads * V_HEAD), BF16),
        ][:n_out],
        compiler_params=_cparams(("parallel",)),
        name="kv_expand",
    )(ckv, kpep, wk, wv, kng)


def _lane_chunk_sum(p):
    out = p[:, :LANE]
    for c in range(1, p.shape[1] // LANE):
        out = out + p[:, c * LANE:(c + 1) * LANE]
    return out


def _gated(lvec, acc, za):
    za = za.astype(F32)
    return acc / jnp.sum(lvec, axis=-1, keepdims=True) * (za * jax.nn.sigmoid(za))


def _softmax_scale():
    return float((QK_NOPE + QK_ROPE) ** -0.5 * 1.4426950408889634)


def _attn_prompt_block(n_full, q_ref, k_ref, km_ref, za_ref, o_ref, s_ref, p_ref, va_ref, vma_ref, *,
                       tq, heads, c, n_meta):
    td = min(tq, MXU_EDGE)
    pieces = [(i * tq, tq, 0, False) for i in range(n_full)]
    pieces += [(n_full * tq + j * td, td, j * td, True) for j in range(tq // td)]

    def upd(full, r0, fn):
        return fn(full) if r0 == 0 else jnp.concatenate([full[:r0], fn(full[r0:])], axis=0)

    def scores(h):
        qs = slice(h * QK_PAD, (h + 1) * QK_PAD)
        q = q_ref[:, qs]
        col = lax.broadcasted_iota(jnp.int32, (tq, LANE), 1)
        t = jnp.where(col < n_meta, _dot_nt(q, km_ref[:, qs]) * c, -jnp.inf)
        s_ref[h, :, :LANE] = t
        mx = t
        off = LANE
        for start, rows, r0, masked in pieces:
            t = _dot_nt(q[r0:], k_ref[start:start + rows, qs]) * c
            if masked:
                rq = lax.broadcasted_iota(jnp.int32, t.shape, 0) // CHUNK
                ck = lax.broadcasted_iota(jnp.int32, t.shape, 1) // CHUNK
                t = jnp.where(rq >= ck, t, -jnp.inf)
            s_ref[h, r0:, off:off + rows] = t
            tmax = t[:, :LANE]
            for ch in range(1, rows // LANE):
                tmax = jnp.maximum(tmax, t[:, ch * LANE:(ch + 1) * LANE])
            mx = upd(mx, r0, lambda a: jnp.maximum(a, tmax))
            off += rows
        return jnp.max(mx, axis=-1, keepdims=True)

    def values(h, m):
        vs = slice(h * V_HEAD, (h + 1) * V_HEAD)
        va = slice(2 * h * V_HEAD, 2 * (h + 1) * V_HEAD)
        p_ref[h, :, :LANE] = jnp.exp2(s_ref[h, :, :LANE] - m).astype(BF16)
        off = LANE
        for start, rows, r0, _ in pieces:
            p_ref[h, r0:, off:off + rows] = jnp.exp2(s_ref[h, r0:, off:off + rows] - m[r0:]).astype(BF16)
            off += rows
        acc = _dot(p_ref[h, :, :LANE], vma_ref[:, va])
        n_main = n_full * tq
        if n_main:
            acc = acc + _dot(p_ref[h, :, LANE:LANE + n_main], va_ref[:n_main, va])
        off = LANE + n_main
        for start, rows, r0, _ in pieces[n_full:]:
            pv = _dot(p_ref[h, r0:, off:off + rows], va_ref[start:start + rows, va])
            acc = upd(acc, r0, lambda a: a + pv)
            off += rows
        za = za_ref[:, vs].astype(F32)
        out = acc[:, :V_HEAD] / acc[:, V_HEAD:V_HEAD + 1] * (za * jax.nn.sigmoid(za))
        o_ref[:, vs] = out.astype(o_ref.dtype)

    maxes = [scores(h) for h in range(heads)]
    for h in range(heads):
        values(h, maxes[h])


def _attn_prompt_kernel(q_ref, k_ref, v_ref, km_ref, vm_ref, za_ref, o_ref, s_ref, p_ref, va_ref, vma_ref, *,
                        nqb, heads, **kw):
    qi = pl.program_id(2)

    @pl.when(qi == 0)
    def _():
        for h in range(heads):
            vs = slice(h * V_HEAD, (h + 1) * V_HEAD)
            va_ref[:, 2 * h * V_HEAD:(2 * h + 1) * V_HEAD] = v_ref[:, vs]
            va_ref[:, (2 * h + 1) * V_HEAD:(2 * h + 2) * V_HEAD] = jnp.ones((va_ref.shape[0], V_HEAD), BF16)
            vma_ref[:, 2 * h * V_HEAD:(2 * h + 1) * V_HEAD] = vm_ref[:, vs]
            vma_ref[:, (2 * h + 1) * V_HEAD:(2 * h + 2) * V_HEAD] = jnp.ones((vma_ref.shape[0], V_HEAD), BF16)

    for n_full in range(nqb):
        pl.when(qi == n_full)(
            functools.partial(_attn_prompt_block, n_full, q_ref, k_ref, km_ref, za_ref, o_ref, s_ref, p_ref, va_ref,
                              vma_ref, heads=heads, **kw))


def _attn_prompt(q, k, v, k_small, v_small, meta_blk, p_bf, za_off, batch, seq, tq, heads, n_meta):
    n_heads = q.shape[1] // QK_PAD
    nqb = seq // tq
    za0 = za_off // (V_HEAD * heads)
    assert za_off % (V_HEAD * heads) == 0 and n_heads % heads == 0 and tq % CHUNK == 0
    return pl.pallas_call(
        functools.partial(_attn_prompt_kernel, nqb=nqb, tq=tq, heads=heads, c=_softmax_scale(), n_meta=n_meta),
        grid=(batch, n_heads // heads, nqb),
        in_specs=[
            pl.BlockSpec((tq, heads * QK_PAD), lambda b, h, i: (b * nqb + i, h)),
            pl.BlockSpec((seq, heads * QK_PAD), lambda b, h, i: (b, h)),
            pl.BlockSpec((seq, heads * V_HEAD), lambda b, h, i: (b, h)),
            pl.BlockSpec((LANE, heads * QK_PAD), lambda b, h, i: (meta_blk, h)),
            pl.BlockSpec((LANE, heads * V_HEAD), lambda b, h, i: (meta_blk, h)),
            pl.BlockSpec((tq, heads * V_HEAD), lambda b, h, i: (b * nqb + i, za0 + h)),
        ],
        out_specs=pl.BlockSpec((tq, heads * V_HEAD), lambda b, h, i: (b * nqb + i, h)),
        out_shape=jax.ShapeDtypeStruct((batch * seq, n_heads * V_HEAD), BF16),
        scratch_shapes=[pltpu.VMEM((heads, tq, LANE + seq), F32), pltpu.VMEM((heads, tq, LANE + seq), BF16),
                        pltpu.VMEM((seq, 2 * heads * V_HEAD), BF16), pltpu.VMEM((LANE, 2 * heads * V_HEAD), BF16)],
        compiler_params=_cparams(("parallel", "parallel", "arbitrary")),
        name="attn_prompt",
    )(q, k, v, k_small, v_small, p_bf)


def _attn_sample_kernel(q_ref, kc_ref, cc_ref, ks_ref, cs_ref, wv_ref, za_ref, o_ref, pc_ref, ps_ref, li_ref, *,
                        heads, lq, c, n_meta, meta_row):
    b = pl.program_id(0)
    col = lax.broadcasted_iota(jnp.int32, (lq, ks_ref.shape[0]), 1)
    own = (col >= b * lq) & (col < (b + 1) * lq)
    meta = (col >= meta_row) & (col < meta_row + n_meta)
    visible = own | meta
    for h in range(heads):
        qs = slice(h * QK_PAD, (h + 1) * QK_PAD)
        rows = slice(h * lq, (h + 1) * lq)
        q = q_ref[:, qs]
        t_cache = _dot_nt(q, kc_ref[:, qs]) * c
        t_new = jnp.where(visible, _dot_nt(q, ks_ref[:, qs]) * c, -jnp.inf)
        m = jnp.maximum(jnp.max(t_cache, axis=-1, keepdims=True), jnp.max(t_new, axis=-1, keepdims=True))
        p_cache = jnp.exp2(t_cache - m)
        p_new = jnp.exp2(t_new - m)
        lsum = jnp.sum(_lane_chunk_sum(p_cache) + _lane_chunk_sum(p_new), axis=-1, keepdims=True)
        pc_ref[rows, :] = p_cache.astype(BF16)
        ps_ref[rows, :] = p_new.astype(BF16)
        li_ref[rows, :] = jnp.broadcast_to(1.0 / lsum, (lq, LANE))
    mixed = (_dot(pc_ref[...], cc_ref[...].astype(BF16)) + _dot(ps_ref[...], cs_ref[...].astype(BF16)))
    mixed = (mixed * li_ref[:, 0:1]).astype(BF16)
    for h in range(heads):
        vs = slice(h * V_HEAD, (h + 1) * V_HEAD)
        za = za_ref[:, vs].astype(F32)
        out = _dot(mixed[h * lq:(h + 1) * lq], wv_ref[:, vs]) * (za * jax.nn.sigmoid(za))
        o_ref[:, vs] = out.astype(o_ref.dtype)


def _attn_sample(q, kc, ckv_cache, ks, ckv_small, wv, p_bf, za_off, batch, lq, past, n_meta):
    n_heads = q.shape[1] // QK_PAD
    rows = ks.shape[0]
    kv_lora = ckv_cache.shape[1]
    za0 = za_off // (V_HEAD * n_heads)
    assert za_off % (V_HEAD * n_heads) == 0
    const = lambda b: (0, 0)
    return pl.pallas_call(
        functools.partial(_attn_sample_kernel, heads=n_heads, lq=lq, c=_softmax_scale(), n_meta=n_meta,
                          meta_row=batch * lq),
        grid=(batch,),
        in_specs=[
            pl.BlockSpec((lq, n_heads * QK_PAD), lambda b: (b, 0)),
            pl.BlockSpec((past, n_heads * QK_PAD), lambda b: (b, 0)),
            pl.BlockSpec((past, kv_lora), lambda b: (b, 0)),
            pl.BlockSpec((rows, n_heads * QK_PAD), const),
            pl.BlockSpec((rows, kv_lora), const),
            pl.BlockSpec((kv_lora, n_heads * V_HEAD), const),
            pl.BlockSpec((lq, n_heads * V_HEAD), lambda b: (b, za0)),
        ],
        out_specs=pl.BlockSpec((lq, n_heads * V_HEAD), lambda b: (b, 0)),
        out_shape=jax.ShapeDtypeStruct((batch * lq, n_heads * V_HEAD), BF16),
        scratch_shapes=[pltpu.VMEM((n_heads * lq, past), BF16), pltpu.VMEM((n_heads * lq, rows), BF16),
                        pltpu.VMEM((n_heads * lq, LANE), F32)],
        compiler_params=_cparams(("parallel",)),
        name="attn_sample",
    )(q, kc, ckv_cache, ks, ckv_small, wv, p_bf)


def _log_sigmoid(x):
    return jnp.minimum(x, 0.0) - jnp.log1p(jnp.exp(-jnp.abs(x)))


def _pad_rows(a, rows):
    if a.shape[0] == rows:
        return a
    return jnp.concatenate([a, jnp.zeros((rows - a.shape[0], a.shape[1]), a.dtype)], axis=0)


def _mlstm_kernel(q_ref, k_ref, v_ref, mo_ref, zb_ref, g_ref, gb_ref, gain_ref, c0_ref, n0_ref, m0_ref,
                  h_ref, c_ref, n_ref, m_ref, *, n_heads, lc, lp, dk, dv):
    @pl.when(pl.program_id(1) == 0)
    def _():
        c_ref[...] = c0_ref[...]
        n_ref[...] = n0_ref[...]
        m_ref[...] = m0_ref[...]

    hi = lax.Precision.HIGHEST
    gates = _pad_rows(g_ref[...] + gb_ref[...], lp)
    gates_t = gates.T
    t_col = lax.broadcasted_iota(jnp.int32, (lp, 1), 0)
    t_row = lax.broadcasted_iota(jnp.int32, (1, lp), 1)
    r_idx = lax.broadcasted_iota(jnp.int32, (lp, lp), 0)
    c_idx = lax.broadcasted_iota(jnp.int32, (lp, lp), 1)
    causal = r_idx >= c_idx
    lower = causal.astype(F32)
    upper = (c_idx >= r_idx).astype(F32)
    b_cols = jnp.dot(lower, jnp.where(t_col < lc, _log_sigmoid(gates), 0.0), precision=hi,
                     preferred_element_type=F32)
    b_rows = jnp.dot(jnp.where(t_row < lc, _log_sigmoid(gates_t), 0.0), upper, precision=hi,
                     preferred_element_type=F32)

    for h in range(n_heads):
        li = QK_ROPE + h
        lf_lane = QK_ROPE + n_heads + h
        ig_col = jnp.where(t_col < lc, gates[:, li:li + 1], NEG_BIG)
        ig_row = jnp.where(t_row < lc, gates_t[li:li + 1, :], NEG_BIG)
        b_col = b_cols[:, lf_lane:lf_lane + 1]
        b_row = b_rows[lf_lane:lf_lane + 1, :]
        m0 = m_ref[0, 0:1, h:h + 1]
        d = jnp.where(causal, b_col - b_row + ig_row, -jnp.inf)
        a_col = b_col + m0
        m = jnp.maximum(a_col, jnp.max(d, axis=-1, keepdims=True))
        w_inter = jnp.exp(a_col - m)

        q = _pad_rows(q_ref[:, h * dk:(h + 1) * dk], lp)
        k = _pad_rows(k_ref[:, h * dk:(h + 1) * dk], lp) * (dk ** -0.5)
        v = _pad_rows(v_ref[:, h * dv:(h + 1) * dv], lp)
        qk = _dot_nt(q, k) * jnp.exp(d - m)
        c_old = c_ref[0, h]
        n_old = n_ref[0, h:h + 1, :]
        num = w_inter * _dot(q, c_old.astype(BF16)) + _dot(qk.astype(BF16), v)
        den = (w_inter * jnp.sum(q.astype(F32) * n_old, axis=-1, keepdims=True)
               + jnp.sum(qk, axis=-1, keepdims=True))
        denc = jnp.maximum(jnp.abs(den), jnp.exp(-m))

        b_last = b_col[lp - 1:lp, :]
        g_row = b_last - b_row + ig_row
        g_col = b_last - b_col + ig_col
        m_new = jnp.maximum(b_last + m0, jnp.max(g_row, axis=-1, keepdims=True))
        decay = jnp.exp(b_last + m0 - m_new)
        kw = k.astype(F32) * jnp.exp(g_col - m_new)
        c_ref[0, h] = decay * c_old + _dot(kw.T.astype(BF16), v)
        n_ref[0, h:h + 1, :] = decay * n_old + jnp.sum(kw, axis=0, keepdims=True)
        m_ref[0, 0:1, h:h + 1] = m_new

        num = num[:lc]
        denc = denc[:lc]
        normed = num * lax.rsqrt(jnp.sum(num * num, axis=-1, keepdims=True) * (1.0 / dv) + EPS * denc * denc)
        mo = mo_ref[:, h * dv:(h + 1) * dv].astype(F32)
        zb = zb_ref[:, h * dv:(h + 1) * dv].astype(F32)
        out = jax.nn.sigmoid(mo) * (normed * gain_ref[:, h * dv:(h + 1) * dv])
        h_ref[:, h * dv:(h + 1) * dv] = (out * (zb * jax.nn.sigmoid(zb))).astype(h_ref.dtype)


def _mlstm(p_bf, p_f32, offs, gate_bias, gain, c0, n0, m0, batch, seq, lc, row0, share_state, n_heads, dk, dv):
    lp = -(-lc // LANE) * LANE
    nch = seq // lc
    blk0 = row0 // lc
    wqk = n_heads * dk
    wv = n_heads * dv
    gate_blk = p_f32.shape[1] // LANE - 1
    rows = lambda col: (lambda b, c: (blk0 + b * nch + c, col))
    state = (lambda b, c: (0, 0, 0, 0)) if share_state else (lambda b, c: (b, 0, 0, 0))
    state3 = (lambda b, c: (0, 0, 0)) if share_state else (lambda b, c: (b, 0, 0))
    out_rows = batch * seq
    return pl.pallas_call(
        functools.partial(_mlstm_kernel, n_heads=n_heads, lc=lc, lp=lp, dk=dk, dv=dv),
        grid=(batch, nch),
        in_specs=[
            pl.BlockSpec((lc, wqk), rows(offs["mq"] // wqk)),
            pl.BlockSpec((lc, wqk), rows(offs["mk"] // wqk)),
            pl.BlockSpec((lc, wv), rows(offs["mv"] // wv)),
            pl.BlockSpec((lc, wv), rows(offs["mo"] // wv)),
            pl.BlockSpec((lc, wv), rows(offs["zb"] // wv)),
            pl.BlockSpec((lc, LANE), rows(gate_blk)),
            pl.BlockSpec((1, LANE), lambda b, c: (0, 0)),
            pl.BlockSpec((1, wv), lambda b, c: (0, 0)),
            pl.BlockSpec((1, n_heads, dk, dv), state),
            pl.BlockSpec((1, n_heads, dk), state3),
            pl.BlockSpec((1, 1, n_heads), state3),
        ],
        out_specs=[
            pl.BlockSpec((lc, wv), lambda b, c: (b * nch + c, 0)),
            pl.BlockSpec((1, n_heads, dk, dv), lambda b, c: (b, 0, 0, 0)),
            pl.BlockSpec((1, n_heads, dk), lambda b, c: (b, 0, 0)),
            pl.BlockSpec((1, 1, n_heads), lambda b, c: (b, 0, 0)),
        ],
        out_shape=[
            jax.ShapeDtypeStruct((out_rows, wv), BF16),
            jax.ShapeDtypeStruct((batch, n_heads, dk, dv), F32),
            jax.ShapeDtypeStruct((batch, n_heads, dk), F32),
            jax.ShapeDtypeStruct((batch, 1, n_heads), F32),
        ],
        compiler_params=_cparams(("parallel", "arbitrary")),
        name="mlstm",
    )(p_bf, p_bf, p_bf, p_bf, p_bf, p_f32, gate_bias, gain, c0, n0, m0)


def _out_proj_kernel(a_ref, m_ref, wa_ref, wm_ref, x_ref, o_ref, *wb_refs):
    wa = wa_ref[...].astype(BF16)
    wm = wm_ref[...].astype(BF16)
    o_ref[...] = x_ref[...] + _dot(a_ref[...], wa) + _dot(m_ref[...], wm)
    if wb_refs:
        wb_refs[0][...] = wa
        wb_refs[1][...] = wm


def _out_proj(a, ml, w_a, w_m, wm_blk, x, tm, tn, emit_bf16=False):
    rows, wa = a.shape
    wm = ml.shape[1]
    d = w_a.shape[1]
    assert wa == wm and (not emit_bf16 or rows == tm)
    out_specs = [pl.BlockSpec((tm, tn), lambda i, j: (i, j))]
    out_shape = [jax.ShapeDtypeStruct((rows, d), F32)]
    if emit_bf16:
        out_specs += [pl.BlockSpec((wa, tn), lambda i, j: (0, j)), pl.BlockSpec((wm, tn), lambda i, j: (0, j))]
        out_shape += [jax.ShapeDtypeStruct((wa, d), BF16), jax.ShapeDtypeStruct((wm, d), BF16)]
    return pl.pallas_call(
        _out_proj_kernel,
        grid=(rows // tm, d // tn),
        in_specs=[
            pl.BlockSpec((tm, wa), lambda i, j: (i, 0)),
            pl.BlockSpec((tm, wm), lambda i, j: (i, 0)),
            pl.BlockSpec((wa, tn), lambda i, j: (0, j)),
            pl.BlockSpec((wm, tn), lambda i, j: (wm_blk, j)),
            pl.BlockSpec((tm, tn), lambda i, j: (i, j)),
        ],
        out_specs=out_specs,
        out_shape=out_shape,
        compiler_params=_cparams(("parallel", "arbitrary")),
        name="out_proj",
    )(a, ml, w_a, w_m, x)


def _rope_tables(pos):
    inv_freq = ROPE_THETA ** (-jnp.arange(HALF_ROPE, dtype=F32) / HALF_ROPE)
    ang = pos.astype(F32)[:, None] * inv_freq[None, :]
    cos, sin = jnp.cos(ang), jnp.sin(ang)
    return jnp.concatenate([cos, cos, cos, cos], axis=1), jnp.concatenate([-sin, sin, -sin, sin], axis=1)


def _pad_lanes(vec):
    return jnp.pad(vec.astype(F32), (0, LANE - vec.shape[0]))[None, :]


def _row_tile(rows, target):
    t = min(rows, target)
    while rows % t:
        t //= 2
    return t


def kernel(x_prompt, x_sample, cache_ckv, cache_kpe, state_C, state_n, state_m, meta_tokens, norm_gain, w_in,
           b_igate, b_fgate, q_a_norm, w_q_up, q_nope_norm, q_rope_norm, kv_a_norm, k_rope_norm, w_kv_up,
           k_nope_norm, ml_out_norm, w_out):
    batch, seq, d = x_prompt.shape
    dbatch, dseq, _ = x_sample.shape
    past = cache_ckv.shape[1]
    n_meta = meta_tokens.shape[0]
    q_lora = q_a_norm.shape[0]
    kv_lora = kv_a_norm.shape[0]
    ml_heads, dv = ml_out_norm.shape
    dk = state_n.shape[-1]
    mla_heads = w_kv_up.shape[1] // (QK_NOPE + V_HEAD)
    mla_w = mla_heads * V_HEAD
    ml_w = ml_heads * dv
    mqk_w = ml_heads * dk
    assert q_lora % COL_TILE == 0 and kv_lora % LANE == 0 and 2 * ml_heads <= LANE - QK_ROPE
    assert mla_w == ml_w and w_out.shape[0] == mla_w + ml_w

    o_cq = 0
    o_ckv = o_cq + q_lora
    o_kpe = o_ckv + kv_lora
    o_mq = o_kpe + QK_ROPE
    o_mk = o_mq + mqk_w
    o_mv = o_mk + mqk_w
    o_mo = o_mv + ml_w
    o_mi = o_mo + ml_w
    o_mf = o_mi + ml_heads
    o_za = o_mf + ml_heads
    o_zb = o_za + mla_w

    offs = {"mq": 0, "mk": mqk_w, "mv": 2 * mqk_w, "mo": 2 * mqk_w + ml_w, "za": 2 * mqk_w + 2 * ml_w,
            "zb": 2 * mqk_w + 2 * ml_w + mla_w, "cq": 2 * mqk_w + 3 * ml_w + mla_w}
    gate_pad = LANE - QK_ROPE - 2 * ml_heads
    w_in_t = w_in.T
    main_segments = [(o_mq, mqk_w, offs["mq"]), (o_mk, mqk_w, offs["mk"]), (o_mv, ml_w, offs["mv"]),
                     (o_mo, ml_w, offs["mo"]), (o_za, mla_w, offs["za"]), (o_zb, ml_w, offs["zb"]),
                     (o_cq, q_lora, offs["cq"])]
    w_small = _stack_rows(w_in_t, [(o_ckv, kv_lora), (o_kpe, QK_ROPE), (o_mi, 2 * ml_heads)], kv_lora + LANE,
                          _row_tile(d, COL_TILE))
    gate_bias = jnp.concatenate([jnp.zeros((QK_ROPE,), F32), b_igate.astype(F32), b_fgate.astype(F32),
                                 jnp.zeros((gate_pad,), F32)])[None, :]
    qk_w = QK_NOPE + QK_ROPE
    kv_w = QK_NOPE + V_HEAD
    wqn, wqp = _regroup(w_q_up, [[(h * qk_w, QK_NOPE, h * QK_NOPE) for h in range(mla_heads)],
                                 [(h * qk_w + QK_NOPE, QK_ROPE, h * QK_ROPE) for h in range(mla_heads)]],
                        _row_tile(q_lora, ROW_TILE))
    wk, wv = _regroup(w_kv_up, [[(h * kv_w, QK_NOPE, h * QK_NOPE) for h in range(mla_heads)],
                                [(h * kv_w + QK_NOPE, V_HEAD, h * V_HEAD) for h in range(mla_heads)]], kv_lora)
    gain_row = norm_gain.astype(F32)[None, :]
    qan = q_a_norm.astype(F32)[None, :]
    kvg = kv_a_norm.astype(F32)[None, :]
    qng = q_nope_norm.astype(F32)[None, :]
    kng = k_nope_norm.astype(F32)[None, :]
    qrg = jnp.tile(q_rope_norm.astype(F32), LANE // QK_ROPE)[None, :]
    krg = _pad_lanes(k_rope_norm)
    ml_gain = ml_out_norm.astype(F32).reshape(1, ml_w)

    n_p = batch * seq
    n_s = dbatch * dseq
    assert n_s % LANE == 0 and n_meta <= LANE and n_meta % 16 == 0
    rows_s = n_s + n_meta
    xp = x_prompt.reshape(n_p, d)
    xs = jnp.concatenate([x_sample.reshape(n_s, d), meta_tokens.astype(x_sample.dtype)], axis=0)

    pos_s = jnp.concatenate([jnp.tile(n_meta + past + jnp.arange(dseq), dbatch), jnp.arange(n_meta)])
    cos_p, sin_p = _rope_tables(n_meta + jnp.arange(seq))
    cos_s, sin_s = _rope_tables(pos_s)

    tm_p = _row_tile(seq, ROW_TILE)
    proj_sub = max(s for s in (1, 2, 4) if n_p % (s * tm_p) == 0)
    proj_tn = COL_TILE * 2 // proj_sub if proj_sub > 1 else COL_TILE
    w_main, pbf_s, pf_s = _proj_regroup(xs, gain_row, w_in_t, main_segments, w_small)
    pbf_p, pf_p = _proj(xp, gain_row, w_main, w_small, proj_sub * tm_p, proj_sub, proj_tn)

    hps = max(2, mla_heads // 2)
    q_p, ckv_p, kpe_p, kpep_p = _latent_q(pbf_p, pf_p, offs["cq"], q_lora, kv_lora, cos_p, sin_p, qan, wqn, wqp,
                                          qng, qrg, kvg, krg, tm_p, hps)
    q_s, ckv_s, kpe_s, kpep_s = _latent_q(pbf_s, pf_s, offs["cq"], q_lora, kv_lora, cos_s, sin_s, qan, wqn, wqp,
                                          qng, qrg, kvg, krg, rows_s, hps)

    small_rows = n_s + LANE
    meta_blk = n_s // LANE
    tail = ((0, small_rows - rows_s), (0, 0))
    k_p, v_p = _kv_expand(ckv_p, kpep_p, wk, wv, kng, tm_p)
    ckv_small = jnp.pad(ckv_s, tail)
    k_s, v_s = _kv_expand(ckv_small, jnp.pad(kpep_s, tail), wk, wv, kng, small_rows)
    cache_rows = dbatch * past
    cache_kpep = jnp.pad(cache_kpe.reshape(cache_rows, QK_ROPE), ((0, 0), (0, LANE - QK_ROPE))).astype(BF16)
    ckv_cache = cache_ckv.reshape(cache_rows, kv_lora).astype(F32)
    k_c, = _kv_expand(ckv_cache, cache_kpep, wk, wv, kng, _row_tile(cache_rows, ROW_TILE), with_values=False)

    tq = _row_tile(seq, ATTN_TQ)
    attn_p = _attn_prompt(q_p, k_p, v_p, k_s, v_s, meta_blk, pbf_p, offs["za"], batch, seq, tq,
                          min(ATTN_HEADS, mla_heads), n_meta)
    attn_s = _attn_sample(q_s, k_c, ckv_cache, k_s, ckv_small, wv, pbf_s, offs["za"], dbatch, dseq, past, n_meta)

    zc = jnp.zeros((1, ml_heads, dk, dv), F32)
    zn = jnp.zeros((1, ml_heads, dk), F32)
    zm = jnp.zeros((1, 1, ml_heads), F32)
    _, c_m, n_m, m_m = _mlstm(pbf_s, pf_s, offs, gate_bias, ml_gain, zc, zn, zm, 1, n_meta, n_meta, n_s, True,
                              ml_heads, dk, dv)
    lc_p = _row_tile(seq, MLSTM_CHUNK)
    ml_p, c_p, n_pr, m_p = _mlstm(pbf_p, pf_p, offs, gate_bias, ml_gain, c_m, n_m, m_m, batch, seq, lc_p, 0,
                                  True, ml_heads, dk, dv)
    ml_s, c_s, n_sm, m_s = _mlstm(pbf_s, pf_s, offs, gate_bias, ml_gain, state_C.astype(F32),
                                  state_n.astype(F32), state_m.astype(F32).reshape(dbatch, 1, ml_heads),
                                  dbatch, dseq, dseq, 0, False, ml_heads, dk, dv)

    y_s, w_oa, w_om = _out_proj(attn_s, ml_s, w_out, w_out, 1, xs, n_s, _row_tile(d, COL_TILE), emit_bf16=True)
    y_p, = _out_proj(attn_p, ml_p, w_oa, w_om, 0, xp, _row_tile(n_p, 2 * ROW_TILE), _row_tile(d, COL_TILE))

    meta_ckv = ckv_s[n_s:n_s + n_meta]
    meta_kpe = kpe_s[n_s:n_s + n_meta]
    ckv_prompt = jnp.concatenate([jnp.broadcast_to(meta_ckv[None], (batch, n_meta, kv_lora)),
                                  ckv_p.reshape(batch, seq, kv_lora)], axis=1)
    kpe_prompt = jnp.concatenate([jnp.broadcast_to(meta_kpe[None], (batch, n_meta, QK_ROPE)),
                                  kpe_p.reshape(batch, seq, QK_ROPE)], axis=1)
    return (y_p.reshape(batch, seq, d), y_s.reshape(dbatch, dseq, d), ckv_prompt, kpe_prompt,
            c_p, n_pr, m_p.reshape(batch, ml_heads),
            ckv_s[:n_s].reshape(dbatch, dseq, kv_lora), kpe_s[:n_s].reshape(dbatch, dseq, QK_ROPE),
            c_s, n_sm, m_s.reshape(dbatch, ml_heads))
```

```python
import functools
import math

import jax
import jax.numpy as jnp
from jax import lax
from jax.experimental import pallas as pl
from jax.experimental.pallas import tpu as pltpu

CHUNK = 64
EPS = 1e-6
ROPE_THETA = 10000.0
V_HEAD = 128
QK_NOPE = 128
QK_ROPE = 64
HALF_ROPE = QK_ROPE // 2
LANE = 128
MXU_EDGE = 256
QK_PAD = QK_NOPE + LANE
COL_TILE = 512
ROW_TILE = 512
NEG_BIG = -1e30
ATTN_TQ = 512
ATTN_HEADS = 4
MLSTM_CHUNK = 256
VMEM_LIMIT = 56 * 1024 * 1024

F32 = jnp.float32
BF16 = jnp.bfloat16


def _cparams(sem):
    return pltpu.CompilerParams(dimension_semantics=sem, vmem_limit_bytes=VMEM_LIMIT)


def _dot(a, b):
    return jnp.dot(a, b, preferred_element_type=F32)


def _dot_nt(a, b):
    return lax.dot_general(a, b, (((1,), (1,)), ((), ())), preferred_element_type=F32)


def _rms(x, n):
    return x * lax.rsqrt(jnp.sum(x * x, axis=-1, keepdims=True) * (1.0 / n) + EPS)


def _rope_lanes(x, cos, sin_signed):
    lane = lax.broadcasted_iota(jnp.int32, x.shape, 1)
    first_half = (lane & (QK_ROPE - 1)) < HALF_ROPE
    rot = jnp.where(first_half, pltpu.roll(x, LANE - HALF_ROPE, 1), pltpu.roll(x, HALF_ROPE, 1))
    return x * cos + rot * sin_signed


def _regroup_kernel(w_ref, *o_refs, plans):
    for o_ref, segments in zip(o_refs, plans):
        for src, n, dst in segments:
            o_ref[:, dst:dst + n] = w_ref[:, src:src + n].astype(o_ref.dtype)


def _regroup(w, plans, tr):
    rows, cols = w.shape
    widths = [sum(n for _, n, _ in segments) for segments in plans]
    for segments in plans:
        pos = 0
        for _, n, dst in sorted(segments, key=lambda s: s[2]):
            assert dst == pos
            pos += n
    return pl.pallas_call(
        functools.partial(_regroup_kernel, plans=tuple(tuple(s) for s in plans)),
        grid=(rows // tr,),
        in_specs=[pl.BlockSpec((tr, cols), lambda i: (i, 0))],
        out_specs=[pl.BlockSpec((tr, width), lambda i: (i, 0)) for width in widths],
        out_shape=[jax.ShapeDtypeStruct((rows, width), BF16) for width in widths],
        compiler_params=_cparams(("parallel",)),
        name="regroup",
    )(w)


def _stack_rows_kernel(*refs):
    *w_refs, o_ref = refs
    parts = [r[...] for r in w_refs]
    used = sum(p.shape[0] for p in parts)
    parts.append(jnp.zeros((o_ref.shape[0] - used, o_ref.shape[1]), parts[0].dtype))
    o_ref[...] = jnp.concatenate(parts, axis=0).astype(o_ref.dtype)


def _stack_rows(wt, segments, out_rows, tc):
    cols = wt.shape[1]
    return pl.pallas_call(
        _stack_rows_kernel,
        grid=(cols // tc,),
        in_specs=[pl.BlockSpec((pl.Element(n), pl.Element(tc)), functools.partial(lambda i, s: (s, i * tc), s=src))
                  for src, n in segments],
        out_specs=pl.BlockSpec((out_rows, tc), lambda i: (0, i)),
        out_shape=jax.ShapeDtypeStruct((out_rows, cols), BF16),
        compiler_params=_cparams(("parallel",)),
        name="stack_rows",
    )(*([wt] * len(segments)))


def _proj_regroup_kernel(x_ref, g_ref, w_ref, ws_ref, om_ref, o_ref, os_ref, hn_ref):
    @pl.when(pl.program_id(0) == 0)
    def _():
        x = x_ref[...]
        hn = (_rms(x, x.shape[-1]) * g_ref[...]).astype(BF16)
        hn_ref[...] = hn
        os_ref[...] = _dot_nt(hn, ws_ref[...])

    wb = w_ref[...].astype(BF16)
    om_ref[...] = wb
    o_ref[...] = _dot_nt(hn_ref[...], wb).astype(o_ref.dtype)


def _proj_regroup(x, gain, wt, segments, w_small):
    m, d = x.shape
    ns = w_small.shape[0]
    n = sum(sz for _, sz, _ in segments)
    merged = []
    for src, sz, dst in sorted(segments, key=lambda s: s[2]):
        if merged and merged[-1][0] + merged[-1][1] == src and merged[-1][2] + merged[-1][1] == dst:
            merged[-1] = (merged[-1][0], merged[-1][1] + sz, merged[-1][2])
        else:
            merged.append((src, sz, dst))
    segments = merged
    assert all(sz % COL_TILE == 0 and dst % COL_TILE == 0 for _, sz, dst in segments)
    align = functools.reduce(math.gcd, [src for src, _, _ in segments], COL_TILE)

    def src_row(j):
        r = j * COL_TILE
        out = 0
        for src, sz, dst in segments:
            out = jnp.where((r >= dst) & (r < dst + sz), src + r - dst, out)
        return out

    const = lambda j: (0, 0)
    return pl.pallas_call(
        _proj_regroup_kernel,
        grid=(n // COL_TILE,),
        in_specs=[
            pl.BlockSpec((m, d), const),
            pl.BlockSpec((1, d), const),
            pl.BlockSpec((pl.Element(COL_TILE), pl.Element(d)), lambda j: (pl.multiple_of(src_row(j), align), 0)),
            pl.BlockSpec((ns, d), const),
        ],
        out_specs=[
            pl.BlockSpec((COL_TILE, d), lambda j: (j, 0)),
            pl.BlockSpec((m, COL_TILE), lambda j: (0, j)),
            pl.BlockSpec((m, ns), const),
        ],
        out_shape=[jax.ShapeDtypeStruct((n, d), BF16), jax.ShapeDtypeStruct((m, n), BF16),
                   jax.ShapeDtypeStruct((m, ns), F32)],
        scratch_shapes=[pltpu.VMEM((m, d), BF16)],
        compiler_params=_cparams(("arbitrary",)),
        name="proj_regroup",
    )(x, gain, wt, w_small)


def _proj_kernel(x_ref, g_ref, w_ref, ws_ref, o_ref, os_ref, hn_ref, *, n_sub):
    j = pl.program_id(1)
    ts = x_ref.shape[0]

    @pl.when(j < n_sub)
    def _():
        x = x_ref[...]
        hn = (_rms(x, x.shape[-1]) * g_ref[...]).astype(BF16)
        hn_ref[pl.ds(pl.multiple_of(j * ts, ts), ts), :] = hn
        os_ref[...] = _dot_nt(hn, ws_ref[...])

    @pl.when(j >= n_sub)
    def _():
        o_ref[...] = _dot_nt(hn_ref[...], w_ref[...]).astype(o_ref.dtype)


def _proj(x, gain, w_main, w_small, tm, n_sub):
    m, d = x.shape
    n = w_main.shape[0]
    ns = w_small.shape[0]
    ts = tm // n_sub
    sub = lambda i, j: (i * n_sub + jnp.minimum(j, n_sub - 1), 0)
    feat = lambda j: jnp.maximum(j - n_sub, 0)
    return pl.pallas_call(
        functools.partial(_proj_kernel, n_sub=n_sub),
        grid=(m // tm, n_sub + n // COL_TILE),
        in_specs=[
            pl.BlockSpec((ts, d), sub),
            pl.BlockSpec((1, d), lambda i, j: (0, 0)),
            pl.BlockSpec((COL_TILE, d), lambda i, j: (feat(j), 0)),
            pl.BlockSpec((ns, d), lambda i, j: (0, 0)),
        ],
        out_specs=[
            pl.BlockSpec((tm, COL_TILE), lambda i, j: (i, feat(j))),
            pl.BlockSpec((ts, ns), sub),
        ],
        out_shape=[jax.ShapeDtypeStruct((m, n), BF16), jax.ShapeDtypeStruct((m, ns), F32)],
        scratch_shapes=[pltpu.VMEM((tm, d), BF16)],
        compiler_params=_cparams(("parallel", "arbitrary")),
        name="proj",
    )(x, gain, w_main, w_small)


def _latent_q_kernel(*refs, n_cq, kv_lora, heads_per_step):
    cq_refs = refs[:n_cq]
    (pf_ref, cos_ref, sin_ref, qan_ref, wqn_ref, wqp_ref, qng_ref, qrg_ref, kvg_ref, krg_ref,
     q_ref, ckv_ref, kpe_ref, kpep_ref, cqn_ref) = refs[n_cq:]
    cos = cos_ref[...]
    sin = sin_ref[...]
    lane = lax.broadcasted_iota(jnp.int32, cos.shape, 1)
    low = lane < QK_ROPE

    @pl.when(pl.program_id(1) == 0)
    def _():
        cq = jnp.concatenate([r[...].astype(F32) for r in cq_refs], axis=1)
        cqn_ref[...] = (_rms(cq, cq.shape[-1]) * qan_ref[...]).astype(BF16)
        pf = pf_ref[...]
        ckv_raw = pf[:, :kv_lora]
        ckv_ref[...] = _rms(ckv_raw, kv_lora) * kvg_ref[...]
        kraw = jnp.where(low, pf[:, kv_lora:kv_lora + LANE], 0.0)
        kpe = _rope_lanes(_rms(kraw, QK_ROPE) * krg_ref[...], cos, sin)
        kpe_ref[...] = kpe[:, :QK_ROPE]
        kpep_ref[...] = kpe.astype(BF16)

    cqn = cqn_ref[...]
    group = 4 if heads_per_step % 4 == 0 else 2
    for g0 in range(0, heads_per_step, group):
        qn = _dot(cqn, wqn_ref[:, g0 * QK_NOPE:(g0 + group) * QK_NOPE])
        qp = _dot(cqn, wqp_ref[:, g0 * QK_ROPE:(g0 + group) * QK_ROPE])
        for pair in range(group // 2):
            x = qp[:, pair * LANE:(pair + 1) * LANE]
            x2 = x * x
            ss_a = jnp.sum(jnp.where(low, x2, 0.0), axis=-1, keepdims=True)
            ss_b = jnp.sum(jnp.where(low, 0.0, x2), axis=-1, keepdims=True)
            scale = jnp.where(low, lax.rsqrt(ss_a * (1.0 / QK_ROPE) + EPS),
                              lax.rsqrt(ss_b * (1.0 / QK_ROPE) + EPS))
            r = _rope_lanes(x * scale * qrg_ref[...], cos, sin)
            halves = (jnp.where(low, r, 0.0), jnp.where(low, pltpu.roll(r, QK_ROPE, 1), 0.0))
            for h in (0, 1):
                base = (g0 + 2 * pair + h) * QK_PAD
                nope = qn[:, (2 * pair + h) * QK_NOPE:(2 * pair + h + 1) * QK_NOPE]
                q_ref[:, base:base + QK_NOPE] = (_rms(nope, QK_NOPE) * qng_ref[...]).astype(BF16)
                q_ref[:, base + QK_NOPE:base + QK_PAD] = halves[h].astype(BF16)


def _latent_q(p_bf, p_f32, cq_off, q_lora, kv_lora, cos, sin, qan, wqn, wqp, qng, qrg, kvg, krg, tm,
              heads_per_step):
    m = p_bf.shape[0]
    n_heads = wqn.shape[1] // QK_NOPE
    n_cq = q_lora // COL_TILE
    cq0 = cq_off // COL_TILE
    ns = p_f32.shape[1]
    wstep = heads_per_step * QK_PAD
    assert heads_per_step % 2 == 0
    row = lambda i, j: (i, 0)
    const = lambda i, j: (0, 0)
    pos_blocks = cos.shape[0] // tm
    pos = lambda i, j: (i % pos_blocks, 0)
    in_specs = [pl.BlockSpec((tm, COL_TILE), functools.partial(lambda i, j, c: (i, c), c=cq0 + c))
                for c in range(n_cq)]
    in_specs += [
        pl.BlockSpec((tm, ns), row),
        pl.BlockSpec((tm, LANE), pos),
        pl.BlockSpec((tm, LANE), pos),
        pl.BlockSpec((1, q_lora), const),
        pl.BlockSpec((q_lora, heads_per_step * QK_NOPE), lambda i, j: (0, j)),
        pl.BlockSpec((q_lora, heads_per_step * QK_ROPE), lambda i, j: (0, j)),
        pl.BlockSpec((1, LANE), const),
        pl.BlockSpec((1, LANE), const),
        pl.BlockSpec((1, kv_lora), const),
        pl.BlockSpec((1, LANE), const),
    ]
    return pl.pallas_call(
        functools.partial(_latent_q_kernel, n_cq=n_cq, kv_lora=kv_lora, heads_per_step=heads_per_step),
        grid=(m // tm, n_heads // heads_per_step),
        in_specs=in_specs,
        out_specs=[
            pl.BlockSpec((tm, wstep), lambda i, j: (i, j)),
            pl.BlockSpec((tm, kv_lora), row),
            pl.BlockSpec((tm, QK_ROPE), row),
            pl.BlockSpec((tm, LANE), row),
        ],
        out_shape=[
            jax.ShapeDtypeStruct((m, n_heads * QK_PAD), BF16),
            jax.ShapeDtypeStruct((m, kv_lora), F32),
            jax.ShapeDtypeStruct((m, QK_ROPE), F32),
            jax.ShapeDtypeStruct((m, LANE), BF16),
        ],
        scratch_shapes=[pltpu.VMEM((tm, q_lora), BF16)],
        compiler_params=_cparams(("parallel", "arbitrary")),
        name="latent_q",
    )(*([p_bf] * n_cq), p_f32, cos, sin, qan, wqn, wqp, qng, qrg, kvg, krg)


def _k_expand_kernel(ckv_ref, wk_ref, kng_ref, k_ref, *, n_heads):
    kf = _dot(ckv_ref[...].astype(BF16), wk_ref[...])
    for h in range(n_heads):
        hs = slice(h * QK_NOPE, (h + 1) * QK_NOPE)
        k_ref[:, hs] = (_rms(kf[:, hs], QK_NOPE) * kng_ref[...]).astype(BF16)


def _k_expand(ckv, wk, kng, tm):
    r, kv_lora = ckv.shape
    n_heads = wk.shape[1] // QK_NOPE
    return pl.pallas_call(
        functools.partial(_k_expand_kernel, n_heads=n_heads),
        grid=(r // tm,),
        in_specs=[
            pl.BlockSpec((tm, kv_lora), lambda i: (i, 0)),
            pl.BlockSpec((kv_lora, n_heads * QK_NOPE), lambda i: (0, 0)),
            pl.BlockSpec((1, LANE), lambda i: (0, 0)),
        ],
        out_specs=pl.BlockSpec((tm, n_heads * QK_NOPE), lambda i: (i, 0)),
        out_shape=jax.ShapeDtypeStruct((r, n_heads * QK_NOPE), BF16),
        compiler_params=_cparams(("parallel",)),
        name="k_expand",
    )(ckv, wk, kng)


def _kv_expand_kernel(ckv_ref, kpep_ref, wk_ref, wv_ref, kng_ref, k_ref, v_ref, *, n_heads):
    c = ckv_ref[...].astype(BF16)
    kf = _dot(c, wk_ref[...])
    v_ref[...] = _dot(c, wv_ref[...]).astype(BF16)
    kpep = kpep_ref[...]
    for h in range(n_heads):
        kn = _rms(kf[:, h * QK_NOPE:(h + 1) * QK_NOPE], QK_NOPE) * kng_ref[...]
        k_ref[:, h * QK_PAD:h * QK_PAD + QK_NOPE] = kn.astype(BF16)
        k_ref[:, h * QK_PAD + QK_NOPE:(h + 1) * QK_PAD] = kpep


def _kv_expand(ckv, kpep, wk, wv, kng, tm):
    r, kv_lora = ckv.shape
    n_heads = wk.shape[1] // QK_NOPE
    row = lambda i: (i, 0)
    const = lambda i: (0, 0)
    return pl.pallas_call(
        functools.partial(_kv_expand_kernel, n_heads=n_heads),
        grid=(r // tm,),
        in_specs=[
            pl.BlockSpec((tm, kv_lora), row),
            pl.BlockSpec((tm, LANE), row),
            pl.BlockSpec((kv_lora, n_heads * QK_NOPE), const),
            pl.BlockSpec((kv_lora, n_heads * V_HEAD), const),
            pl.BlockSpec((1, LANE), const),
        ],
        out_specs=[
            pl.BlockSpec((tm, n_heads * QK_PAD), row),
            pl.BlockSpec((tm, n_heads * V_HEAD), row),
        ],
        out_shape=[
            jax.ShapeDtypeStruct((r, n_heads * QK_PAD), BF16),
            jax.ShapeDtypeStruct((r, n_heads * V_HEAD), BF16),
        ],
        compiler_params=_cparams(("parallel",)),
        name="kv_expand",
    )(ckv, kpep, wk, wv, kng)


def _lane_chunk_sum(p):
    out = p[:, :LANE]
    for c in range(1, p.shape[1] // LANE):
        out = out + p[:, c * LANE:(c + 1) * LANE]
    return out


def _gated(lvec, acc, za):
    za = za.astype(F32)
    return acc / jnp.sum(lvec, axis=-1, keepdims=True) * (za * jax.nn.sigmoid(za))


def _softmax_scale():
    return float((QK_NOPE + QK_ROPE) ** -0.5 * 1.4426950408889634)


def _attn_prompt_block(n_full, q_ref, k_ref, km_ref, za_ref, o_ref, s_ref, p_ref, va_ref, vma_ref, *,
                       tq, heads, c, n_meta):
    td = min(tq, MXU_EDGE)
    pieces = [(i * tq, tq, 0, False) for i in range(n_full)]
    pieces += [(n_full * tq + j * td, td, j * td, True) for j in range(tq // td)]

    def upd(full, r0, fn):
        return fn(full) if r0 == 0 else jnp.concatenate([full[:r0], fn(full[r0:])], axis=0)

    def scores(h):
        qs = slice(h * QK_PAD, (h + 1) * QK_PAD)
        q = q_ref[:, qs]
        col = lax.broadcasted_iota(jnp.int32, (tq, LANE), 1)
        t = jnp.where(col < n_meta, _dot_nt(q, km_ref[:, qs]) * c, -jnp.inf)
        s_ref[h, :, :LANE] = t
        mx = t
        off = LANE
        for start, rows, r0, masked in pieces:
            t = _dot_nt(q[r0:], k_ref[start:start + rows, qs]) * c
            if masked:
                rq = lax.broadcasted_iota(jnp.int32, t.shape, 0) // CHUNK
                ck = lax.broadcasted_iota(jnp.int32, t.shape, 1) // CHUNK
                t = jnp.where(rq >= ck, t, -jnp.inf)
            s_ref[h, r0:, off:off + rows] = t
            tmax = t[:, :LANE]
            for ch in range(1, rows // LANE):
                tmax = jnp.maximum(tmax, t[:, ch * LANE:(ch + 1) * LANE])
            mx = upd(mx, r0, lambda a: jnp.maximum(a, tmax))
            off += rows
        return jnp.max(mx, axis=-1, keepdims=True)

    def values(h, m):
        vs = slice(h * V_HEAD, (h + 1) * V_HEAD)
        va = slice(2 * h * V_HEAD, 2 * (h + 1) * V_HEAD)
        p_ref[h, :, :LANE] = jnp.exp2(s_ref[h, :, :LANE] - m).astype(BF16)
        off = LANE
        for start, rows, r0, _ in pieces:
            p_ref[h, r0:, off:off + rows] = jnp.exp2(s_ref[h, r0:, off:off + rows] - m[r0:]).astype(BF16)
            off += rows
        acc = _dot(p_ref[h, :, :LANE], vma_ref[:, va])
        n_main = n_full * tq
        if n_main:
            acc = acc + _dot(p_ref[h, :, LANE:LANE + n_main], va_ref[:n_main, va])
        off = LANE + n_main
        for start, rows, r0, _ in pieces[n_full:]:
            pv = _dot(p_ref[h, r0:, off:off + rows], va_ref[start:start + rows, va])
            acc = upd(acc, r0, lambda a: a + pv)
            off += rows
        za = za_ref[:, vs].astype(F32)
        out = acc[:, :V_HEAD] / acc[:, V_HEAD:V_HEAD + 1] * (za * jax.nn.sigmoid(za))
        o_ref[:, vs] = out.astype(o_ref.dtype)

    maxes = [scores(h) for h in range(heads)]
    for h in range(heads):
        values(h, maxes[h])


def _attn_prompt_kernel(q_ref, k_ref, v_ref, km_ref, vm_ref, za_ref, o_ref, s_ref, p_ref, va_ref, vma_ref, *,
                        nqb, heads, **kw):
    qi = pl.program_id(2)

    @pl.when(qi == 0)
    def _():
        for h in range(heads):
            vs = slice(h * V_HEAD, (h + 1) * V_HEAD)
            va_ref[:, 2 * h * V_HEAD:(2 * h + 1) * V_HEAD] = v_ref[:, vs]
            va_ref[:, (2 * h + 1) * V_HEAD:(2 * h + 2) * V_HEAD] = jnp.ones((va_ref.shape[0], V_HEAD), BF16)
            vma_ref[:, 2 * h * V_HEAD:(2 * h + 1) * V_HEAD] = vm_ref[:, vs]
            vma_ref[:, (2 * h + 1) * V_HEAD:(2 * h + 2) * V_HEAD] = jnp.ones((vma_ref.shape[0], V_HEAD), BF16)

    for n_full in range(nqb):
        pl.when(qi == n_full)(
            functools.partial(_attn_prompt_block, n_full, q_ref, k_ref, km_ref, za_ref, o_ref, s_ref, p_ref, va_ref,
                              vma_ref, heads=heads, **kw))


def _attn_prompt(q, k, v, k_small, v_small, meta_blk, p_bf, za_off, batch, seq, tq, heads, n_meta):
    n_heads = q.shape[1] // QK_PAD
    nqb = seq // tq
    za0 = za_off // (V_HEAD * heads)
    assert za_off % (V_HEAD * heads) == 0 and n_heads % heads == 0 and tq % CHUNK == 0
    return pl.pallas_call(
        functools.partial(_attn_prompt_kernel, nqb=nqb, tq=tq, heads=heads, c=_softmax_scale(), n_meta=n_meta),
        grid=(batch, n_heads // heads, nqb),
        in_specs=[
            pl.BlockSpec((tq, heads * QK_PAD), lambda b, h, i: (b * nqb + i, h)),
            pl.BlockSpec((seq, heads * QK_PAD), lambda b, h, i: (b, h)),
            pl.BlockSpec((seq, heads * V_HEAD), lambda b, h, i: (b, h)),
            pl.BlockSpec((LANE, heads * QK_PAD), lambda b, h, i: (meta_blk, h)),
            pl.BlockSpec((LANE, heads * V_HEAD), lambda b, h, i: (meta_blk, h)),
            pl.BlockSpec((tq, heads * V_HEAD), lambda b, h, i: (b * nqb + i, za0 + h)),
        ],
        out_specs=pl.BlockSpec((tq, heads * V_HEAD), lambda b, h, i: (b * nqb + i, h)),
        out_shape=jax.ShapeDtypeStruct((batch * seq, n_heads * V_HEAD), BF16),
        scratch_shapes=[pltpu.VMEM((heads, tq, LANE + seq), F32), pltpu.VMEM((heads, tq, LANE + seq), BF16),
                        pltpu.VMEM((seq, 2 * heads * V_HEAD), BF16), pltpu.VMEM((LANE, 2 * heads * V_HEAD), BF16)],
        compiler_params=_cparams(("parallel", "parallel", "arbitrary")),
        name="attn_prompt",
    )(q, k, v, k_small, v_small, p_bf)


def _attn_sample_kernel(q_ref, kc_ref, rc_ref, cc_ref, ks_ref, cs_ref, wv_ref, za_ref, o_ref, pc_ref, ps_ref, li_ref, *,
                        heads, lq, c, n_meta, meta_row):
    b = pl.program_id(0)
    col = lax.broadcasted_iota(jnp.int32, (lq, ks_ref.shape[0]), 1)
    own = (col >= b * lq) & (col < (b + 1) * lq)
    meta = (col >= meta_row) & (col < meta_row + n_meta)
    visible = own | meta
    q_rope = jnp.concatenate([q_ref[:, h * QK_PAD + QK_NOPE:(h + 1) * QK_PAD] for h in range(heads)], axis=0)
    rope_cache = _dot_nt(q_rope, rc_ref[...])
    for h in range(heads):
        qs = slice(h * QK_PAD, (h + 1) * QK_PAD)
        rows = slice(h * lq, (h + 1) * lq)
        q = q_ref[:, qs]
        t_cache = (_dot_nt(q[:, :QK_NOPE], kc_ref[:, h * QK_NOPE:(h + 1) * QK_NOPE]) + rope_cache[rows]) * c
        t_new = jnp.where(visible, _dot_nt(q, ks_ref[:, qs]) * c, -jnp.inf)
        m = jnp.maximum(jnp.max(t_cache, axis=-1, keepdims=True), jnp.max(t_new, axis=-1, keepdims=True))
        p_cache = jnp.exp2(t_cache - m)
        p_new = jnp.exp2(t_new - m)
        lsum = jnp.sum(_lane_chunk_sum(p_cache) + _lane_chunk_sum(p_new), axis=-1, keepdims=True)
        pc_ref[rows, :] = p_cache.astype(BF16)
        ps_ref[rows, :] = p_new.astype(BF16)
        li_ref[rows, :] = jnp.broadcast_to(1.0 / lsum, (lq, LANE))
    mixed = (_dot(pc_ref[...], cc_ref[...].astype(BF16)) + _dot(ps_ref[...], cs_ref[...].astype(BF16)))
    mixed = (mixed * li_ref[:, 0:1]).astype(BF16)
    for h in range(heads):
        vs = slice(h * V_HEAD, (h + 1) * V_HEAD)
        za = za_ref[:, vs].astype(F32)
        out = _dot(mixed[h * lq:(h + 1) * lq], wv_ref[:, vs]) * (za * jax.nn.sigmoid(za))
        o_ref[:, vs] = out.astype(o_ref.dtype)


def _attn_sample(q, kc, rope_cache, ckv_cache, ks, ckv_small, wv, p_bf, za_off, batch, lq, past, n_meta):
    n_heads = q.shape[1] // QK_PAD
    rows = ks.shape[0]
    kv_lora = ckv_cache.shape[1]
    za0 = za_off // (V_HEAD * n_heads)
    assert za_off % (V_HEAD * n_heads) == 0
    const = lambda b: (0, 0)
    return pl.pallas_call(
        functools.partial(_attn_sample_kernel, heads=n_heads, lq=lq, c=_softmax_scale(), n_meta=n_meta,
                          meta_row=batch * lq),
        grid=(batch,),
        in_specs=[
            pl.BlockSpec((lq, n_heads * QK_PAD), lambda b: (b, 0)),
            pl.BlockSpec((past, n_heads * QK_NOPE), lambda b: (b, 0)),
            pl.BlockSpec((past, LANE), lambda b: (b, 0)),
            pl.BlockSpec((past, kv_lora), lambda b: (b, 0)),
            pl.BlockSpec((rows, n_heads * QK_PAD), const),
            pl.BlockSpec((rows, kv_lora), const),
            pl.BlockSpec((kv_lora, n_heads * V_HEAD), const),
            pl.BlockSpec((lq, n_heads * V_HEAD), lambda b: (b, za0)),
        ],
        out_specs=pl.BlockSpec((lq, n_heads * V_HEAD), lambda b: (b, 0)),
        out_shape=jax.ShapeDtypeStruct((batch * lq, n_heads * V_HEAD), BF16),
        scratch_shapes=[pltpu.VMEM((n_heads * lq, past), BF16), pltpu.VMEM((n_heads * lq, rows), BF16),
                        pltpu.VMEM((n_heads * lq, LANE), F32)],
        compiler_params=_cparams(("parallel",)),
        name="attn_sample",
    )(q, kc, rope_cache, ckv_cache, ks, ckv_small, wv, p_bf)


def _log_sigmoid(x):
    return jnp.minimum(x, 0.0) - jnp.log1p(jnp.exp(-jnp.abs(x)))


def _pad_rows(a, rows):
    if a.shape[0] == rows:
        return a
    return jnp.concatenate([a, jnp.zeros((rows - a.shape[0], a.shape[1]), a.dtype)], axis=0)


def _mlstm_kernel(q_ref, k_ref, v_ref, mo_ref, zb_ref, g_ref, gb_ref, gain_ref, c0_ref, n0_ref, m0_ref,
                  h_ref, c_ref, n_ref, m_ref, *, n_heads, lc, lp, dk, dv):
    @pl.when(pl.program_id(1) == 0)
    def _():
        c_ref[...] = c0_ref[...]
        n_ref[...] = n0_ref[...]
        m_ref[...] = m0_ref[...]

    hi = lax.Precision.HIGHEST
    gates = _pad_rows(g_ref[...] + gb_ref[...], lp)
    gates_t = gates.T
    t_col = lax.broadcasted_iota(jnp.int32, (lp, 1), 0)
    t_row = lax.broadcasted_iota(jnp.int32, (1, lp), 1)
    r_idx = lax.broadcasted_iota(jnp.int32, (lp, lp), 0)
    c_idx = lax.broadcasted_iota(jnp.int32, (lp, lp), 1)
    causal = r_idx >= c_idx
    lower = causal.astype(F32)
    upper = (c_idx >= r_idx).astype(F32)
    b_cols = jnp.dot(lower, jnp.where(t_col < lc, _log_sigmoid(gates), 0.0), precision=hi,
                     preferred_element_type=F32)
    b_rows = jnp.dot(jnp.where(t_row < lc, _log_sigmoid(gates_t), 0.0), upper, precision=hi,
                     preferred_element_type=F32)

    for h in range(n_heads):
        li = QK_ROPE + h
        lf_lane = QK_ROPE + n_heads + h
        ig_col = jnp.where(t_col < lc, gates[:, li:li + 1], NEG_BIG)
        ig_row = jnp.where(t_row < lc, gates_t[li:li + 1, :], NEG_BIG)
        b_col = b_cols[:, lf_lane:lf_lane + 1]
        b_row = b_rows[lf_lane:lf_lane + 1, :]
        m0 = m_ref[0, 0:1, h:h + 1]
        d = jnp.where(causal, b_col - b_row + ig_row, -jnp.inf)
        a_col = b_col + m0
        m = jnp.maximum(a_col, jnp.max(d, axis=-1, keepdims=True))
        w_inter = jnp.exp(a_col - m)

        q = _pad_rows(q_ref[:, h * dk:(h + 1) * dk], lp)
        k = _pad_rows(k_ref[:, h * dk:(h + 1) * dk], lp) * (dk ** -0.5)
        v = _pad_rows(v_ref[:, h * dv:(h + 1) * dv], lp)
        qk = _dot_nt(q, k) * jnp.exp(d - m)
        c_old = c_ref[0, h]
        n_old = n_ref[0, h:h + 1, :]
        num = w_inter * _dot(q, c_old.astype(BF16)) + _dot(qk.astype(BF16), v)
        den = (w_inter * jnp.sum(q.astype(F32) * n_old, axis=-1, keepdims=True)
               + jnp.sum(qk, axis=-1, keepdims=True))
        denc = jnp.maximum(jnp.abs(den), jnp.exp(-m))

        b_last = b_col[lp - 1:lp, :]
        g_row = b_last - b_row + ig_row
        g_col = b_last - b_col + ig_col
        m_new = jnp.maximum(b_last + m0, jnp.max(g_row, axis=-1, keepdims=True))
        decay = jnp.exp(b_last + m0 - m_new)
        kw = k.astype(F32) * jnp.exp(g_col - m_new)
        c_ref[0, h] = decay * c_old + _dot(kw.T.astype(BF16), v)
        n_ref[0, h:h + 1, :] = decay * n_old + jnp.sum(kw, axis=0, keepdims=True)
        m_ref[0, 0:1, h:h + 1] = m_new

        num = num[:lc]
        denc = denc[:lc]
        normed = num * lax.rsqrt(jnp.sum(num * num, axis=-1, keepdims=True) * (1.0 / dv) + EPS * denc * denc)
        mo = mo_ref[:, h * dv:(h + 1) * dv].astype(F32)
        zb = zb_ref[:, h * dv:(h + 1) * dv].astype(F32)
        out = jax.nn.sigmoid(mo) * (normed * gain_ref[:, h * dv:(h + 1) * dv])
        h_ref[:, h * dv:(h + 1) * dv] = (out * (zb * jax.nn.sigmoid(zb))).astype(h_ref.dtype)


def _mlstm(p_bf, p_f32, offs, gate_bias, gain, c0, n0, m0, batch, seq, lc, row0, share_state, n_heads, dk, dv):
    lp = -(-lc // LANE) * LANE
    nch = seq // lc
    blk0 = row0 // lc
    wqk = n_heads * dk
    wv = n_heads * dv
    gate_blk = p_f32.shape[1] // LANE - 1
    rows = lambda col: (lambda b, c: (blk0 + b * nch + c, col))
    state = (lambda b, c: (0, 0, 0, 0)) if share_state else (lambda b, c: (b, 0, 0, 0))
    state3 = (lambda b, c: (0, 0, 0)) if share_state else (lambda b, c: (b, 0, 0))
    out_rows = batch * seq
    return pl.pallas_call(
        functools.partial(_mlstm_kernel, n_heads=n_heads, lc=lc, lp=lp, dk=dk, dv=dv),
        grid=(batch, nch),
        in_specs=[
            pl.BlockSpec((lc, wqk), rows(offs["mq"] // wqk)),
            pl.BlockSpec((lc, wqk), rows(offs["mk"] // wqk)),
            pl.BlockSpec((lc, wv), rows(offs["mv"] // wv)),
            pl.BlockSpec((lc, wv), rows(offs["mo"] // wv)),
            pl.BlockSpec((lc, wv), rows(offs["zb"] // wv)),
            pl.BlockSpec((lc, LANE), rows(gate_blk)),
            pl.BlockSpec((1, LANE), lambda b, c: (0, 0)),
            pl.BlockSpec((1, wv), lambda b, c: (0, 0)),
            pl.BlockSpec((1, n_heads, dk, dv), state),
            pl.BlockSpec((1, n_heads, dk), state3),
            pl.BlockSpec((1, 1, n_heads), state3),
        ],
        out_specs=[
            pl.BlockSpec((lc, wv), lambda b, c: (b * nch + c, 0)),
            pl.BlockSpec((1, n_heads, dk, dv), lambda b, c: (b, 0, 0, 0)),
            pl.BlockSpec((1, n_heads, dk), lambda b, c: (b, 0, 0)),
            pl.BlockSpec((1, 1, n_heads), lambda b, c: (b, 0, 0)),
        ],
        out_shape=[
            jax.ShapeDtypeStruct((out_rows, wv), BF16),
            jax.ShapeDtypeStruct((batch, n_heads, dk, dv), F32),
            jax.ShapeDtypeStruct((batch, n_heads, dk), F32),
            jax.ShapeDtypeStruct((batch, 1, n_heads), F32),
        ],
        compiler_params=_cparams(("parallel", "arbitrary")),
        name="mlstm",
    )(p_bf, p_bf, p_bf, p_bf, p_bf, p_f32, gate_bias, gain, c0, n0, m0)


def _out_proj_kernel(a_ref, m_ref, wa_ref, wm_ref, x_ref, o_ref, *wb_refs):
    wa = wa_ref[...].astype(BF16)
    wm = wm_ref[...].astype(BF16)
    o_ref[...] = x_ref[...] + _dot(a_ref[...], wa) + _dot(m_ref[...], wm)
    if wb_refs:
        wb_refs[0][...] = wa
        wb_refs[1][...] = wm


def _out_proj(a, ml, w_a, w_m, wm_blk, x, tm, tn, emit_bf16=False):
    rows, wa = a.shape
    wm = ml.shape[1]
    d = w_a.shape[1]
    assert wa == wm and (not emit_bf16 or rows == tm)
    out_specs = [pl.BlockSpec((tm, tn), lambda i, j: (i, j))]
    out_shape = [jax.ShapeDtypeStruct((rows, d), F32)]
    if emit_bf16:
        out_specs += [pl.BlockSpec((wa, tn), lambda i, j: (0, j)), pl.BlockSpec((wm, tn), lambda i, j: (0, j))]
        out_shape += [jax.ShapeDtypeStruct((wa, d), BF16), jax.ShapeDtypeStruct((wm, d), BF16)]
    return pl.pallas_call(
        _out_proj_kernel,
        grid=(rows // tm, d // tn),
        in_specs=[
            pl.BlockSpec((tm, wa), lambda i, j: (i, 0)),
            pl.BlockSpec((tm, wm), lambda i, j: (i, 0)),
            pl.BlockSpec((wa, tn), lambda i, j: (0, j)),
            pl.BlockSpec((wm, tn), lambda i, j: (wm_blk, j)),
            pl.BlockSpec((tm, tn), lambda i, j: (i, j)),
        ],
        out_specs=out_specs,
        out_shape=out_shape,
        compiler_params=_cparams(("parallel", "arbitrary")),
        name="out_proj",
    )(a, ml, w_a, w_m, x)


def _rope_tables(pos):
    inv_freq = ROPE_THETA ** (-jnp.arange(HALF_ROPE, dtype=F32) / HALF_ROPE)
    ang = pos.astype(F32)[:, None] * inv_freq[None, :]
    cos, sin = jnp.cos(ang), jnp.sin(ang)
    return jnp.concatenate([cos, cos, cos, cos], axis=1), jnp.concatenate([-sin, sin, -sin, sin], axis=1)


def _pad_lanes(vec):
    return jnp.pad(vec.astype(F32), (0, LANE - vec.shape[0]))[None, :]


def _row_tile(rows, target):
    t = min(rows, target)
    while rows % t:
        t //= 2
    return t


def kernel(x_prompt, x_sample, cache_ckv, cache_kpe, state_C, state_n, state_m, meta_tokens, norm_gain, w_in,
           b_igate, b_fgate, q_a_norm, w_q_up, q_nope_norm, q_rope_norm, kv_a_norm, k_rope_norm, w_kv_up,
           k_nope_norm, ml_out_norm, w_out):
    batch, seq, d = x_prompt.shape
    dbatch, dseq, _ = x_sample.shape
    past = cache_ckv.shape[1]
    n_meta = meta_tokens.shape[0]
    q_lora = q_a_norm.shape[0]
    kv_lora = kv_a_norm.shape[0]
    ml_heads, dv = ml_out_norm.shape
    dk = state_n.shape[-1]
    mla_heads = w_kv_up.shape[1] // (QK_NOPE + V_HEAD)
    mla_w = mla_heads * V_HEAD
    ml_w = ml_heads * dv
    mqk_w = ml_heads * dk
    assert q_lora % COL_TILE == 0 and kv_lora % LANE == 0 and 2 * ml_heads <= LANE - QK_ROPE
    assert mla_w == ml_w and w_out.shape[0] == mla_w + ml_w

    o_cq = 0
    o_ckv = o_cq + q_lora
    o_kpe = o_ckv + kv_lora
    o_mq = o_kpe + QK_ROPE
    o_mk = o_mq + mqk_w
    o_mv = o_mk + mqk_w
    o_mo = o_mv + ml_w
    o_mi = o_mo + ml_w
    o_mf = o_mi + ml_heads
    o_za = o_mf + ml_heads
    o_zb = o_za + mla_w

    offs = {"mq": 0, "mk": mqk_w, "mv": 2 * mqk_w, "mo": 2 * mqk_w + ml_w, "za": 2 * mqk_w + 2 * ml_w,
            "zb": 2 * mqk_w + 2 * ml_w + mla_w, "cq": 2 * mqk_w + 3 * ml_w + mla_w}
    gate_pad = LANE - QK_ROPE - 2 * ml_heads
    w_in_t = w_in.T
    main_segments = [(o_mq, mqk_w, offs["mq"]), (o_mk, mqk_w, offs["mk"]), (o_mv, ml_w, offs["mv"]),
                     (o_mo, ml_w, offs["mo"]), (o_za, mla_w, offs["za"]), (o_zb, ml_w, offs["zb"]),
                     (o_cq, q_lora, offs["cq"])]
    w_small = _stack_rows(w_in_t, [(o_ckv, kv_lora), (o_kpe, QK_ROPE), (o_mi, 2 * ml_heads)], kv_lora + LANE,
                          _row_tile(d, COL_TILE))
    gate_bias = jnp.concatenate([jnp.zeros((QK_ROPE,), F32), b_igate.astype(F32), b_fgate.astype(F32),
                                 jnp.zeros((gate_pad,), F32)])[None, :]
    qk_w = QK_NOPE + QK_ROPE
    kv_w = QK_NOPE + V_HEAD
    wqn, wqp = _regroup(w_q_up, [[(h * qk_w, QK_NOPE, h * QK_NOPE) for h in range(mla_heads)],
                                 [(h * qk_w + QK_NOPE, QK_ROPE, h * QK_ROPE) for h in range(mla_heads)]],
                        _row_tile(q_lora, ROW_TILE))
    wk, wv = _regroup(w_kv_up, [[(h * kv_w, QK_NOPE, h * QK_NOPE) for h in range(mla_heads)],
                                [(h * kv_w + QK_NOPE, V_HEAD, h * V_HEAD) for h in range(mla_heads)]], kv_lora)
    gain_row = norm_gain.astype(F32)[None, :]
    qan = q_a_norm.astype(F32)[None, :]
    kvg = kv_a_norm.astype(F32)[None, :]
    qng = q_nope_norm.astype(F32)[None, :]
    kng = k_nope_norm.astype(F32)[None, :]
    qrg = jnp.tile(q_rope_norm.astype(F32), LANE // QK_ROPE)[None, :]
    krg = _pad_lanes(k_rope_norm)
    ml_gain = ml_out_norm.astype(F32).reshape(1, ml_w)

    n_p = batch * seq
    n_s = dbatch * dseq
    assert n_s % LANE == 0 and n_meta <= LANE and n_meta % 16 == 0
    rows_s = n_s + n_meta
    xp = x_prompt.reshape(n_p, d)
    xs = jnp.concatenate([x_sample.reshape(n_s, d), meta_tokens.astype(x_sample.dtype)], axis=0)

    pos_s = jnp.concatenate([jnp.tile(n_meta + past + jnp.arange(dseq), dbatch), jnp.arange(n_meta)])
    cos_p, sin_p = _rope_tables(n_meta + jnp.arange(seq))
    cos_s, sin_s = _rope_tables(pos_s)

    tm_p = _row_tile(seq, ROW_TILE)
    proj_sub = 2 if n_p % (2 * tm_p) == 0 else 1
    w_main, pbf_s, pf_s = _proj_regroup(xs, gain_row, w_in_t, main_segments, w_small)
    pbf_p, pf_p = _proj(xp, gain_row, w_main, w_small, proj_sub * tm_p, proj_sub)

    hps = mla_heads
    q_p, ckv_p, kpe_p, kpep_p = _latent_q(pbf_p, pf_p, offs["cq"], q_lora, kv_lora, cos_p, sin_p, qan, wqn, wqp,
                                          qng, qrg, kvg, krg, tm_p, hps)
    q_s, ckv_s, kpe_s, kpep_s = _latent_q(pbf_s, pf_s, offs["cq"], q_lora, kv_lora, cos_s, sin_s, qan, wqn, wqp,
                                          qng, qrg, kvg, krg, rows_s, hps)

    small_rows = n_s + LANE
    meta_blk = n_s // LANE
    tail = ((0, small_rows - rows_s), (0, 0))
    k_p, v_p = _kv_expand(ckv_p, kpep_p, wk, wv, kng, tm_p)
    ckv_small = jnp.pad(ckv_s, tail)
    k_s, v_s = _kv_expand(ckv_small, jnp.pad(kpep_s, tail), wk, wv, kng, small_rows)
    cache_rows = dbatch * past
    cache_kpep = jnp.pad(cache_kpe.reshape(cache_rows, QK_ROPE), ((0, 0), (0, LANE - QK_ROPE))).astype(BF16)
    ckv_cache = cache_ckv.reshape(cache_rows, kv_lora).astype(F32)
    k_c = _k_expand(ckv_cache, wk, kng, _row_tile(cache_rows, ROW_TILE))

    tq = _row_tile(seq, ATTN_TQ)
    attn_p = _attn_prompt(q_p, k_p, v_p, k_s, v_s, meta_blk, pbf_p, offs["za"], batch, seq, tq,
                          min(ATTN_HEADS, mla_heads), n_meta)
    attn_s = _attn_sample(q_s, k_c, cache_kpep, ckv_cache, k_s, ckv_small, wv, pbf_s, offs["za"], dbatch, dseq, past,
                          n_meta)

    zc = jnp.zeros((1, ml_heads, dk, dv), F32)
    zn = jnp.zeros((1, ml_heads, dk), F32)
    zm = jnp.zeros((1, 1, ml_heads), F32)
    _, c_m, n_m, m_m = _mlstm(pbf_s, pf_s, offs, gate_bias, ml_gain, zc, zn, zm, 1, n_meta, n_meta, n_s, True,
                              ml_heads, dk, dv)
    lc_p = _row_tile(seq, MLSTM_CHUNK)
    ml_p, c_p, n_pr, m_p = _mlstm(pbf_p, pf_p, offs, gate_bias, ml_gain, c_m, n_m, m_m, batch, seq, lc_p, 0,
                                  True, ml_heads, dk, dv)
    ml_s, c_s, n_sm, m_s = _mlstm(pbf_s, pf_s, offs, gate_bias, ml_gain, state_C.astype(F32),
                                  state_n.astype(F32), state_m.astype(F32).reshape(dbatch, 1, ml_heads),
                                  dbatch, dseq, dseq, 0, False, ml_heads, dk, dv)

    y_s, w_oa, w_om = _out_proj(attn_s, ml_s, w_out, w_out, 1, xs, n_s, _row_tile(d, COL_TILE), emit_bf16=True)
    y_p, = _out_proj(attn_p, ml_p, w_oa, w_om, 0, xp, _row_tile(n_p, 2 * ROW_TILE), _row_tile(d, COL_TILE))

    meta_ckv = ckv_s[n_s:n_s + n_meta]
    meta_kpe = kpe_s[n_s:n_s + n_meta]
    ckv_prompt = jnp.concatenate([jnp.broadcast_to(meta_ckv[None], (batch, n_meta, kv_lora)),
                                  ckv_p.reshape(batch, seq, kv_lora)], axis=1)
    kpe_prompt = jnp.concatenate([jnp.broadcast_to(meta_kpe[None], (batch, n_meta, QK_ROPE)),
                                  kpe_p.reshape(batch, seq, QK_ROPE)], axis=1)
    return (y_p.reshape(batch, seq, d), y_s.reshape(dbatch, dseq, d), ckv_prompt, kpe_prompt,
            c_p, n_pr, m_p.reshape(batch, ml_heads),
            ckv_s[:n_s].reshape(dbatch, dseq, kv_lora), kpe_s[:n_s].reshape(dbatch, dseq, QK_ROPE),
            c_s, n_sm, m_s.reshape(dbatch, ml_heads))
```

```python
import functools
import math

import jax
import jax.numpy as jnp
from jax import lax
from jax.experimental import pallas as pl
from jax.experimental.pallas import tpu as pltpu

CHUNK = 64
EPS = 1e-6
ROPE_THETA = 10000.0
V_HEAD = 128
QK_NOPE = 128
QK_ROPE = 64
HALF_ROPE = QK_ROPE // 2
LANE = 128
MXU_EDGE = 256
QK_PAD = QK_NOPE + LANE
COL_TILE = 512
ROW_TILE = 512
NEG_BIG = -1e30
ATTN_TQ = 512
ATTN_HEADS = 4
MLSTM_CHUNK = 256
VMEM_LIMIT = 56 * 1024 * 1024

F32 = jnp.float32
BF16 = jnp.bfloat16


def _cparams(sem):
    return pltpu.CompilerParams(dimension_semantics=sem, vmem_limit_bytes=VMEM_LIMIT)


def _dot(a, b):
    return jnp.dot(a, b, preferred_element_type=F32)


def _dot_nt(a, b):
    return lax.dot_general(a, b, (((1,), (1,)), ((), ())), preferred_element_type=F32)


def _rms(x, n):
    return x * lax.rsqrt(jnp.sum(x * x, axis=-1, keepdims=True) * (1.0 / n) + EPS)


def _rope_lanes(x, cos, sin_signed):
    lane = lax.broadcasted_iota(jnp.int32, x.shape, 1)
    first_half = (lane & (QK_ROPE - 1)) < HALF_ROPE
    rot = jnp.where(first_half, pltpu.roll(x, LANE - HALF_ROPE, 1), pltpu.roll(x, HALF_ROPE, 1))
    return x * cos + rot * sin_signed


def _regroup_kernel(w_ref, *o_refs, plans):
    for o_ref, segments in zip(o_refs, plans):
        for src, n, dst in segments:
            o_ref[:, dst:dst + n] = w_ref[:, src:src + n].astype(o_ref.dtype)


def _regroup(w, plans, tr):
    rows, cols = w.shape
    widths = [sum(n for _, n, _ in segments) for segments in plans]
    for segments in plans:
        pos = 0
        for _, n, dst in sorted(segments, key=lambda s: s[2]):
            assert dst == pos
            pos += n
    return pl.pallas_call(
        functools.partial(_regroup_kernel, plans=tuple(tuple(s) for s in plans)),
        grid=(rows // tr,),
        in_specs=[pl.BlockSpec((tr, cols), lambda i: (i, 0))],
        out_specs=[pl.BlockSpec((tr, width), lambda i: (i, 0)) for width in widths],
        out_shape=[jax.ShapeDtypeStruct((rows, width), BF16) for width in widths],
        compiler_params=_cparams(("parallel",)),
        name="regroup",
    )(w)


def _stack_rows_kernel(*refs):
    *w_refs, o_ref = refs
    parts = [r[...] for r in w_refs]
    used = sum(p.shape[0] for p in parts)
    parts.append(jnp.zeros((o_ref.shape[0] - used, o_ref.shape[1]), parts[0].dtype))
    o_ref[...] = jnp.concatenate(parts, axis=0).astype(o_ref.dtype)


def _stack_rows(wt, segments, out_rows, tc):
    cols = wt.shape[1]
    return pl.pallas_call(
        _stack_rows_kernel,
        grid=(cols // tc,),
        in_specs=[pl.BlockSpec((pl.Element(n), pl.Element(tc)), functools.partial(lambda i, s: (s, i * tc), s=src))
                  for src, n in segments],
        out_specs=pl.BlockSpec((out_rows, tc), lambda i: (0, i)),
        out_shape=jax.ShapeDtypeStruct((out_rows, cols), BF16),
        compiler_params=_cparams(("parallel",)),
        name="stack_rows",
    )(*([wt] * len(segments)))


def _proj_regroup_kernel(x_ref, g_ref, w_ref, ws_ref, om_ref, o_ref, os_ref, hn_ref):
    @pl.when(pl.program_id(0) == 0)
    def _():
        x = x_ref[...]
        hn = (_rms(x, x.shape[-1]) * g_ref[...]).astype(BF16)
        hn_ref[...] = hn
        os_ref[...] = _dot_nt(hn, ws_ref[...])

    wb = w_ref[...].astype(BF16)
    om_ref[...] = wb
    o_ref[...] = _dot_nt(hn_ref[...], wb).astype(o_ref.dtype)


def _proj_regroup(x, gain, wt, segments, w_small):
    m, d = x.shape
    ns = w_small.shape[0]
    n = sum(sz for _, sz, _ in segments)
    merged = []
    for src, sz, dst in sorted(segments, key=lambda s: s[2]):
        if merged and merged[-1][0] + merged[-1][1] == src and merged[-1][2] + merged[-1][1] == dst:
            merged[-1] = (merged[-1][0], merged[-1][1] + sz, merged[-1][2])
        else:
            merged.append((src, sz, dst))
    segments = merged
    assert all(sz % COL_TILE == 0 and dst % COL_TILE == 0 for _, sz, dst in segments)
    align = functools.reduce(math.gcd, [src for src, _, _ in segments], COL_TILE)

    def src_row(j):
        r = j * COL_TILE
        out = 0
        for src, sz, dst in segments:
            out = jnp.where((r >= dst) & (r < dst + sz), src + r - dst, out)
        return out

    const = lambda j: (0, 0)
    return pl.pallas_call(
        _proj_regroup_kernel,
        grid=(n // COL_TILE,),
        in_specs=[
            pl.BlockSpec((m, d), const),
            pl.BlockSpec((1, d), const),
            pl.BlockSpec((pl.Element(COL_TILE), pl.Element(d)), lambda j: (pl.multiple_of(src_row(j), align), 0)),
            pl.BlockSpec((ns, d), const),
        ],
        out_specs=[
            pl.BlockSpec((COL_TILE, d), lambda j: (j, 0)),
            pl.BlockSpec((m, COL_TILE), lambda j: (0, j)),
            pl.BlockSpec((m, ns), const),
        ],
        out_shape=[jax.ShapeDtypeStruct((n, d), BF16), jax.ShapeDtypeStruct((m, n), BF16),
                   jax.ShapeDtypeStruct((m, ns), F32)],
        scratch_shapes=[pltpu.VMEM((m, d), BF16)],
        compiler_params=_cparams(("arbitrary",)),
        name="proj_regroup",
    )(x, gain, wt, w_small)


def _proj_kernel(x_ref, g_ref, w_ref, ws_ref, o_ref, os_ref, hn_ref, *, n_sub):
    j = pl.program_id(1)
    ts = x_ref.shape[0]

    @pl.when(j < n_sub)
    def _():
        x = x_ref[...]
        hn = (_rms(x, x.shape[-1]) * g_ref[...]).astype(BF16)
        hn_ref[pl.ds(pl.multiple_of(j * ts, ts), ts), :] = hn
        os_ref[...] = _dot_nt(hn, ws_ref[...])

    @pl.when(j >= n_sub)
    def _():
        o_ref[...] = _dot_nt(hn_ref[...], w_ref[...]).astype(o_ref.dtype)


def _proj(x, gain, w_main, w_small, tm, n_sub):
    m, d = x.shape
    n = w_main.shape[0]
    ns = w_small.shape[0]
    ts = tm // n_sub
    sub = lambda i, j: (i * n_sub + jnp.minimum(j, n_sub - 1), 0)
    feat = lambda j: jnp.maximum(j - n_sub, 0)
    return pl.pallas_call(
        functools.partial(_proj_kernel, n_sub=n_sub),
        grid=(m // tm, n_sub + n // COL_TILE),
        in_specs=[
            pl.BlockSpec((ts, d), sub),
            pl.BlockSpec((1, d), lambda i, j: (0, 0)),
            pl.BlockSpec((COL_TILE, d), lambda i, j: (feat(j), 0)),
            pl.BlockSpec((ns, d), lambda i, j: (0, 0)),
        ],
        out_specs=[
            pl.BlockSpec((tm, COL_TILE), lambda i, j: (i, feat(j))),
            pl.BlockSpec((ts, ns), sub),
        ],
        out_shape=[jax.ShapeDtypeStruct((m, n), BF16), jax.ShapeDtypeStruct((m, ns), F32)],
        scratch_shapes=[pltpu.VMEM((tm, d), BF16)],
        compiler_params=_cparams(("parallel", "arbitrary")),
        name="proj",
    )(x, gain, w_main, w_small)


def _latent_q_kernel(*refs, n_cq, kv_lora, heads_per_step):
    cq_refs = refs[:n_cq]
    (pf_ref, cos_ref, sin_ref, qan_ref, wqn_ref, wqp_ref, qng_ref, qrg_ref, kvg_ref, krg_ref,
     q_ref, ckv_ref, kpe_ref, kpep_ref, cqn_ref) = refs[n_cq:]
    cos = cos_ref[...]
    sin = sin_ref[...]
    lane = lax.broadcasted_iota(jnp.int32, cos.shape, 1)
    low = lane < QK_ROPE

    @pl.when(pl.program_id(1) == 0)
    def _():
        cq = jnp.concatenate([r[...].astype(F32) for r in cq_refs], axis=1)
        cqn_ref[...] = (_rms(cq, cq.shape[-1]) * qan_ref[...]).astype(BF16)
        pf = pf_ref[...]
        ckv_raw = pf[:, :kv_lora]
        ckv_ref[...] = _rms(ckv_raw, kv_lora) * kvg_ref[...]
        kraw = jnp.where(low, pf[:, kv_lora:kv_lora + LANE], 0.0)
        kpe = _rope_lanes(_rms(kraw, QK_ROPE) * krg_ref[...], cos, sin)
        kpe_ref[...] = kpe[:, :QK_ROPE]
        kpep_ref[...] = kpe.astype(BF16)

    cqn = cqn_ref[...]
    group = 4 if heads_per_step % 4 == 0 else 2
    for g0 in range(0, heads_per_step, group):
        qn = _dot(cqn, wqn_ref[:, g0 * QK_NOPE:(g0 + group) * QK_NOPE])
        qp = _dot(cqn, wqp_ref[:, g0 * QK_ROPE:(g0 + group) * QK_ROPE])
        for pair in range(group // 2):
            x = qp[:, pair * LANE:(pair + 1) * LANE]
            x2 = x * x
            ss_a = jnp.sum(jnp.where(low, x2, 0.0), axis=-1, keepdims=True)
            ss_b = jnp.sum(jnp.where(low, 0.0, x2), axis=-1, keepdims=True)
            scale = jnp.where(low, lax.rsqrt(ss_a * (1.0 / QK_ROPE) + EPS),
                              lax.rsqrt(ss_b * (1.0 / QK_ROPE) + EPS))
            r = _rope_lanes(x * scale * qrg_ref[...], cos, sin)
            halves = (jnp.where(low, r, 0.0), jnp.where(low, pltpu.roll(r, QK_ROPE, 1), 0.0))
            for h in (0, 1):
                base = (g0 + 2 * pair + h) * QK_PAD
                nope = qn[:, (2 * pair + h) * QK_NOPE:(2 * pair + h + 1) * QK_NOPE]
                q_ref[:, base:base + QK_NOPE] = (_rms(nope, QK_NOPE) * qng_ref[...]).astype(BF16)
                q_ref[:, base + QK_NOPE:base + QK_PAD] = halves[h].astype(BF16)


def _latent_q(p_bf, p_f32, cq_off, q_lora, kv_lora, cos, sin, qan, wqn, wqp, qng, qrg, kvg, krg, tm,
              heads_per_step):
    m = p_bf.shape[0]
    n_heads = wqn.shape[1] // QK_NOPE
    n_cq = q_lora // COL_TILE
    cq0 = cq_off // COL_TILE
    ns = p_f32.shape[1]
    wstep = heads_per_step * QK_PAD
    assert heads_per_step % 2 == 0
    row = lambda i, j: (i, 0)
    const = lambda i, j: (0, 0)
    pos_blocks = cos.shape[0] // tm
    pos = lambda i, j: (i % pos_blocks, 0)
    in_specs = [pl.BlockSpec((tm, COL_TILE), functools.partial(lambda i, j, c: (i, c), c=cq0 + c))
                for c in range(n_cq)]
    in_specs += [
        pl.BlockSpec((tm, ns), row),
        pl.BlockSpec((tm, LANE), pos),
        pl.BlockSpec((tm, LANE), pos),
        pl.BlockSpec((1, q_lora), const),
        pl.BlockSpec((q_lora, heads_per_step * QK_NOPE), lambda i, j: (0, j)),
        pl.BlockSpec((q_lora, heads_per_step * QK_ROPE), lambda i, j: (0, j)),
        pl.BlockSpec((1, LANE), const),
        pl.BlockSpec((1, LANE), const),
        pl.BlockSpec((1, kv_lora), const),
        pl.BlockSpec((1, LANE), const),
    ]
    return pl.pallas_call(
        functools.partial(_latent_q_kernel, n_cq=n_cq, kv_lora=kv_lora, heads_per_step=heads_per_step),
        grid=(m // tm, n_heads // heads_per_step),
        in_specs=in_specs,
        out_specs=[
            pl.BlockSpec((tm, wstep), lambda i, j: (i, j)),
            pl.BlockSpec((tm, kv_lora), row),
            pl.BlockSpec((tm, QK_ROPE), row),
            pl.BlockSpec((tm, LANE), row),
        ],
        out_shape=[
            jax.ShapeDtypeStruct((m, n_heads * QK_PAD), BF16),
            jax.ShapeDtypeStruct((m, kv_lora), F32),
            jax.ShapeDtypeStruct((m, QK_ROPE), F32),
            jax.ShapeDtypeStruct((m, LANE), BF16),
        ],
        scratch_shapes=[pltpu.VMEM((tm, q_lora), BF16)],
        compiler_params=_cparams(("parallel", "arbitrary")),
        name="latent_q",
    )(*([p_bf] * n_cq), p_f32, cos, sin, qan, wqn, wqp, qng, qrg, kvg, krg)


def _row_pieces(rows):
    piece = min(rows, ROW_TILE)
    return [slice(r, r + piece) for r in range(0, rows, piece)]


def _k_expand_kernel(ckv_ref, wk_ref, kng_ref, k_ref, *, n_heads):
    for rows in _row_pieces(ckv_ref.shape[0]):
        kf = _dot(ckv_ref[rows, :].astype(BF16), wk_ref[...])
        for h in range(n_heads):
            hs = slice(h * QK_NOPE, (h + 1) * QK_NOPE)
            k_ref[rows, hs] = (_rms(kf[:, hs], QK_NOPE) * kng_ref[...]).astype(BF16)


def _k_expand(ckv, wk, kng, tm):
    r, kv_lora = ckv.shape
    n_heads = wk.shape[1] // QK_NOPE
    return pl.pallas_call(
        functools.partial(_k_expand_kernel, n_heads=n_heads),
        grid=(r // tm,),
        in_specs=[
            pl.BlockSpec((tm, kv_lora), lambda i: (i, 0)),
            pl.BlockSpec((kv_lora, n_heads * QK_NOPE), lambda i: (0, 0)),
            pl.BlockSpec((1, LANE), lambda i: (0, 0)),
        ],
        out_specs=pl.BlockSpec((tm, n_heads * QK_NOPE), lambda i: (i, 0)),
        out_shape=jax.ShapeDtypeStruct((r, n_heads * QK_NOPE), BF16),
        compiler_params=_cparams(("parallel",)),
        name="k_expand",
    )(ckv, wk, kng)


def _kv_expand_kernel(ckv_ref, kpep_ref, wk_ref, wv_ref, kng_ref, k_ref, v_ref, *, n_heads):
    for rows in _row_pieces(ckv_ref.shape[0]):
        c = ckv_ref[rows, :].astype(BF16)
        kf = _dot(c, wk_ref[...])
        v_ref[rows, :] = _dot(c, wv_ref[...]).astype(BF16)
        kpep = kpep_ref[rows, :]
        for h in range(n_heads):
            kn = _rms(kf[:, h * QK_NOPE:(h + 1) * QK_NOPE], QK_NOPE) * kng_ref[...]
            k_ref[rows, h * QK_PAD:h * QK_PAD + QK_NOPE] = kn.astype(BF16)
            k_ref[rows, h * QK_PAD + QK_NOPE:(h + 1) * QK_PAD] = kpep


def _kv_expand(ckv, kpep, wk, wv, kng, tm):
    r, kv_lora = ckv.shape
    n_heads = wk.shape[1] // QK_NOPE
    row = lambda i: (i, 0)
    const = lambda i: (0, 0)
    return pl.pallas_call(
        functools.partial(_kv_expand_kernel, n_heads=n_heads),
        grid=(r // tm,),
        in_specs=[
            pl.BlockSpec((tm, kv_lora), row),
            pl.BlockSpec((tm, LANE), row),
            pl.BlockSpec((kv_lora, n_heads * QK_NOPE), const),
            pl.BlockSpec((kv_lora, n_heads * V_HEAD), const),
            pl.BlockSpec((1, LANE), const),
        ],
        out_specs=[
            pl.BlockSpec((tm, n_heads * QK_PAD), row),
            pl.BlockSpec((tm, n_heads * V_HEAD), row),
        ],
        out_shape=[
            jax.ShapeDtypeStruct((r, n_heads * QK_PAD), BF16),
            jax.ShapeDtypeStruct((r, n_heads * V_HEAD), BF16),
        ],
        compiler_params=_cparams(("parallel",)),
        name="kv_expand",
    )(ckv, kpep, wk, wv, kng)


def _lane_chunk_sum(p):
    out = p[:, :LANE]
    for c in range(1, p.shape[1] // LANE):
        out = out + p[:, c * LANE:(c + 1) * LANE]
    return out


def _softmax_scale():
    return float((QK_NOPE + QK_ROPE) ** -0.5 * 1.4426950408889634)


def _attn_prompt_block(n_full, q_ref, k_ref, km_ref, za_ref, o_ref, s_ref, p_ref, va_ref, *,
                       tq, heads, c, n_meta):
    td = min(tq, MXU_EDGE)
    pieces = [(i * tq, tq, 0, False) for i in range(n_full)]
    pieces += [(n_full * tq + j * td, td, j * td, True) for j in range(tq // td)]

    def upd(full, r0, fn):
        return fn(full) if r0 == 0 else jnp.concatenate([full[:r0], fn(full[r0:])], axis=0)

    def scores(h):
        qs = slice(h * QK_PAD, (h + 1) * QK_PAD)
        q = q_ref[:, qs]
        col = lax.broadcasted_iota(jnp.int32, (tq, LANE), 1)
        t = jnp.where(col < n_meta, _dot_nt(q, km_ref[:, qs]) * c, -jnp.inf)
        s_ref[h, :, :LANE] = t
        mx = t
        off = LANE
        for start, rows, r0, masked in pieces:
            t = _dot_nt(q[r0:], k_ref[start:start + rows, qs]) * c
            if masked:
                rq = lax.broadcasted_iota(jnp.int32, t.shape, 0) // CHUNK
                ck = lax.broadcasted_iota(jnp.int32, t.shape, 1) // CHUNK
                t = jnp.where(rq >= ck, t, -jnp.inf)
            s_ref[h, r0:, off:off + rows] = t
            tmax = t[:, :LANE]
            for ch in range(1, rows // LANE):
                tmax = jnp.maximum(tmax, t[:, ch * LANE:(ch + 1) * LANE])
            mx = upd(mx, r0, lambda a: jnp.maximum(a, tmax))
            off += rows
        return jnp.max(mx, axis=-1, keepdims=True)

    def values(h, m):
        vs = slice(h * V_HEAD, (h + 1) * V_HEAD)
        va = slice(2 * h * V_HEAD, 2 * (h + 1) * V_HEAD)
        p_ref[h, :, :LANE] = jnp.exp2(s_ref[h, :, :LANE] - m).astype(BF16)
        off = LANE
        for start, rows, r0, _ in pieces:
            p_ref[h, r0:, off:off + rows] = jnp.exp2(s_ref[h, r0:, off:off + rows] - m[r0:]).astype(BF16)
            off += rows
        off = LANE + n_full * tq
        acc = _dot(p_ref[h, :, :off], va_ref[:off, va])
        for start, rows, r0, _ in pieces[n_full:]:
            pv = _dot(p_ref[h, r0:, off:off + rows], va_ref[LANE + start:LANE + start + rows, va])
            acc = upd(acc, r0, lambda a: a + pv)
            off += rows
        za = za_ref[:, vs].astype(F32)
        out = acc[:, :V_HEAD] / acc[:, V_HEAD:V_HEAD + 1] * (za * jax.nn.sigmoid(za))
        o_ref[:, vs] = out.astype(o_ref.dtype)

    maxes = [scores(h) for h in range(heads)]
    for h in range(heads):
        values(h, maxes[h])


def _attn_prompt_kernel(q_ref, k_ref, v_ref, km_ref, vm_ref, za_ref, o_ref, s_ref, p_ref, va_ref, *,
                        nqb, heads, **kw):
    qi = pl.program_id(2)

    @pl.when(qi == 0)
    def _():
        for h in range(heads):
            vs = slice(h * V_HEAD, (h + 1) * V_HEAD)
            va_ref[:LANE, 2 * h * V_HEAD:(2 * h + 1) * V_HEAD] = vm_ref[:, vs]
            va_ref[LANE:, 2 * h * V_HEAD:(2 * h + 1) * V_HEAD] = v_ref[:, vs]
            va_ref[:, (2 * h + 1) * V_HEAD:(2 * h + 2) * V_HEAD] = jnp.ones((va_ref.shape[0], V_HEAD), BF16)

    for n_full in range(nqb):
        pl.when(qi == n_full)(
            functools.partial(_attn_prompt_block, n_full, q_ref, k_ref, km_ref, za_ref, o_ref, s_ref, p_ref, va_ref,
                              heads=heads, **kw))


def _attn_prompt(q, k, v, k_small, v_small, meta_blk, p_bf, za_off, batch, seq, tq, heads, n_meta):
    n_heads = q.shape[1] // QK_PAD
    nqb = seq // tq
    za0 = za_off // (V_HEAD * heads)
    assert za_off % (V_HEAD * heads) == 0 and n_heads % heads == 0 and tq % CHUNK == 0
    return pl.pallas_call(
        functools.partial(_attn_prompt_kernel, nqb=nqb, tq=tq, heads=heads, c=_softmax_scale(), n_meta=n_meta),
        grid=(batch, n_heads // heads, nqb),
        in_specs=[
            pl.BlockSpec((tq, heads * QK_PAD), lambda b, h, i: (b * nqb + i, h)),
            pl.BlockSpec((seq, heads * QK_PAD), lambda b, h, i: (b, h)),
            pl.BlockSpec((seq, heads * V_HEAD), lambda b, h, i: (b, h)),
            pl.BlockSpec((LANE, heads * QK_PAD), lambda b, h, i: (meta_blk, h)),
            pl.BlockSpec((LANE, heads * V_HEAD), lambda b, h, i: (meta_blk, h)),
            pl.BlockSpec((tq, heads * V_HEAD), lambda b, h, i: (b * nqb + i, za0 + h)),
        ],
        out_specs=pl.BlockSpec((tq, heads * V_HEAD), lambda b, h, i: (b * nqb + i, h)),
        out_shape=jax.ShapeDtypeStruct((batch * seq, n_heads * V_HEAD), BF16),
        scratch_shapes=[pltpu.VMEM((heads, tq, LANE + seq), F32), pltpu.VMEM((heads, tq, LANE + seq), BF16),
                        pltpu.VMEM((LANE + seq, 2 * heads * V_HEAD), BF16)],
        compiler_params=_cparams(("parallel", "parallel", "arbitrary")),
        name="attn_prompt",
    )(q, k, v, k_small, v_small, p_bf)


def _attn_sample_kernel(q_ref, kc_ref, rc_ref, cc_ref, ks_ref, cs_ref, wv_ref, za_ref, o_ref, pc_ref, ps_ref, li_ref, *,
                        heads, lq, c, n_meta, meta_row):
    b = pl.program_id(0)
    col = lax.broadcasted_iota(jnp.int32, (lq, ks_ref.shape[0]), 1)
    own = (col >= b * lq) & (col < (b + 1) * lq)
    meta = (col >= meta_row) & (col < meta_row + n_meta)
    visible = own | meta
    q_rope = jnp.concatenate([q_ref[:, h * QK_PAD + QK_NOPE:(h + 1) * QK_PAD] for h in range(heads)], axis=0)
    rope_cache = _dot_nt(q_rope, rc_ref[...])
    for h in range(heads):
        qs = slice(h * QK_PAD, (h + 1) * QK_PAD)
        rows = slice(h * lq, (h + 1) * lq)
        q = q_ref[:, qs]
        t_cache = (_dot_nt(q[:, :QK_NOPE], kc_ref[:, h * QK_NOPE:(h + 1) * QK_NOPE]) + rope_cache[rows]) * c
        t_new = jnp.where(visible, _dot_nt(q, ks_ref[:, qs]) * c, -jnp.inf)
        m = jnp.maximum(jnp.max(t_cache, axis=-1, keepdims=True), jnp.max(t_new, axis=-1, keepdims=True))
        p_cache = jnp.exp2(t_cache - m)
        p_new = jnp.exp2(t_new - m)
        lsum = jnp.sum(_lane_chunk_sum(p_cache) + _lane_chunk_sum(p_new), axis=-1, keepdims=True)
        pc_ref[rows, :] = p_cache.astype(BF16)
        ps_ref[rows, :] = p_new.astype(BF16)
        li_ref[rows, :] = jnp.broadcast_to(1.0 / lsum, (lq, LANE))
    mixed = (_dot(pc_ref[...], cc_ref[...].astype(BF16)) + _dot(ps_ref[...], cs_ref[...].astype(BF16)))
    mixed = (mixed * li_ref[:, 0:1]).astype(BF16)
    for h in range(heads):
        vs = slice(h * V_HEAD, (h + 1) * V_HEAD)
        za = za_ref[:, vs].astype(F32)
        out = _dot(mixed[h * lq:(h + 1) * lq], wv_ref[:, vs]) * (za * jax.nn.sigmoid(za))
        o_ref[:, vs] = out.astype(o_ref.dtype)


def _attn_sample(q, kc, rope_cache, ckv_cache, ks, ckv_small, wv, p_bf, za_off, batch, lq, past, n_meta):
    n_heads = q.shape[1] // QK_PAD
    rows = ks.shape[0]
    kv_lora = ckv_cache.shape[1]
    za0 = za_off // (V_HEAD * n_heads)
    assert za_off % (V_HEAD * n_heads) == 0
    const = lambda b: (0, 0)
    return pl.pallas_call(
        functools.partial(_attn_sample_kernel, heads=n_heads, lq=lq, c=_softmax_scale(), n_meta=n_meta,
                          meta_row=batch * lq),
        grid=(batch,),
        in_specs=[
            pl.BlockSpec((lq, n_heads * QK_PAD), lambda b: (b, 0)),
            pl.BlockSpec((past, n_heads * QK_NOPE), lambda b: (b, 0)),
            pl.BlockSpec((past, LANE), lambda b: (b, 0)),
            pl.BlockSpec((past, kv_lora), lambda b: (b, 0)),
            pl.BlockSpec((rows, n_heads * QK_PAD), const),
            pl.BlockSpec((rows, kv_lora), const),
            pl.BlockSpec((kv_lora, n_heads * V_HEAD), const),
            pl.BlockSpec((lq, n_heads * V_HEAD), lambda b: (b, za0)),
        ],
        out_specs=pl.BlockSpec((lq, n_heads * V_HEAD), lambda b: (b, 0)),
        out_shape=jax.ShapeDtypeStruct((batch * lq, n_heads * V_HEAD), BF16),
        scratch_shapes=[pltpu.VMEM((n_heads * lq, past), BF16), pltpu.VMEM((n_heads * lq, rows), BF16),
                        pltpu.VMEM((n_heads * lq, LANE), F32)],
        compiler_params=_cparams(("parallel",)),
        name="attn_sample",
    )(q, kc, rope_cache, ckv_cache, ks, ckv_small, wv, p_bf)


def _log_sigmoid(x):
    return jnp.minimum(x, 0.0) - jnp.log1p(jnp.exp(-jnp.abs(x)))


def _pad_rows(a, rows):
    if a.shape[0] == rows:
        return a
    return jnp.concatenate([a, jnp.zeros((rows - a.shape[0], a.shape[1]), a.dtype)], axis=0)


def _mlstm_kernel(q_ref, k_ref, v_ref, mo_ref, zb_ref, g_ref, gb_ref, gain_ref, c0_ref, n0_ref, m0_ref,
                  h_ref, c_ref, n_ref, m_ref, *, n_heads, lc, lp, dk, dv):
    @pl.when(pl.program_id(1) == 0)
    def _():
        c_ref[...] = c0_ref[...]
        n_ref[...] = n0_ref[...]
        m_ref[...] = m0_ref[...]

    hi = lax.Precision.HIGHEST
    gates = _pad_rows(g_ref[...] + gb_ref[...], lp)
    gates_t = gates.T
    t_col = lax.broadcasted_iota(jnp.int32, (lp, 1), 0)
    t_row = lax.broadcasted_iota(jnp.int32, (1, lp), 1)
    r_idx = lax.broadcasted_iota(jnp.int32, (lp, lp), 0)
    c_idx = lax.broadcasted_iota(jnp.int32, (lp, lp), 1)
    causal = r_idx >= c_idx
    lower = causal.astype(F32)
    upper = (c_idx >= r_idx).astype(F32)
    b_cols = jnp.dot(lower, jnp.where(t_col < lc, _log_sigmoid(gates), 0.0), precision=hi,
                     preferred_element_type=F32)
    b_rows = jnp.dot(jnp.where(t_row < lc, _log_sigmoid(gates_t), 0.0), upper, precision=hi,
                     preferred_element_type=F32)

    for h in range(n_heads):
        li = QK_ROPE + h
        lf_lane = QK_ROPE + n_heads + h
        ig_col = jnp.where(t_col < lc, gates[:, li:li + 1], NEG_BIG)
        ig_row = jnp.where(t_row < lc, gates_t[li:li + 1, :], NEG_BIG)
        b_col = b_cols[:, lf_lane:lf_lane + 1]
        b_row = b_rows[lf_lane:lf_lane + 1, :]
        m0 = m_ref[0, 0:1, h:h + 1]
        d = jnp.where(causal, b_col - b_row + ig_row, -jnp.inf)
        a_col = b_col + m0
        m = jnp.maximum(a_col, jnp.max(d, axis=-1, keepdims=True))
        w_inter = jnp.exp(a_col - m)

        q = _pad_rows(q_ref[:, h * dk:(h + 1) * dk], lp)
        k = _pad_rows(k_ref[:, h * dk:(h + 1) * dk], lp) * (dk ** -0.5)
        v = _pad_rows(v_ref[:, h * dv:(h + 1) * dv], lp)
        qk = _dot_nt(q, k) * jnp.exp(d - m)
        c_old = c_ref[0, h]
        n_old = n_ref[0, h:h + 1, :]
        num = w_inter * _dot(q, c_old.astype(BF16)) + _dot(qk.astype(BF16), v)
        den = (w_inter * jnp.sum(q.astype(F32) * n_old, axis=-1, keepdims=True)
               + jnp.sum(qk, axis=-1, keepdims=True))
        denc = jnp.maximum(jnp.abs(den), jnp.exp(-m))

        b_last = b_col[lp - 1:lp, :]
        g_row = b_last - b_row + ig_row
        g_col = b_last - b_col + ig_col
        m_new = jnp.maximum(b_last + m0, jnp.max(g_row, axis=-1, keepdims=True))
        decay = jnp.exp(b_last + m0 - m_new)
        kw = k.astype(F32) * jnp.exp(g_col - m_new)
        c_ref[0, h] = decay * c_old + _dot(kw.T.astype(BF16), v)
        n_ref[0, h:h + 1, :] = decay * n_old + jnp.sum(kw, axis=0, keepdims=True)
        m_ref[0, 0:1, h:h + 1] = m_new

        num = num[:lc]
        denc = denc[:lc]
        normed = num * lax.rsqrt(jnp.sum(num * num, axis=-1, keepdims=True) * (1.0 / dv) + EPS * denc * denc)
        mo = mo_ref[:, h * dv:(h + 1) * dv].astype(F32)
        zb = zb_ref[:, h * dv:(h + 1) * dv].astype(F32)
        out = jax.nn.sigmoid(mo) * (normed * gain_ref[:, h * dv:(h + 1) * dv])
        h_ref[:, h * dv:(h + 1) * dv] = (out * (zb * jax.nn.sigmoid(zb))).astype(h_ref.dtype)


def _mlstm(p_bf, p_f32, offs, gate_bias, gain, c0, n0, m0, batch, seq, lc, row0, share_state, n_heads, dk, dv):
    lp = -(-lc // LANE) * LANE
    nch = seq // lc
    blk0 = row0 // lc
    wqk = n_heads * dk
    wv = n_heads * dv
    gate_blk = p_f32.shape[1] // LANE - 1
    rows = lambda col: (lambda b, c: (blk0 + b * nch + c, col))
    state = (lambda b, c: (0, 0, 0, 0)) if share_state else (lambda b, c: (b, 0, 0, 0))
    state3 = (lambda b, c: (0, 0, 0)) if share_state else (lambda b, c: (b, 0, 0))
    out_rows = batch * seq
    return pl.pallas_call(
        functools.partial(_mlstm_kernel, n_heads=n_heads, lc=lc, lp=lp, dk=dk, dv=dv),
        grid=(batch, nch),
        in_specs=[
            pl.BlockSpec((lc, wqk), rows(offs["mq"] // wqk)),
            pl.BlockSpec((lc, wqk), rows(offs["mk"] // wqk)),
            pl.BlockSpec((lc, wv), rows(offs["mv"] // wv)),
            pl.BlockSpec((lc, wv), rows(offs["mo"] // wv)),
            pl.BlockSpec((lc, wv), rows(offs["zb"] // wv)),
            pl.BlockSpec((lc, LANE), rows(gate_blk)),
            pl.BlockSpec((1, LANE), lambda b, c: (0, 0)),
            pl.BlockSpec((1, wv), lambda b, c: (0, 0)),
            pl.BlockSpec((1, n_heads, dk, dv), state),
            pl.BlockSpec((1, n_heads, dk), state3),
            pl.BlockSpec((1, 1, n_heads), state3),
        ],
        out_specs=[
            pl.BlockSpec((lc, wv), lambda b, c: (b * nch + c, 0)),
            pl.BlockSpec((1, n_heads, dk, dv), lambda b, c: (b, 0, 0, 0)),
            pl.BlockSpec((1, n_heads, dk), lambda b, c: (b, 0, 0)),
            pl.BlockSpec((1, 1, n_heads), lambda b, c: (b, 0, 0)),
        ],
        out_shape=[
            jax.ShapeDtypeStruct((out_rows, wv), BF16),
            jax.ShapeDtypeStruct((batch, n_heads, dk, dv), F32),
            jax.ShapeDtypeStruct((batch, n_heads, dk), F32),
            jax.ShapeDtypeStruct((batch, 1, n_heads), F32),
        ],
        compiler_params=_cparams(("parallel", "arbitrary")),
        name="mlstm",
    )(p_bf, p_bf, p_bf, p_bf, p_bf, p_f32, gate_bias, gain, c0, n0, m0)


def _out_proj_kernel(a_ref, m_ref, wa_ref, wm_ref, x_ref, o_ref, *wb_refs):
    wa = wa_ref[...].astype(BF16)
    wm = wm_ref[...].astype(BF16)
    o_ref[...] = x_ref[...] + _dot(a_ref[...], wa) + _dot(m_ref[...], wm)
    if wb_refs:
        wb_refs[0][...] = wa
        wb_refs[1][...] = wm


def _out_proj(a, ml, w_a, w_m, wm_blk, x, tm, tn, emit_bf16=False):
    rows, wa = a.shape
    wm = ml.shape[1]
    d = w_a.shape[1]
    assert wa == wm and (not emit_bf16 or rows == tm)
    out_specs = [pl.BlockSpec((tm, tn), lambda i, j: (i, j))]
    out_shape = [jax.ShapeDtypeStruct((rows, d), F32)]
    if emit_bf16:
        out_specs += [pl.BlockSpec((wa, tn), lambda i, j: (0, j)), pl.BlockSpec((wm, tn), lambda i, j: (0, j))]
        out_shape += [jax.ShapeDtypeStruct((wa, d), BF16), jax.ShapeDtypeStruct((wm, d), BF16)]
    return pl.pallas_call(
        _out_proj_kernel,
        grid=(rows // tm, d // tn),
        in_specs=[
            pl.BlockSpec((tm, wa), lambda i, j: (i, 0)),
            pl.BlockSpec((tm, wm), lambda i, j: (i, 0)),
            pl.BlockSpec((wa, tn), lambda i, j: (0, j)),
            pl.BlockSpec((wm, tn), lambda i, j: (wm_blk, j)),
            pl.BlockSpec((tm, tn), lambda i, j: (i, j)),
        ],
        out_specs=out_specs,
        out_shape=out_shape,
        compiler_params=_cparams(("parallel", "arbitrary")),
        name="out_proj",
    )(a, ml, w_a, w_m, x)


def _rope_tables(pos):
    inv_freq = ROPE_THETA ** (-jnp.arange(HALF_ROPE, dtype=F32) / HALF_ROPE)
    ang = pos.astype(F32)[:, None] * inv_freq[None, :]
    cos, sin = jnp.cos(ang), jnp.sin(ang)
    return jnp.concatenate([cos, cos, cos, cos], axis=1), jnp.concatenate([-sin, sin, -sin, sin], axis=1)


def _pad_lanes(vec):
    return jnp.pad(vec.astype(F32), (0, LANE - vec.shape[0]))[None, :]


def _row_tile(rows, target):
    t = min(rows, target)
    while rows % t:
        t //= 2
    return t


def kernel(x_prompt, x_sample, cache_ckv, cache_kpe, state_C, state_n, state_m, meta_tokens, norm_gain, w_in,
           b_igate, b_fgate, q_a_norm, w_q_up, q_nope_norm, q_rope_norm, kv_a_norm, k_rope_norm, w_kv_up,
           k_nope_norm, ml_out_norm, w_out):
    batch, seq, d = x_prompt.shape
    dbatch, dseq, _ = x_sample.shape
    past = cache_ckv.shape[1]
    n_meta = meta_tokens.shape[0]
    q_lora = q_a_norm.shape[0]
    kv_lora = kv_a_norm.shape[0]
    ml_heads, dv = ml_out_norm.shape
    dk = state_n.shape[-1]
    mla_heads = w_kv_up.shape[1] // (QK_NOPE + V_HEAD)
    mla_w = mla_heads * V_HEAD
    ml_w = ml_heads * dv
    mqk_w = ml_heads * dk
    assert q_lora % COL_TILE == 0 and kv_lora % LANE == 0 and 2 * ml_heads <= LANE - QK_ROPE
    assert mla_w == ml_w and w_out.shape[0] == mla_w + ml_w

    o_cq = 0
    o_ckv = o_cq + q_lora
    o_kpe = o_ckv + kv_lora
    o_mq = o_kpe + QK_ROPE
    o_mk = o_mq + mqk_w
    o_mv = o_mk + mqk_w
    o_mo = o_mv + ml_w
    o_mi = o_mo + ml_w
    o_mf = o_mi + ml_heads
    o_za = o_mf + ml_heads
    o_zb = o_za + mla_w

    offs = {"mq": 0, "mk": mqk_w, "mv": 2 * mqk_w, "mo": 2 * mqk_w + ml_w, "za": 2 * mqk_w + 2 * ml_w,
            "zb": 2 * mqk_w + 2 * ml_w + mla_w, "cq": 2 * mqk_w + 3 * ml_w + mla_w}
    gate_pad = LANE - QK_ROPE - 2 * ml_heads
    w_in_t = w_in.T
    main_segments = [(o_mq, mqk_w, offs["mq"]), (o_mk, mqk_w, offs["mk"]), (o_mv, ml_w, offs["mv"]),
                     (o_mo, ml_w, offs["mo"]), (o_za, mla_w, offs["za"]), (o_zb, ml_w, offs["zb"]),
                     (o_cq, q_lora, offs["cq"])]
    w_small = _stack_rows(w_in_t, [(o_ckv, kv_lora), (o_kpe, QK_ROPE), (o_mi, 2 * ml_heads)], kv_lora + LANE,
                          _row_tile(d, COL_TILE))
    gate_bias = jnp.concatenate([jnp.zeros((QK_ROPE,), F32), b_igate.astype(F32), b_fgate.astype(F32),
                                 jnp.zeros((gate_pad,), F32)])[None, :]
    qk_w = QK_NOPE + QK_ROPE
    kv_w = QK_NOPE + V_HEAD
    wqn, wqp = _regroup(w_q_up, [[(h * qk_w, QK_NOPE, h * QK_NOPE) for h in range(mla_heads)],
                                 [(h * qk_w + QK_NOPE, QK_ROPE, h * QK_ROPE) for h in range(mla_heads)]],
                        _row_tile(q_lora, ROW_TILE))
    wk, wv = _regroup(w_kv_up, [[(h * kv_w, QK_NOPE, h * QK_NOPE) for h in range(mla_heads)],
                                [(h * kv_w + QK_NOPE, V_HEAD, h * V_HEAD) for h in range(mla_heads)]], kv_lora)
    gain_row = norm_gain.astype(F32)[None, :]
    qan = q_a_norm.astype(F32)[None, :]
    kvg = kv_a_norm.astype(F32)[None, :]
    qng = q_nope_norm.astype(F32)[None, :]
    kng = k_nope_norm.astype(F32)[None, :]
    qrg = jnp.tile(q_rope_norm.astype(F32), LANE // QK_ROPE)[None, :]
    krg = _pad_lanes(k_rope_norm)
    ml_gain = ml_out_norm.astype(F32).reshape(1, ml_w)

    n_p = batch * seq
    n_s = dbatch * dseq
    assert n_s % LANE == 0 and n_meta <= LANE and n_meta % 16 == 0
    rows_s = n_s + n_meta
    xp = x_prompt.reshape(n_p, d)
    xs = jnp.concatenate([x_sample.reshape(n_s, d), meta_tokens.astype(x_sample.dtype)], axis=0)

    pos_s = jnp.concatenate([jnp.tile(n_meta + past + jnp.arange(dseq), dbatch), jnp.arange(n_meta)])
    cos_p, sin_p = _rope_tables(n_meta + jnp.arange(seq))
    cos_s, sin_s = _rope_tables(pos_s)

    tm_p = _row_tile(seq, ROW_TILE)
    proj_sub = 2 if n_p % (2 * tm_p) == 0 else 1
    w_main, pbf_s, pf_s = _proj_regroup(xs, gain_row, w_in_t, main_segments, w_small)
    pbf_p, pf_p = _proj(xp, gain_row, w_main, w_small, proj_sub * tm_p, proj_sub)

    hps = mla_heads
    q_p, ckv_p, kpe_p, kpep_p = _latent_q(pbf_p, pf_p, offs["cq"], q_lora, kv_lora, cos_p, sin_p, qan, wqn, wqp,
                                          qng, qrg, kvg, krg, tm_p, hps)
    q_s, ckv_s, kpe_s, kpep_s = _latent_q(pbf_s, pf_s, offs["cq"], q_lora, kv_lora, cos_s, sin_s, qan, wqn, wqp,
                                          qng, qrg, kvg, krg, rows_s, hps)

    small_rows = n_s + LANE
    meta_blk = n_s // LANE
    tail = ((0, small_rows - rows_s), (0, 0))
    k_p, v_p = _kv_expand(ckv_p, kpep_p, wk, wv, kng, _row_tile(n_p, 2 * ROW_TILE))
    ckv_small = jnp.pad(ckv_s, tail)
    k_s, v_s = _kv_expand(ckv_small, jnp.pad(kpep_s, tail), wk, wv, kng, small_rows)
    cache_rows = dbatch * past
    cache_kpep = jnp.pad(cache_kpe.reshape(cache_rows, QK_ROPE), ((0, 0), (0, LANE - QK_ROPE))).astype(BF16)
    ckv_cache = cache_ckv.reshape(cache_rows, kv_lora).astype(F32)
    k_c = _k_expand(ckv_cache, wk, kng, _row_tile(cache_rows, 2 * ROW_TILE))

    tq = _row_tile(seq, ATTN_TQ)
    attn_p = _attn_prompt(q_p, k_p, v_p, k_s, v_s, meta_blk, pbf_p, offs["za"], batch, seq, tq,
                          min(ATTN_HEADS, mla_heads), n_meta)
    attn_s = _attn_sample(q_s, k_c, cache_kpep, ckv_cache, k_s, ckv_small, wv, pbf_s, offs["za"], dbatch, dseq, past,
                          n_meta)

    zc = jnp.zeros((1, ml_heads, dk, dv), F32)
    zn = jnp.zeros((1, ml_heads, dk), F32)
    zm = jnp.zeros((1, 1, ml_heads), F32)
    _, c_m, n_m, m_m = _mlstm(pbf_s, pf_s, offs, gate_bias, ml_gain, zc, zn, zm, 1, n_meta, n_meta, n_s, True,
                              ml_heads, dk, dv)
    lc_p = _row_tile(seq, MLSTM_CHUNK)
    ml_p, c_p, n_pr, m_p = _mlstm(pbf_p, pf_p, offs, gate_bias, ml_gain, c_m, n_m, m_m, batch, seq, lc_p, 0,
                                  True, ml_heads, dk, dv)
    ml_s, c_s, n_sm, m_s = _mlstm(pbf_s, pf_s, offs, gate_bias, ml_gain, state_C.astype(F32),
                                  state_n.astype(F32), state_m.astype(F32).reshape(dbatch, 1, ml_heads),
                                  dbatch, dseq, dseq, 0, False, ml_heads, dk, dv)

    y_s, w_oa, w_om = _out_proj(attn_s, ml_s, w_out, w_out, 1, xs, n_s, _row_tile(d, COL_TILE), emit_bf16=True)
    y_p, = _out_proj(attn_p, ml_p, w_oa, w_om, 0, xp, _row_tile(n_p, 2 * ROW_TILE), _row_tile(d, 2 * COL_TILE))

    meta_ckv = ckv_s[n_s:n_s + n_meta]
    meta_kpe = kpe_s[n_s:n_s + n_meta]
    ckv_prompt = jnp.concatenate([jnp.broadcast_to(meta_ckv[None], (batch, n_meta, kv_lora)),
                                  ckv_p.reshape(batch, seq, kv_lora)], axis=1)
    kpe_prompt = jnp.concatenate([jnp.broadcast_to(meta_kpe[None], (batch, n_meta, QK_ROPE)),
                                  kpe_p.reshape(batch, seq, QK_ROPE)], axis=1)
    return (y_p.reshape(batch, seq, d), y_s.reshape(dbatch, dseq, d), ckv_prompt, kpe_prompt,
            c_p, n_pr, m_p.reshape(batch, ml_heads),
            ckv_s[:n_s].reshape(dbatch, dseq, kv_lora), kpe_s[:n_s].reshape(dbatch, dseq, QK_ROPE),
            c_s, n_sm, m_s.reshape(dbatch, ml_heads))
```

```python
import functools
import math

import jax
import jax.numpy as jnp
from jax import lax
from jax.experimental import pallas as pl
from jax.experimental.pallas import tpu as pltpu

CHUNK = 64
EPS = 1e-6
ROPE_THETA = 10000.0
V_HEAD = 128
QK_NOPE = 128
QK_ROPE = 64
HALF_ROPE = QK_ROPE // 2
LANE = 128
MXU_EDGE = 256
QK_PAD = QK_NOPE + LANE
COL_TILE = 512
ROW_TILE = 512
NEG_BIG = -1e30
ATTN_TQ = 512
ATTN_HEADS = 4
MLSTM_CHUNK = 256
VMEM_LIMIT = 56 * 1024 * 1024

F32 = jnp.float32
BF16 = jnp.bfloat16


def _cparams(sem):
    return pltpu.CompilerParams(dimension_semantics=sem, vmem_limit_bytes=VMEM_LIMIT)


def _dot(a, b):
    return jnp.dot(a, b, preferred_element_type=F32)


def _dot_nt(a, b):
    return lax.dot_general(a, b, (((1,), (1,)), ((), ())), preferred_element_type=F32)


def _rms(x, n):
    return x * lax.rsqrt(jnp.sum(x * x, axis=-1, keepdims=True) * (1.0 / n) + EPS)


def _rope_lanes(x, cos, sin_signed):
    lane = lax.broadcasted_iota(jnp.int32, x.shape, 1)
    first_half = (lane & (QK_ROPE - 1)) < HALF_ROPE
    rot = jnp.where(first_half, pltpu.roll(x, LANE - HALF_ROPE, 1), pltpu.roll(x, HALF_ROPE, 1))
    return x * cos + rot * sin_signed


def _regroup_kernel(w_ref, *o_refs, plans):
    for o_ref, segments in zip(o_refs, plans):
        for src, n, dst in segments:
            o_ref[:, dst:dst + n] = w_ref[:, src:src + n].astype(o_ref.dtype)


def _regroup(w, plans, tr):
    rows, cols = w.shape
    widths = [sum(n for _, n, _ in segments) for segments in plans]
    for segments in plans:
        pos = 0
        for _, n, dst in sorted(segments, key=lambda s: s[2]):
            assert dst == pos
            pos += n
    return pl.pallas_call(
        functools.partial(_regroup_kernel, plans=tuple(tuple(s) for s in plans)),
        grid=(rows // tr,),
        in_specs=[pl.BlockSpec((tr, cols), lambda i: (i, 0))],
        out_specs=[pl.BlockSpec((tr, width), lambda i: (i, 0)) for width in widths],
        out_shape=[jax.ShapeDtypeStruct((rows, width), BF16) for width in widths],
        compiler_params=_cparams(("parallel",)),
        name="regroup",
    )(w)


def _stack_rows_kernel(*refs):
    *w_refs, o_ref = refs
    parts = [r[...] for r in w_refs]
    used = sum(p.shape[0] for p in parts)
    parts.append(jnp.zeros((o_ref.shape[0] - used, o_ref.shape[1]), parts[0].dtype))
    o_ref[...] = jnp.concatenate(parts, axis=0).astype(o_ref.dtype)


def _stack_rows(wt, segments, out_rows, tc):
    cols = wt.shape[1]
    return pl.pallas_call(
        _stack_rows_kernel,
        grid=(cols // tc,),
        in_specs=[pl.BlockSpec((pl.Element(n), pl.Element(tc)), functools.partial(lambda i, s: (s, i * tc), s=src))
                  for src, n in segments],
        out_specs=pl.BlockSpec((out_rows, tc), lambda i: (0, i)),
        out_shape=jax.ShapeDtypeStruct((out_rows, cols), BF16),
        compiler_params=_cparams(("parallel",)),
        name="stack_rows",
    )(*([wt] * len(segments)))


def _proj_regroup_kernel(x_ref, g_ref, w_ref, ws_ref, om_ref, o_ref, os_ref, hn_ref):
    @pl.when(pl.program_id(0) == 0)
    def _():
        x = x_ref[...]
        hn = (_rms(x, x.shape[-1]) * g_ref[...]).astype(BF16)
        hn_ref[...] = hn
        os_ref[...] = _dot_nt(hn, ws_ref[...])

    wb = w_ref[...].astype(BF16)
    om_ref[...] = wb
    o_ref[...] = _dot_nt(hn_ref[...], wb).astype(o_ref.dtype)


def _proj_regroup(x, gain, wt, segments, w_small):
    m, d = x.shape
    ns = w_small.shape[0]
    n = sum(sz for _, sz, _ in segments)
    merged = []
    for src, sz, dst in sorted(segments, key=lambda s: s[2]):
        if merged and merged[-1][0] + merged[-1][1] == src and merged[-1][2] + merged[-1][1] == dst:
            merged[-1] = (merged[-1][0], merged[-1][1] + sz, merged[-1][2])
        else:
            merged.append((src, sz, dst))
    segments = merged
    assert all(sz % COL_TILE == 0 and dst % COL_TILE == 0 for _, sz, dst in segments)
    align = functools.reduce(math.gcd, [src for src, _, _ in segments], COL_TILE)

    def src_row(j):
        r = j * COL_TILE
        out = 0
        for src, sz, dst in segments:
            out = jnp.where((r >= dst) & (r < dst + sz), src + r - dst, out)
        return out

    const = lambda j: (0, 0)
    return pl.pallas_call(
        _proj_regroup_kernel,
        grid=(n // COL_TILE,),
        in_specs=[
            pl.BlockSpec((m, d), const),
            pl.BlockSpec((1, d), const),
            pl.BlockSpec((pl.Element(COL_TILE), pl.Element(d)), lambda j: (pl.multiple_of(src_row(j), align), 0)),
            pl.BlockSpec((ns, d), const),
        ],
        out_specs=[
            pl.BlockSpec((COL_TILE, d), lambda j: (j, 0)),
            pl.BlockSpec((m, COL_TILE), lambda j: (0, j)),
            pl.BlockSpec((m, ns), const),
        ],
        out_shape=[jax.ShapeDtypeStruct((n, d), BF16), jax.ShapeDtypeStruct((m, n), BF16),
                   jax.ShapeDtypeStruct((m, ns), F32)],
        scratch_shapes=[pltpu.VMEM((m, d), BF16)],
        compiler_params=_cparams(("arbitrary",)),
        name="proj_regroup",
    )(x, gain, wt, w_small)


def _proj_kernel(x_ref, g_ref, w_ref, ws_ref, o_ref, os_ref, hn_ref, *, n_sub):
    j = pl.program_id(1)
    ts = x_ref.shape[0]

    @pl.when(j < n_sub)
    def _():
        x = x_ref[...]
        hn = (_rms(x, x.shape[-1]) * g_ref[...]).astype(BF16)
        hn_ref[pl.ds(pl.multiple_of(j * ts, ts), ts), :] = hn
        os_ref[...] = _dot_nt(hn, ws_ref[...])

    @pl.when(j >= n_sub)
    def _():
        o_ref[...] = _dot_nt(hn_ref[...], w_ref[...]).astype(o_ref.dtype)


def _proj(x, gain, w_main, w_small, tm, n_sub):
    m, d = x.shape
    n = w_main.shape[0]
    ns = w_small.shape[0]
    ts = tm // n_sub
    sub = lambda i, j: (i * n_sub + jnp.minimum(j, n_sub - 1), 0)
    feat = lambda j: jnp.maximum(j - n_sub, 0)
    return pl.pallas_call(
        functools.partial(_proj_kernel, n_sub=n_sub),
        grid=(m // tm, n_sub + n // COL_TILE),
        in_specs=[
            pl.BlockSpec((ts, d), sub),
            pl.BlockSpec((1, d), lambda i, j: (0, 0)),
            pl.BlockSpec((COL_TILE, d), lambda i, j: (feat(j), 0)),
            pl.BlockSpec((ns, d), lambda i, j: (0, 0)),
        ],
        out_specs=[
            pl.BlockSpec((tm, COL_TILE), lambda i, j: (i, feat(j))),
            pl.BlockSpec((ts, ns), sub),
        ],
        out_shape=[jax.ShapeDtypeStruct((m, n), BF16), jax.ShapeDtypeStruct((m, ns), F32)],
        scratch_shapes=[pltpu.VMEM((tm, d), BF16)],
        compiler_params=_cparams(("parallel", "arbitrary")),
        name="proj",
    )(x, gain, w_main, w_small)


def _latent_q_kernel(*refs, n_cq, kv_lora, heads_per_step):
    cq_refs = refs[:n_cq]
    (pf_ref, cos_ref, sin_ref, qan_ref, wqn_ref, wqp_ref, qng_ref, qrg_ref, kvg_ref, krg_ref,
     q_ref, ckv_ref, kpe_ref, kpep_ref, cqn_ref) = refs[n_cq:]
    cos = cos_ref[...]
    sin = sin_ref[...]
    lane = lax.broadcasted_iota(jnp.int32, cos.shape, 1)
    low = lane < QK_ROPE

    @pl.when(pl.program_id(1) == 0)
    def _():
        cq = jnp.concatenate([r[...].astype(F32) for r in cq_refs], axis=1)
        cqn_ref[...] = (_rms(cq, cq.shape[-1]) * qan_ref[...]).astype(BF16)
        pf = pf_ref[...]
        ckv_raw = pf[:, :kv_lora]
        ckv_ref[...] = _rms(ckv_raw, kv_lora) * kvg_ref[...]
        kraw = jnp.where(low, pf[:, kv_lora:kv_lora + LANE], 0.0)
        kpe = _rope_lanes(_rms(kraw, QK_ROPE) * krg_ref[...], cos, sin)
        kpe_ref[...] = kpe[:, :QK_ROPE]
        kpep_ref[...] = kpe.astype(BF16)

    cqn = cqn_ref[...]
    group = 4 if heads_per_step % 4 == 0 else 2
    for g0 in range(0, heads_per_step, group):
        qn = _dot(cqn, wqn_ref[:, g0 * QK_NOPE:(g0 + group) * QK_NOPE])
        qp = _dot(cqn, wqp_ref[:, g0 * QK_ROPE:(g0 + group) * QK_ROPE])
        for pair in range(group // 2):
            x = qp[:, pair * LANE:(pair + 1) * LANE]
            x2 = x * x
            ss_a = jnp.sum(jnp.where(low, x2, 0.0), axis=-1, keepdims=True)
            ss_b = jnp.sum(jnp.where(low, 0.0, x2), axis=-1, keepdims=True)
            scale = jnp.where(low, lax.rsqrt(ss_a * (1.0 / QK_ROPE) + EPS),
                              lax.rsqrt(ss_b * (1.0 / QK_ROPE) + EPS))
            r = _rope_lanes(x * scale * qrg_ref[...], cos, sin)
            halves = (jnp.where(low, r, 0.0), jnp.where(low, pltpu.roll(r, QK_ROPE, 1), 0.0))
            for h in (0, 1):
                base = (g0 + 2 * pair + h) * QK_PAD
                nope = qn[:, (2 * pair + h) * QK_NOPE:(2 * pair + h + 1) * QK_NOPE]
                q_ref[:, base:base + QK_NOPE] = (_rms(nope, QK_NOPE) * qng_ref[...]).astype(BF16)
                q_ref[:, base + QK_NOPE:base + QK_PAD] = halves[h].astype(BF16)


def _latent_q(p_bf, p_f32, cq_off, q_lora, kv_lora, cos, sin, qan, wqn, wqp, qng, qrg, kvg, krg, tm,
              heads_per_step):
    m = p_bf.shape[0]
    n_heads = wqn.shape[1] // QK_NOPE
    n_cq = q_lora // COL_TILE
    cq0 = cq_off // COL_TILE
    ns = p_f32.shape[1]
    wstep = heads_per_step * QK_PAD
    assert heads_per_step % 2 == 0
    row = lambda i, j: (i, 0)
    const = lambda i, j: (0, 0)
    pos_blocks = cos.shape[0] // tm
    pos = lambda i, j: (i % pos_blocks, 0)
    in_specs = [pl.BlockSpec((tm, COL_TILE), functools.partial(lambda i, j, c: (i, c), c=cq0 + c))
                for c in range(n_cq)]
    in_specs += [
        pl.BlockSpec((tm, ns), row),
        pl.BlockSpec((tm, LANE), pos),
        pl.BlockSpec((tm, LANE), pos),
        pl.BlockSpec((1, q_lora), const),
        pl.BlockSpec((q_lora, heads_per_step * QK_NOPE), lambda i, j: (0, j)),
        pl.BlockSpec((q_lora, heads_per_step * QK_ROPE), lambda i, j: (0, j)),
        pl.BlockSpec((1, LANE), const),
        pl.BlockSpec((1, LANE), const),
        pl.BlockSpec((1, kv_lora), const),
        pl.BlockSpec((1, LANE), const),
    ]
    return pl.pallas_call(
        functools.partial(_latent_q_kernel, n_cq=n_cq, kv_lora=kv_lora, heads_per_step=heads_per_step),
        grid=(m // tm, n_heads // heads_per_step),
        in_specs=in_specs,
        out_specs=[
            pl.BlockSpec((tm, wstep), lambda i, j: (i, j)),
            pl.BlockSpec((tm, kv_lora), row),
            pl.BlockSpec((tm, QK_ROPE), row),
            pl.BlockSpec((tm, LANE), row),
        ],
        out_shape=[
            jax.ShapeDtypeStruct((m, n_heads * QK_PAD), BF16),
            jax.ShapeDtypeStruct((m, kv_lora), F32),
            jax.ShapeDtypeStruct((m, QK_ROPE), F32),
            jax.ShapeDtypeStruct((m, LANE), BF16),
        ],
        scratch_shapes=[pltpu.VMEM((tm, q_lora), BF16)],
        compiler_params=_cparams(("parallel", "arbitrary")),
        name="latent_q",
    )(*([p_bf] * n_cq), p_f32, cos, sin, qan, wqn, wqp, qng, qrg, kvg, krg)


def _row_pieces(rows):
    piece = min(rows, ROW_TILE)
    return [slice(r, r + piece) for r in range(0, rows, piece)]


def _k_expand_kernel(ckv_ref, wk_ref, kng_ref, k_ref, *, n_heads):
    for rows in _row_pieces(ckv_ref.shape[0]):
        kf = _dot(ckv_ref[rows, :].astype(BF16), wk_ref[...])
        for h in range(n_heads):
            hs = slice(h * QK_NOPE, (h + 1) * QK_NOPE)
            k_ref[rows, hs] = (_rms(kf[:, hs], QK_NOPE) * kng_ref[...]).astype(BF16)


def _k_expand(ckv, wk, kng, tm):
    r, kv_lora = ckv.shape
    n_heads = wk.shape[1] // QK_NOPE
    return pl.pallas_call(
        functools.partial(_k_expand_kernel, n_heads=n_heads),
        grid=(r // tm,),
        in_specs=[
            pl.BlockSpec((tm, kv_lora), lambda i: (i, 0)),
            pl.BlockSpec((kv_lora, n_heads * QK_NOPE), lambda i: (0, 0)),
            pl.BlockSpec((1, LANE), lambda i: (0, 0)),
        ],
        out_specs=pl.BlockSpec((tm, n_heads * QK_NOPE), lambda i: (i, 0)),
        out_shape=jax.ShapeDtypeStruct((r, n_heads * QK_NOPE), BF16),
        compiler_params=_cparams(("parallel",)),
        name="k_expand",
    )(ckv, wk, kng)


def _kv_expand_kernel(ckv_ref, kpep_ref, wk_ref, wv_ref, kng_ref, k_ref, v_ref, *, n_heads):
    for rows in _row_pieces(ckv_ref.shape[0]):
        c = ckv_ref[rows, :].astype(BF16)
        kf = _dot(c, wk_ref[...])
        v_ref[rows, :] = _dot(c, wv_ref[...]).astype(BF16)
        kpep = kpep_ref[rows, :]
        for h in range(n_heads):
            kn = _rms(kf[:, h * QK_NOPE:(h + 1) * QK_NOPE], QK_NOPE) * kng_ref[...]
            k_ref[rows, h * QK_PAD:h * QK_PAD + QK_NOPE] = kn.astype(BF16)
            k_ref[rows, h * QK_PAD + QK_NOPE:(h + 1) * QK_PAD] = kpep


def _kv_expand(ckv, kpep, wk, wv, kng, tm):
    r, kv_lora = ckv.shape
    n_heads = wk.shape[1] // QK_NOPE
    row = lambda i: (i, 0)
    const = lambda i: (0, 0)
    return pl.pallas_call(
        functools.partial(_kv_expand_kernel, n_heads=n_heads),
        grid=(r // tm,),
        in_specs=[
            pl.BlockSpec((tm, kv_lora), row),
            pl.BlockSpec((tm, LANE), row),
            pl.BlockSpec((kv_lora, n_heads * QK_NOPE), const),
            pl.BlockSpec((kv_lora, n_heads * V_HEAD), const),
            pl.BlockSpec((1, LANE), const),
        ],
        out_specs=[
            pl.BlockSpec((tm, n_heads * QK_PAD), row),
            pl.BlockSpec((tm, n_heads * V_HEAD), row),
        ],
        out_shape=[
            jax.ShapeDtypeStruct((r, n_heads * QK_PAD), BF16),
            jax.ShapeDtypeStruct((r, n_heads * V_HEAD), BF16),
        ],
        compiler_params=_cparams(("parallel",)),
        name="kv_expand",
    )(ckv, kpep, wk, wv, kng)


def _lane_chunk_sum(p):
    out = p[:, :LANE]
    for c in range(1, p.shape[1] // LANE):
        out = out + p[:, c * LANE:(c + 1) * LANE]
    return out


def _softmax_scale():
    return float((QK_NOPE + QK_ROPE) ** -0.5 * 1.4426950408889634)


def _attn_prompt_block(n_full, q_ref, k_ref, km_ref, za_ref, o_ref, s_ref, p_ref, va_ref, *,
                       tq, heads, c, n_meta):
    td = min(tq, MXU_EDGE)
    pieces = [(i * tq, tq, 0, False) for i in range(n_full)]
    pieces += [(n_full * tq + j * td, td, j * td, True) for j in range(tq // td)]

    def upd(full, r0, fn):
        return fn(full) if r0 == 0 else jnp.concatenate([full[:r0], fn(full[r0:])], axis=0)

    def scores(h):
        qs = slice(h * QK_PAD, (h + 1) * QK_PAD)
        q = q_ref[:, qs]
        col = lax.broadcasted_iota(jnp.int32, (tq, LANE), 1)
        t = jnp.where(col < n_meta, _dot_nt(q, km_ref[:, qs]) * c, -jnp.inf)
        s_ref[h, :, :LANE] = t
        mx = t
        off = LANE
        for start, rows, r0, masked in pieces:
            t = _dot_nt(q[r0:], k_ref[start:start + rows, qs]) * c
            if masked:
                rq = lax.broadcasted_iota(jnp.int32, t.shape, 0) // CHUNK
                ck = lax.broadcasted_iota(jnp.int32, t.shape, 1) // CHUNK
                t = jnp.where(rq >= ck, t, -jnp.inf)
            s_ref[h, r0:, off:off + rows] = t
            tmax = t[:, :LANE]
            for ch in range(1, rows // LANE):
                tmax = jnp.maximum(tmax, t[:, ch * LANE:(ch + 1) * LANE])
            mx = upd(mx, r0, lambda a: jnp.maximum(a, tmax))
            off += rows
        return jnp.max(mx, axis=-1, keepdims=True)

    def values(h, m):
        vs = slice(h * V_HEAD, (h + 1) * V_HEAD)
        va = slice(2 * h * V_HEAD, 2 * (h + 1) * V_HEAD)
        p_ref[h, :, :LANE] = jnp.exp2(s_ref[h, :, :LANE] - m).astype(BF16)
        off = LANE
        for start, rows, r0, _ in pieces:
            p_ref[h, r0:, off:off + rows] = jnp.exp2(s_ref[h, r0:, off:off + rows] - m[r0:]).astype(BF16)
            off += rows
        off = LANE + n_full * tq
        acc = _dot(p_ref[h, :, :off], va_ref[:off, va])
        for start, rows, r0, _ in pieces[n_full:]:
            pv = _dot(p_ref[h, r0:, off:off + rows], va_ref[LANE + start:LANE + start + rows, va])
            acc = upd(acc, r0, lambda a: a + pv)
            off += rows
        za = za_ref[:, vs].astype(F32)
        out = acc[:, :V_HEAD] / acc[:, V_HEAD:V_HEAD + 1] * (za * jax.nn.sigmoid(za))
        o_ref[:, vs] = out.astype(o_ref.dtype)

    maxes = [scores(h) for h in range(heads)]
    for h in range(heads):
        values(h, maxes[h])


def _attn_prompt_kernel(q_ref, k_ref, v_ref, km_ref, vm_ref, za_ref, o_ref, s_ref, p_ref, va_ref, *,
                        nqb, heads, **kw):
    qi = pl.program_id(2)

    @pl.when(qi == 0)
    def _():
        for h in range(heads):
            vs = slice(h * V_HEAD, (h + 1) * V_HEAD)
            va_ref[:LANE, 2 * h * V_HEAD:(2 * h + 1) * V_HEAD] = vm_ref[:, vs]
            va_ref[LANE:, 2 * h * V_HEAD:(2 * h + 1) * V_HEAD] = v_ref[:, vs]
            va_ref[:, (2 * h + 1) * V_HEAD:(2 * h + 2) * V_HEAD] = jnp.ones((va_ref.shape[0], V_HEAD), BF16)

    for n_full in range(nqb):
        pl.when(qi == n_full)(
            functools.partial(_attn_prompt_block, n_full, q_ref, k_ref, km_ref, za_ref, o_ref, s_ref, p_ref, va_ref,
                              heads=heads, **kw))


def _attn_prompt(q, k, v, k_small, v_small, meta_blk, p_bf, za_off, batch, seq, tq, heads, n_meta):
    n_heads = q.shape[1] // QK_PAD
    nqb = seq // tq
    za0 = za_off // (V_HEAD * heads)
    assert za_off % (V_HEAD * heads) == 0 and n_heads % heads == 0 and tq % CHUNK == 0
    return pl.pallas_call(
        functools.partial(_attn_prompt_kernel, nqb=nqb, tq=tq, heads=heads, c=_softmax_scale(), n_meta=n_meta),
        grid=(batch, n_heads // heads, nqb),
        in_specs=[
            pl.BlockSpec((tq, heads * QK_PAD), lambda b, h, i: (b * nqb + i, h)),
            pl.BlockSpec((seq, heads * QK_PAD), lambda b, h, i: (b, h)),
            pl.BlockSpec((seq, heads * V_HEAD), lambda b, h, i: (b, h)),
            pl.BlockSpec((LANE, heads * QK_PAD), lambda b, h, i: (meta_blk, h)),
            pl.BlockSpec((LANE, heads * V_HEAD), lambda b, h, i: (meta_blk, h)),
            pl.BlockSpec((tq, heads * V_HEAD), lambda b, h, i: (b * nqb + i, za0 + h)),
        ],
        out_specs=pl.BlockSpec((tq, heads * V_HEAD), lambda b, h, i: (b * nqb + i, h)),
        out_shape=jax.ShapeDtypeStruct((batch * seq, n_heads * V_HEAD), BF16),
        scratch_shapes=[pltpu.VMEM((heads, tq, LANE + seq), F32), pltpu.VMEM((heads, tq, LANE + seq), BF16),
                        pltpu.VMEM((LANE + seq, 2 * heads * V_HEAD), BF16)],
        compiler_params=_cparams(("parallel", "parallel", "arbitrary")),
        name="attn_prompt",
    )(q, k, v, k_small, v_small, p_bf)


def _attn_sample_kernel(q_ref, kc_ref, rc_ref, cc_ref, ks_ref, cs_ref, wv_ref, za_ref, o_ref, pc_ref, ps_ref, li_ref, *,
                        heads, lq, c, n_meta, meta_row):
    b = pl.program_id(0)
    col = lax.broadcasted_iota(jnp.int32, (lq, ks_ref.shape[0]), 1)
    own = (col >= b * lq) & (col < (b + 1) * lq)
    meta = (col >= meta_row) & (col < meta_row + n_meta)
    visible = own | meta
    q_rope = jnp.concatenate([q_ref[:, h * QK_PAD + QK_NOPE:(h + 1) * QK_PAD] for h in range(heads)], axis=0)
    rope_cache = _dot_nt(q_rope, rc_ref[...])
    for h in range(heads):
        qs = slice(h * QK_PAD, (h + 1) * QK_PAD)
        rows = slice(h * lq, (h + 1) * lq)
        q = q_ref[:, qs]
        t_cache = (_dot_nt(q[:, :QK_NOPE], kc_ref[:, h * QK_NOPE:(h + 1) * QK_NOPE]) + rope_cache[rows]) * c
        t_new = jnp.where(visible, _dot_nt(q, ks_ref[:, qs]) * c, -jnp.inf)
        m = jnp.maximum(jnp.max(t_cache, axis=-1, keepdims=True), jnp.max(t_new, axis=-1, keepdims=True))
        p_cache = jnp.exp2(t_cache - m)
        p_new = jnp.exp2(t_new - m)
        lsum = jnp.sum(_lane_chunk_sum(p_cache) + _lane_chunk_sum(p_new), axis=-1, keepdims=True)
        pc_ref[rows, :] = p_cache.astype(BF16)
        ps_ref[rows, :] = p_new.astype(BF16)
        li_ref[rows, :] = jnp.broadcast_to(1.0 / lsum, (lq, LANE))
    mixed = (_dot(pc_ref[...], cc_ref[...].astype(BF16)) + _dot(ps_ref[...], cs_ref[...].astype(BF16)))
    mixed = (mixed * li_ref[:, 0:1]).astype(BF16)
    for h in range(heads):
        vs = slice(h * V_HEAD, (h + 1) * V_HEAD)
        za = za_ref[:, vs].astype(F32)
        out = _dot(mixed[h * lq:(h + 1) * lq], wv_ref[:, vs]) * (za * jax.nn.sigmoid(za))
        o_ref[:, vs] = out.astype(o_ref.dtype)


def _attn_sample(q, kc, rope_cache, ckv_cache, ks, ckv_small, wv, p_bf, za_off, batch, lq, past, n_meta):
    n_heads = q.shape[1] // QK_PAD
    rows = ks.shape[0]
    kv_lora = ckv_cache.shape[1]
    za0 = za_off // (V_HEAD * n_heads)
    assert za_off % (V_HEAD * n_heads) == 0
    const = lambda b: (0, 0)
    return pl.pallas_call(
        functools.partial(_attn_sample_kernel, heads=n_heads, lq=lq, c=_softmax_scale(), n_meta=n_meta,
                          meta_row=batch * lq),
        grid=(batch,),
        in_specs=[
            pl.BlockSpec((lq, n_heads * QK_PAD), lambda b: (b, 0)),
            pl.BlockSpec((past, n_heads * QK_NOPE), lambda b: (b, 0)),
            pl.BlockSpec((past, LANE), lambda b: (b, 0)),
            pl.BlockSpec((past, kv_lora), lambda b: (b, 0)),
            pl.BlockSpec((rows, n_heads * QK_PAD), const),
            pl.BlockSpec((rows, kv_lora), const),
            pl.BlockSpec((kv_lora, n_heads * V_HEAD), const),
            pl.BlockSpec((lq, n_heads * V_HEAD), lambda b: (b, za0)),
        ],
        out_specs=pl.BlockSpec((lq, n_heads * V_HEAD), lambda b: (b, 0)),
        out_shape=jax.ShapeDtypeStruct((batch * lq, n_heads * V_HEAD), BF16),
        scratch_shapes=[pltpu.VMEM((n_heads * lq, past), BF16), pltpu.VMEM((n_heads * lq, rows), BF16),
                        pltpu.VMEM((n_heads * lq, LANE), F32)],
        compiler_params=_cparams(("parallel",)),
        name="attn_sample",
    )(q, kc, rope_cache, ckv_cache, ks, ckv_small, wv, p_bf)


def _log_sigmoid(x):
    return jnp.minimum(x, 0.0) - jnp.log1p(jnp.exp(-jnp.abs(x)))


def _pad_rows(a, rows):
    if a.shape[0] == rows:
        return a
    return jnp.concatenate([a, jnp.zeros((rows - a.shape[0], a.shape[1]), a.dtype)], axis=0)


def _mlstm_kernel(q_ref, k_ref, v_ref, mo_ref, zb_ref, g_ref, gb_ref, gain_ref, c0_ref, n0_ref, m0_ref,
                  h_ref, c_ref, n_ref, m_ref, *, n_heads, lc, lp, dk, dv):
    @pl.when(pl.program_id(1) == 0)
    def _():
        c_ref[...] = c0_ref[...]
        n_ref[...] = n0_ref[...]
        m_ref[...] = m0_ref[...]

    hi = lax.Precision.HIGHEST
    gates = _pad_rows(g_ref[...] + gb_ref[...], lp)
    gates_t = gates.T
    t_col = lax.broadcasted_iota(jnp.int32, (lp, 1), 0)
    t_row = lax.broadcasted_iota(jnp.int32, (1, lp), 1)
    r_idx = lax.broadcasted_iota(jnp.int32, (lp, lp), 0)
    c_idx = lax.broadcasted_iota(jnp.int32, (lp, lp), 1)
    causal = r_idx >= c_idx
    lower = causal.astype(F32)
    upper = (c_idx >= r_idx).astype(F32)
    b_cols = jnp.dot(lower, jnp.where(t_col < lc, _log_sigmoid(gates), 0.0), precision=hi,
                     preferred_element_type=F32)
    b_rows = jnp.dot(jnp.where(t_row < lc, _log_sigmoid(gates_t), 0.0), upper, precision=hi,
                     preferred_element_type=F32)

    for h in range(n_heads):
        li = QK_ROPE + h
        lf_lane = QK_ROPE + n_heads + h
        ig_col = jnp.where(t_col < lc, gates[:, li:li + 1], NEG_BIG)
        ig_row = jnp.where(t_row < lc, gates_t[li:li + 1, :], NEG_BIG)
        b_col = b_cols[:, lf_lane:lf_lane + 1]
        b_row = b_rows[lf_lane:lf_lane + 1, :]
        m0 = m_ref[0, 0:1, h:h + 1]
        d = jnp.where(causal, b_col - b_row + ig_row, -jnp.inf)
        a_col = b_col + m0
        m = jnp.maximum(a_col, jnp.max(d, axis=-1, keepdims=True))
        w_inter = jnp.exp(a_col - m)

        q = _pad_rows(q_ref[:, h * dk:(h + 1) * dk], lp)
        k = _pad_rows(k_ref[:, h * dk:(h + 1) * dk], lp) * (dk ** -0.5)
        v = _pad_rows(v_ref[:, h * dv:(h + 1) * dv], lp)
        qk = _dot_nt(q, k) * jnp.exp(d - m)
        c_old = c_ref[0, h]
        n_old = n_ref[0, h:h + 1, :]
        num = w_inter * _dot(q, c_old.astype(BF16)) + _dot(qk.astype(BF16), v)
        den = (w_inter * jnp.sum(q.astype(F32) * n_old, axis=-1, keepdims=True)
               + jnp.sum(qk, axis=-1, keepdims=True))
        denc = jnp.maximum(jnp.abs(den), jnp.exp(-m))

        b_last = b_col[lp - 1:lp, :]
        g_row = b_last - b_row + ig_row
        g_col = b_last - b_col + ig_col
        m_new = jnp.maximum(b_last + m0, jnp.max(g_row, axis=-1, keepdims=True))
        decay = jnp.exp(b_last + m0 - m_new)
        kw = k.astype(F32) * jnp.exp(g_col - m_new)
        c_ref[0, h] = decay * c_old + _dot(kw.T.astype(BF16), v)
        n_ref[0, h:h + 1, :] = decay * n_old + jnp.sum(kw, axis=0, keepdims=True)
        m_ref[0, 0:1, h:h + 1] = m_new

        num = num[:lc]
        denc = denc[:lc]
        normed = num * lax.rsqrt(jnp.sum(num * num, axis=-1, keepdims=True) * (1.0 / dv) + EPS * denc * denc)
        mo = mo_ref[:, h * dv:(h + 1) * dv].astype(F32)
        zb = zb_ref[:, h * dv:(h + 1) * dv].astype(F32)
        out = jax.nn.sigmoid(mo) * (normed * gain_ref[:, h * dv:(h + 1) * dv])
        h_ref[:, h * dv:(h + 1) * dv] = (out * (zb * jax.nn.sigmoid(zb))).astype(h_ref.dtype)


def _mlstm(p_bf, p_f32, offs, gate_bias, gain, c0, n0, m0, batch, seq, lc, row0, share_state, n_heads, dk, dv):
    lp = -(-lc // LANE) * LANE
    nch = seq // lc
    blk0 = row0 // lc
    wqk = n_heads * dk
    wv = n_heads * dv
    gate_blk = p_f32.shape[1] // LANE - 1
    rows = lambda col: (lambda b, c: (blk0 + b * nch + c, col))
    state = (lambda b, c: (0, 0, 0, 0)) if share_state else (lambda b, c: (b, 0, 0, 0))
    state3 = (lambda b, c: (0, 0, 0)) if share_state else (lambda b, c: (b, 0, 0))
    out_rows = batch * seq
    return pl.pallas_call(
        functools.partial(_mlstm_kernel, n_heads=n_heads, lc=lc, lp=lp, dk=dk, dv=dv),
        grid=(batch, nch),
        in_specs=[
            pl.BlockSpec((lc, wqk), rows(offs["mq"] // wqk)),
            pl.BlockSpec((lc, wqk), rows(offs["mk"] // wqk)),
            pl.BlockSpec((lc, wv), rows(offs["mv"] // wv)),
            pl.BlockSpec((lc, wv), rows(offs["mo"] // wv)),
            pl.BlockSpec((lc, wv), rows(offs["zb"] // wv)),
            pl.BlockSpec((lc, LANE), rows(gate_blk)),
            pl.BlockSpec((1, LANE), lambda b, c: (0, 0)),
            pl.BlockSpec((1, wv), lambda b, c: (0, 0)),
            pl.BlockSpec((1, n_heads, dk, dv), state),
            pl.BlockSpec((1, n_heads, dk), state3),
            pl.BlockSpec((1, 1, n_heads), state3),
        ],
        out_specs=[
            pl.BlockSpec((lc, wv), lambda b, c: (b * nch + c, 0)),
            pl.BlockSpec((1, n_heads, dk, dv), lambda b, c: (b, 0, 0, 0)),
            pl.BlockSpec((1, n_heads, dk), lambda b, c: (b, 0, 0)),
            pl.BlockSpec((1, 1, n_heads), lambda b, c: (b, 0, 0)),
        ],
        out_shape=[
            jax.ShapeDtypeStruct((out_rows, wv), BF16),
            jax.ShapeDtypeStruct((batch, n_heads, dk, dv), F32),
            jax.ShapeDtypeStruct((batch, n_heads, dk), F32),
            jax.ShapeDtypeStruct((batch, 1, n_heads), F32),
        ],
        compiler_params=_cparams(("parallel", "arbitrary")),
        name="mlstm",
    )(p_bf, p_bf, p_bf, p_bf, p_bf, p_f32, gate_bias, gain, c0, n0, m0)


def _out_proj_kernel(a_ref, m_ref, wa_ref, wm_ref, x_ref, o_ref, *wb_refs):
    wa = wa_ref[...].astype(BF16)
    wm = wm_ref[...].astype(BF16)
    o_ref[...] = x_ref[...] + _dot(a_ref[...], wa) + _dot(m_ref[...], wm)
    if wb_refs:
        wb_refs[0][...] = wa
        wb_refs[1][...] = wm


def _out_proj(a, ml, w_a, w_m, wm_blk, x, tm, tn, emit_bf16=False):
    rows, wa = a.shape
    wm = ml.shape[1]
    d = w_a.shape[1]
    assert wa == wm and (not emit_bf16 or rows == tm)
    out_specs = [pl.BlockSpec((tm, tn), lambda i, j: (i, j))]
    out_shape = [jax.ShapeDtypeStruct((rows, d), F32)]
    if emit_bf16:
        out_specs += [pl.BlockSpec((wa, tn), lambda i, j: (0, j)), pl.BlockSpec((wm, tn), lambda i, j: (0, j))]
        out_shape += [jax.ShapeDtypeStruct((wa, d), BF16), jax.ShapeDtypeStruct((wm, d), BF16)]
    return pl.pallas_call(
        _out_proj_kernel,
        grid=(rows // tm, d // tn),
        in_specs=[
            pl.BlockSpec((tm, wa), lambda i, j: (i, 0)),
            pl.BlockSpec((tm, wm), lambda i, j: (i, 0)),
            pl.BlockSpec((wa, tn), lambda i, j: (0, j)),
            pl.BlockSpec((wm, tn), lambda i, j: (wm_blk, j)),
            pl.BlockSpec((tm, tn), lambda i, j: (i, j)),
        ],
        out_specs=out_specs,
        out_shape=out_shape,
        compiler_params=_cparams(("parallel", "arbitrary")),
        name="out_proj",
    )(a, ml, w_a, w_m, x)


def _rope_tables(pos):
    inv_freq = ROPE_THETA ** (-jnp.arange(HALF_ROPE, dtype=F32) / HALF_ROPE)
    ang = pos.astype(F32)[:, None] * inv_freq[None, :]
    cos, sin = jnp.cos(ang), jnp.sin(ang)
    return jnp.concatenate([cos, cos, cos, cos], axis=1), jnp.concatenate([-sin, sin, -sin, sin], axis=1)


def _pad_lanes(vec):
    return jnp.pad(vec.astype(F32), (0, LANE - vec.shape[0]))[None, :]


def _row_tile(rows, target):
    t = min(rows, target)
    while rows % t:
        t //= 2
    return t


def kernel(x_prompt, x_sample, cache_ckv, cache_kpe, state_C, state_n, state_m, meta_tokens, norm_gain, w_in,
           b_igate, b_fgate, q_a_norm, w_q_up, q_nope_norm, q_rope_norm, kv_a_norm, k_rope_norm, w_kv_up,
           k_nope_norm, ml_out_norm, w_out):
    batch, seq, d = x_prompt.shape
    dbatch, dseq, _ = x_sample.shape
    past = cache_ckv.shape[1]
    n_meta = meta_tokens.shape[0]
    q_lora = q_a_norm.shape[0]
    kv_lora = kv_a_norm.shape[0]
    ml_heads, dv = ml_out_norm.shape
    dk = state_n.shape[-1]
    mla_heads = w_kv_up.shape[1] // (QK_NOPE + V_HEAD)
    mla_w = mla_heads * V_HEAD
    ml_w = ml_heads * dv
    mqk_w = ml_heads * dk
    assert q_lora % COL_TILE == 0 and kv_lora % LANE == 0 and 2 * ml_heads <= LANE - QK_ROPE
    assert mla_w == ml_w and w_out.shape[0] == mla_w + ml_w

    o_cq = 0
    o_ckv = o_cq + q_lora
    o_kpe = o_ckv + kv_lora
    o_mq = o_kpe + QK_ROPE
    o_mk = o_mq + mqk_w
    o_mv = o_mk + mqk_w
    o_mo = o_mv + ml_w
    o_mi = o_mo + ml_w
    o_mf = o_mi + ml_heads
    o_za = o_mf + ml_heads
    o_zb = o_za + mla_w

    offs = {"mq": 0, "mk": mqk_w, "mv": 2 * mqk_w, "mo": 2 * mqk_w + ml_w, "za": 2 * mqk_w + 2 * ml_w,
            "zb": 2 * mqk_w + 2 * ml_w + mla_w, "cq": 2 * mqk_w + 3 * ml_w + mla_w}
    gate_pad = LANE - QK_ROPE - 2 * ml_heads
    w_in_t = w_in.T
    main_segments = [(o_mq, mqk_w, offs["mq"]), (o_mk, mqk_w, offs["mk"]), (o_mv, ml_w, offs["mv"]),
                     (o_mo, ml_w, offs["mo"]), (o_za, mla_w, offs["za"]), (o_zb, ml_w, offs["zb"]),
                     (o_cq, q_lora, offs["cq"])]
    w_small = _stack_rows(w_in_t, [(o_ckv, kv_lora), (o_kpe, QK_ROPE), (o_mi, 2 * ml_heads)], kv_lora + LANE,
                          _row_tile(d, 4 * COL_TILE))
    gate_bias = jnp.concatenate([jnp.zeros((QK_ROPE,), F32), b_igate.astype(F32), b_fgate.astype(F32),
                                 jnp.zeros((gate_pad,), F32)])[None, :]
    qk_w = QK_NOPE + QK_ROPE
    kv_w = QK_NOPE + V_HEAD
    wqn, wqp = _regroup(w_q_up, [[(h * qk_w, QK_NOPE, h * QK_NOPE) for h in range(mla_heads)],
                                 [(h * qk_w + QK_NOPE, QK_ROPE, h * QK_ROPE) for h in range(mla_heads)]],
                        _row_tile(q_lora, ROW_TILE))
    wk, wv = _regroup(w_kv_up, [[(h * kv_w, QK_NOPE, h * QK_NOPE) for h in range(mla_heads)],
                                [(h * kv_w + QK_NOPE, V_HEAD, h * V_HEAD) for h in range(mla_heads)]], kv_lora)
    gain_row = norm_gain.astype(F32)[None, :]
    qan = q_a_norm.astype(F32)[None, :]
    kvg = kv_a_norm.astype(F32)[None, :]
    qng = q_nope_norm.astype(F32)[None, :]
    kng = k_nope_norm.astype(F32)[None, :]
    qrg = jnp.tile(q_rope_norm.astype(F32), LANE // QK_ROPE)[None, :]
    krg = _pad_lanes(k_rope_norm)
    ml_gain = ml_out_norm.astype(F32).reshape(1, ml_w)

    n_p = batch * seq
    n_s = dbatch * dseq
    assert n_s % LANE == 0 and n_meta <= LANE and n_meta % 16 == 0
    rows_s = n_s + n_meta
    xp = x_prompt.reshape(n_p, d)
    xs = jnp.concatenate([x_sample.reshape(n_s, d), meta_tokens.astype(x_sample.dtype)], axis=0)

    pos_s = jnp.concatenate([jnp.tile(n_meta + past + jnp.arange(dseq), dbatch), jnp.arange(n_meta)])
    cos_p, sin_p = _rope_tables(n_meta + jnp.arange(seq))
    cos_s, sin_s = _rope_tables(pos_s)

    tm_p = _row_tile(seq, ROW_TILE)
    proj_sub = 2 if n_p % (2 * tm_p) == 0 else 1
    w_main, pbf_s, pf_s = _proj_regroup(xs, gain_row, w_in_t, main_segments, w_small)
    pbf_p, pf_p = _proj(xp, gain_row, w_main, w_small, proj_sub * tm_p, proj_sub)

    hps = mla_heads
    q_p, ckv_p, kpe_p, kpep_p = _latent_q(pbf_p, pf_p, offs["cq"], q_lora, kv_lora, cos_p, sin_p, qan, wqn, wqp,
                                          qng, qrg, kvg, krg, tm_p, hps)
    q_s, ckv_s, kpe_s, kpep_s = _latent_q(pbf_s, pf_s, offs["cq"], q_lora, kv_lora, cos_s, sin_s, qan, wqn, wqp,
                                          qng, qrg, kvg, krg, rows_s, hps)

    small_rows = n_s + LANE
    meta_blk = n_s // LANE
    tail = ((0, small_rows - rows_s), (0, 0))
    k_p, v_p = _kv_expand(ckv_p, kpep_p, wk, wv, kng, _row_tile(n_p, 2 * ROW_TILE))
    ckv_small = jnp.pad(ckv_s, tail)
    k_s, v_s = _kv_expand(ckv_small, jnp.pad(kpep_s, tail), wk, wv, kng, small_rows)
    cache_rows = dbatch * past
    cache_kpep = jnp.pad(cache_kpe.reshape(cache_rows, QK_ROPE), ((0, 0), (0, LANE - QK_ROPE))).astype(BF16)
    ckv_cache = cache_ckv.reshape(cache_rows, kv_lora).astype(F32)
    k_c = _k_expand(ckv_cache, wk, kng, _row_tile(cache_rows, 2 * ROW_TILE))

    tq = _row_tile(seq, ATTN_TQ)
    attn_p = _attn_prompt(q_p, k_p, v_p, k_s, v_s, meta_blk, pbf_p, offs["za"], batch, seq, tq,
                          min(ATTN_HEADS, mla_heads), n_meta)
    attn_s = _attn_sample(q_s, k_c, cache_kpep, ckv_cache, k_s, ckv_small, wv, pbf_s, offs["za"], dbatch, dseq, past,
                          n_meta)

    zc = jnp.zeros((1, ml_heads, dk, dv), F32)
    zn = jnp.zeros((1, ml_heads, dk), F32)
    zm = jnp.zeros((1, 1, ml_heads), F32)
    _, c_m, n_m, m_m = _mlstm(pbf_s, pf_s, offs, gate_bias, ml_gain, zc, zn, zm, 1, n_meta, n_meta, n_s, True,
                              ml_heads, dk, dv)
    lc_p = _row_tile(seq, MLSTM_CHUNK)
    ml_p, c_p, n_pr, m_p = _mlstm(pbf_p, pf_p, offs, gate_bias, ml_gain, c_m, n_m, m_m, batch, seq, lc_p, 0,
                                  True, ml_heads, dk, dv)
    ml_s, c_s, n_sm, m_s = _mlstm(pbf_s, pf_s, offs, gate_bias, ml_gain, state_C.astype(F32),
                                  state_n.astype(F32), state_m.astype(F32).reshape(dbatch, 1, ml_heads),
                                  dbatch, dseq, dseq, 0, False, ml_heads, dk, dv)

    y_s, w_oa, w_om = _out_proj(attn_s, ml_s, w_out, w_out, 1, xs, n_s, _row_tile(d, 2 * COL_TILE), emit_bf16=True)
    y_p, = _out_proj(attn_p, ml_p, w_oa, w_om, 0, xp, _row_tile(n_p, 2 * ROW_TILE), _row_tile(d, 2 * COL_TILE))

    meta_ckv = ckv_s[n_s:n_s + n_meta]
    meta_kpe = kpe_s[n_s:n_s + n_meta]
    ckv_prompt = jnp.concatenate([jnp.broadcast_to(meta_ckv[None], (batch, n_meta, kv_lora)),
                                  ckv_p.reshape(batch, seq, kv_lora)], axis=1)
    kpe_prompt = jnp.concatenate([jnp.broadcast_to(meta_kpe[None], (batch, n_meta, QK_ROPE)),
                                  kpe_p.reshape(batch, seq, QK_ROPE)], axis=1)
    return (y_p.reshape(batch, seq, d), y_s.reshape(dbatch, dseq, d), ckv_prompt, kpe_prompt,
            c_p, n_pr, m_p.reshape(batch, ml_heads),
            ckv_s[:n_s].reshape(dbatch, dseq, kv_lora), kpe_s[:n_s].reshape(dbatch, dseq, QK_ROPE),
            c_s, n_sm, m_s.reshape(dbatch, ml_heads))
```
